```python
import math
import jax, jax.numpy as jnp
from jax import lax
import numpy as np

D_MODEL = 2048
BATCH = 2
SEQ = 4096
DEPTH = 2

MEM_LEN = 256
MEM_HEADS = 4
MEM_HEAD_DIM = D_MODEL // MEM_HEADS

SC_W = D_MODEL // 4
SC_K = 3

RW_HEAD_DIM = 64
RW_W = D_MODEL // 4
RW_HEADS = RW_W // RW_HEAD_DIM
RW_LORA_W = 96
RW_LORA_A = 96
RW_LORA_G = 256
RW_DECAY_SCALE = math.exp(-0.5)
RW_LN_EPS = 64e-5

RET_HEADS = 4
RET_DK = 128
RET_DV = 2 * RET_DK
RET_CHUNK = 128
RET_LN_EPS = 1e-5

NSA_HEADS = 8
NSA_KV_HEADS = 2
NSA_DH = 64
NSA_GROUP = NSA_HEADS // NSA_KV_HEADS
CMP_LEN = 32
CMP_STRIDE = 16
CMP_HIDDEN = 256
SEL_BLOCK = 64
SEL_TOP = 16
WINDOW = 512
Q_BLOCK = 128

FFN_DIM = 5632
FFN_K = 3

ROPE_THETA = 10000.0
EPS = 1e-6
N_BRANCH = 4

SC_SIZES = (SC_W, SC_W, SC_W)
RW_SIZES = (RW_W, RW_W, RW_W, RW_LORA_W, RW_LORA_A, RW_LORA_G)
RET_SIZES = (RET_HEADS * RET_DK, RET_HEADS * RET_DK, RET_HEADS * RET_DV, RET_HEADS * RET_DV)
NSA_SIZES = (NSA_HEADS * NSA_DH,) + (NSA_KV_HEADS * NSA_DH,) * 6 + (NSA_HEADS * 3,)
GATE_SIZES = (D_MODEL,) * N_BRANCH
GROUP_SIZES = (sum(SC_SIZES), sum(RW_SIZES), sum(RET_SIZES), sum(NSA_SIZES), sum(GATE_SIZES))
N_IN = sum(GROUP_SIZES)
RW_TOTAL = sum(RW_SIZES)
BR_SIZES = (SC_W, RW_W, RET_HEADS * RET_DV, NSA_HEADS * NSA_DH)
N_BR = sum(BR_SIZES)

kernel_name = "hybrid_gated_parallel_mixer_trunk"


def _split(x, sizes, axis=-1):
    return jnp.split(x, [int(c) for c in np.cumsum(sizes)[:-1]], axis=axis)


def rms_norm(x, g):
    xf = x.astype(jnp.float32)
    y = xf * lax.rsqrt(jnp.mean(xf * xf, axis=-1, keepdims=True) + EPS)
    return (y * g.astype(jnp.float32)).astype(x.dtype)


def head_norm(x, eps):
    xf = x.astype(jnp.float32)
    mu = jnp.mean(xf, axis=-1, keepdims=True)
    var = jnp.mean(jnp.square(xf - mu), axis=-1, keepdims=True)
    return (xf - mu) * lax.rsqrt(var + eps)


def causal_dwconv(x, w):
    k_w = w.shape[0]
    s = x.shape[1]
    xp = jnp.pad(x, ((0, 0), (k_w - 1, 0), (0, 0)))
    y = w[0] * xp[:, 0:s]
    for j in range(1, k_w):
        y = y + w[j] * xp[:, j:j + s]
    return y


def rope_freqs(d):
    return ROPE_THETA ** (-jnp.arange(0, d, 2, dtype=jnp.float32) / d)


def retnet_freqs(d):
    return 1.0 / (ROPE_THETA ** jnp.linspace(0.0, 1.0, d // 2, dtype=jnp.float32))


def rope(x, pos, inv_freq):
    ang = pos.astype(jnp.float32)[..., None] * inv_freq
    cos = jnp.cos(ang)[:, :, None, :].astype(x.dtype)
    sin = jnp.sin(ang)[:, :, None, :].astype(x.dtype)
    x1, x2 = jnp.split(x, 2, axis=-1)
    return jnp.concatenate([x1 * cos - x2 * sin, x2 * cos + x1 * sin], axis=-1)


def masked_softmax(s, mask):
    s = jnp.where(mask, s.astype(jnp.float32), -jnp.inf)
    m = jnp.max(s, axis=-1, keepdims=True)
    m = jnp.where(jnp.isfinite(m), m, 0.0)
    e = jnp.where(mask, jnp.exp(s - m), 0.0)
    return e / jnp.maximum(jnp.sum(e, axis=-1, keepdims=True), 1e-30)


def short_conv_mixer(b_gate, c_gate, xin, w_conv):
    return b_gate * causal_dwconv(c_gate * xin, w_conv)


def wkv7_scan(r, w, k, v, kk, a):
    bsz, _, nh, nd = r.shape
    xs = tuple(jnp.moveaxis(t, 1, 0) for t in (r, w, k, v, kk, a))

    def step(state, inp):
        r_t, w_t, k_t, v_t, kk_t, a_t = inp
        sa = jnp.einsum('bhvk,bhk->bhv', state, -kk_t)
        state = (state * w_t[:, :, None, :]
                 + sa[..., None] * (kk_t * a_t)[:, :, None, :]
                 + v_t[..., None] * k_t[:, :, None, :])
        return state, jnp.einsum('bhvk,bhk->bhv', state, r_t)

    s0 = jnp.zeros((bsz, nh, nd, nd), jnp.float32)
    _, y = lax.scan(step, s0, xs)
    return jnp.moveaxis(y, 0, 1)


def rwkv7_mixer(z, mu, w0, w_w2, a0, w_a2, w_g2, k_k, k_a, r_k, lnx_g, lnx_b):
    bsz, s, _ = z.shape
    z_prev = jnp.pad(z, ((0, 0), (1, 0), (0, 0)))[:, :-1]
    z = z + (z_prev - z) * mu
    r, k, v, zw, za, zg = _split(z, RW_SIZES)
    d = w0 + jnp.tanh(zw) @ w_w2
    w = jnp.exp(-RW_DECAY_SCALE * jax.nn.sigmoid(d.astype(jnp.float32)))
    a = jax.nn.sigmoid(a0 + za @ w_a2)
    g = jax.nn.sigmoid(zg) @ w_g2
    heads = lambda t: t.reshape(bsz, s, RW_HEADS, RW_HEAD_DIM).astype(jnp.float32)
    kk = heads(k * k_k)
    kk = kk / jnp.maximum(jnp.sqrt(jnp.sum(kk * kk, axis=-1, keepdims=True)), 1e-12)
    k = k * (1.0 + (a - 1.0) * k_a)
    rh, kh, vh, ah, wh = heads(r), heads(k), heads(v), heads(a), heads(w)
    y = wkv7_scan(rh, wh, kh, vh, kk, ah)
    y = head_norm(y, RW_LN_EPS) * lnx_g.reshape(RW_HEADS, RW_HEAD_DIM) + lnx_b.reshape(RW_HEADS, RW_HEAD_DIM)
    bonus = jnp.sum(rh * kh * r_k, axis=-1, keepdims=True) * vh
    return (y + bonus).reshape(bsz, s, RW_W).astype(z.dtype) * g


def retention_mixer(q, k, v, gate, positions):
    bsz, s, _ = q.shape
    nh, dk, dv, c = RET_HEADS, RET_DK, RET_DV, RET_CHUNK
    n = s // c
    inv = retnet_freqs(dk)
    q = rope(q.reshape(bsz, s, nh, dk), positions, inv)
    k = rope(k.reshape(bsz, s, nh, dk), positions, inv) * (dk ** -0.5)
    v = v.reshape(bsz, s, nh, dv)
    chunk = lambda t: t.reshape(bsz, n, c, nh, t.shape[-1]).transpose(0, 3, 1, 2, 4)
    qc, kc, vc = chunk(q), chunk(k), chunk(v)
    lg = jnp.log(1.0 - 2.0 ** (-5.0 - jnp.arange(nh, dtype=jnp.float32)))
    idx = jnp.arange(c, dtype=jnp.float32)
    diff = idx[:, None] - idx[None, :]
    decay_in = jnp.where(diff >= 0, jnp.exp(jnp.maximum(diff, 0.0) * lg[:, None, None]), 0.0)
    scores = jnp.einsum('bhncd,bhnmd->bhncm', qc, kc) * decay_in[:, None]
    o_inner = jnp.einsum('bhncm,bhnme->bhnce', scores, vc)
    zeta = jnp.exp((c - 1 - idx)[None, :] * lg[:, None])
    xi = jnp.exp((idx + 1.0)[None, :] * lg[:, None])
    u = jnp.einsum('bhnmd,bhnme->bhnde', kc * zeta[:, None, :, None], vc)
    chunk_decay = jnp.exp(c * lg)[:, None, None]

    def step(state, u_i):
        return state * chunk_decay + u_i, state

    _, r_prev = lax.scan(step, jnp.zeros((bsz, nh, dk, dv), u.dtype), jnp.moveaxis(u, 2, 0))
    r_prev = jnp.moveaxis(r_prev, 0, 2)
    o_cross = jnp.einsum('bhncd,bhnde->bhnce', qc * xi[:, None, :, None], r_prev)
    o = (o_inner + o_cross).transpose(0, 2, 3, 1, 4).reshape(bsz, s, nh, dv)
    o = head_norm(o, RET_LN_EPS).reshape(bsz, s, nh * dv).astype(gate.dtype)
    return o * jax.nn.silu(gate)


def nsa_mixer(q, kc, vc, ks, vs, kw, vw, gates, positions, cmp_pos, cmp_w1, cmp_w2):
    bsz, s, _ = q.shape
    nh, ng, hpg, dh = NSA_HEADS, NSA_KV_HEADS, NSA_GROUP, NSA_DH
    inv = rope_freqs(dh)
    scale = dh ** -0.5
    q = rope(q.reshape(bsz, s, nh, dh), positions, inv)
    kv_shape = (bsz, s, ng, dh)
    kc, vc, vs, vw = kc.reshape(kv_shape), vc.reshape(kv_shape), vs.reshape(kv_shape), vw.reshape(kv_shape)
    ks = rope(ks.reshape(kv_shape), positions, inv)
    kw = rope(kw.reshape(kv_shape), positions, inv)

    n_cmp = (s - CMP_LEN) // CMP_STRIDE + 1
    starts = np.arange(n_cmp) * CMP_STRIDE
    tok = starts[:, None] + np.arange(CMP_LEN)[None, :]
    cmp_end = starts + CMP_LEN - 1

    def compress(t, pos_emb, w1, w2):
        blk = t[:, tok] + pos_emb[:, None, :]
        blk = blk.transpose(0, 1, 3, 2, 4).reshape(bsz, n_cmp, ng, CMP_LEN * dh)
        return jax.nn.gelu(blk @ w1) @ w2

    k_cmp = rope(compress(kc, cmp_pos[0], cmp_w1[0], cmp_w2[0]), positions[:, cmp_end], inv)
    v_cmp = compress(vc, cmp_pos[1], cmp_w1[1], cmp_w2[1])
    qg = q.reshape(bsz, s, ng, hpg, dh)
    t_idx = np.arange(s)
    cmask = cmp_end[None, :] <= t_idx[:, None]
    s_cmp = jnp.einsum('bsgqd,bcgd->bgqsc', qg, k_cmp) * scale
    p_cmp = masked_softmax(s_cmp, cmask)
    o_cmp = jnp.einsum('bgqsc,bcgd->bsgqd', p_cmp.astype(v_cmp.dtype), v_cmp).reshape(bsz, s, nh, dh)

    n_sel = s // SEL_BLOCK
    jb = np.arange(n_sel)
    overlap = ((starts[:, None] < (jb[None, :] + 1) * SEL_BLOCK)
               & (starts[:, None] + CMP_LEN > jb[None, :] * SEL_BLOCK)).astype(np.float32)
    imp = jnp.einsum('bgqsc,cj->bgsj', p_cmp, overlap)
    cur = t_idx // SEL_BLOCK
    valid = jb[None, :] <= cur[:, None]
    forced = (jb[None, :] == 0) | (jb[None, :] == cur[:, None]) | (jb[None, :] == cur[:, None] - 1)
    imp = jnp.where(forced, jnp.inf, jnp.where(valid, imp, -jnp.inf))
    n_top = min(SEL_TOP, n_sel)
    _, sel_idx = lax.top_k(imp, n_top)

    qh = qg.transpose(0, 2, 3, 1, 4) * scale
    ks_blk = ks.transpose(0, 2, 1, 3).reshape(bsz, ng, n_sel, SEL_BLOCK, dh)
    vs_blk = vs.transpose(0, 2, 1, 3).reshape(bsz, ng, n_sel, SEL_BLOCK, dh)
    pad = ((0, 0), (0, 0), (WINDOW, 0), (0, 0))
    kw_pad = jnp.pad(kw.transpose(0, 2, 1, 3), pad)
    vw_pad = jnp.pad(vw.transpose(0, 2, 1, 3), pad)
    b_ix = jnp.arange(bsz)[:, None, None, None]
    g_ix = jnp.arange(ng)[None, :, None, None]

    def block(qb):
        q0 = qb * Q_BLOCK
        qblk = lax.dynamic_slice_in_dim(qh, q0, Q_BLOCK, axis=3)
        tq = q0 + jnp.arange(Q_BLOCK)
        idx = lax.dynamic_slice_in_dim(sel_idx, q0, Q_BLOCK, axis=2)
        kg = ks_blk[b_ix, g_ix, idx]
        vg = vs_blk[b_ix, g_ix, idx]
        tok_pos = idx[..., None] * SEL_BLOCK + jnp.arange(SEL_BLOCK)
        smask = (tok_pos <= tq[:, None, None]).reshape(bsz, ng, 1, Q_BLOCK, n_top * SEL_BLOCK)
        s_sel = jnp.einsum('bgqtd,bgtkld->bgqtkl', qblk, kg).reshape(bsz, ng, hpg, Q_BLOCK, n_top * SEL_BLOCK)
        p_sel = masked_softmax(s_sel, smask).reshape(bsz, ng, hpg, Q_BLOCK, n_top, SEL_BLOCK)
        o_sel = jnp.einsum('bgqtkl,bgtkld->bgqtd', p_sel.astype(vg.dtype), vg)
        kwb = lax.dynamic_slice_in_dim(kw_pad, q0, Q_BLOCK + WINDOW, axis=2)
        vwb = lax.dynamic_slice_in_dim(vw_pad, q0, Q_BLOCK + WINDOW, axis=2)
        kpos = q0 - WINDOW + jnp.arange(Q_BLOCK + WINDOW)
        dlt = tq[:, None] - kpos[None, :]
        wmask = (dlt >= 0) & (dlt < WINDOW) & (kpos[None, :] >= 0)
        s_win = jnp.einsum('bgqtd,bgkd->bgqtk', qblk, kwb)
        p_win = masked_softmax(s_win, wmask)
        o_win = jnp.einsum('bgqtk,bgkd->bgqtd', p_win.astype(vwb.dtype), vwb)
        return o_sel, o_win

    o_sel, o_win = lax.map(block, jnp.arange(s // Q_BLOCK))
    unblock = lambda o: o.transpose(1, 0, 4, 2, 3, 5).reshape(bsz, s, nh, dh)
    o_sel, o_win = unblock(o_sel), unblock(o_win)
    gt = jax.nn.sigmoid(gates.reshape(bsz, s, nh, 3))
    o = gt[..., 0:1] * o_cmp + gt[..., 1:2] * o_sel + gt[..., 2:3] * o_win
    return o.reshape(bsz, s, nh * dh)


def hybrid_mixer(h, positions, w_in, sc_conv, rw_mu, rw_w0, rw_w2, rw_a0, rw_a2, rw_g2,
                 rw_k_k, rw_k_a, rw_r_k, rw_lnx_g, rw_lnx_b, nsa_cmp_pos, nsa_cmp_w1, nsa_cmp_w2,
                 w_branch, w_out):
    z = h @ w_in
    z_sc, z_rw, z_ret, z_nsa, z_gate = _split(z, GROUP_SIZES)
    o_a = short_conv_mixer(*_split(z_sc, SC_SIZES), sc_conv)
    o_b = rwkv7_mixer(z_rw, rw_mu, rw_w0, rw_w2, rw_a0, rw_a2, rw_g2, rw_k_k, rw_k_a, rw_r_k, rw_lnx_g, rw_lnx_b)
    o_c = retention_mixer(*_split(z_ret, RET_SIZES), positions)
    o_d = nsa_mixer(*_split(z_nsa, NSA_SIZES), positions, nsa_cmp_pos, nsa_cmp_w1, nsa_cmp_w2)
    g_a, g_b, g_c, g_d = [jax.nn.sigmoid(g) for g in _split(z_gate, GATE_SIZES)]
    w_a, w_b, w_c, w_d = _split(w_branch, BR_SIZES, axis=0)
    merged = g_a * (o_a @ w_a) + g_b * (o_b @ w_b) + g_c * (o_c @ w_c) + g_d * (o_d @ w_d)
    return merged @ w_out


def memory_cross_attention(h, mem_n, wq, wkv, wo):
    bsz, s, _ = h.shape
    m = mem_n.shape[1]
    q = (h @ wq).reshape(bsz, s, MEM_HEADS, MEM_HEAD_DIM)
    k, v = jnp.split(mem_n @ wkv, 2, axis=-1)
    k = k.reshape(bsz, m, MEM_HEADS, MEM_HEAD_DIM)
    v = v.reshape(bsz, m, MEM_HEADS, MEM_HEAD_DIM)
    sc = jnp.einsum('bshd,bmhd->bhsm', q, k) * (MEM_HEAD_DIM ** -0.5)
    p = jax.nn.softmax(sc.astype(jnp.float32), axis=-1).astype(v.dtype)
    o = jnp.einsum('bhsm,bmhd->bshd', p, v).reshape(bsz, s, D_MODEL)
    return o @ wo


def conv_ffn(h, w_up, w_conv, b_conv, w_down):
    u = causal_dwconv(h @ w_up, w_conv) + b_conv
    a, b = jnp.split(u, 2, axis=-1)
    return (jax.nn.gelu(a, approximate=True) * b) @ w_down


def setup_inputs(seed: int = 0) -> dict:
    key = jax.random.key(seed)
    ks = iter(jax.random.split(key, 40))
    nrm = lambda shape, sc: sc * jax.random.normal(next(ks), shape, jnp.float32)
    gain = lambda shape: 1.0 + nrm(shape, 0.05)
    L = DEPTH
    return {
        "x": nrm((BATCH, SEQ, D_MODEL), 1.0),
        "mem": nrm((BATCH, MEM_LEN, D_MODEL), 1.0),
        "positions": jnp.tile(jnp.arange(SEQ, dtype=jnp.int32)[None, :], (BATCH, 1)),
        "ln_mix_pre": gain((L, D_MODEL)),
        "ln_mix_post": gain((L, D_MODEL)),
        "ln_mem_q": gain((L, D_MODEL)),
        "ln_mem_kv": gain((L, D_MODEL)),
        "ln_mem_post": gain((L, D_MODEL)),
        "ln_ffn_pre": gain((L, D_MODEL)),
        "ln_ffn_post": gain((L, D_MODEL)),
        "w_in": nrm((L, D_MODEL, N_IN), D_MODEL ** -0.5),
        "sc_conv": nrm((L, SC_K, SC_W), SC_K ** -0.5),
        "rw_mu": jax.random.uniform(next(ks), (L, RW_TOTAL), jnp.float32),
        "rw_w0": nrm((L, RW_W), 1.0),
        "rw_w2": nrm((L, RW_LORA_W, RW_W), 0.1),
        "rw_a0": nrm((L, RW_W), 0.5),
        "rw_a2": nrm((L, RW_LORA_A, RW_W), 0.5 * RW_LORA_A ** -0.5),
        "rw_g2": nrm((L, RW_LORA_G, RW_W), RW_LORA_G ** -0.5),
        "rw_k_k": 0.85 + nrm((L, RW_W), 0.05),
        "rw_k_a": 1.0 + nrm((L, RW_W), 0.05),
        "rw_r_k": nrm((L, RW_HEADS, RW_HEAD_DIM), 0.1),
        "rw_lnx_g": gain((L, RW_W)),
        "rw_lnx_b": nrm((L, RW_W), 0.01),
        "nsa_cmp_pos": nrm((L, 2, CMP_LEN, NSA_DH), 0.1),
        "nsa_cmp_w1": nrm((L, 2, CMP_LEN * NSA_DH, CMP_HIDDEN), (CMP_LEN * NSA_DH) ** -0.5),
        "nsa_cmp_w2": nrm((L, 2, CMP_HIDDEN, NSA_DH), CMP_HIDDEN ** -0.5),
        "w_branch": nrm((L, N_BR, D_MODEL), SC_W ** -0.5),
        "w_out": nrm((L, D_MODEL, D_MODEL), D_MODEL ** -0.5),
        "mem_wq": nrm((L, D_MODEL, D_MODEL), D_MODEL ** -0.5),
        "mem_wkv": nrm((L, D_MODEL, 2 * D_MODEL), D_MODEL ** -0.5),
        "mem_wo": nrm((L, D_MODEL, D_MODEL), D_MODEL ** -0.5),
        "ffn_w_up": nrm((L, D_MODEL, 2 * FFN_DIM), D_MODEL ** -0.5),
        "ffn_conv_w": nrm((L, FFN_K, 2 * FFN_DIM), FFN_K ** -0.5),
        "ffn_conv_b": nrm((L, 2 * FFN_DIM), 0.01),
        "ffn_w_down": nrm((L, FFN_DIM, D_MODEL), FFN_DIM ** -0.5),
    }


def reference(x, mem, positions, ln_mix_pre, ln_mix_post, ln_mem_q, ln_mem_kv, ln_mem_post,
              ln_ffn_pre, ln_ffn_post, w_in, sc_conv, rw_mu, rw_w0, rw_w2, rw_a0, rw_a2, rw_g2,
              rw_k_k, rw_k_a, rw_r_k, rw_lnx_g, rw_lnx_b, nsa_cmp_pos, nsa_cmp_w1, nsa_cmp_w2,
              w_branch, w_out, mem_wq, mem_wkv, mem_wo, ffn_w_up, ffn_conv_w, ffn_conv_b, ffn_w_down):
    for l in range(DEPTH):
        h = rms_norm(x, ln_mix_pre[l])
        y = hybrid_mixer(h, positions, w_in[l], sc_conv[l], rw_mu[l], rw_w0[l], rw_w2[l], rw_a0[l],
                         rw_a2[l], rw_g2[l], rw_k_k[l], rw_k_a[l], rw_r_k[l], rw_lnx_g[l], rw_lnx_b[l],
                         nsa_cmp_pos[l], nsa_cmp_w1[l], nsa_cmp_w2[l], w_branch[l], w_out[l])
        x = x + rms_norm(y, ln_mix_post[l])
        h = rms_norm(x, ln_mem_q[l])
        y = memory_cross_attention(h, rms_norm(mem, ln_mem_kv[l]), mem_wq[l], mem_wkv[l], mem_wo[l])
        x = x + rms_norm(y, ln_mem_post[l])
        h = rms_norm(x, ln_ffn_pre[l])
        y = conv_ffn(h, ffn_w_up[l], ffn_conv_w[l], ffn_conv_b[l], ffn_w_down[l])
        x = x + rms_norm(y, ln_ffn_post[l])
    return x
```

```python
import functools
import math

import jax
import jax.numpy as jnp
import numpy as np
from jax import lax
from jax.experimental import pallas as pl
from jax.experimental.pallas import tpu as pltpu

D_MODEL = 2048
DEPTH = 2
MEM_HEADS = 4
MEM_HEAD_DIM = D_MODEL // MEM_HEADS
SC_W = D_MODEL // 4
RW_HEAD_DIM = 64
RW_W = D_MODEL // 4
RW_HEADS = RW_W // RW_HEAD_DIM
RW_LORA_W = 96
RW_LORA_A = 96
RW_LORA_G = 256
RW_DECAY_SCALE = math.exp(-0.5)
RW_LN_EPS = 64e-5
RET_HEADS = 4
RET_DK = 128
RET_DV = 2 * RET_DK
RET_CHUNK = 128
RET_LN_EPS = 1e-5
NSA_HEADS = 8
NSA_KV_HEADS = 2
NSA_DH = 64
NSA_GROUP = NSA_HEADS // NSA_KV_HEADS
CMP_LEN = 32
CMP_STRIDE = 16
CMP_HIDDEN = 256
SEL_BLOCK = 64
SEL_TOP = 16
WINDOW = 512
Q_BLOCK = 128
FFN_DIM = 5632
ROPE_THETA = 10000.0
EPS = 1e-6
N_BRANCH = 4

SC_SIZES = (SC_W, SC_W, SC_W)
RW_SIZES = (RW_W, RW_W, RW_W, RW_LORA_W, RW_LORA_A, RW_LORA_G)
RET_SIZES = (RET_HEADS * RET_DK, RET_HEADS * RET_DK, RET_HEADS * RET_DV, RET_HEADS * RET_DV)
NSA_SIZES = (NSA_HEADS * NSA_DH,) + (NSA_KV_HEADS * NSA_DH,) * 6 + (NSA_HEADS * 3,)
GATE_SIZES = (D_MODEL,) * N_BRANCH
GROUP_SIZES = (sum(SC_SIZES), sum(RW_SIZES), sum(RET_SIZES), sum(NSA_SIZES), sum(GATE_SIZES))
BR_SIZES = (SC_W, RW_W, RET_HEADS * RET_DV, NSA_HEADS * NSA_DH)

V7X_VMEM_LIMIT_BYTES = 48 * 1024 * 1024


def _split(x, sizes, axis=-1):
    return jnp.split(x, [int(c) for c in np.cumsum(sizes)[:-1]], axis=axis)


def _mm_kernel(a_ref, w_ref, o_ref, acc_ref):
    k = pl.program_id(2)

    @pl.when(k == 0)
    def _():
        acc_ref[...] = jnp.zeros_like(acc_ref)

    acc_ref[...] += jnp.dot(a_ref[...], w_ref[...], preferred_element_type=jnp.float32)

    @pl.when(k == pl.num_programs(2) - 1)
    def _():
        o_ref[...] = acc_ref[...]


def _pick_tile(n, candidates):
    for c in candidates:
        if n % c == 0:
            return c
    return n


def _mm(a, w):
    m, k = a.shape
    n = w.shape[1]
    a = a.astype(jnp.bfloat16)
    w = w.astype(jnp.bfloat16)
    tn = 1024 if n >= 1024 else 128 * ((n + 127) // 128)
    n_pad = tn * ((n + tn - 1) // tn)
    if n_pad != n:
        w = jnp.pad(w, ((0, 0), (0, n_pad - n)))
    tm = _pick_tile(m, (1024, 512, 256, 128, 8))
    tk = _pick_tile(k, (2048, 1280, 1408, 1024, 512, 256, 128))
    out = pl.pallas_call(
        _mm_kernel,
        out_shape=jax.ShapeDtypeStruct((m, n_pad), jnp.float32),
        grid=(m // tm, n_pad // tn, k // tk),
        in_specs=[pl.BlockSpec((tm, tk), lambda i, j, kk: (i, kk)),
                  pl.BlockSpec((tk, tn), lambda i, j, kk: (kk, j))],
        out_specs=pl.BlockSpec((tm, tn), lambda i, j, kk: (i, j)),
        scratch_shapes=[pltpu.VMEM((tm, tn), jnp.float32)],
        compiler_params=pltpu.CompilerParams(
            dimension_semantics=("parallel", "parallel", "arbitrary"),
            vmem_limit_bytes=V7X_VMEM_LIMIT_BYTES),
        name="mm",
    )(a, w)
    return out[:, :n] if n_pad != n else out


def _proj(x, w):
    lead = x.shape[:-1]
    return _mm(x.reshape(-1, x.shape[-1]), w).reshape(*lead, w.shape[1])


def _rms_norm(x, g):
    y = x * lax.rsqrt(jnp.mean(x * x, axis=-1, keepdims=True) + EPS)
    return y * g


def _head_norm(x, eps):
    mu = jnp.mean(x, axis=-1, keepdims=True)
    var = jnp.mean(jnp.square(x - mu), axis=-1, keepdims=True)
    return (x - mu) * lax.rsqrt(var + eps)


def _causal_dwconv(x, w):
    k_w = w.shape[0]
    s = x.shape[1]
    xp = jnp.pad(x, ((0, 0), (k_w - 1, 0), (0, 0)))
    y = w[0] * xp[:, 0:s]
    for j in range(1, k_w):
        y = y + w[j] * xp[:, j:j + s]
    return y


def _rope_freqs(d):
    return ROPE_THETA ** (-jnp.arange(0, d, 2, dtype=jnp.float32) / d)


def _retnet_freqs(d):
    return 1.0 / (ROPE_THETA ** jnp.linspace(0.0, 1.0, d // 2, dtype=jnp.float32))


def _rope(x, pos, inv_freq):
    ang = pos.astype(jnp.float32)[..., None] * inv_freq
    cos = jnp.cos(ang)[:, :, None, :].astype(x.dtype)
    sin = jnp.sin(ang)[:, :, None, :].astype(x.dtype)
    x1, x2 = jnp.split(x, 2, axis=-1)
    return jnp.concatenate([x1 * cos - x2 * sin, x2 * cos + x1 * sin], axis=-1)


def _masked_softmax(s, mask):
    s = jnp.where(mask, s.astype(jnp.float32), -jnp.inf)
    m = jnp.max(s, axis=-1, keepdims=True)
    m = jnp.where(jnp.isfinite(m), m, 0.0)
    e = jnp.where(mask, jnp.exp(s - m), 0.0)
    return e / jnp.maximum(jnp.sum(e, axis=-1, keepdims=True), 1e-30)


def _wkv7_scan(r, w, k, v, kk, a):
    bsz, _, nh, nd = r.shape
    xs = tuple(jnp.moveaxis(t, 1, 0) for t in (r, w, k, v, kk, a))

    def step(state, inp):
        r_t, w_t, k_t, v_t, kk_t, a_t = inp
        sa = jnp.einsum('bhvk,bhk->bhv', state, -kk_t)
        state = (state * w_t[:, :, None, :]
                 + sa[..., None] * (kk_t * a_t)[:, :, None, :]
                 + v_t[..., None] * k_t[:, :, None, :])
        return state, jnp.einsum('bhvk,bhk->bhv', state, r_t)

    s0 = jnp.zeros((bsz, nh, nd, nd), jnp.float32)
    _, y = lax.scan(step, s0, xs)
    return jnp.moveaxis(y, 0, 1)


def _rwkv7_mixer(z, mu, w0, w_w2, a0, w_a2, w_g2, k_k, k_a, r_k, lnx_g, lnx_b):
    bsz, s, _ = z.shape
    z_prev = jnp.pad(z, ((0, 0), (1, 0), (0, 0)))[:, :-1]
    z = z + (z_prev - z) * mu
    r, k, v, zw, za, zg = _split(z, RW_SIZES)
    d = w0 + _proj(jnp.tanh(zw), w_w2)
    w = jnp.exp(-RW_DECAY_SCALE * jax.nn.sigmoid(d))
    a = jax.nn.sigmoid(a0 + _proj(za, w_a2))
    g = _proj(jax.nn.sigmoid(zg), w_g2)
    heads = lambda t: t.reshape(bsz, s, RW_HEADS, RW_HEAD_DIM)
    kk = heads(k * k_k)
    kk = kk / jnp.maximum(jnp.sqrt(jnp.sum(kk * kk, axis=-1, keepdims=True)), 1e-12)
    k = k * (1.0 + (a - 1.0) * k_a)
    rh, kh, vh, ah, wh = heads(r), heads(k), heads(v), heads(a), heads(w)
    y = _wkv7_scan(rh, wh, kh, vh, kk, ah)
    y = _head_norm(y, RW_LN_EPS) * lnx_g.reshape(RW_HEADS, RW_HEAD_DIM) + lnx_b.reshape(RW_HEADS, RW_HEAD_DIM)
    bonus = jnp.sum(rh * kh * r_k, axis=-1, keepdims=True) * vh
    return (y + bonus).reshape(bsz, s, RW_W) * g


def _retention_mixer(q, k, v, gate, positions):
    bsz, s, _ = q.shape
    nh, dk, dv, c = RET_HEADS, RET_DK, RET_DV, RET_CHUNK
    n = s // c
    inv = _retnet_freqs(dk)
    q = _rope(q.reshape(bsz, s, nh, dk), positions, inv)
    k = _rope(k.reshape(bsz, s, nh, dk), positions, inv) * (dk ** -0.5)
    v = v.reshape(bsz, s, nh, dv)
    chunk = lambda t: t.reshape(bsz, n, c, nh, t.shape[-1]).transpose(0, 3, 1, 2, 4)
    qc, kc, vc = chunk(q), chunk(k), chunk(v)
    lg = jnp.log(1.0 - 2.0 ** (-5.0 - jnp.arange(nh, dtype=jnp.float32)))
    idx = jnp.arange(c, dtype=jnp.float32)
    diff = idx[:, None] - idx[None, :]
    decay_in = jnp.where(diff >= 0, jnp.exp(jnp.maximum(diff, 0.0) * lg[:, None, None]), 0.0)
    scores = jnp.einsum('bhncd,bhnmd->bhncm', qc, kc) * decay_in[:, None]
    o_inner = jnp.einsum('bhncm,bhnme->bhnce', scores, vc)
    zeta = jnp.exp((c - 1 - idx)[None, :] * lg[:, None])
    xi = jnp.exp((idx + 1.0)[None, :] * lg[:, None])
    u = jnp.einsum('bhnmd,bhnme->bhnde', kc * zeta[:, None, :, None], vc)
    chunk_decay = jnp.exp(c * lg)[:, None, None]

    def step(state, u_i):
        return state * chunk_decay + u_i, state

    _, r_prev = lax.scan(step, jnp.zeros((bsz, nh, dk, dv), u.dtype), jnp.moveaxis(u, 2, 0))
    r_prev = jnp.moveaxis(r_prev, 0, 2)
    o_cross = jnp.einsum('bhncd,bhnde->bhnce', qc * xi[:, None, :, None], r_prev)
    o = (o_inner + o_cross).transpose(0, 2, 3, 1, 4).reshape(bsz, s, nh, dv)
    o = _head_norm(o, RET_LN_EPS).reshape(bsz, s, nh * dv)
    return o * jax.nn.silu(gate)


def _nsa_mixer(q, kc, vc, ks, vs, kw, vw, gates, positions, cmp_pos, cmp_w1, cmp_w2):
    bsz, s, _ = q.shape
    nh, ng, hpg, dh = NSA_HEADS, NSA_KV_HEADS, NSA_GROUP, NSA_DH
    inv = _rope_freqs(dh)
    scale = dh ** -0.5
    q = _rope(q.reshape(bsz, s, nh, dh), positions, inv)
    kv_shape = (bsz, s, ng, dh)
    kc, vc, vs, vw = kc.reshape(kv_shape), vc.reshape(kv_shape), vs.reshape(kv_shape), vw.reshape(kv_shape)
    ks = _rope(ks.reshape(kv_shape), positions, inv)
    kw = _rope(kw.reshape(kv_shape), positions, inv)

    n_cmp = (s - CMP_LEN) // CMP_STRIDE + 1
    starts = np.arange(n_cmp) * CMP_STRIDE
    tok = starts[:, None] + np.arange(CMP_LEN)[None, :]
    cmp_end = starts + CMP_LEN - 1

    def compress(t, pos_emb, w1, w2):
        blk = t[:, tok] + pos_emb[:, None, :]
        blk = blk.transpose(0, 1, 3, 2, 4).reshape(bsz, n_cmp, ng, CMP_LEN * dh)
        return jax.nn.gelu(blk @ w1) @ w2

    k_cmp = _rope(compress(kc, cmp_pos[0], cmp_w1[0], cmp_w2[0]), positions[:, cmp_end], inv)
    v_cmp = compress(vc, cmp_pos[1], cmp_w1[1], cmp_w2[1])
    qg = q.reshape(bsz, s, ng, hpg, dh)
    t_idx = np.arange(s)
    cmask = cmp_end[None, :] <= t_idx[:, None]
    s_cmp = jnp.einsum('bsgqd,bcgd->bgqsc', qg, k_cmp) * scale
    p_cmp = _masked_softmax(s_cmp, cmask)
    o_cmp = jnp.einsum('bgqsc,bcgd->bsgqd', p_cmp.astype(v_cmp.dtype), v_cmp).reshape(bsz, s, nh, dh)

    n_sel = s // SEL_BLOCK
    jb = np.arange(n_sel)
    overlap = ((starts[:, None] < (jb[None, :] + 1) * SEL_BLOCK)
               & (starts[:, None] + CMP_LEN > jb[None, :] * SEL_BLOCK)).astype(np.float32)
    imp = jnp.einsum('bgqsc,cj->bgsj', p_cmp, overlap)
    cur = t_idx // SEL_BLOCK
    valid = jb[None, :] <= cur[:, None]
    forced = (jb[None, :] == 0) | (jb[None, :] == cur[:, None]) | (jb[None, :] == cur[:, None] - 1)
    imp = jnp.where(forced, jnp.inf, jnp.where(valid, imp, -jnp.inf))
    n_top = min(SEL_TOP, n_sel)
    _, sel_idx = lax.top_k(imp, n_top)

    qh = qg.transpose(0, 2, 3, 1, 4) * scale
    ks_blk = ks.transpose(0, 2, 1, 3).reshape(bsz, ng, n_sel, SEL_BLOCK, dh)
    vs_blk = vs.transpose(0, 2, 1, 3).reshape(bsz, ng, n_sel, SEL_BLOCK, dh)
    pad = ((0, 0), (0, 0), (WINDOW, 0), (0, 0))
    kw_pad = jnp.pad(kw.transpose(0, 2, 1, 3), pad)
    vw_pad = jnp.pad(vw.transpose(0, 2, 1, 3), pad)
    b_ix = jnp.arange(bsz)[:, None, None, None]
    g_ix = jnp.arange(ng)[None, :, None, None]

    def block(qb):
        q0 = qb * Q_BLOCK
        qblk = lax.dynamic_slice_in_dim(qh, q0, Q_BLOCK, axis=3)
        tq = q0 + jnp.arange(Q_BLOCK)
        idx = lax.dynamic_slice_in_dim(sel_idx, q0, Q_BLOCK, axis=2)
        kg = ks_blk[b_ix, g_ix, idx]
        vg = vs_blk[b_ix, g_ix, idx]
        tok_pos = idx[..., None] * SEL_BLOCK + jnp.arange(SEL_BLOCK)
        smask = (tok_pos <= tq[:, None, None]).reshape(bsz, ng, 1, Q_BLOCK, n_top * SEL_BLOCK)
        s_sel = jnp.einsum('bgqtd,bgtkld->bgqtkl', qblk, kg).reshape(bsz, ng, hpg, Q_BLOCK, n_top * SEL_BLOCK)
        p_sel = _masked_softmax(s_sel, smask).reshape(bsz, ng, hpg, Q_BLOCK, n_top, SEL_BLOCK)
        o_sel = jnp.einsum('bgqtkl,bgtkld->bgqtd', p_sel.astype(vg.dtype), vg)
        kwb = lax.dynamic_slice_in_dim(kw_pad, q0, Q_BLOCK + WINDOW, axis=2)
        vwb = lax.dynamic_slice_in_dim(vw_pad, q0, Q_BLOCK + WINDOW, axis=2)
        kpos = q0 - WINDOW + jnp.arange(Q_BLOCK + WINDOW)
        dlt = tq[:, None] - kpos[None, :]
        wmask = (dlt >= 0) & (dlt < WINDOW) & (kpos[None, :] >= 0)
        s_win = jnp.einsum('bgqtd,bgkd->bgqtk', qblk, kwb)
        p_win = _masked_softmax(s_win, wmask)
        o_win = jnp.einsum('bgqtk,bgkd->bgqtd', p_win.astype(vwb.dtype), vwb)
        return o_sel, o_win

    o_sel, o_win = lax.map(block, jnp.arange(s // Q_BLOCK))
    unblock = lambda o: o.transpose(1, 0, 4, 2, 3, 5).reshape(bsz, s, nh, dh)
    o_sel, o_win = unblock(o_sel), unblock(o_win)
    gt = jax.nn.sigmoid(gates.reshape(bsz, s, nh, 3))
    o = gt[..., 0:1] * o_cmp + gt[..., 1:2] * o_sel + gt[..., 2:3] * o_win
    return o.reshape(bsz, s, nh * dh)


def _hybrid_mixer(h, positions, w_in, sc_conv, rw_mu, rw_w0, rw_w2, rw_a0, rw_a2, rw_g2,
                  rw_k_k, rw_k_a, rw_r_k, rw_lnx_g, rw_lnx_b, nsa_cmp_pos, nsa_cmp_w1, nsa_cmp_w2,
                  w_branch, w_out):
    z = _proj(h, w_in)
    z_sc, z_rw, z_ret, z_nsa, z_gate = _split(z, GROUP_SIZES)
    b_gate, c_gate, xin = _split(z_sc, SC_SIZES)
    o_a = b_gate * _causal_dwconv(c_gate * xin, sc_conv)
    o_b = _rwkv7_mixer(z_rw, rw_mu, rw_w0, rw_w2, rw_a0, rw_a2, rw_g2, rw_k_k, rw_k_a, rw_r_k, rw_lnx_g, rw_lnx_b)
    o_c = _retention_mixer(*_split(z_ret, RET_SIZES), positions)
    o_d = _nsa_mixer(*_split(z_nsa, NSA_SIZES), positions, nsa_cmp_pos, nsa_cmp_w1, nsa_cmp_w2)
    g_a, g_b, g_c, g_d = [jax.nn.sigmoid(g) for g in _split(z_gate, GATE_SIZES)]
    w_a, w_b, w_c, w_d = _split(w_branch, BR_SIZES, axis=0)
    merged = g_a * _proj(o_a, w_a) + g_b * _proj(o_b, w_b) + g_c * _proj(o_c, w_c) + g_d * _proj(o_d, w_d)
    return _proj(merged, w_out)


def _memory_cross_attention(h, mem_n, wq, wkv, wo):
    bsz, s, _ = h.shape
    m = mem_n.shape[1]
    q = _proj(h, wq).reshape(bsz, s, MEM_HEADS, MEM_HEAD_DIM)
    k, v = jnp.split(_proj(mem_n, wkv), 2, axis=-1)
    k = k.reshape(bsz, m, MEM_HEADS, MEM_HEAD_DIM)
    v = v.reshape(bsz, m, MEM_HEADS, MEM_HEAD_DIM)
    sc = jnp.einsum('bshd,bmhd->bhsm', q, k) * (MEM_HEAD_DIM ** -0.5)
    p = jax.nn.softmax(sc, axis=-1)
    o = jnp.einsum('bhsm,bmhd->bshd', p, v).reshape(bsz, s, D_MODEL)
    return _proj(o, wo)


def _conv_ffn(h, w_up, w_conv, b_conv, w_down):
    u = _causal_dwconv(_proj(h, w_up), w_conv) + b_conv
    a, b = jnp.split(u, 2, axis=-1)
    return _proj(jax.nn.gelu(a, approximate=True) * b, w_down)


def kernel(x, mem, positions, ln_mix_pre, ln_mix_post, ln_mem_q, ln_mem_kv, ln_mem_post, ln_ffn_pre, ln_ffn_post, w_in, sc_conv, rw_mu, rw_w0, rw_w2, rw_a0, rw_a2, rw_g2, rw_k_k, rw_k_a, rw_r_k, rw_lnx_g, rw_lnx_b, nsa_cmp_pos, nsa_cmp_w1, nsa_cmp_w2, w_branch, w_out, mem_wq, mem_wkv, mem_wo, ffn_w_up, ffn_conv_w, ffn_conv_b, ffn_w_down):
    for l in range(DEPTH):
        h = _rms_norm(x, ln_mix_pre[l])
        y = _hybrid_mixer(h, positions, w_in[l], sc_conv[l], rw_mu[l], rw_w0[l], rw_w2[l], rw_a0[l],
                          rw_a2[l], rw_g2[l], rw_k_k[l], rw_k_a[l], rw_r_k[l], rw_lnx_g[l], rw_lnx_b[l],
                          nsa_cmp_pos[l], nsa_cmp_w1[l], nsa_cmp_w2[l], w_branch[l], w_out[l])
        x = x + _rms_norm(y, ln_mix_post[l])
        h = _rms_norm(x, ln_mem_q[l])
        y = _memory_cross_attention(h, _rms_norm(mem, ln_mem_kv[l]), mem_wq[l], mem_wkv[l], mem_wo[l])
        x = x + _rms_norm(y, ln_mem_post[l])
        h = _rms_norm(x, ln_ffn_pre[l])
        y = _conv_ffn(h, ffn_w_up[l], ffn_conv_w[l], ffn_conv_b[l], ffn_w_down[l])
        x = x + _rms_norm(y, ln_ffn_post[l])
    return x
```

```python
import functools
import math

import jax
import jax.numpy as jnp
import numpy as np
from jax import lax
from jax.experimental import pallas as pl
from jax.experimental.pallas import tpu as pltpu

D_MODEL = 2048
DEPTH = 2
MEM_HEADS = 4
MEM_HEAD_DIM = D_MODEL // MEM_HEADS
SC_W = D_MODEL // 4
RW_HEAD_DIM = 64
RW_W = D_MODEL // 4
RW_HEADS = RW_W // RW_HEAD_DIM
RW_LORA_W = 96
RW_LORA_A = 96
RW_LORA_G = 256
RW_DECAY_SCALE = math.exp(-0.5)
RW_LN_EPS = 64e-5
RET_HEADS = 4
RET_DK = 128
RET_DV = 2 * RET_DK
RET_CHUNK = 128
RET_LN_EPS = 1e-5
NSA_HEADS = 8
NSA_KV_HEADS = 2
NSA_DH = 64
NSA_GROUP = NSA_HEADS // NSA_KV_HEADS
CMP_LEN = 32
CMP_STRIDE = 16
CMP_HIDDEN = 256
SEL_BLOCK = 64
SEL_TOP = 16
WINDOW = 512
Q_BLOCK = 128
FFN_DIM = 5632
ROPE_THETA = 10000.0
EPS = 1e-6
N_BRANCH = 4

SC_SIZES = (SC_W, SC_W, SC_W)
RW_SIZES = (RW_W, RW_W, RW_W, RW_LORA_W, RW_LORA_A, RW_LORA_G)
RET_SIZES = (RET_HEADS * RET_DK, RET_HEADS * RET_DK, RET_HEADS * RET_DV, RET_HEADS * RET_DV)
NSA_SIZES = (NSA_HEADS * NSA_DH,) + (NSA_KV_HEADS * NSA_DH,) * 6 + (NSA_HEADS * 3,)
GATE_SIZES = (D_MODEL,) * N_BRANCH
GROUP_SIZES = (sum(SC_SIZES), sum(RW_SIZES), sum(RET_SIZES), sum(NSA_SIZES), sum(GATE_SIZES))
BR_SIZES = (SC_W, RW_W, RET_HEADS * RET_DV, NSA_HEADS * NSA_DH)

V7X_VMEM_LIMIT_BYTES = 48 * 1024 * 1024


def _split(x, sizes, axis=-1):
    return jnp.split(x, [int(c) for c in np.cumsum(sizes)[:-1]], axis=axis)


def _mm_kernel(a_ref, w_ref, o_ref, acc_ref):
    k = pl.program_id(2)

    @pl.when(k == 0)
    def _():
        acc_ref[...] = jnp.zeros_like(acc_ref)

    acc_ref[...] += jnp.dot(a_ref[...], w_ref[...], preferred_element_type=jnp.float32)

    @pl.when(k == pl.num_programs(2) - 1)
    def _():
        o_ref[...] = acc_ref[...]


def _pick_tile(n, candidates):
    for c in candidates:
        if n % c == 0:
            return c
    return n


def _mm(a, w):
    m, k = a.shape
    n = w.shape[1]
    a = a.astype(jnp.bfloat16)
    w = w.astype(jnp.bfloat16)
    tn = 1024 if n >= 1024 else 128 * ((n + 127) // 128)
    n_pad = tn * ((n + tn - 1) // tn)
    if n_pad != n:
        w = jnp.pad(w, ((0, 0), (0, n_pad - n)))
    tm = _pick_tile(m, (1024, 512, 256, 128, 8))
    tk = _pick_tile(k, (2048, 1280, 1408, 1024, 512, 256, 128))
    out = pl.pallas_call(
        _mm_kernel,
        out_shape=jax.ShapeDtypeStruct((m, n_pad), jnp.float32),
        grid=(m // tm, n_pad // tn, k // tk),
        in_specs=[pl.BlockSpec((tm, tk), lambda i, j, kk: (i, kk)),
                  pl.BlockSpec((tk, tn), lambda i, j, kk: (kk, j))],
        out_specs=pl.BlockSpec((tm, tn), lambda i, j, kk: (i, j)),
        scratch_shapes=[pltpu.VMEM((tm, tn), jnp.float32)],
        compiler_params=pltpu.CompilerParams(
            dimension_semantics=("parallel", "parallel", "arbitrary"),
            vmem_limit_bytes=V7X_VMEM_LIMIT_BYTES),
        name="mm",
    )(a, w)
    return out[:, :n] if n_pad != n else out


def _proj(x, w):
    lead = x.shape[:-1]
    return _mm(x.reshape(-1, x.shape[-1]), w).reshape(*lead, w.shape[1])


def _rms_norm(x, g):
    y = x * lax.rsqrt(jnp.mean(x * x, axis=-1, keepdims=True) + EPS)
    return y * g


def _head_norm(x, eps):
    mu = jnp.mean(x, axis=-1, keepdims=True)
    var = jnp.mean(jnp.square(x - mu), axis=-1, keepdims=True)
    return (x - mu) * lax.rsqrt(var + eps)


def _causal_dwconv(x, w):
    k_w = w.shape[0]
    s = x.shape[1]
    xp = jnp.pad(x, ((0, 0), (k_w - 1, 0), (0, 0)))
    y = w[0] * xp[:, 0:s]
    for j in range(1, k_w):
        y = y + w[j] * xp[:, j:j + s]
    return y


def _rope_freqs(d):
    return ROPE_THETA ** (-jnp.arange(0, d, 2, dtype=jnp.float32) / d)


def _retnet_freqs(d):
    return 1.0 / (ROPE_THETA ** jnp.linspace(0.0, 1.0, d // 2, dtype=jnp.float32))


def _rope(x, pos, inv_freq):
    ang = pos.astype(jnp.float32)[..., None] * inv_freq
    cos = jnp.cos(ang)[:, :, None, :].astype(x.dtype)
    sin = jnp.sin(ang)[:, :, None, :].astype(x.dtype)
    x1, x2 = jnp.split(x, 2, axis=-1)
    return jnp.concatenate([x1 * cos - x2 * sin, x2 * cos + x1 * sin], axis=-1)


def _masked_softmax(s, mask):
    s = jnp.where(mask, s.astype(jnp.float32), -jnp.inf)
    m = jnp.max(s, axis=-1, keepdims=True)
    m = jnp.where(jnp.isfinite(m), m, 0.0)
    e = jnp.where(mask, jnp.exp(s - m), 0.0)
    return e / jnp.maximum(jnp.sum(e, axis=-1, keepdims=True), 1e-30)


WKV_CHUNK = 64
_NT = (((1,), (1,)), ((), ()))


def _bdot(a, b):
    return jnp.dot(a.astype(jnp.bfloat16), b.astype(jnp.bfloat16), preferred_element_type=jnp.float32)


def _bdot_nt(a, b):
    return lax.dot_general(a.astype(jnp.bfloat16), b.astype(jnp.bfloat16), _NT,
                           preferred_element_type=jnp.float32)


def _wkv7_kernel(r_ref, lw_ref, k_ref, v_ref, kk_ref, a_ref, y_ref, s_ref):
    c = WKV_CHUNK
    hd = RW_HEAD_DIM
    f32 = jnp.float32

    @pl.when(pl.program_id(1) == 0)
    def _():
        s_ref[...] = jnp.zeros_like(s_ref)

    lw = lw_ref[0]
    row = lax.broadcasted_iota(jnp.int32, (c, c), 0)
    col = lax.broadcasted_iota(jnp.int32, (c, c), 1)
    tri = jnp.where(row >= col, 1.0, 0.0).astype(jnp.bfloat16)
    hi = lw.astype(jnp.bfloat16)
    rem = lw - hi.astype(f32)
    mid = rem.astype(jnp.bfloat16)
    lo = (rem - mid.astype(f32)).astype(jnp.bfloat16)
    cum = (jnp.dot(tri, hi, preferred_element_type=f32) + jnp.dot(tri, mid, preferred_element_type=f32)
           + jnp.dot(tri, lo, preferred_element_type=f32))
    kk = kk_ref[0]
    kka = kk * a_ref[0]
    k = k_ref[0]
    v = v_ref[0]
    e_neg = jnp.exp(-cum)
    qh = (kk * jnp.exp(cum - lw)).astype(jnp.bfloat16)
    rh = (r_ref[0] * jnp.exp(cum)).astype(jnp.bfloat16)
    bh = (kka * e_neg).astype(jnp.bfloat16)
    kh = (k * e_neg).astype(jnp.bfloat16)
    last = cum[c - 1:c, :]
    dec = jnp.exp(last - cum)
    bd = (kka * dec).astype(jnp.bfloat16)
    kd = (k * dec).astype(jnp.bfloat16)
    g_last = jnp.exp(last)

    r2 = lax.broadcasted_iota(jnp.int32, (2 * c, 2 * c), 0)
    c2 = lax.broadcasted_iota(jnp.int32, (2 * c, 2 * c), 1)
    rr = jnp.where(r2 >= c, r2 - c, r2)
    cc = jnp.where(c2 >= c, c2 - c, c2)
    tri_mask = cc < rr + jnp.where(r2 >= c, 1, 0)
    eye = jnp.where(row == col, 1.0, 0.0).astype(f32)

    for h in range(RW_HEADS):
        sl = slice(h * hd, (h + 1) * hd)
        a1 = jnp.concatenate([qh[:, sl], rh[:, sl]], axis=0)
        b1 = jnp.concatenate([bh[:, sl], kh[:, sl]], axis=0)
        ss = jnp.where(tri_mask, lax.dot_general(a1, b1, _NT, preferred_element_type=f32), 0.0)
        n = -ss[0:c, 0:c]
        t = eye + n
        for _ in range(5):
            n = _bdot(n, n)
            t = t + _bdot(t, n)
        s_old = s_ref[h]
        qr = lax.dot_general(a1, s_old.astype(jnp.bfloat16), _NT, preferred_element_type=f32)
        vh = v[:, sl]
        lm = _bdot(ss[:, c:2 * c], vh)
        u = -_bdot(t, qr[0:c] + lm[0:c])
        y = qr[c:2 * c] + lm[c:2 * c] + _bdot(ss[c:2 * c, 0:c], u)
        zt = jnp.concatenate([u, vh], axis=0).T
        x = jnp.concatenate([bd[:, sl], kd[:, sl]], axis=0)
        s_ref[h] = s_old * g_last[:, sl] + _bdot(zt, x)
        y_ref[0, :, sl] = y


def _wkv7(r, logw, k, v, kk, a):
    bsz, s, width = r.shape
    spec = pl.BlockSpec((1, WKV_CHUNK, width), lambda b, c: (b, c, 0))
    return pl.pallas_call(
        _wkv7_kernel,
        out_shape=jax.ShapeDtypeStruct((bsz, s, width), jnp.float32),
        grid=(bsz, s // WKV_CHUNK),
        in_specs=[spec] * 6,
        out_specs=spec,
        scratch_shapes=[pltpu.VMEM((RW_HEADS, RW_HEAD_DIM, RW_HEAD_DIM), jnp.float32)],
        compiler_params=pltpu.CompilerParams(dimension_semantics=("parallel", "arbitrary")),
        name="wkv7",
    )(r, logw, k, v, kk, a)


def _rwkv7_mixer(z, mu, w0, w_w2, a0, w_a2, w_g2, k_k, k_a, r_k, lnx_g, lnx_b):
    bsz, s, _ = z.shape
    z_prev = jnp.pad(z, ((0, 0), (1, 0), (0, 0)))[:, :-1]
    z = z + (z_prev - z) * mu
    r, k, v, zw, za, zg = _split(z, RW_SIZES)
    d = w0 + _proj(jnp.tanh(zw), w_w2)
    logw = -RW_DECAY_SCALE * jax.nn.sigmoid(d)
    a = jax.nn.sigmoid(a0 + _proj(za, w_a2))
    g = _proj(jax.nn.sigmoid(zg), w_g2)
    heads = lambda t: t.reshape(bsz, s, RW_HEADS, RW_HEAD_DIM)
    kk = heads(k * k_k)
    kk = kk / jnp.maximum(jnp.sqrt(jnp.sum(kk * kk, axis=-1, keepdims=True)), 1e-12)
    k = k * (1.0 + (a - 1.0) * k_a)
    rh, kh, vh = heads(r), heads(k), heads(v)
    y = heads(_wkv7(r, logw, k, v, kk.reshape(bsz, s, RW_W), a))
    y = _head_norm(y, RW_LN_EPS) * lnx_g.reshape(RW_HEADS, RW_HEAD_DIM) + lnx_b.reshape(RW_HEADS, RW_HEAD_DIM)
    bonus = jnp.sum(rh * kh * r_k, axis=-1, keepdims=True) * vh
    return (y + bonus).reshape(bsz, s, RW_W) * g


def _retention_mixer(q, k, v, gate, positions):
    bsz, s, _ = q.shape
    nh, dk, dv, c = RET_HEADS, RET_DK, RET_DV, RET_CHUNK
    n = s // c
    inv = _retnet_freqs(dk)
    q = _rope(q.reshape(bsz, s, nh, dk), positions, inv)
    k = _rope(k.reshape(bsz, s, nh, dk), positions, inv) * (dk ** -0.5)
    v = v.reshape(bsz, s, nh, dv)
    chunk = lambda t: t.reshape(bsz, n, c, nh, t.shape[-1]).transpose(0, 3, 1, 2, 4)
    qc, kc, vc = chunk(q), chunk(k), chunk(v)
    lg = jnp.log(1.0 - 2.0 ** (-5.0 - jnp.arange(nh, dtype=jnp.float32)))
    idx = jnp.arange(c, dtype=jnp.float32)
    diff = idx[:, None] - idx[None, :]
    decay_in = jnp.where(diff >= 0, jnp.exp(jnp.maximum(diff, 0.0) * lg[:, None, None]), 0.0)
    scores = jnp.einsum('bhncd,bhnmd->bhncm', qc, kc) * decay_in[:, None]
    o_inner = jnp.einsum('bhncm,bhnme->bhnce', scores, vc)
    zeta = jnp.exp((c - 1 - idx)[None, :] * lg[:, None])
    xi = jnp.exp((idx + 1.0)[None, :] * lg[:, None])
    u = jnp.einsum('bhnmd,bhnme->bhnde', kc * zeta[:, None, :, None], vc)
    chunk_decay = jnp.exp(c * lg)[:, None, None]

    def step(state, u_i):
        return state * chunk_decay + u_i, state

    _, r_prev = lax.scan(step, jnp.zeros((bsz, nh, dk, dv), u.dtype), jnp.moveaxis(u, 2, 0))
    r_prev = jnp.moveaxis(r_prev, 0, 2)
    o_cross = jnp.einsum('bhncd,bhnde->bhnce', qc * xi[:, None, :, None], r_prev)
    o = (o_inner + o_cross).transpose(0, 2, 3, 1, 4).reshape(bsz, s, nh, dv)
    o = _head_norm(o, RET_LN_EPS).reshape(bsz, s, nh * dv)
    return o * jax.nn.silu(gate)


NSA_KEY_TILE = 512
NSA_WIN_SPAN = WINDOW + Q_BLOCK
NEG_BIG = -1e30


def _nsa_attn_kernel(q_ref, ks_ref, vs_ref, kw_ref, vw_ref, sel_ref, osel_ref, owin_ref):
    f32 = jnp.float32
    qi = pl.program_id(2)
    q0 = qi * Q_BLOCK
    hpg, dh, tk = NSA_GROUP, NSA_DH, NSA_KEY_TILE
    rows = hpg * Q_BLOCK
    q = q_ref[0, 0].reshape(rows, dh).astype(jnp.bfloat16)
    sel = sel_ref[0, 0].astype(jnp.bfloat16)
    n_sel = sel.shape[1]
    t_pos = q0 + lax.broadcasted_iota(jnp.int32, (Q_BLOCK, tk), 0)
    col = lax.broadcasted_iota(jnp.int32, (Q_BLOCK, tk), 1)
    blk_row = lax.broadcasted_iota(jnp.int32, (n_sel, tk), 0)
    blk_col = lax.broadcasted_iota(jnp.int32, (n_sel, tk), 1) // SEL_BLOCK

    def sweep(kt, carry):
        m, l, acc = carry
        k0 = pl.multiple_of(kt * tk, tk)
        kt_keys = ks_ref[0, 0, pl.ds(k0, tk), :].astype(jnp.bfloat16)
        kt_vals = vs_ref[0, 0, pl.ds(k0, tk), :].astype(jnp.bfloat16)
        expand = jnp.where(blk_row == blk_col + kt * (tk // SEL_BLOCK), 1.0, 0.0).astype(jnp.bfloat16)
        picked = jnp.dot(sel, expand, preferred_element_type=f32)
        ok = (picked > 0.5) & (k0 + col <= t_pos)
        bias = jnp.where(ok, 0.0, NEG_BIG).astype(f32)
        bias = jnp.concatenate([bias] * hpg, axis=0)
        s = lax.dot_general(q, kt_keys, _NT, preferred_element_type=f32) + bias
        m_new = jnp.maximum(m, jnp.max(s, axis=-1, keepdims=True))
        alpha = jnp.exp(m - m_new)
        p = jnp.exp(s - m_new)
        l = alpha * l + jnp.sum(p, axis=-1, keepdims=True)
        acc = alpha * acc + jnp.dot(p.astype(jnp.bfloat16), kt_vals, preferred_element_type=f32)
        return m_new, l, acc

    init = (jnp.full((rows, 1), NEG_BIG, f32), jnp.zeros((rows, 1), f32), jnp.zeros((rows, dh), f32))
    n_tiles = (q0 + Q_BLOCK + tk - 1) // tk
    m, l, acc = lax.fori_loop(0, n_tiles, sweep, init)
    osel_ref[0, 0] = (acc / jnp.maximum(l, 1e-30)).reshape(hpg, Q_BLOCK, dh)

    w0 = pl.multiple_of(jnp.maximum(q0 - WINDOW, 0), Q_BLOCK)
    kwin = kw_ref[0, 0, pl.ds(w0, NSA_WIN_SPAN), :].astype(jnp.bfloat16)
    vwin = vw_ref[0, 0, pl.ds(w0, NSA_WIN_SPAN), :].astype(jnp.bfloat16)
    tq = q0 + lax.broadcasted_iota(jnp.int32, (Q_BLOCK, NSA_WIN_SPAN), 0)
    kp = w0 + lax.broadcasted_iota(jnp.int32, (Q_BLOCK, NSA_WIN_SPAN), 1)
    dlt = tq - kp
    wbias = jnp.where((dlt >= 0) & (dlt < WINDOW), 0.0, NEG_BIG).astype(f32)
    wbias = jnp.concatenate([wbias] * hpg, axis=0)
    s = lax.dot_general(q, kwin, _NT, preferred_element_type=f32) + wbias
    mw = jnp.max(s, axis=-1, keepdims=True)
    p = jnp.exp(s - mw)
    lw = jnp.sum(p, axis=-1, keepdims=True)
    ow = jnp.dot(p.astype(jnp.bfloat16), vwin, preferred_element_type=f32) / jnp.maximum(lw, 1e-30)
    owin_ref[0, 0] = ow.reshape(hpg, Q_BLOCK, dh)


def _nsa_attn(qh, ks, vs, kw, vw, sel):
    bsz, ng, hpg, s, dh = qh.shape
    n_sel = sel.shape[-1]
    kv_spec = pl.BlockSpec((1, 1, s, dh), lambda b, g, i: (b, g, 0, 0))
    q_spec = pl.BlockSpec((1, 1, hpg, Q_BLOCK, dh), lambda b, g, i: (b, g, 0, i, 0))
    out_sd = jax.ShapeDtypeStruct(qh.shape, jnp.float32)
    return pl.pallas_call(
        _nsa_attn_kernel,
        out_shape=(out_sd, out_sd),
        grid=(bsz, ng, s // Q_BLOCK),
        in_specs=[q_spec, kv_spec, kv_spec, kv_spec, kv_spec,
                  pl.BlockSpec((1, 1, Q_BLOCK, n_sel), lambda b, g, i: (b, g, i, 0))],
        out_specs=(q_spec, q_spec),
        compiler_params=pltpu.CompilerParams(
            dimension_semantics=("parallel", "parallel", "arbitrary"),
            vmem_limit_bytes=V7X_VMEM_LIMIT_BYTES),
        name="nsa_attn",
    )(qh, ks, vs, kw, vw, sel)


def _nsa_mixer(q, kc, vc, ks, vs, kw, vw, gates, positions, cmp_pos, cmp_w1, cmp_w2):
    bsz, s, _ = q.shape
    nh, ng, hpg, dh = NSA_HEADS, NSA_KV_HEADS, NSA_GROUP, NSA_DH
    inv = _rope_freqs(dh)
    scale = dh ** -0.5
    q = _rope(q.reshape(bsz, s, nh, dh), positions, inv)
    kv_shape = (bsz, s, ng, dh)
    kc, vc, vs, vw = kc.reshape(kv_shape), vc.reshape(kv_shape), vs.reshape(kv_shape), vw.reshape(kv_shape)
    ks = _rope(ks.reshape(kv_shape), positions, inv)
    kw = _rope(kw.reshape(kv_shape), positions, inv)

    n_cmp = (s - CMP_LEN) // CMP_STRIDE + 1
    starts = np.arange(n_cmp) * CMP_STRIDE
    tok = starts[:, None] + np.arange(CMP_LEN)[None, :]
    cmp_end = starts + CMP_LEN - 1

    def compress(t, pos_emb, w1, w2):
        blk = t[:, tok] + pos_emb[:, None, :]
        blk = blk.transpose(0, 1, 3, 2, 4).reshape(bsz, n_cmp, ng, CMP_LEN * dh)
        return jax.nn.gelu(blk @ w1) @ w2

    k_cmp = _rope(compress(kc, cmp_pos[0], cmp_w1[0], cmp_w2[0]), positions[:, cmp_end], inv)
    v_cmp = compress(vc, cmp_pos[1], cmp_w1[1], cmp_w2[1])
    qg = q.reshape(bsz, s, ng, hpg, dh)
    t_idx = np.arange(s)
    cmask = cmp_end[None, :] <= t_idx[:, None]
    s_cmp = jnp.einsum('bsgqd,bcgd->bgqsc', qg, k_cmp) * scale
    p_cmp = _masked_softmax(s_cmp, cmask)
    o_cmp = jnp.einsum('bgqsc,bcgd->bsgqd', p_cmp.astype(v_cmp.dtype), v_cmp).reshape(bsz, s, nh, dh)

    n_sel = s // SEL_BLOCK
    jb = np.arange(n_sel)
    overlap = ((starts[:, None] < (jb[None, :] + 1) * SEL_BLOCK)
               & (starts[:, None] + CMP_LEN > jb[None, :] * SEL_BLOCK)).astype(np.float32)
    imp = jnp.einsum('bgqsc,cj->bgsj', p_cmp, overlap)
    cur = t_idx // SEL_BLOCK
    valid = jb[None, :] <= cur[:, None]
    forced = (jb[None, :] == 0) | (jb[None, :] == cur[:, None]) | (jb[None, :] == cur[:, None] - 1)
    imp = jnp.where(forced, jnp.inf, jnp.where(valid, imp, -jnp.inf))
    n_top = min(SEL_TOP, n_sel)
    _, sel_idx = lax.top_k(imp, n_top)

    sel = jnp.any(sel_idx[..., :, None] == jnp.arange(n_sel, dtype=sel_idx.dtype), axis=-2)
    qh = qg.transpose(0, 2, 3, 1, 4) * scale
    to_group = lambda t: t.transpose(0, 2, 1, 3)
    o_sel, o_win = _nsa_attn(qh, to_group(ks), to_group(vs), to_group(kw), to_group(vw),
                             sel.astype(jnp.float32))
    ungroup = lambda o: o.transpose(0, 3, 1, 2, 4).reshape(bsz, s, nh, dh)
    o_sel, o_win = ungroup(o_sel), ungroup(o_win)
    gt = jax.nn.sigmoid(gates.reshape(bsz, s, nh, 3))
    o = gt[..., 0:1] * o_cmp + gt[..., 1:2] * o_sel + gt[..., 2:3] * o_win
    return o.reshape(bsz, s, nh * dh)


def _hybrid_mixer(h, positions, w_in, sc_conv, rw_mu, rw_w0, rw_w2, rw_a0, rw_a2, rw_g2,
                  rw_k_k, rw_k_a, rw_r_k, rw_lnx_g, rw_lnx_b, nsa_cmp_pos, nsa_cmp_w1, nsa_cmp_w2,
                  w_branch, w_out):
    z = _proj(h, w_in)
    z_sc, z_rw, z_ret, z_nsa, z_gate = _split(z, GROUP_SIZES)
    b_gate, c_gate, xin = _split(z_sc, SC_SIZES)
    o_a = b_gate * _causal_dwconv(c_gate * xin, sc_conv)
    o_b = _rwkv7_mixer(z_rw, rw_mu, rw_w0, rw_w2, rw_a0, rw_a2, rw_g2, rw_k_k, rw_k_a, rw_r_k, rw_lnx_g, rw_lnx_b)
    o_c = _retention_mixer(*_split(z_ret, RET_SIZES), positions)
    o_d = _nsa_mixer(*_split(z_nsa, NSA_SIZES), positions, nsa_cmp_pos, nsa_cmp_w1, nsa_cmp_w2)
    g_a, g_b, g_c, g_d = [jax.nn.sigmoid(g) for g in _split(z_gate, GATE_SIZES)]
    w_a, w_b, w_c, w_d = _split(w_branch, BR_SIZES, axis=0)
    merged = g_a * _proj(o_a, w_a) + g_b * _proj(o_b, w_b) + g_c * _proj(o_c, w_c) + g_d * _proj(o_d, w_d)
    return _proj(merged, w_out)


def _memory_cross_attention(h, mem_n, wq, wkv, wo):
    bsz, s, _ = h.shape
    m = mem_n.shape[1]
    q = _proj(h, wq).reshape(bsz, s, MEM_HEADS, MEM_HEAD_DIM)
    k, v = jnp.split(_proj(mem_n, wkv), 2, axis=-1)
    k = k.reshape(bsz, m, MEM_HEADS, MEM_HEAD_DIM)
    v = v.reshape(bsz, m, MEM_HEADS, MEM_HEAD_DIM)
    sc = jnp.einsum('bshd,bmhd->bhsm', q, k) * (MEM_HEAD_DIM ** -0.5)
    p = jax.nn.softmax(sc, axis=-1)
    o = jnp.einsum('bhsm,bmhd->bshd', p, v).reshape(bsz, s, D_MODEL)
    return _proj(o, wo)


def _conv_ffn(h, w_up, w_conv, b_conv, w_down):
    u = _causal_dwconv(_proj(h, w_up), w_conv) + b_conv
    a, b = jnp.split(u, 2, axis=-1)
    return _proj(jax.nn.gelu(a, approximate=True) * b, w_down)


def kernel(x, mem, positions, ln_mix_pre, ln_mix_post, ln_mem_q, ln_mem_kv, ln_mem_post, ln_ffn_pre, ln_ffn_post, w_in, sc_conv, rw_mu, rw_w0, rw_w2, rw_a0, rw_a2, rw_g2, rw_k_k, rw_k_a, rw_r_k, rw_lnx_g, rw_lnx_b, nsa_cmp_pos, nsa_cmp_w1, nsa_cmp_w2, w_branch, w_out, mem_wq, mem_wkv, mem_wo, ffn_w_up, ffn_conv_w, ffn_conv_b, ffn_w_down):
    for l in range(DEPTH):
        h = _rms_norm(x, ln_mix_pre[l])
        y = _hybrid_mixer(h, positions, w_in[l], sc_conv[l], rw_mu[l], rw_w0[l], rw_w2[l], rw_a0[l],
                          rw_a2[l], rw_g2[l], rw_k_k[l], rw_k_a[l], rw_r_k[l], rw_lnx_g[l], rw_lnx_b[l],
                          nsa_cmp_pos[l], nsa_cmp_w1[l], nsa_cmp_w2[l], w_branch[l], w_out[l])
        x = x + _rms_norm(y, ln_mix_post[l])
        h = _rms_norm(x, ln_mem_q[l])
        y = _memory_cross_attention(h, _rms_norm(mem, ln_mem_kv[l]), mem_wq[l], mem_wkv[l], mem_wo[l])
        x = x + _rms_norm(y, ln_mem_post[l])
        h = _rms_norm(x, ln_ffn_pre[l])
        y = _conv_ffn(h, ffn_w_up[l], ffn_conv_w[l], ffn_conv_b[l], ffn_w_down[l])
        x = x + _rms_norm(y, ln_ffn_post[l])
    return x
```

```python
import functools
import math

import jax
import jax.numpy as jnp
import numpy as np
from jax import lax
from jax.experimental import pallas as pl
from jax.experimental.pallas import tpu as pltpu

D_MODEL = 2048
DEPTH = 2
MEM_HEADS = 4
MEM_HEAD_DIM = D_MODEL // MEM_HEADS
SC_W = D_MODEL // 4
RW_HEAD_DIM = 64
RW_W = D_MODEL // 4
RW_HEADS = RW_W // RW_HEAD_DIM
RW_LORA_W = 96
RW_LORA_A = 96
RW_LORA_G = 256
RW_DECAY_SCALE = math.exp(-0.5)
RW_LN_EPS = 64e-5
RET_HEADS = 4
RET_DK = 128
RET_DV = 2 * RET_DK
RET_CHUNK = 128
RET_LN_EPS = 1e-5
NSA_HEADS = 8
NSA_KV_HEADS = 2
NSA_DH = 64
NSA_GROUP = NSA_HEADS // NSA_KV_HEADS
CMP_LEN = 32
CMP_STRIDE = 16
CMP_HIDDEN = 256
SEL_BLOCK = 64
SEL_TOP = 16
WINDOW = 512
Q_BLOCK = 128
FFN_DIM = 5632
ROPE_THETA = 10000.0
EPS = 1e-6
N_BRANCH = 4

SC_SIZES = (SC_W, SC_W, SC_W)
RW_SIZES = (RW_W, RW_W, RW_W, RW_LORA_W, RW_LORA_A, RW_LORA_G)
RET_SIZES = (RET_HEADS * RET_DK, RET_HEADS * RET_DK, RET_HEADS * RET_DV, RET_HEADS * RET_DV)
NSA_SIZES = (NSA_HEADS * NSA_DH,) + (NSA_KV_HEADS * NSA_DH,) * 6 + (NSA_HEADS * 3,)
GATE_SIZES = (D_MODEL,) * N_BRANCH
GROUP_SIZES = (sum(SC_SIZES), sum(RW_SIZES), sum(RET_SIZES), sum(NSA_SIZES), sum(GATE_SIZES))
BR_SIZES = (SC_W, RW_W, RET_HEADS * RET_DV, NSA_HEADS * NSA_DH)

V7X_VMEM_LIMIT_BYTES = 48 * 1024 * 1024


def _split(x, sizes, axis=-1):
    return jnp.split(x, [int(c) for c in np.cumsum(sizes)[:-1]], axis=axis)


def _mm_kernel(a_ref, w_ref, o_ref, acc_ref):
    k = pl.program_id(2)

    @pl.when(k == 0)
    def _():
        acc_ref[...] = jnp.zeros_like(acc_ref)

    acc_ref[...] += jnp.dot(a_ref[...], w_ref[...], preferred_element_type=jnp.float32)

    @pl.when(k == pl.num_programs(2) - 1)
    def _():
        o_ref[...] = acc_ref[...]


def _pick_tile(n, candidates):
    for c in candidates:
        if n % c == 0:
            return c
    return n


LANES = 128


def _mm(a, w, keep_pad=False):
    m, k = a.shape
    n = w.shape[1]
    a = a.astype(jnp.bfloat16)
    w = w.astype(jnp.bfloat16)
    n_pad = LANES * ((n + LANES - 1) // LANES)
    tn = n_pad if n_pad <= 1536 else _pick_tile(n_pad, (1024, 768, 512, 256, 128))
    if n_pad != n:
        w = jnp.pad(w, ((0, 0), (0, n_pad - n)))
    tm = _pick_tile(m, (1024, 512, 256, 128, 8))
    tk = _pick_tile(k, (2048, 1280, 1408, 1024, 512, 256, 128))
    out = pl.pallas_call(
        _mm_kernel,
        out_shape=jax.ShapeDtypeStruct((m, n_pad), jnp.float32),
        grid=(m // tm, n_pad // tn, k // tk),
        in_specs=[pl.BlockSpec((tm, tk), lambda i, j, kk: (i, kk)),
                  pl.BlockSpec((tk, tn), lambda i, j, kk: (kk, j))],
        out_specs=pl.BlockSpec((tm, tn), lambda i, j, kk: (i, j)),
        scratch_shapes=[pltpu.VMEM((tm, tn), jnp.float32)],
        compiler_params=pltpu.CompilerParams(
            dimension_semantics=("parallel", "parallel", "arbitrary"),
            vmem_limit_bytes=V7X_VMEM_LIMIT_BYTES),
        name="mm",
    )(a, w)
    return out if keep_pad or n_pad == n else out[:, :n]


def _proj(x, w):
    lead = x.shape[:-1]
    return _mm(x.reshape(-1, x.shape[-1]), w).reshape(*lead, w.shape[1])


def _rms(y, g):
    return y * lax.rsqrt(jnp.mean(y * y, axis=-1, keepdims=True) + EPS) * g


def _prenorm_kernel(x_ref, g_ref, h_ref):
    h_ref[...] = _rms(x_ref[...], g_ref[...]).astype(h_ref.dtype)


def _prenorm(x, g):
    m, d = x.shape
    tm = _pick_tile(m, (512, 256, 128, 8))
    return pl.pallas_call(
        _prenorm_kernel,
        out_shape=jax.ShapeDtypeStruct((m, d), jnp.bfloat16),
        grid=(m // tm,),
        in_specs=[pl.BlockSpec((tm, d), lambda i: (i, 0)), pl.BlockSpec((1, d), lambda i: (0, 0))],
        out_specs=pl.BlockSpec((tm, d), lambda i: (i, 0)),
        compiler_params=pltpu.CompilerParams(dimension_semantics=("parallel",)),
        name="prenorm",
    )(x, g.reshape(1, d))


def _mm_norm_res_kernel(emit_h, a_ref, w_ref, res_ref, g_ref, *rest):
    if emit_h:
        g2_ref, x_ref, h_ref, acc_ref = rest
    else:
        x_ref, acc_ref = rest
    k = pl.program_id(1)

    @pl.when(k == 0)
    def _():
        acc_ref[...] = jnp.zeros_like(acc_ref)

    acc_ref[...] += jnp.dot(a_ref[...], w_ref[...], preferred_element_type=jnp.float32)

    @pl.when(k == pl.num_programs(1) - 1)
    def _():
        x_new = res_ref[...] + _rms(acc_ref[...], g_ref[...])
        x_ref[...] = x_new
        if emit_h:
            h_ref[...] = _rms(x_new, g2_ref[...]).astype(h_ref.dtype)


def _mm_norm_res(a, w, res, g, g_next):
    m, k = a.shape
    d = w.shape[1]
    tk = _pick_tile(k, (2048, 1408, 1024, 512))
    tm = 256 if tk == k else 512
    emit_h = g_next is not None
    row = pl.BlockSpec((tm, d), lambda i, kk: (i, 0))
    vec = pl.BlockSpec((1, d), lambda i, kk: (0, 0))
    x_sd = jax.ShapeDtypeStruct((m, d), jnp.float32)
    h_sd = jax.ShapeDtypeStruct((m, d), jnp.bfloat16)
    out = pl.pallas_call(
        functools.partial(_mm_norm_res_kernel, emit_h),
        out_shape=(x_sd, h_sd) if emit_h else x_sd,
        grid=(m // tm, k // tk),
        in_specs=[pl.BlockSpec((tm, tk), lambda i, kk: (i, kk)),
                  pl.BlockSpec((tk, d), lambda i, kk: (kk, 0)),
                  row, vec] + ([vec] if emit_h else []),
        out_specs=(row, row) if emit_h else row,
        scratch_shapes=[pltpu.VMEM((tm, d), jnp.float32)],
        compiler_params=pltpu.CompilerParams(
            dimension_semantics=("parallel", "arbitrary"), vmem_limit_bytes=V7X_VMEM_LIMIT_BYTES),
        name="mm_norm_res",
    )(a.astype(jnp.bfloat16), w.astype(jnp.bfloat16), res, g.reshape(1, d),
      *([g_next.reshape(1, d)] if emit_h else []))
    return out if emit_h else (out, None)


FFN_TM = 1024
FFN_TN = 512
HALO = 8


def _shift_rows(u, prev, n):
    rolled = pltpu.roll(u, n, axis=0)
    rows = lax.broadcasted_iota(jnp.int32, u.shape, 0)
    for r in range(n):
        rolled = jnp.where(rows == r, prev[HALO - n + r:HALO - n + r + 1, :], rolled)
    return rolled


def _ffn_up_kernel(seq_len, h_ref, halo_ref, wa_ref, wb_ref, cwa_ref, cwb_ref, ba_ref, bb_ref, o_ref):
    i = pl.program_id(0)
    h = h_ref[...]
    at_start = (i * FFN_TM) % seq_len == 0
    halo = jnp.where(at_start, jnp.zeros_like(halo_ref[...]), halo_ref[...])

    def conv(w_ref, cw_ref, b_ref):
        u = jnp.dot(h, w_ref[...], preferred_element_type=jnp.float32)
        up = jnp.dot(halo, w_ref[...], preferred_element_type=jnp.float32)
        cw = cw_ref[...]
        y = cw[0:1] * _shift_rows(u, up, 2)
        y = y + cw[1:2] * _shift_rows(u, up, 1)
        y = y + cw[2:3] * u
        return y + b_ref[...]

    a = conv(wa_ref, cwa_ref, ba_ref)
    b = conv(wb_ref, cwb_ref, bb_ref)
    o_ref[...] = (jax.nn.gelu(a, approximate=True) * b).astype(o_ref.dtype)


def _ffn_up(h, w_up, w_conv, b_conv, seq_len):
    m, d = h.shape
    f = w_up.shape[1] // 2
    nj = f // FFN_TN
    w_up = w_up.astype(jnp.bfloat16)
    b_conv = b_conv.reshape(1, 2 * f)
    tiles_per_halo = FFN_TM // HALO
    return pl.pallas_call(
        functools.partial(_ffn_up_kernel, seq_len),
        out_shape=jax.ShapeDtypeStruct((m, f), jnp.bfloat16),
        grid=(m // FFN_TM, nj),
        in_specs=[pl.BlockSpec((FFN_TM, d), lambda i, j: (i, 0)),
                  pl.BlockSpec((HALO, d), lambda i, j: (jnp.maximum(i * tiles_per_halo - 1, 0), 0)),
                  pl.BlockSpec((d, FFN_TN), lambda i, j: (0, j)),
                  pl.BlockSpec((d, FFN_TN), lambda i, j: (0, j + nj)),
                  pl.BlockSpec((3, FFN_TN), lambda i, j: (0, j)),
                  pl.BlockSpec((3, FFN_TN), lambda i, j: (0, j + nj)),
                  pl.BlockSpec((1, FFN_TN), lambda i, j: (0, j)),
                  pl.BlockSpec((1, FFN_TN), lambda i, j: (0, j + nj))],
        out_specs=pl.BlockSpec((FFN_TM, FFN_TN), lambda i, j: (i, j)),
        compiler_params=pltpu.CompilerParams(
            dimension_semantics=("parallel", "arbitrary"), vmem_limit_bytes=V7X_VMEM_LIMIT_BYTES),
        name="ffn_up",
    )(h, h, w_up, w_up, w_conv, w_conv, b_conv, b_conv)


MERGE_TM = 512
MERGE_TN = 512


def _merge_kernel(h_ref, wg0, wg1, wg2, wg3, o0, o1, o2, o3, wb0, wb1, wb2, wb3, out_ref):
    h = h_ref[...]
    acc = None
    for wg, o, wb in ((wg0, o0, wb0), (wg1, o1, wb1), (wg2, o2, wb2), (wg3, o3, wb3)):
        gate = jax.nn.sigmoid(jnp.dot(h, wg[...], preferred_element_type=jnp.float32))
        term = gate * jnp.dot(o[...], wb[...], preferred_element_type=jnp.float32)
        acc = term if acc is None else acc + term
    out_ref[...] = acc.astype(out_ref.dtype)


def _merge(h, w_gate, outs, w_branches):
    m, d = h.shape
    nj = d // MERGE_TN
    gate_specs = [pl.BlockSpec((d, MERGE_TN), functools.partial(lambda b, i, j: (0, b * nj + j), b))
                  for b in range(N_BRANCH)]
    o_specs = [pl.BlockSpec((MERGE_TM, o.shape[1]), lambda i, j: (i, 0)) for o in outs]
    wb_specs = [pl.BlockSpec((wb.shape[0], MERGE_TN), lambda i, j: (0, j)) for wb in w_branches]
    return pl.pallas_call(
        _merge_kernel,
        out_shape=jax.ShapeDtypeStruct((m, d), jnp.bfloat16),
        grid=(m // MERGE_TM, nj),
        in_specs=[pl.BlockSpec((MERGE_TM, d), lambda i, j: (i, 0))] + gate_specs + o_specs + wb_specs,
        out_specs=pl.BlockSpec((MERGE_TM, MERGE_TN), lambda i, j: (i, j)),
        compiler_params=pltpu.CompilerParams(
            dimension_semantics=("parallel", "arbitrary"), vmem_limit_bytes=V7X_VMEM_LIMIT_BYTES),
        name="merge",
    )(h, *([w_gate] * N_BRANCH), *[o.astype(jnp.bfloat16) for o in outs],
      *[wb.astype(jnp.bfloat16) for wb in w_branches])


XATT_TM = 512


def _xattn_kernel(h_ref, wq_ref, k_ref, v_ref, o_ref):
    q = jnp.dot(h_ref[...], wq_ref[...], preferred_element_type=jnp.float32)
    q = (q * (MEM_HEAD_DIM ** -0.5)).astype(jnp.bfloat16)
    for hh in range(MEM_HEADS):
        sl = slice(hh * MEM_HEAD_DIM, (hh + 1) * MEM_HEAD_DIM)
        s = lax.dot_general(q[:, sl], k_ref[0, :, sl], _NT, preferred_element_type=jnp.float32)
        p = jnp.exp(s - jnp.max(s, axis=-1, keepdims=True))
        p = p / jnp.sum(p, axis=-1, keepdims=True)
        o_ref[:, sl] = jnp.dot(p.astype(jnp.bfloat16), v_ref[0, :, sl],
                               preferred_element_type=jnp.float32).astype(o_ref.dtype)


def _xattn(h, wq, k, v, seq_len):
    m, d = h.shape
    mem_len = k.shape[1]
    per_batch = seq_len // XATT_TM
    kv_spec = pl.BlockSpec((1, mem_len, d), lambda i: (i // per_batch, 0, 0))
    return pl.pallas_call(
        _xattn_kernel,
        out_shape=jax.ShapeDtypeStruct((m, d), jnp.bfloat16),
        grid=(m // XATT_TM,),
        in_specs=[pl.BlockSpec((XATT_TM, d), lambda i: (i, 0)),
                  pl.BlockSpec((d, d), lambda i: (0, 0)), kv_spec, kv_spec],
        out_specs=pl.BlockSpec((XATT_TM, d), lambda i: (i, 0)),
        compiler_params=pltpu.CompilerParams(
            dimension_semantics=("parallel",), vmem_limit_bytes=V7X_VMEM_LIMIT_BYTES),
        name="xattn",
    )(h, wq.astype(jnp.bfloat16), k, v)


def _rms_norm(x, g):
    y = x * lax.rsqrt(jnp.mean(x * x, axis=-1, keepdims=True) + EPS)
    return y * g


def _head_norm(x, eps):
    mu = jnp.mean(x, axis=-1, keepdims=True)
    var = jnp.mean(jnp.square(x - mu), axis=-1, keepdims=True)
    return (x - mu) * lax.rsqrt(var + eps)


def _causal_dwconv(x, w):
    k_w = w.shape[0]
    s = x.shape[1]
    xp = jnp.pad(x, ((0, 0), (k_w - 1, 0), (0, 0)))
    y = w[0] * xp[:, 0:s]
    for j in range(1, k_w):
        y = y + w[j] * xp[:, j:j + s]
    return y


def _rope_freqs(d):
    return ROPE_THETA ** (-jnp.arange(0, d, 2, dtype=jnp.float32) / d)


def _retnet_freqs(d):
    return 1.0 / (ROPE_THETA ** jnp.linspace(0.0, 1.0, d // 2, dtype=jnp.float32))


def _rope(x, pos, inv_freq):
    ang = pos.astype(jnp.float32)[..., None] * inv_freq
    cos = jnp.cos(ang)[:, :, None, :].astype(x.dtype)
    sin = jnp.sin(ang)[:, :, None, :].astype(x.dtype)
    x1, x2 = jnp.split(x, 2, axis=-1)
    return jnp.concatenate([x1 * cos - x2 * sin, x2 * cos + x1 * sin], axis=-1)


def _masked_softmax(s, mask):
    s = jnp.where(mask, s.astype(jnp.float32), -jnp.inf)
    m = jnp.max(s, axis=-1, keepdims=True)
    m = jnp.where(jnp.isfinite(m), m, 0.0)
    e = jnp.where(mask, jnp.exp(s - m), 0.0)
    return e / jnp.maximum(jnp.sum(e, axis=-1, keepdims=True), 1e-30)


WKV_CHUNK = 64
_NT = (((1,), (1,)), ((), ()))


def _bdot(a, b):
    return jnp.dot(a.astype(jnp.bfloat16), b.astype(jnp.bfloat16), preferred_element_type=jnp.float32)


def _bdot_nt(a, b):
    return lax.dot_general(a.astype(jnp.bfloat16), b.astype(jnp.bfloat16), _NT,
                           preferred_element_type=jnp.float32)


def _wkv7_kernel(r_ref, lw_ref, k_ref, v_ref, kk_ref, a_ref, y_ref, s_ref):
    c = WKV_CHUNK
    hd = RW_HEAD_DIM
    f32 = jnp.float32

    @pl.when(pl.program_id(1) == 0)
    def _():
        s_ref[...] = jnp.zeros_like(s_ref)

    lw = lw_ref[0]
    row = lax.broadcasted_iota(jnp.int32, (c, c), 0)
    col = lax.broadcasted_iota(jnp.int32, (c, c), 1)
    tri = jnp.where(row >= col, 1.0, 0.0).astype(jnp.bfloat16)
    hi = lw.astype(jnp.bfloat16)
    rem = lw - hi.astype(f32)
    mid = rem.astype(jnp.bfloat16)
    lo = (rem - mid.astype(f32)).astype(jnp.bfloat16)
    cum = (jnp.dot(tri, hi, preferred_element_type=f32) + jnp.dot(tri, mid, preferred_element_type=f32)
           + jnp.dot(tri, lo, preferred_element_type=f32))
    kk = kk_ref[0]
    kka = kk * a_ref[0]
    k = k_ref[0]
    v = v_ref[0]
    e_neg = jnp.exp(-cum)
    qh = (kk * jnp.exp(cum - lw)).astype(jnp.bfloat16)
    rh = (r_ref[0] * jnp.exp(cum)).astype(jnp.bfloat16)
    bh = (kka * e_neg).astype(jnp.bfloat16)
    kh = (k * e_neg).astype(jnp.bfloat16)
    last = cum[c - 1:c, :]
    dec = jnp.exp(last - cum)
    bd = (kka * dec).astype(jnp.bfloat16)
    kd = (k * dec).astype(jnp.bfloat16)
    g_last = jnp.exp(last)

    r2 = lax.broadcasted_iota(jnp.int32, (2 * c, 2 * c), 0)
    c2 = lax.broadcasted_iota(jnp.int32, (2 * c, 2 * c), 1)
    rr = jnp.where(r2 >= c, r2 - c, r2)
    cc = jnp.where(c2 >= c, c2 - c, c2)
    tri_mask = cc < rr + jnp.where(r2 >= c, 1, 0)
    eye = jnp.where(row == col, 1.0, 0.0).astype(f32)

    for h in range(RW_HEADS):
        sl = slice(h * hd, (h + 1) * hd)
        a1 = jnp.concatenate([qh[:, sl], rh[:, sl]], axis=0)
        b1 = jnp.concatenate([bh[:, sl], kh[:, sl]], axis=0)
        ss = jnp.where(tri_mask, lax.dot_general(a1, b1, _NT, preferred_element_type=f32), 0.0)
        n = -ss[0:c, 0:c]
        t = eye + n
        for _ in range(5):
            n = _bdot(n, n)
            t = t + _bdot(t, n)
        s_old = s_ref[h]
        qr = lax.dot_general(a1, s_old.astype(jnp.bfloat16), _NT, preferred_element_type=f32)
        vh = v[:, sl]
        lm = _bdot(ss[:, c:2 * c], vh)
        u = -_bdot(t, qr[0:c] + lm[0:c])
        y = qr[c:2 * c] + lm[c:2 * c] + _bdot(ss[c:2 * c, 0:c], u)
        zt = jnp.concatenate([u, vh], axis=0).T
        x = jnp.concatenate([bd[:, sl], kd[:, sl]], axis=0)
        s_ref[h] = s_old * g_last[:, sl] + _bdot(zt, x)
        y_ref[0, :, sl] = y


def _wkv7(r, logw, k, v, kk, a):
    bsz, s, width = r.shape
    spec = pl.BlockSpec((1, WKV_CHUNK, width), lambda b, c: (b, c, 0))
    return pl.pallas_call(
        _wkv7_kernel,
        out_shape=jax.ShapeDtypeStruct((bsz, s, width), jnp.float32),
        grid=(bsz, s // WKV_CHUNK),
        in_specs=[spec] * 6,
        out_specs=spec,
        scratch_shapes=[pltpu.VMEM((RW_HEADS, RW_HEAD_DIM, RW_HEAD_DIM), jnp.float32)],
        compiler_params=pltpu.CompilerParams(dimension_semantics=("parallel", "arbitrary")),
        name="wkv7",
    )(r, logw, k, v, kk, a)


def _rwkv7_mixer(z, mu, w0, w_w2, a0, w_a2, w_g2, k_k, k_a, r_k, lnx_g, lnx_b):
    bsz, s, _ = z.shape
    z_prev = jnp.pad(z, ((0, 0), (1, 0), (0, 0)))[:, :-1]
    z = z + (z_prev - z) * mu
    r, k, v, zw, za, zg = _split(z, RW_SIZES)
    d = w0 + _proj(jnp.tanh(zw), w_w2)
    logw = -RW_DECAY_SCALE * jax.nn.sigmoid(d)
    a = jax.nn.sigmoid(a0 + _proj(za, w_a2))
    g = _proj(jax.nn.sigmoid(zg), w_g2)
    heads = lambda t: t.reshape(bsz, s, RW_HEADS, RW_HEAD_DIM)
    kk = heads(k * k_k)
    kk = kk / jnp.maximum(jnp.sqrt(jnp.sum(kk * kk, axis=-1, keepdims=True)), 1e-12)
    k = k * (1.0 + (a - 1.0) * k_a)
    rh, kh, vh = heads(r), heads(k), heads(v)
    y = heads(_wkv7(r, logw, k, v, kk.reshape(bsz, s, RW_W), a))
    y = _head_norm(y, RW_LN_EPS) * lnx_g.reshape(RW_HEADS, RW_HEAD_DIM) + lnx_b.reshape(RW_HEADS, RW_HEAD_DIM)
    bonus = jnp.sum(rh * kh * r_k, axis=-1, keepdims=True) * vh
    return (y + bonus).reshape(bsz, s, RW_W) * g


def _retention_mixer(q, k, v, gate, positions):
    bsz, s, _ = q.shape
    nh, dk, dv, c = RET_HEADS, RET_DK, RET_DV, RET_CHUNK
    n = s // c
    inv = _retnet_freqs(dk)
    q = _rope(q.reshape(bsz, s, nh, dk), positions, inv)
    k = _rope(k.reshape(bsz, s, nh, dk), positions, inv) * (dk ** -0.5)
    v = v.reshape(bsz, s, nh, dv)
    chunk = lambda t: t.reshape(bsz, n, c, nh, t.shape[-1]).transpose(0, 3, 1, 2, 4)
    qc, kc, vc = chunk(q), chunk(k), chunk(v)
    lg = jnp.log(1.0 - 2.0 ** (-5.0 - jnp.arange(nh, dtype=jnp.float32)))
    idx = jnp.arange(c, dtype=jnp.float32)
    diff = idx[:, None] - idx[None, :]
    decay_in = jnp.where(diff >= 0, jnp.exp(jnp.maximum(diff, 0.0) * lg[:, None, None]), 0.0)
    scores = jnp.einsum('bhncd,bhnmd->bhncm', qc, kc) * decay_in[:, None]
    o_inner = jnp.einsum('bhncm,bhnme->bhnce', scores, vc)
    zeta = jnp.exp((c - 1 - idx)[None, :] * lg[:, None])
    xi = jnp.exp((idx + 1.0)[None, :] * lg[:, None])
    u = jnp.einsum('bhnmd,bhnme->bhnde', kc * zeta[:, None, :, None], vc)
    chunk_decay = jnp.exp(c * lg)[:, None, None]

    def step(state, u_i):
        return state * chunk_decay + u_i, state

    _, r_prev = lax.scan(step, jnp.zeros((bsz, nh, dk, dv), u.dtype), jnp.moveaxis(u, 2, 0))
    r_prev = jnp.moveaxis(r_prev, 0, 2)
    o_cross = jnp.einsum('bhncd,bhnde->bhnce', qc * xi[:, None, :, None], r_prev)
    o = (o_inner + o_cross).transpose(0, 2, 3, 1, 4).reshape(bsz, s, nh, dv)
    o = _head_norm(o, RET_LN_EPS).reshape(bsz, s, nh * dv)
    return o * jax.nn.silu(gate)


NSA_KEY_TILE = 512
NSA_WIN_SPAN = WINDOW + Q_BLOCK
NEG_BIG = -1e30


def _nsa_attn_kernel(q_ref, ks_ref, vs_ref, kw_ref, vw_ref, sel_ref, osel_ref, owin_ref):
    f32 = jnp.float32
    qi = pl.program_id(2)
    q0 = qi * Q_BLOCK
    hpg, dh, tk = NSA_GROUP, NSA_DH, NSA_KEY_TILE
    rows = hpg * Q_BLOCK
    q = q_ref[0, 0].reshape(rows, dh).astype(jnp.bfloat16)
    sel = sel_ref[0, 0].astype(jnp.bfloat16)
    n_sel = sel.shape[1]
    t_pos = q0 + lax.broadcasted_iota(jnp.int32, (Q_BLOCK, tk), 0)
    col = lax.broadcasted_iota(jnp.int32, (Q_BLOCK, tk), 1)
    blk_row = lax.broadcasted_iota(jnp.int32, (n_sel, tk), 0)
    blk_col = lax.broadcasted_iota(jnp.int32, (n_sel, tk), 1) // SEL_BLOCK

    def sweep(kt, carry):
        m, l, acc = carry
        k0 = pl.multiple_of(kt * tk, tk)
        kt_keys = ks_ref[0, 0, pl.ds(k0, tk), :].astype(jnp.bfloat16)
        kt_vals = vs_ref[0, 0, pl.ds(k0, tk), :].astype(jnp.bfloat16)
        expand = jnp.where(blk_row == blk_col + kt * (tk // SEL_BLOCK), 1.0, 0.0).astype(jnp.bfloat16)
        picked = jnp.dot(sel, expand, preferred_element_type=f32)
        ok = (picked > 0.5) & (k0 + col <= t_pos)
        bias = jnp.where(ok, 0.0, NEG_BIG).astype(f32)
        bias = jnp.concatenate([bias] * hpg, axis=0)
        s = lax.dot_general(q, kt_keys, _NT, preferred_element_type=f32) + bias
        m_new = jnp.maximum(m, jnp.max(s, axis=-1, keepdims=True))
        alpha = jnp.exp(m - m_new)
        p = jnp.exp(s - m_new)
        l = alpha * l + jnp.sum(p, axis=-1, keepdims=True)
        acc = alpha * acc + jnp.dot(p.astype(jnp.bfloat16), kt_vals, preferred_element_type=f32)
        return m_new, l, acc

    init = (jnp.full((rows, 1), NEG_BIG, f32), jnp.zeros((rows, 1), f32), jnp.zeros((rows, dh), f32))
    n_tiles = (q0 + Q_BLOCK + tk - 1) // tk
    m, l, acc = lax.fori_loop(0, n_tiles, sweep, init)
    osel_ref[0, 0] = (acc / jnp.maximum(l, 1e-30)).reshape(hpg, Q_BLOCK, dh)

    w0 = pl.multiple_of(jnp.maximum(q0 - WINDOW, 0), Q_BLOCK)
    kwin = kw_ref[0, 0, pl.ds(w0, NSA_WIN_SPAN), :].astype(jnp.bfloat16)
    vwin = vw_ref[0, 0, pl.ds(w0, NSA_WIN_SPAN), :].astype(jnp.bfloat16)
    tq = q0 + lax.broadcasted_iota(jnp.int32, (Q_BLOCK, NSA_WIN_SPAN), 0)
    kp = w0 + lax.broadcasted_iota(jnp.int32, (Q_BLOCK, NSA_WIN_SPAN), 1)
    dlt = tq - kp
    wbias = jnp.where((dlt >= 0) & (dlt < WINDOW), 0.0, NEG_BIG).astype(f32)
    wbias = jnp.concatenate([wbias] * hpg, axis=0)
    s = lax.dot_general(q, kwin, _NT, preferred_element_type=f32) + wbias
    mw = jnp.max(s, axis=-1, keepdims=True)
    p = jnp.exp(s - mw)
    lw = jnp.sum(p, axis=-1, keepdims=True)
    ow = jnp.dot(p.astype(jnp.bfloat16), vwin, preferred_element_type=f32) / jnp.maximum(lw, 1e-30)
    owin_ref[0, 0] = ow.reshape(hpg, Q_BLOCK, dh)


def _nsa_attn(qh, ks, vs, kw, vw, sel):
    bsz, ng, hpg, s, dh = qh.shape
    n_sel = sel.shape[-1]
    kv_spec = pl.BlockSpec((1, 1, s, dh), lambda b, g, i: (b, g, 0, 0))
    q_spec = pl.BlockSpec((1, 1, hpg, Q_BLOCK, dh), lambda b, g, i: (b, g, 0, i, 0))
    out_sd = jax.ShapeDtypeStruct(qh.shape, jnp.float32)
    return pl.pallas_call(
        _nsa_attn_kernel,
        out_shape=(out_sd, out_sd),
        grid=(bsz, ng, s // Q_BLOCK),
        in_specs=[q_spec, kv_spec, kv_spec, kv_spec, kv_spec,
                  pl.BlockSpec((1, 1, Q_BLOCK, n_sel), lambda b, g, i: (b, g, i, 0))],
        out_specs=(q_spec, q_spec),
        compiler_params=pltpu.CompilerParams(
            dimension_semantics=("parallel", "parallel", "arbitrary"),
            vmem_limit_bytes=V7X_VMEM_LIMIT_BYTES),
        name="nsa_attn",
    )(qh, ks, vs, kw, vw, sel)


def _nsa_mixer(q, kc, vc, ks, vs, kw, vw, gates, positions, cmp_pos, cmp_w1, cmp_w2):
    bsz, s, _ = q.shape
    nh, ng, hpg, dh = NSA_HEADS, NSA_KV_HEADS, NSA_GROUP, NSA_DH
    inv = _rope_freqs(dh)
    scale = dh ** -0.5
    q = _rope(q.reshape(bsz, s, nh, dh), positions, inv)
    kv_shape = (bsz, s, ng, dh)
    kc, vc, vs, vw = kc.reshape(kv_shape), vc.reshape(kv_shape), vs.reshape(kv_shape), vw.reshape(kv_shape)
    ks = _rope(ks.reshape(kv_shape), positions, inv)
    kw = _rope(kw.reshape(kv_shape), positions, inv)

    n_cmp = (s - CMP_LEN) // CMP_STRIDE + 1
    starts = np.arange(n_cmp) * CMP_STRIDE
    tok = starts[:, None] + np.arange(CMP_LEN)[None, :]
    cmp_end = starts + CMP_LEN - 1

    def compress(t, pos_emb, w1, w2):
        blk = t[:, tok] + pos_emb[:, None, :]
        blk = blk.transpose(0, 1, 3, 2, 4).reshape(bsz, n_cmp, ng, CMP_LEN * dh)
        return jax.nn.gelu(blk @ w1) @ w2

    k_cmp = _rope(compress(kc, cmp_pos[0], cmp_w1[0], cmp_w2[0]), positions[:, cmp_end], inv)
    v_cmp = compress(vc, cmp_pos[1], cmp_w1[1], cmp_w2[1])
    qg = q.reshape(bsz, s, ng, hpg, dh)
    t_idx = np.arange(s)
    cmask = cmp_end[None, :] <= t_idx[:, None]
    s_cmp = jnp.einsum('bsgqd,bcgd->bgqsc', qg, k_cmp) * scale
    p_cmp = _masked_softmax(s_cmp, cmask)
    o_cmp = jnp.einsum('bgqsc,bcgd->bsgqd', p_cmp.astype(v_cmp.dtype), v_cmp).reshape(bsz, s, nh, dh)

    n_sel = s // SEL_BLOCK
    jb = np.arange(n_sel)
    overlap = ((starts[:, None] < (jb[None, :] + 1) * SEL_BLOCK)
               & (starts[:, None] + CMP_LEN > jb[None, :] * SEL_BLOCK)).astype(np.float32)
    imp = jnp.einsum('bgqsc,cj->bgsj', p_cmp, overlap)
    cur = t_idx // SEL_BLOCK
    valid = jb[None, :] <= cur[:, None]
    forced = (jb[None, :] == 0) | (jb[None, :] == cur[:, None]) | (jb[None, :] == cur[:, None] - 1)
    imp = jnp.where(forced, jnp.inf, jnp.where(valid, imp, -jnp.inf))
    n_top = min(SEL_TOP, n_sel)
    _, sel_idx = lax.top_k(imp, n_top)

    sel = jnp.any(sel_idx[..., :, None] == jnp.arange(n_sel, dtype=sel_idx.dtype), axis=-2)
    qh = qg.transpose(0, 2, 3, 1, 4) * scale
    to_group = lambda t: t.transpose(0, 2, 1, 3)
    o_sel, o_win = _nsa_attn(qh, to_group(ks), to_group(vs), to_group(kw), to_group(vw),
                             sel.astype(jnp.float32))
    ungroup = lambda o: o.transpose(0, 3, 1, 2, 4).reshape(bsz, s, nh, dh)
    o_sel, o_win = ungroup(o_sel), ungroup(o_win)
    gt = jax.nn.sigmoid(gates.reshape(bsz, s, nh, 3))
    o = gt[..., 0:1] * o_cmp + gt[..., 1:2] * o_sel + gt[..., 2:3] * o_win
    return o.reshape(bsz, s, nh * dh)


def _hybrid_mixer(h, positions, bsz, w_in, sc_conv, rw_mu, rw_w0, rw_w2, rw_a0, rw_a2, rw_g2,
                  rw_k_k, rw_k_a, rw_r_k, rw_lnx_g, rw_lnx_b, nsa_cmp_pos, nsa_cmp_w1, nsa_cmp_w2,
                  w_branch):
    m = h.shape[0]
    s = m // bsz
    offs = np.cumsum((0,) + GROUP_SIZES)

    def group(i):
        z = _mm(h, w_in[:, offs[i]:offs[i + 1]], keep_pad=True)
        return z.reshape(bsz, s, z.shape[1])[..., :GROUP_SIZES[i]]

    z_sc, z_rw, z_ret, z_nsa = group(0), group(1), group(2), group(3)
    b_gate, c_gate, xin = _split(z_sc, SC_SIZES)
    o_a = b_gate * _causal_dwconv(c_gate * xin, sc_conv)
    o_b = _rwkv7_mixer(z_rw, rw_mu, rw_w0, rw_w2, rw_a0, rw_a2, rw_g2, rw_k_k, rw_k_a, rw_r_k, rw_lnx_g, rw_lnx_b)
    o_c = _retention_mixer(*_split(z_ret, RET_SIZES), positions)
    o_d = _nsa_mixer(*_split(z_nsa, NSA_SIZES), positions, nsa_cmp_pos, nsa_cmp_w1, nsa_cmp_w2)
    outs = [o.reshape(m, o.shape[-1]) for o in (o_a, o_b, o_c, o_d)]
    w_gate = w_in[:, offs[4]:offs[5]].astype(jnp.bfloat16)
    return _merge(h, w_gate, outs, _split(w_branch, BR_SIZES, axis=0))


def kernel(x, mem, positions, ln_mix_pre, ln_mix_post, ln_mem_q, ln_mem_kv, ln_mem_post, ln_ffn_pre, ln_ffn_post, w_in, sc_conv, rw_mu, rw_w0, rw_w2, rw_a0, rw_a2, rw_g2, rw_k_k, rw_k_a, rw_r_k, rw_lnx_g, rw_lnx_b, nsa_cmp_pos, nsa_cmp_w1, nsa_cmp_w2, w_branch, w_out, mem_wq, mem_wkv, mem_wo, ffn_w_up, ffn_conv_w, ffn_conv_b, ffn_w_down):
    bsz, s, d = x.shape
    mem_len = mem.shape[1]
    xf = x.reshape(bsz * s, d)
    h = _prenorm(xf, ln_mix_pre[0])
    for l in range(DEPTH):
        merged = _hybrid_mixer(h, positions, bsz, w_in[l], sc_conv[l], rw_mu[l], rw_w0[l], rw_w2[l], rw_a0[l],
                               rw_a2[l], rw_g2[l], rw_k_k[l], rw_k_a[l], rw_r_k[l], rw_lnx_g[l], rw_lnx_b[l],
                               nsa_cmp_pos[l], nsa_cmp_w1[l], nsa_cmp_w2[l], w_branch[l])
        xf, h = _mm_norm_res(merged, w_out[l], xf, ln_mix_post[l], ln_mem_q[l])
        mem_n = _prenorm(mem.reshape(bsz * mem_len, d), ln_mem_kv[l])
        kv = _mm(mem_n, mem_wkv[l]).astype(jnp.bfloat16)
        k_mem = kv[:, :d].reshape(bsz, mem_len, d)
        v_mem = kv[:, d:].reshape(bsz, mem_len, d)
        o = _xattn(h, mem_wq[l], k_mem, v_mem, s)
        xf, h = _mm_norm_res(o, mem_wo[l], xf, ln_mem_post[l], ln_ffn_pre[l])
        act = _ffn_up(h, ffn_w_up[l], ffn_conv_w[l], ffn_conv_b[l], s)
        g_next = ln_mix_pre[l + 1] if l + 1 < DEPTH else None
        xf, h = _mm_norm_res(act, ffn_w_down[l], xf, ln_ffn_post[l], g_next)
    return xf.reshape(bsz, s, d)
```

```python
import functools
import math

import jax
import jax.numpy as jnp
import numpy as np
from jax import lax
from jax.experimental import pallas as pl
from jax.experimental.pallas import tpu as pltpu

D_MODEL = 2048
DEPTH = 2
MEM_HEADS = 4
MEM_HEAD_DIM = D_MODEL // MEM_HEADS
SC_W = D_MODEL // 4
SC_K = 3
RW_HEAD_DIM = 64
RW_W = D_MODEL // 4
RW_HEADS = RW_W // RW_HEAD_DIM
RW_LORA_W = 96
RW_LORA_A = 96
RW_LORA_G = 256
RW_DECAY_SCALE = math.exp(-0.5)
RW_LN_EPS = 64e-5
RET_HEADS = 4
RET_DK = 128
RET_DV = 2 * RET_DK
RET_CHUNK = 128
RET_LN_EPS = 1e-5
NSA_HEADS = 8
NSA_KV_HEADS = 2
NSA_DH = 64
NSA_GROUP = NSA_HEADS // NSA_KV_HEADS
CMP_LEN = 32
CMP_STRIDE = 16
CMP_HIDDEN = 256
SEL_BLOCK = 64
SEL_TOP = 16
WINDOW = 512
Q_BLOCK = 128
FFN_DIM = 5632
ROPE_THETA = 10000.0
EPS = 1e-6
N_BRANCH = 4

SC_SIZES = (SC_W, SC_W, SC_W)
RW_SIZES = (RW_W, RW_W, RW_W, RW_LORA_W, RW_LORA_A, RW_LORA_G)
RET_SIZES = (RET_HEADS * RET_DK, RET_HEADS * RET_DK, RET_HEADS * RET_DV, RET_HEADS * RET_DV)
NSA_SIZES = (NSA_HEADS * NSA_DH,) + (NSA_KV_HEADS * NSA_DH,) * 6 + (NSA_HEADS * 3,)
GATE_SIZES = (D_MODEL,) * N_BRANCH
GROUP_SIZES = (sum(SC_SIZES), sum(RW_SIZES), sum(RET_SIZES), sum(NSA_SIZES), sum(GATE_SIZES))
BR_SIZES = (SC_W, RW_W, RET_HEADS * RET_DV, NSA_HEADS * NSA_DH)

V7X_VMEM_LIMIT_BYTES = 48 * 1024 * 1024


def _split(x, sizes, axis=-1):
    return jnp.split(x, [int(c) for c in np.cumsum(sizes)[:-1]], axis=axis)


def _mm_kernel(a_ref, w_ref, o_ref, acc_ref):
    k = pl.program_id(2)

    @pl.when(k == 0)
    def _():
        acc_ref[...] = jnp.zeros_like(acc_ref)

    acc_ref[...] += jnp.dot(a_ref[...], w_ref[...], preferred_element_type=jnp.float32)

    @pl.when(k == pl.num_programs(2) - 1)
    def _():
        o_ref[...] = acc_ref[...]


def _pick_tile(n, candidates):
    for c in candidates:
        if n % c == 0:
            return c
    return n


LANES = 128


def _mm(a, w, keep_pad=False):
    m, k = a.shape
    n = w.shape[1]
    a = a.astype(jnp.bfloat16)
    w = w.astype(jnp.bfloat16)
    n_pad = LANES * ((n + LANES - 1) // LANES)
    tn = n_pad if n_pad <= 1536 else _pick_tile(n_pad, (1024, 768, 512, 256, 128))
    if n_pad != n:
        w = jnp.pad(w, ((0, 0), (0, n_pad - n)))
    tm = _pick_tile(m, (1024, 512, 256, 128, 8))
    tk = _pick_tile(k, (2048, 1280, 1408, 1024, 512, 256, 128))
    out = pl.pallas_call(
        _mm_kernel,
        out_shape=jax.ShapeDtypeStruct((m, n_pad), jnp.float32),
        grid=(m // tm, n_pad // tn, k // tk),
        in_specs=[pl.BlockSpec((tm, tk), lambda i, j, kk: (i, kk)),
                  pl.BlockSpec((tk, tn), lambda i, j, kk: (kk, j))],
        out_specs=pl.BlockSpec((tm, tn), lambda i, j, kk: (i, j)),
        scratch_shapes=[pltpu.VMEM((tm, tn), jnp.float32)],
        compiler_params=pltpu.CompilerParams(
            dimension_semantics=("parallel", "parallel", "arbitrary"),
            vmem_limit_bytes=V7X_VMEM_LIMIT_BYTES),
        name="mm",
    )(a, w)
    return out if keep_pad or n_pad == n else out[:, :n]


def _proj(x, w):
    lead = x.shape[:-1]
    return _mm(x.reshape(-1, x.shape[-1]), w).reshape(*lead, w.shape[1])


def _rms(y, g):
    return y * lax.rsqrt(jnp.mean(y * y, axis=-1, keepdims=True) + EPS) * g


def _prenorm_kernel(x_ref, g_ref, h_ref):
    h_ref[...] = _rms(x_ref[...], g_ref[...]).astype(h_ref.dtype)


def _prenorm(x, g):
    m, d = x.shape
    tm = _pick_tile(m, (512, 256, 128, 8))
    return pl.pallas_call(
        _prenorm_kernel,
        out_shape=jax.ShapeDtypeStruct((m, d), jnp.bfloat16),
        grid=(m // tm,),
        in_specs=[pl.BlockSpec((tm, d), lambda i: (i, 0)), pl.BlockSpec((1, d), lambda i: (0, 0))],
        out_specs=pl.BlockSpec((tm, d), lambda i: (i, 0)),
        compiler_params=pltpu.CompilerParams(dimension_semantics=("parallel",)),
        name="prenorm",
    )(x, g.reshape(1, d))


def _mm_norm_res_kernel(emit_h, a_ref, w_ref, res_ref, g_ref, *rest):
    if emit_h:
        g2_ref, x_ref, h_ref, acc_ref = rest
    else:
        x_ref, acc_ref = rest
    k = pl.program_id(1)

    @pl.when(k == 0)
    def _():
        acc_ref[...] = jnp.zeros_like(acc_ref)

    acc_ref[...] += jnp.dot(a_ref[...], w_ref[...], preferred_element_type=jnp.float32)

    @pl.when(k == pl.num_programs(1) - 1)
    def _():
        x_new = res_ref[...] + _rms(acc_ref[...], g_ref[...])
        x_ref[...] = x_new
        if emit_h:
            h_ref[...] = _rms(x_new, g2_ref[...]).astype(h_ref.dtype)


def _mm_norm_res(a, w, res, g, g_next):
    m, k = a.shape
    d = w.shape[1]
    tk = _pick_tile(k, (2048, 1408, 1024, 512))
    tm = 256 if tk == k else 512
    emit_h = g_next is not None
    row = pl.BlockSpec((tm, d), lambda i, kk: (i, 0))
    vec = pl.BlockSpec((1, d), lambda i, kk: (0, 0))
    x_sd = jax.ShapeDtypeStruct((m, d), jnp.float32)
    h_sd = jax.ShapeDtypeStruct((m, d), jnp.bfloat16)
    out = pl.pallas_call(
        functools.partial(_mm_norm_res_kernel, emit_h),
        out_shape=(x_sd, h_sd) if emit_h else x_sd,
        grid=(m // tm, k // tk),
        in_specs=[pl.BlockSpec((tm, tk), lambda i, kk: (i, kk)),
                  pl.BlockSpec((tk, d), lambda i, kk: (kk, 0)),
                  row, vec] + ([vec] if emit_h else []),
        out_specs=(row, row) if emit_h else row,
        scratch_shapes=[pltpu.VMEM((tm, d), jnp.float32)],
        compiler_params=pltpu.CompilerParams(
            dimension_semantics=("parallel", "arbitrary"), vmem_limit_bytes=V7X_VMEM_LIMIT_BYTES),
        name="mm_norm_res",
    )(a.astype(jnp.bfloat16), w.astype(jnp.bfloat16), res, g.reshape(1, d),
      *([g_next.reshape(1, d)] if emit_h else []))
    return out if emit_h else (out, None)


FFN_TM = 1024
FFN_TN = 512
HALO = 8


def _shift_rows(u, prev, n):
    rolled = pltpu.roll(u, n, axis=0)
    rows = lax.broadcasted_iota(jnp.int32, u.shape, 0)
    for r in range(n):
        rolled = jnp.where(rows == r, prev[HALO - n + r:HALO - n + r + 1, :], rolled)
    return rolled


def _ffn_up_kernel(seq_len, h_ref, halo_ref, wa_ref, wb_ref, cwa_ref, cwb_ref, ba_ref, bb_ref, o_ref):
    i = pl.program_id(0)
    h = h_ref[...]
    at_start = (i * FFN_TM) % seq_len == 0
    halo = jnp.where(at_start, jnp.zeros_like(halo_ref[...]), halo_ref[...])

    def conv(w_ref, cw_ref, b_ref):
        u = jnp.dot(h, w_ref[...], preferred_element_type=jnp.float32)
        up = jnp.dot(halo, w_ref[...], preferred_element_type=jnp.float32)
        cw = cw_ref[...]
        y = cw[0:1] * _shift_rows(u, up, 2)
        y = y + cw[1:2] * _shift_rows(u, up, 1)
        y = y + cw[2:3] * u
        return y + b_ref[...]

    a = conv(wa_ref, cwa_ref, ba_ref)
    b = conv(wb_ref, cwb_ref, bb_ref)
    o_ref[...] = (jax.nn.gelu(a, approximate=True) * b).astype(o_ref.dtype)


def _ffn_up(h, w_up, w_conv, b_conv, seq_len):
    m, d = h.shape
    f = w_up.shape[1] // 2
    nj = f // FFN_TN
    w_up = w_up.astype(jnp.bfloat16)
    b_conv = b_conv.reshape(1, 2 * f)
    tiles_per_halo = FFN_TM // HALO
    return pl.pallas_call(
        functools.partial(_ffn_up_kernel, seq_len),
        out_shape=jax.ShapeDtypeStruct((m, f), jnp.bfloat16),
        grid=(m // FFN_TM, nj),
        in_specs=[pl.BlockSpec((FFN_TM, d), lambda i, j: (i, 0)),
                  pl.BlockSpec((HALO, d), lambda i, j: (jnp.maximum(i * tiles_per_halo - 1, 0), 0)),
                  pl.BlockSpec((d, FFN_TN), lambda i, j: (0, j)),
                  pl.BlockSpec((d, FFN_TN), lambda i, j: (0, j + nj)),
                  pl.BlockSpec((3, FFN_TN), lambda i, j: (0, j)),
                  pl.BlockSpec((3, FFN_TN), lambda i, j: (0, j + nj)),
                  pl.BlockSpec((1, FFN_TN), lambda i, j: (0, j)),
                  pl.BlockSpec((1, FFN_TN), lambda i, j: (0, j + nj))],
        out_specs=pl.BlockSpec((FFN_TM, FFN_TN), lambda i, j: (i, j)),
        compiler_params=pltpu.CompilerParams(
            dimension_semantics=("parallel", "arbitrary"), vmem_limit_bytes=V7X_VMEM_LIMIT_BYTES),
        name="ffn_up",
    )(h, h, w_up, w_up, w_conv, w_conv, b_conv, b_conv)


MERGE_TM = 512
MERGE_TN = 512


def _merge_kernel(h_ref, wg0, wg1, wg2, wg3, o0, o1, o2, o3, wb0, wb1, wb2, wb3, out_ref):
    h = h_ref[...]
    acc = None
    for wg, o, wb in ((wg0, o0, wb0), (wg1, o1, wb1), (wg2, o2, wb2), (wg3, o3, wb3)):
        gate = jax.nn.sigmoid(jnp.dot(h, wg[...], preferred_element_type=jnp.float32))
        term = gate * jnp.dot(o[...], wb[...], preferred_element_type=jnp.float32)
        acc = term if acc is None else acc + term
    out_ref[...] = acc.astype(out_ref.dtype)


def _merge(h, w_gate, outs, w_branches):
    m, d = h.shape
    nj = d // MERGE_TN
    gate_specs = [pl.BlockSpec((d, MERGE_TN), functools.partial(lambda b, i, j: (0, b * nj + j), b))
                  for b in range(N_BRANCH)]
    o_specs = [pl.BlockSpec((MERGE_TM, o.shape[1]), lambda i, j: (i, 0)) for o in outs]
    wb_specs = [pl.BlockSpec((wb.shape[0], MERGE_TN), lambda i, j: (0, j)) for wb in w_branches]
    return pl.pallas_call(
        _merge_kernel,
        out_shape=jax.ShapeDtypeStruct((m, d), jnp.bfloat16),
        grid=(m // MERGE_TM, nj),
        in_specs=[pl.BlockSpec((MERGE_TM, d), lambda i, j: (i, 0))] + gate_specs + o_specs + wb_specs,
        out_specs=pl.BlockSpec((MERGE_TM, MERGE_TN), lambda i, j: (i, j)),
        compiler_params=pltpu.CompilerParams(
            dimension_semantics=("parallel", "arbitrary"), vmem_limit_bytes=V7X_VMEM_LIMIT_BYTES),
        name="merge",
    )(h, *([w_gate] * N_BRANCH), *[o.astype(jnp.bfloat16) for o in outs],
      *[wb.astype(jnp.bfloat16) for wb in w_branches])


XATT_TM = 512


def _xattn_kernel(h_ref, wq_ref, k_ref, v_ref, o_ref):
    q = jnp.dot(h_ref[...], wq_ref[...], preferred_element_type=jnp.float32)
    q = (q * (MEM_HEAD_DIM ** -0.5)).astype(jnp.bfloat16)
    for hh in range(MEM_HEADS):
        sl = slice(hh * MEM_HEAD_DIM, (hh + 1) * MEM_HEAD_DIM)
        s = lax.dot_general(q[:, sl], k_ref[0, :, sl], _NT, preferred_element_type=jnp.float32)
        p = jnp.exp(s - jnp.max(s, axis=-1, keepdims=True))
        p = p / jnp.sum(p, axis=-1, keepdims=True)
        o_ref[:, sl] = jnp.dot(p.astype(jnp.bfloat16), v_ref[0, :, sl],
                               preferred_element_type=jnp.float32).astype(o_ref.dtype)


def _xattn(h, wq, k, v, seq_len):
    m, d = h.shape
    mem_len = k.shape[1]
    per_batch = seq_len // XATT_TM
    kv_spec = pl.BlockSpec((1, mem_len, d), lambda i: (i // per_batch, 0, 0))
    return pl.pallas_call(
        _xattn_kernel,
        out_shape=jax.ShapeDtypeStruct((m, d), jnp.bfloat16),
        grid=(m // XATT_TM,),
        in_specs=[pl.BlockSpec((XATT_TM, d), lambda i: (i, 0)),
                  pl.BlockSpec((d, d), lambda i: (0, 0)), kv_spec, kv_spec],
        out_specs=pl.BlockSpec((XATT_TM, d), lambda i: (i, 0)),
        compiler_params=pltpu.CompilerParams(
            dimension_semantics=("parallel",), vmem_limit_bytes=V7X_VMEM_LIMIT_BYTES),
        name="xattn",
    )(h, wq.astype(jnp.bfloat16), k, v)


def _rms_norm(x, g):
    y = x * lax.rsqrt(jnp.mean(x * x, axis=-1, keepdims=True) + EPS)
    return y * g


def _head_norm(x, eps):
    mu = jnp.mean(x, axis=-1, keepdims=True)
    var = jnp.mean(jnp.square(x - mu), axis=-1, keepdims=True)
    return (x - mu) * lax.rsqrt(var + eps)


def _causal_dwconv(x, w):
    k_w = w.shape[0]
    s = x.shape[1]
    xp = jnp.pad(x, ((0, 0), (k_w - 1, 0), (0, 0)))
    y = w[0] * xp[:, 0:s]
    for j in range(1, k_w):
        y = y + w[j] * xp[:, j:j + s]
    return y


def _rope_freqs(d):
    return ROPE_THETA ** (-jnp.arange(0, d, 2, dtype=jnp.float32) / d)


def _retnet_freqs(d):
    return 1.0 / (ROPE_THETA ** jnp.linspace(0.0, 1.0, d // 2, dtype=jnp.float32))


def _rope(x, pos, inv_freq):
    ang = pos.astype(jnp.float32)[..., None] * inv_freq
    cos = jnp.cos(ang)[:, :, None, :].astype(x.dtype)
    sin = jnp.sin(ang)[:, :, None, :].astype(x.dtype)
    x1, x2 = jnp.split(x, 2, axis=-1)
    return jnp.concatenate([x1 * cos - x2 * sin, x2 * cos + x1 * sin], axis=-1)


def _masked_softmax(s, mask):
    s = jnp.where(mask, s.astype(jnp.float32), -jnp.inf)
    m = jnp.max(s, axis=-1, keepdims=True)
    m = jnp.where(jnp.isfinite(m), m, 0.0)
    e = jnp.where(mask, jnp.exp(s - m), 0.0)
    return e / jnp.maximum(jnp.sum(e, axis=-1, keepdims=True), 1e-30)


WKV_CHUNK = 64
_NT = (((1,), (1,)), ((), ()))


def _bdot(a, b):
    return jnp.dot(a.astype(jnp.bfloat16), b.astype(jnp.bfloat16), preferred_element_type=jnp.float32)


def _bdot_nt(a, b):
    return lax.dot_general(a.astype(jnp.bfloat16), b.astype(jnp.bfloat16), _NT,
                           preferred_element_type=jnp.float32)


def _head_sum(x, bd):
    hi = x.astype(jnp.bfloat16)
    lo = (x - hi.astype(jnp.float32)).astype(jnp.bfloat16)
    return (jnp.dot(hi, bd, preferred_element_type=jnp.float32)
            + jnp.dot(lo, bd, preferred_element_type=jnp.float32))


def _rwkv7_kernel(z_ref, mu_ref, w0_ref, a0_ref, kk_ref, ka_ref, rk_ref, lng_ref, lnb_ref,
                  ww2_ref, wa2_ref, wg2_ref, o_ref, s_ref, prev_ref, y_ref):
    c = WKV_CHUNK
    hd = RW_HEAD_DIM
    w = RW_W
    f32, bf16 = jnp.float32, jnp.bfloat16

    @pl.when(pl.program_id(1) == 0)
    def _():
        s_ref[...] = jnp.zeros_like(s_ref)
        prev_ref[...] = jnp.zeros_like(prev_ref)

    z = z_ref[0]
    rows = lax.broadcasted_iota(jnp.int32, z.shape, 0)
    z_prev = jnp.where(rows == 0, prev_ref[...], pltpu.roll(z, 1, axis=0))
    prev_ref[...] = z[c - 1:c, :]
    z = z + (z_prev - z) * mu_ref[...]
    r, k, v, lora = z[:, 0:w], z[:, w:2 * w], z[:, 2 * w:3 * w], z[:, 3 * w:4 * w]

    logw = -RW_DECAY_SCALE * jax.nn.sigmoid(w0_ref[...] + _bdot(jnp.tanh(lora), ww2_ref[...]))
    a = jax.nn.sigmoid(a0_ref[...] + _bdot(lora, wa2_ref[...]))
    g = _bdot(jax.nn.sigmoid(lora), wg2_ref[...])

    lane_h = lax.broadcasted_iota(jnp.int32, (w, w), 0) // hd
    lane_w = lax.broadcasted_iota(jnp.int32, (w, w), 1) // hd
    head_bd = jnp.where(lane_h == lane_w, 1.0, 0.0).astype(bf16)

    kk = k * kk_ref[...]
    kk = kk / jnp.maximum(jnp.sqrt(_head_sum(kk * kk, head_bd)), 1e-12)
    k = k * (1.0 + (a - 1.0) * ka_ref[...])

    row = lax.broadcasted_iota(jnp.int32, (c, c), 0)
    col = lax.broadcasted_iota(jnp.int32, (c, c), 1)
    tri = jnp.where(row >= col, 1.0, 0.0).astype(bf16)
    hi = logw.astype(bf16)
    rem = logw - hi.astype(f32)
    mid = rem.astype(bf16)
    lo = (rem - mid.astype(f32)).astype(bf16)
    cum = (jnp.dot(tri, hi, preferred_element_type=f32) + jnp.dot(tri, mid, preferred_element_type=f32)
           + jnp.dot(tri, lo, preferred_element_type=f32))
    kka = kk * a
    e_neg = jnp.exp(-cum)
    qh = (kk * jnp.exp(cum - logw)).astype(bf16)
    rh = (r * jnp.exp(cum)).astype(bf16)
    bh = (kka * e_neg).astype(bf16)
    kh = (k * e_neg).astype(bf16)
    last = cum[c - 1:c, :]
    dec = jnp.exp(last - cum)
    bd = (kka * dec).astype(bf16)
    kd = (k * dec).astype(bf16)
    g_last = jnp.exp(last)
    vb = v.astype(bf16)

    r2 = lax.broadcasted_iota(jnp.int32, (2 * c, 2 * c), 0)
    c2 = lax.broadcasted_iota(jnp.int32, (2 * c, 2 * c), 1)
    rr = jnp.where(r2 >= c, r2 - c, r2)
    cc = jnp.where(c2 >= c, c2 - c, c2)
    tri_mask = cc < rr + jnp.where(r2 >= c, 1, 0)
    eye = jnp.where(row == col, 1.0, 0.0).astype(f32)

    heads = range(RW_HEADS)
    sls = [slice(h * hd, (h + 1) * hd) for h in heads]
    a1 = [jnp.concatenate([qh[:, sl], rh[:, sl]], axis=0) for sl in sls]
    b1 = [jnp.concatenate([bh[:, sl], kh[:, sl]], axis=0) for sl in sls]
    ss = [jnp.where(tri_mask, lax.dot_general(a1[h], b1[h], _NT, preferred_element_type=f32), 0.0)
          for h in heads]
    s_old = [s_ref[h] for h in heads]
    qr = [lax.dot_general(a1[h], s_old[h].astype(bf16), _NT, preferred_element_type=f32) for h in heads]
    lm = [_bdot(ss[h][:, c:2 * c], vb[:, sls[h]]) for h in heads]
    n = [-ss[h][0:c, 0:c] for h in heads]
    t = [eye + n[h] for h in heads]
    for _ in range(5):
        n = [_bdot(n[h], n[h]) for h in heads]
        t = [t[h] + _bdot(t[h], n[h]) for h in heads]
    u = [-_bdot(t[h], qr[h][0:c] + lm[h][0:c]) for h in heads]
    for h in heads:
        y_ref[:, sls[h]] = qr[h][c:2 * c] + lm[h][c:2 * c] + _bdot(ss[h][c:2 * c, 0:c], u[h])
    for h in heads:
        zt = jnp.concatenate([u[h], v[:, sls[h]]], axis=0).T
        x = jnp.concatenate([bd[:, sls[h]], kd[:, sls[h]]], axis=0)
        s_ref[h] = s_old[h] * g_last[:, sls[h]] + _bdot(zt, x)

    y = y_ref[...]
    mean = _head_sum(y, head_bd) * (1.0 / hd)
    yc = y - mean
    var = _head_sum(yc * yc, head_bd) * (1.0 / hd)
    yn = yc * lax.rsqrt(var + RW_LN_EPS) * lng_ref[...] + lnb_ref[...]
    bonus = _head_sum(r * k * rk_ref[...], head_bd) * v
    o_ref[0] = ((yn + bonus) * g).astype(o_ref.dtype)


def _rwkv7_mixer(z, mu, w0, w_w2, a0, w_a2, w_g2, k_k, k_a, r_k, lnx_g, lnx_b):
    bsz, s, zw = z.shape
    w = RW_W
    n_lora = RW_LORA_W + RW_LORA_A + RW_LORA_G
    row = lambda t: t.reshape(1, -1)
    mu = jnp.pad(mu, (0, zw - mu.shape[0])).reshape(1, zw)
    o_w, o_a = RW_LORA_W, RW_LORA_W + RW_LORA_A
    pad_rows = lambda m, lo: jnp.pad(m, ((lo, w - lo - m.shape[0]), (0, 0))).astype(jnp.bfloat16)
    vec = pl.BlockSpec((1, w), lambda b, c: (0, 0))
    mat = pl.BlockSpec((w, w), lambda b, c: (0, 0))
    assert zw == 4 * w and n_lora <= w
    return pl.pallas_call(
        _rwkv7_kernel,
        out_shape=jax.ShapeDtypeStruct((bsz, s, w), jnp.bfloat16),
        grid=(bsz, s // WKV_CHUNK),
        in_specs=[pl.BlockSpec((1, WKV_CHUNK, zw), lambda b, c: (b, c, 0)),
                  pl.BlockSpec((1, zw), lambda b, c: (0, 0))] + [vec] * 7 + [mat] * 3,
        out_specs=pl.BlockSpec((1, WKV_CHUNK, w), lambda b, c: (b, c, 0)),
        scratch_shapes=[pltpu.VMEM((RW_HEADS, RW_HEAD_DIM, RW_HEAD_DIM), jnp.float32),
                        pltpu.VMEM((1, zw), jnp.float32),
                        pltpu.VMEM((WKV_CHUNK, w), jnp.float32)],
        compiler_params=pltpu.CompilerParams(dimension_semantics=("parallel", "arbitrary")),
        name="rwkv7",
    )(z, mu, row(w0), row(a0), row(k_k), row(k_a), row(r_k), row(lnx_g), row(lnx_b),
      pad_rows(w_w2, 0), pad_rows(w_a2, o_w), pad_rows(w_g2, o_a))


RET_LOG_DECAY = tuple(math.log(1.0 - 2.0 ** (-5.0 - h)) for h in range(RET_HEADS))


def _retention_kernel(q_ref, k_ref, v_ref, g_ref, cos_ref, sin_ref, o_ref, s_ref):
    c, dk, dv = RET_CHUNK, RET_DK, RET_DV
    f32 = jnp.float32

    @pl.when(pl.program_id(1) == 0)
    def _():
        s_ref[...] = jnp.zeros_like(s_ref)

    cos = cos_ref[0]
    sin = sin_ref[0]
    diff = (lax.broadcasted_iota(jnp.int32, (c, c), 0) - lax.broadcasted_iota(jnp.int32, (c, c), 1)).astype(f32)
    tok = lax.broadcasted_iota(jnp.int32, (c, 1), 0).astype(f32)
    rope = lambda x: x * cos + pltpu.roll(x, dk // 2, axis=1) * sin
    for h in range(RET_HEADS):
        lg = RET_LOG_DECAY[h]
        q = rope(q_ref[0, :, h * dk:(h + 1) * dk])
        k = rope(k_ref[0, :, h * dk:(h + 1) * dk]) * (dk ** -0.5)
        v = v_ref[0, :, h * dv:(h + 1) * dv]
        decay_in = jnp.where(diff >= 0, jnp.exp(jnp.maximum(diff, 0.0) * lg), 0.0)
        scores = _bdot_nt(q, k) * decay_in
        state = s_ref[h]
        o = _bdot(scores, v) + _bdot(q * jnp.exp((tok + 1.0) * lg), state)
        s_ref[h] = state * math.exp(c * lg) + _bdot((k * jnp.exp((c - 1.0 - tok) * lg)).T, v)
        mu = jnp.mean(o, axis=-1, keepdims=True)
        oc = o - mu
        var = jnp.mean(oc * oc, axis=-1, keepdims=True)
        gate = g_ref[0, :, h * dv:(h + 1) * dv]
        o_ref[0, :, h * dv:(h + 1) * dv] = (oc * lax.rsqrt(var + RET_LN_EPS) * gate * jax.nn.sigmoid(gate)
                                            ).astype(o_ref.dtype)


def _retention_mixer(z, cos, sin):
    bsz, s, _ = z.shape
    qk_w, v_w = RET_HEADS * RET_DK, RET_HEADS * RET_DV
    c = RET_CHUNK
    tab = pl.BlockSpec((1, c, RET_DK), lambda b, i: (b, i, 0))
    return pl.pallas_call(
        _retention_kernel,
        out_shape=jax.ShapeDtypeStruct((bsz, s, v_w), jnp.bfloat16),
        grid=(bsz, s // c),
        in_specs=[pl.BlockSpec((1, c, qk_w), lambda b, i: (b, i, 0)),
                  pl.BlockSpec((1, c, qk_w), lambda b, i: (b, i, 1)),
                  pl.BlockSpec((1, c, v_w), lambda b, i: (b, i, 1)),
                  pl.BlockSpec((1, c, v_w), lambda b, i: (b, i, 2)), tab, tab],
        out_specs=pl.BlockSpec((1, c, v_w), lambda b, i: (b, i, 0)),
        scratch_shapes=[pltpu.VMEM((RET_HEADS, RET_DK, RET_DV), jnp.float32)],
        compiler_params=pltpu.CompilerParams(dimension_semantics=("parallel", "arbitrary")),
        name="retention",
    )(z, z, z, z, cos, sin)


def _rope_tables(positions, inv_freq):
    ang = positions.astype(jnp.float32)[..., None] * inv_freq
    cos, sin = jnp.cos(ang), jnp.sin(ang)
    return jnp.concatenate([cos, cos], axis=-1), jnp.concatenate([-sin, sin], axis=-1)


SC_TM = 512


def _short_conv_kernel(seq_len, b_ref, c_ref, x_ref, ch_ref, xh_ref, w_ref, o_ref):
    at_start = (pl.program_id(0) * SC_TM) % seq_len == 0
    u = c_ref[...] * x_ref[...]
    up = jnp.where(at_start, 0.0, ch_ref[...] * xh_ref[...])
    w = w_ref[...]
    y = w[0:1] * _shift_rows(u, up, 2) + w[1:2] * _shift_rows(u, up, 1) + w[2:3] * u
    o_ref[...] = (b_ref[...] * y).astype(o_ref.dtype)


def _short_conv_mixer(z, w_conv, seq_len):
    m = z.shape[0]
    w = SC_W
    per_halo = SC_TM // HALO
    tile = lambda j: pl.BlockSpec((SC_TM, w), lambda i: (i, j))
    halo = lambda j: pl.BlockSpec((HALO, w), lambda i: (jnp.maximum(i * per_halo - 1, 0), j))
    return pl.pallas_call(
        functools.partial(_short_conv_kernel, seq_len),
        out_shape=jax.ShapeDtypeStruct((m, w), jnp.bfloat16),
        grid=(m // SC_TM,),
        in_specs=[tile(0), tile(1), tile(2), halo(1), halo(2), pl.BlockSpec((SC_K, w), lambda i: (0, 0))],
        out_specs=pl.BlockSpec((SC_TM, w), lambda i: (i, 0)),
        compiler_params=pltpu.CompilerParams(dimension_semantics=("parallel",)),
        name="short_conv",
    )(z, z, z, z, z, w_conv)


NSA_KEY_TILE = 512
NSA_WIN_SPAN = WINDOW + Q_BLOCK
NEG_BIG = -1e30


def _nsa_attn_kernel(q_ref, ks_ref, vs_ref, kw_ref, vw_ref, sel_ref, osel_ref, owin_ref):
    f32 = jnp.float32
    qi = pl.program_id(2)
    q0 = qi * Q_BLOCK
    hpg, dh, tk = NSA_GROUP, NSA_DH, NSA_KEY_TILE
    rows = hpg * Q_BLOCK
    q = q_ref[0, 0].reshape(rows, dh).astype(jnp.bfloat16)
    sel = sel_ref[0, 0].astype(jnp.bfloat16)
    n_sel = sel.shape[1]
    t_pos = q0 + lax.broadcasted_iota(jnp.int32, (Q_BLOCK, tk), 0)
    col = lax.broadcasted_iota(jnp.int32, (Q_BLOCK, tk), 1)
    blk_row = lax.broadcasted_iota(jnp.int32, (n_sel, tk), 0)
    blk_col = lax.broadcasted_iota(jnp.int32, (n_sel, tk), 1) // SEL_BLOCK

    def sweep(kt, carry):
        m, l, acc = carry
        k0 = pl.multiple_of(kt * tk, tk)
        kt_keys = ks_ref[0, 0, pl.ds(k0, tk), :].astype(jnp.bfloat16)
        kt_vals = vs_ref[0, 0, pl.ds(k0, tk), :].astype(jnp.bfloat16)
        expand = jnp.where(blk_row == blk_col + kt * (tk // SEL_BLOCK), 1.0, 0.0).astype(jnp.bfloat16)
        picked = jnp.dot(sel, expand, preferred_element_type=f32)
        ok = (picked > 0.5) & (k0 + col <= t_pos)
        bias = jnp.where(ok, 0.0, NEG_BIG).astype(f32)
        bias = jnp.concatenate([bias] * hpg, axis=0)
        s = lax.dot_general(q, kt_keys, _NT, preferred_element_type=f32) + bias
        m_new = jnp.maximum(m, jnp.max(s, axis=-1, keepdims=True))
        alpha = jnp.exp(m - m_new)
        p = jnp.exp(s - m_new)
        l = alpha * l + jnp.sum(p, axis=-1, keepdims=True)
        acc = alpha * acc + jnp.dot(p.astype(jnp.bfloat16), kt_vals, preferred_element_type=f32)
        return m_new, l, acc

    init = (jnp.full((rows, 1), NEG_BIG, f32), jnp.zeros((rows, 1), f32), jnp.zeros((rows, dh), f32))
    n_tiles = (q0 + Q_BLOCK + tk - 1) // tk
    m, l, acc = lax.fori_loop(0, n_tiles, sweep, init)
    osel_ref[0, 0] = (acc / jnp.maximum(l, 1e-30)).reshape(hpg, Q_BLOCK, dh)

    w0 = pl.multiple_of(jnp.maximum(q0 - WINDOW, 0), Q_BLOCK)
    kwin = kw_ref[0, 0, pl.ds(w0, NSA_WIN_SPAN), :].astype(jnp.bfloat16)
    vwin = vw_ref[0, 0, pl.ds(w0, NSA_WIN_SPAN), :].astype(jnp.bfloat16)
    tq = q0 + lax.broadcasted_iota(jnp.int32, (Q_BLOCK, NSA_WIN_SPAN), 0)
    kp = w0 + lax.broadcasted_iota(jnp.int32, (Q_BLOCK, NSA_WIN_SPAN), 1)
    dlt = tq - kp
    wbias = jnp.where((dlt >= 0) & (dlt < WINDOW), 0.0, NEG_BIG).astype(f32)
    wbias = jnp.concatenate([wbias] * hpg, axis=0)
    s = lax.dot_general(q, kwin, _NT, preferred_element_type=f32) + wbias
    mw = jnp.max(s, axis=-1, keepdims=True)
    p = jnp.exp(s - mw)
    lw = jnp.sum(p, axis=-1, keepdims=True)
    ow = jnp.dot(p.astype(jnp.bfloat16), vwin, preferred_element_type=f32) / jnp.maximum(lw, 1e-30)
    owin_ref[0, 0] = ow.reshape(hpg, Q_BLOCK, dh)


def _nsa_attn(qh, ks, vs, kw, vw, sel):
    bsz, ng, hpg, s, dh = qh.shape
    n_sel = sel.shape[-1]
    kv_spec = pl.BlockSpec((1, 1, s, dh), lambda b, g, i: (b, g, 0, 0))
    q_spec = pl.BlockSpec((1, 1, hpg, Q_BLOCK, dh), lambda b, g, i: (b, g, 0, i, 0))
    out_sd = jax.ShapeDtypeStruct(qh.shape, jnp.float32)
    return pl.pallas_call(
        _nsa_attn_kernel,
        out_shape=(out_sd, out_sd),
        grid=(bsz, ng, s // Q_BLOCK),
        in_specs=[q_spec, kv_spec, kv_spec, kv_spec, kv_spec,
                  pl.BlockSpec((1, 1, Q_BLOCK, n_sel), lambda b, g, i: (b, g, i, 0))],
        out_specs=(q_spec, q_spec),
        compiler_params=pltpu.CompilerParams(
            dimension_semantics=("parallel", "parallel", "arbitrary"),
            vmem_limit_bytes=V7X_VMEM_LIMIT_BYTES),
        name="nsa_attn",
    )(qh, ks, vs, kw, vw, sel)


def _nsa_mixer(q, kc, vc, ks, vs, kw, vw, gates, positions, cmp_pos, cmp_w1, cmp_w2):
    bsz, s, _ = q.shape
    nh, ng, hpg, dh = NSA_HEADS, NSA_KV_HEADS, NSA_GROUP, NSA_DH
    inv = _rope_freqs(dh)
    scale = dh ** -0.5
    q = _rope(q.reshape(bsz, s, nh, dh), positions, inv)
    kv_shape = (bsz, s, ng, dh)
    kc, vc, vs, vw = kc.reshape(kv_shape), vc.reshape(kv_shape), vs.reshape(kv_shape), vw.reshape(kv_shape)
    ks = _rope(ks.reshape(kv_shape), positions, inv)
    kw = _rope(kw.reshape(kv_shape), positions, inv)

    n_cmp = (s - CMP_LEN) // CMP_STRIDE + 1
    starts = np.arange(n_cmp) * CMP_STRIDE
    tok = starts[:, None] + np.arange(CMP_LEN)[None, :]
    cmp_end = starts + CMP_LEN - 1

    def compress(t, pos_emb, w1, w2):
        blk = t[:, tok] + pos_emb[:, None, :]
        blk = blk.transpose(0, 1, 3, 2, 4).reshape(bsz, n_cmp, ng, CMP_LEN * dh)
        return jax.nn.gelu(blk @ w1) @ w2

    k_cmp = _rope(compress(kc, cmp_pos[0], cmp_w1[0], cmp_w2[0]), positions[:, cmp_end], inv)
    v_cmp = compress(vc, cmp_pos[1], cmp_w1[1], cmp_w2[1])
    qg = q.reshape(bsz, s, ng, hpg, dh)
    t_idx = np.arange(s)
    cmask = cmp_end[None, :] <= t_idx[:, None]
    s_cmp = jnp.einsum('bsgqd,bcgd->bgqsc', qg, k_cmp) * scale
    p_cmp = _masked_softmax(s_cmp, cmask)
    o_cmp = jnp.einsum('bgqsc,bcgd->bsgqd', p_cmp.astype(v_cmp.dtype), v_cmp).reshape(bsz, s, nh, dh)

    n_sel = s // SEL_BLOCK
    jb = np.arange(n_sel)
    overlap = ((starts[:, None] < (jb[None, :] + 1) * SEL_BLOCK)
               & (starts[:, None] + CMP_LEN > jb[None, :] * SEL_BLOCK)).astype(np.float32)
    imp = jnp.einsum('bgqsc,cj->bgsj', p_cmp, overlap)
    cur = t_idx // SEL_BLOCK
    valid = jb[None, :] <= cur[:, None]
    forced = (jb[None, :] == 0) | (jb[None, :] == cur[:, None]) | (jb[None, :] == cur[:, None] - 1)
    imp = jnp.where(forced, jnp.inf, jnp.where(valid, imp, -jnp.inf))
    n_top = min(SEL_TOP, n_sel)
    _, sel_idx = lax.top_k(imp, n_top)

    sel = jnp.any(sel_idx[..., :, None] == jnp.arange(n_sel, dtype=sel_idx.dtype), axis=-2)
    qh = qg.transpose(0, 2, 3, 1, 4) * scale
    to_group = lambda t: t.transpose(0, 2, 1, 3)
    o_sel, o_win = _nsa_attn(qh, to_group(ks), to_group(vs), to_group(kw), to_group(vw),
                             sel.astype(jnp.float32))
    ungroup = lambda o: o.transpose(0, 3, 1, 2, 4).reshape(bsz, s, nh, dh)
    o_sel, o_win = ungroup(o_sel), ungroup(o_win)
    gt = jax.nn.sigmoid(gates.reshape(bsz, s, nh, 3))
    o = gt[..., 0:1] * o_cmp + gt[..., 1:2] * o_sel + gt[..., 2:3] * o_win
    return o.reshape(bsz, s, nh * dh)


def _hybrid_mixer(h, positions, bsz, ret_tabs, w_in, sc_conv, rw_mu, rw_w0, rw_w2, rw_a0, rw_a2, rw_g2,
                  rw_k_k, rw_k_a, rw_r_k, rw_lnx_g, rw_lnx_b, nsa_cmp_pos, nsa_cmp_w1, nsa_cmp_w2,
                  w_branch):
    m = h.shape[0]
    s = m // bsz
    offs = np.cumsum((0,) + GROUP_SIZES)
    group = lambda i: _mm(h, w_in[:, offs[i]:offs[i + 1]], keep_pad=True)
    per_batch = lambda z: z.reshape(bsz, s, z.shape[1])
    o_a = _short_conv_mixer(group(0), sc_conv, s)
    o_b = _rwkv7_mixer(per_batch(group(1)), rw_mu, rw_w0, rw_w2, rw_a0, rw_a2, rw_g2, rw_k_k, rw_k_a, rw_r_k,
                       rw_lnx_g, rw_lnx_b)
    o_c = _retention_mixer(per_batch(group(2)), *ret_tabs)
    z_nsa = per_batch(group(3))[..., :GROUP_SIZES[3]]
    o_d = _nsa_mixer(*_split(z_nsa, NSA_SIZES), positions, nsa_cmp_pos, nsa_cmp_w1, nsa_cmp_w2)
    outs = [o.reshape(m, o.shape[-1]) for o in (o_a, o_b, o_c, o_d)]
    w_gate = w_in[:, offs[4]:offs[5]].astype(jnp.bfloat16)
    return _merge(h, w_gate, outs, _split(w_branch, BR_SIZES, axis=0))


def kernel(x, mem, positions, ln_mix_pre, ln_mix_post, ln_mem_q, ln_mem_kv, ln_mem_post, ln_ffn_pre, ln_ffn_post, w_in, sc_conv, rw_mu, rw_w0, rw_w2, rw_a0, rw_a2, rw_g2, rw_k_k, rw_k_a, rw_r_k, rw_lnx_g, rw_lnx_b, nsa_cmp_pos, nsa_cmp_w1, nsa_cmp_w2, w_branch, w_out, mem_wq, mem_wkv, mem_wo, ffn_w_up, ffn_conv_w, ffn_conv_b, ffn_w_down):
    bsz, s, d = x.shape
    mem_len = mem.shape[1]
    xf = x.reshape(bsz * s, d)
    h = _prenorm(xf, ln_mix_pre[0])
    ret_tabs = _rope_tables(positions, _retnet_freqs(RET_DK))
    for l in range(DEPTH):
        merged = _hybrid_mixer(h, positions, bsz, ret_tabs, w_in[l], sc_conv[l], rw_mu[l], rw_w0[l], rw_w2[l], rw_a0[l],
                               rw_a2[l], rw_g2[l], rw_k_k[l], rw_k_a[l], rw_r_k[l], rw_lnx_g[l], rw_lnx_b[l],
                               nsa_cmp_pos[l], nsa_cmp_w1[l], nsa_cmp_w2[l], w_branch[l])
        xf, h = _mm_norm_res(merged, w_out[l], xf, ln_mix_post[l], ln_mem_q[l])
        mem_n = _prenorm(mem.reshape(bsz * mem_len, d), ln_mem_kv[l])
        kv = _mm(mem_n, mem_wkv[l]).astype(jnp.bfloat16)
        k_mem = kv[:, :d].reshape(bsz, mem_len, d)
        v_mem = kv[:, d:].reshape(bsz, mem_len, d)
        o = _xattn(h, mem_wq[l], k_mem, v_mem, s)
        xf, h = _mm_norm_res(o, mem_wo[l], xf, ln_mem_post[l], ln_ffn_pre[l])
        act = _ffn_up(h, ffn_w_up[l], ffn_conv_w[l], ffn_conv_b[l], s)
        g_next = ln_mix_pre[l + 1] if l + 1 < DEPTH else None
        xf, h = _mm_norm_res(act, ffn_w_down[l], xf, ln_ffn_post[l], g_next)
    return xf.reshape(bsz, s, d)
```

```python
import functools
import math

import jax
import jax.numpy as jnp
import numpy as np
from jax import lax
from jax.experimental import pallas as pl
from jax.experimental.pallas import tpu as pltpu

D_MODEL = 2048
DEPTH = 2
MEM_HEADS = 4
MEM_HEAD_DIM = D_MODEL // MEM_HEADS
SC_W = D_MODEL // 4
SC_K = 3
RW_HEAD_DIM = 64
RW_W = D_MODEL // 4
RW_HEADS = RW_W // RW_HEAD_DIM
RW_LORA_W = 96
RW_LORA_A = 96
RW_LORA_G = 256
RW_DECAY_SCALE = math.exp(-0.5)
RW_LN_EPS = 64e-5
RET_HEADS = 4
RET_DK = 128
RET_DV = 2 * RET_DK
RET_CHUNK = 128
RET_LN_EPS = 1e-5
NSA_HEADS = 8
NSA_KV_HEADS = 2
NSA_DH = 64
NSA_GROUP = NSA_HEADS // NSA_KV_HEADS
CMP_LEN = 32
CMP_STRIDE = 16
CMP_HIDDEN = 256
SEL_BLOCK = 64
SEL_TOP = 16
WINDOW = 512
Q_BLOCK = 128
FFN_DIM = 5632
ROPE_THETA = 10000.0
EPS = 1e-6
N_BRANCH = 4

SC_SIZES = (SC_W, SC_W, SC_W)
RW_SIZES = (RW_W, RW_W, RW_W, RW_LORA_W, RW_LORA_A, RW_LORA_G)
RET_SIZES = (RET_HEADS * RET_DK, RET_HEADS * RET_DK, RET_HEADS * RET_DV, RET_HEADS * RET_DV)
NSA_SIZES = (NSA_HEADS * NSA_DH,) + (NSA_KV_HEADS * NSA_DH,) * 6 + (NSA_HEADS * 3,)
GATE_SIZES = (D_MODEL,) * N_BRANCH
GROUP_SIZES = (sum(SC_SIZES), sum(RW_SIZES), sum(RET_SIZES), sum(NSA_SIZES), sum(GATE_SIZES))
BR_SIZES = (SC_W, RW_W, RET_HEADS * RET_DV, NSA_HEADS * NSA_DH)

V7X_VMEM_LIMIT_BYTES = 48 * 1024 * 1024


def _split(x, sizes, axis=-1):
    return jnp.split(x, [int(c) for c in np.cumsum(sizes)[:-1]], axis=axis)


def _mm_kernel(a_ref, w_ref, o_ref, acc_ref):
    k = pl.program_id(2)

    @pl.when(k == 0)
    def _():
        acc_ref[...] = jnp.zeros_like(acc_ref)

    acc_ref[...] += jnp.dot(a_ref[...], w_ref[...], preferred_element_type=jnp.float32)

    @pl.when(k == pl.num_programs(2) - 1)
    def _():
        o_ref[...] = acc_ref[...]


def _pick_tile(n, candidates):
    for c in candidates:
        if n % c == 0:
            return c
    return n


LANES = 128


def _mm(a, w, keep_pad=False):
    m, k = a.shape
    n = w.shape[1]
    a = a.astype(jnp.bfloat16)
    w = w.astype(jnp.bfloat16)
    n_pad = LANES * ((n + LANES - 1) // LANES)
    tn = n_pad if n_pad <= 1536 else _pick_tile(n_pad, (1024, 768, 512, 256, 128))
    if n_pad != n:
        w = jnp.pad(w, ((0, 0), (0, n_pad - n)))
    tm = _pick_tile(m, (1024, 512, 256, 128, 8))
    tk = _pick_tile(k, (2048, 1280, 1408, 1024, 512, 256, 128))
    out = pl.pallas_call(
        _mm_kernel,
        out_shape=jax.ShapeDtypeStruct((m, n_pad), jnp.float32),
        grid=(m // tm, n_pad // tn, k // tk),
        in_specs=[pl.BlockSpec((tm, tk), lambda i, j, kk: (i, kk)),
                  pl.BlockSpec((tk, tn), lambda i, j, kk: (kk, j))],
        out_specs=pl.BlockSpec((tm, tn), lambda i, j, kk: (i, j)),
        scratch_shapes=[pltpu.VMEM((tm, tn), jnp.float32)],
        compiler_params=pltpu.CompilerParams(
            dimension_semantics=("parallel", "parallel", "arbitrary"),
            vmem_limit_bytes=V7X_VMEM_LIMIT_BYTES),
        name="mm",
    )(a, w)
    return out if keep_pad or n_pad == n else out[:, :n]


def _proj(x, w):
    lead = x.shape[:-1]
    return _mm(x.reshape(-1, x.shape[-1]), w).reshape(*lead, w.shape[1])


def _rms(y, g):
    return y * lax.rsqrt(jnp.mean(y * y, axis=-1, keepdims=True) + EPS) * g


def _prenorm_kernel(x_ref, g_ref, h_ref):
    h_ref[...] = _rms(x_ref[...], g_ref[...]).astype(h_ref.dtype)


def _prenorm(x, g):
    m, d = x.shape
    tm = _pick_tile(m, (512, 256, 128, 8))
    return pl.pallas_call(
        _prenorm_kernel,
        out_shape=jax.ShapeDtypeStruct((m, d), jnp.bfloat16),
        grid=(m // tm,),
        in_specs=[pl.BlockSpec((tm, d), lambda i: (i, 0)), pl.BlockSpec((1, d), lambda i: (0, 0))],
        out_specs=pl.BlockSpec((tm, d), lambda i: (i, 0)),
        compiler_params=pltpu.CompilerParams(dimension_semantics=("parallel",)),
        name="prenorm",
    )(x, g.reshape(1, d))


def _mm_norm_res_kernel(emit_h, a_ref, w_ref, res_ref, g_ref, *rest):
    if emit_h:
        g2_ref, x_ref, h_ref, acc_ref = rest
    else:
        x_ref, acc_ref = rest
    k = pl.program_id(1)

    @pl.when(k == 0)
    def _():
        acc_ref[...] = jnp.zeros_like(acc_ref)

    acc_ref[...] += jnp.dot(a_ref[...], w_ref[...], preferred_element_type=jnp.float32)

    @pl.when(k == pl.num_programs(1) - 1)
    def _():
        x_new = res_ref[...] + _rms(acc_ref[...], g_ref[...])
        x_ref[...] = x_new
        if emit_h:
            h_ref[...] = _rms(x_new, g2_ref[...]).astype(h_ref.dtype)


def _mm_norm_res(a, w, res, g, g_next):
    m, k = a.shape
    d = w.shape[1]
    tk = _pick_tile(k, (2048, 1408, 1024, 512))
    tm = 256 if tk == k else 512
    emit_h = g_next is not None
    row = pl.BlockSpec((tm, d), lambda i, kk: (i, 0))
    vec = pl.BlockSpec((1, d), lambda i, kk: (0, 0))
    x_sd = jax.ShapeDtypeStruct((m, d), jnp.float32)
    h_sd = jax.ShapeDtypeStruct((m, d), jnp.bfloat16)
    out = pl.pallas_call(
        functools.partial(_mm_norm_res_kernel, emit_h),
        out_shape=(x_sd, h_sd) if emit_h else x_sd,
        grid=(m // tm, k // tk),
        in_specs=[pl.BlockSpec((tm, tk), lambda i, kk: (i, kk)),
                  pl.BlockSpec((tk, d), lambda i, kk: (kk, 0)),
                  row, vec] + ([vec] if emit_h else []),
        out_specs=(row, row) if emit_h else row,
        scratch_shapes=[pltpu.VMEM((tm, d), jnp.float32)],
        compiler_params=pltpu.CompilerParams(
            dimension_semantics=("parallel", "arbitrary"), vmem_limit_bytes=V7X_VMEM_LIMIT_BYTES),
        name="mm_norm_res",
    )(a.astype(jnp.bfloat16), w.astype(jnp.bfloat16), res, g.reshape(1, d),
      *([g_next.reshape(1, d)] if emit_h else []))
    return out if emit_h else (out, None)


FFN_TM = 1024
FFN_TN = 512
HALO = 8


def _shift_rows(u, prev, n):
    rolled = pltpu.roll(u, n, axis=0)
    rows = lax.broadcasted_iota(jnp.int32, u.shape, 0)
    for r in range(n):
        rolled = jnp.where(rows == r, prev[HALO - n + r:HALO - n + r + 1, :], rolled)
    return rolled


def _ffn_up_kernel(seq_len, h_ref, halo_ref, wa_ref, wb_ref, cwa_ref, cwb_ref, ba_ref, bb_ref, o_ref):
    i = pl.program_id(0)
    h = h_ref[...]
    at_start = (i * FFN_TM) % seq_len == 0
    halo = jnp.where(at_start, jnp.zeros_like(halo_ref[...]), halo_ref[...])

    def conv(w_ref, cw_ref, b_ref):
        u = jnp.dot(h, w_ref[...], preferred_element_type=jnp.float32)
        up = jnp.dot(halo, w_ref[...], preferred_element_type=jnp.float32)
        cw = cw_ref[...]
        y = cw[0:1] * _shift_rows(u, up, 2)
        y = y + cw[1:2] * _shift_rows(u, up, 1)
        y = y + cw[2:3] * u
        return y + b_ref[...]

    a = conv(wa_ref, cwa_ref, ba_ref)
    b = conv(wb_ref, cwb_ref, bb_ref)
    o_ref[...] = (jax.nn.gelu(a, approximate=True) * b).astype(o_ref.dtype)


def _ffn_up(h, w_up, w_conv, b_conv, seq_len):
    m, d = h.shape
    f = w_up.shape[1] // 2
    nj = f // FFN_TN
    w_up = w_up.astype(jnp.bfloat16)
    b_conv = b_conv.reshape(1, 2 * f)
    tiles_per_halo = FFN_TM // HALO
    return pl.pallas_call(
        functools.partial(_ffn_up_kernel, seq_len),
        out_shape=jax.ShapeDtypeStruct((m, f), jnp.bfloat16),
        grid=(m // FFN_TM, nj),
        in_specs=[pl.BlockSpec((FFN_TM, d), lambda i, j: (i, 0)),
                  pl.BlockSpec((HALO, d), lambda i, j: (jnp.maximum(i * tiles_per_halo - 1, 0), 0)),
                  pl.BlockSpec((d, FFN_TN), lambda i, j: (0, j)),
                  pl.BlockSpec((d, FFN_TN), lambda i, j: (0, j + nj)),
                  pl.BlockSpec((3, FFN_TN), lambda i, j: (0, j)),
                  pl.BlockSpec((3, FFN_TN), lambda i, j: (0, j + nj)),
                  pl.BlockSpec((1, FFN_TN), lambda i, j: (0, j)),
                  pl.BlockSpec((1, FFN_TN), lambda i, j: (0, j + nj))],
        out_specs=pl.BlockSpec((FFN_TM, FFN_TN), lambda i, j: (i, j)),
        compiler_params=pltpu.CompilerParams(
            dimension_semantics=("parallel", "arbitrary"), vmem_limit_bytes=V7X_VMEM_LIMIT_BYTES),
        name="ffn_up",
    )(h, h, w_up, w_up, w_conv, w_conv, b_conv, b_conv)


MERGE_TM = 512
MERGE_TN = 512


def _merge_kernel(h_ref, wg0, wg1, wg2, wg3, o0, o1, o2, o3, wb0, wb1, wb2, wb3, out_ref):
    h = h_ref[...]
    acc = None
    for wg, o, wb in ((wg0, o0, wb0), (wg1, o1, wb1), (wg2, o2, wb2), (wg3, o3, wb3)):
        gate = jax.nn.sigmoid(jnp.dot(h, wg[...], preferred_element_type=jnp.float32))
        term = gate * jnp.dot(o[...], wb[...], preferred_element_type=jnp.float32)
        acc = term if acc is None else acc + term
    out_ref[...] = acc.astype(out_ref.dtype)


def _merge(h, w_gate, outs, w_branches):
    m, d = h.shape
    nj = d // MERGE_TN
    gate_specs = [pl.BlockSpec((d, MERGE_TN), functools.partial(lambda b, i, j: (0, b * nj + j), b))
                  for b in range(N_BRANCH)]
    o_specs = [pl.BlockSpec((MERGE_TM, o.shape[1]), lambda i, j: (i, 0)) for o in outs]
    wb_specs = [pl.BlockSpec((wb.shape[0], MERGE_TN), lambda i, j: (0, j)) for wb in w_branches]
    return pl.pallas_call(
        _merge_kernel,
        out_shape=jax.ShapeDtypeStruct((m, d), jnp.bfloat16),
        grid=(m // MERGE_TM, nj),
        in_specs=[pl.BlockSpec((MERGE_TM, d), lambda i, j: (i, 0))] + gate_specs + o_specs + wb_specs,
        out_specs=pl.BlockSpec((MERGE_TM, MERGE_TN), lambda i, j: (i, j)),
        compiler_params=pltpu.CompilerParams(
            dimension_semantics=("parallel", "arbitrary"), vmem_limit_bytes=V7X_VMEM_LIMIT_BYTES),
        name="merge",
    )(h, *([w_gate] * N_BRANCH), *[o.astype(jnp.bfloat16) for o in outs],
      *[wb.astype(jnp.bfloat16) for wb in w_branches])


XATT_TM = 512


def _xattn_kernel(h_ref, wq_ref, k_ref, v_ref, o_ref):
    q = jnp.dot(h_ref[...], wq_ref[...], preferred_element_type=jnp.float32)
    q = (q * (MEM_HEAD_DIM ** -0.5)).astype(jnp.bfloat16)
    for hh in range(MEM_HEADS):
        sl = slice(hh * MEM_HEAD_DIM, (hh + 1) * MEM_HEAD_DIM)
        s = lax.dot_general(q[:, sl], k_ref[0, :, sl], _NT, preferred_element_type=jnp.float32)
        p = jnp.exp(s - jnp.max(s, axis=-1, keepdims=True))
        p = p / jnp.sum(p, axis=-1, keepdims=True)
        o_ref[:, sl] = jnp.dot(p.astype(jnp.bfloat16), v_ref[0, :, sl],
                               preferred_element_type=jnp.float32).astype(o_ref.dtype)


def _xattn(h, wq, k, v, seq_len):
    m, d = h.shape
    mem_len = k.shape[1]
    per_batch = seq_len // XATT_TM
    kv_spec = pl.BlockSpec((1, mem_len, d), lambda i: (i // per_batch, 0, 0))
    return pl.pallas_call(
        _xattn_kernel,
        out_shape=jax.ShapeDtypeStruct((m, d), jnp.bfloat16),
        grid=(m // XATT_TM,),
        in_specs=[pl.BlockSpec((XATT_TM, d), lambda i: (i, 0)),
                  pl.BlockSpec((d, d), lambda i: (0, 0)), kv_spec, kv_spec],
        out_specs=pl.BlockSpec((XATT_TM, d), lambda i: (i, 0)),
        compiler_params=pltpu.CompilerParams(
            dimension_semantics=("parallel",), vmem_limit_bytes=V7X_VMEM_LIMIT_BYTES),
        name="xattn",
    )(h, wq.astype(jnp.bfloat16), k, v)


def _rms_norm(x, g):
    y = x * lax.rsqrt(jnp.mean(x * x, axis=-1, keepdims=True) + EPS)
    return y * g


def _head_norm(x, eps):
    mu = jnp.mean(x, axis=-1, keepdims=True)
    var = jnp.mean(jnp.square(x - mu), axis=-1, keepdims=True)
    return (x - mu) * lax.rsqrt(var + eps)


def _causal_dwconv(x, w):
    k_w = w.shape[0]
    s = x.shape[1]
    xp = jnp.pad(x, ((0, 0), (k_w - 1, 0), (0, 0)))
    y = w[0] * xp[:, 0:s]
    for j in range(1, k_w):
        y = y + w[j] * xp[:, j:j + s]
    return y


def _rope_freqs(d):
    return ROPE_THETA ** (-jnp.arange(0, d, 2, dtype=jnp.float32) / d)


def _retnet_freqs(d):
    return 1.0 / (ROPE_THETA ** jnp.linspace(0.0, 1.0, d // 2, dtype=jnp.float32))


def _rope(x, pos, inv_freq):
    ang = pos.astype(jnp.float32)[..., None] * inv_freq
    cos = jnp.cos(ang)[:, :, None, :].astype(x.dtype)
    sin = jnp.sin(ang)[:, :, None, :].astype(x.dtype)
    x1, x2 = jnp.split(x, 2, axis=-1)
    return jnp.concatenate([x1 * cos - x2 * sin, x2 * cos + x1 * sin], axis=-1)


def _masked_softmax(s, mask):
    s = jnp.where(mask, s.astype(jnp.float32), -jnp.inf)
    m = jnp.max(s, axis=-1, keepdims=True)
    m = jnp.where(jnp.isfinite(m), m, 0.0)
    e = jnp.where(mask, jnp.exp(s - m), 0.0)
    return e / jnp.maximum(jnp.sum(e, axis=-1, keepdims=True), 1e-30)


WKV_CHUNK = 64
_NT = (((1,), (1,)), ((), ()))


def _bdot(a, b):
    return jnp.dot(a.astype(jnp.bfloat16), b.astype(jnp.bfloat16), preferred_element_type=jnp.float32)


def _bdot_nt(a, b):
    return lax.dot_general(a.astype(jnp.bfloat16), b.astype(jnp.bfloat16), _NT,
                           preferred_element_type=jnp.float32)


def _head_sum(x, bd):
    hi = x.astype(jnp.bfloat16)
    lo = (x - hi.astype(jnp.float32)).astype(jnp.bfloat16)
    return (jnp.dot(hi, bd, preferred_element_type=jnp.float32)
            + jnp.dot(lo, bd, preferred_element_type=jnp.float32))


def _rwkv7_kernel(z_ref, mu_ref, w0_ref, a0_ref, kk_ref, ka_ref, rk_ref, lng_ref, lnb_ref,
                  ww2_ref, wa2_ref, wg2_ref, o_ref, s_ref, prev_ref, y_ref):
    c = WKV_CHUNK
    hd = RW_HEAD_DIM
    w = RW_W
    f32, bf16 = jnp.float32, jnp.bfloat16

    @pl.when(pl.program_id(1) == 0)
    def _():
        s_ref[...] = jnp.zeros_like(s_ref)
        prev_ref[...] = jnp.zeros_like(prev_ref)

    z = z_ref[0]
    rows = lax.broadcasted_iota(jnp.int32, z.shape, 0)
    z_prev = jnp.where(rows == 0, prev_ref[...], pltpu.roll(z, 1, axis=0))
    prev_ref[...] = z[c - 1:c, :]
    z = z + (z_prev - z) * mu_ref[...]
    r, k, v, lora = z[:, 0:w], z[:, w:2 * w], z[:, 2 * w:3 * w], z[:, 3 * w:4 * w]

    logw = -RW_DECAY_SCALE * jax.nn.sigmoid(w0_ref[...] + _bdot(jnp.tanh(lora), ww2_ref[...]))
    a = jax.nn.sigmoid(a0_ref[...] + _bdot(lora, wa2_ref[...]))
    g = _bdot(jax.nn.sigmoid(lora), wg2_ref[...])

    lane_h = lax.broadcasted_iota(jnp.int32, (w, w), 0) // hd
    lane_w = lax.broadcasted_iota(jnp.int32, (w, w), 1) // hd
    head_bd = jnp.where(lane_h == lane_w, 1.0, 0.0).astype(bf16)

    kk = k * kk_ref[...]
    kk = kk / jnp.maximum(jnp.sqrt(_head_sum(kk * kk, head_bd)), 1e-12)
    k = k * (1.0 + (a - 1.0) * ka_ref[...])

    row = lax.broadcasted_iota(jnp.int32, (c, c), 0)
    col = lax.broadcasted_iota(jnp.int32, (c, c), 1)
    tri = jnp.where(row >= col, 1.0, 0.0).astype(bf16)
    hi = logw.astype(bf16)
    rem = logw - hi.astype(f32)
    mid = rem.astype(bf16)
    lo = (rem - mid.astype(f32)).astype(bf16)
    cum = (jnp.dot(tri, hi, preferred_element_type=f32) + jnp.dot(tri, mid, preferred_element_type=f32)
           + jnp.dot(tri, lo, preferred_element_type=f32))
    kka = kk * a
    e_neg = jnp.exp(-cum)
    qh = (kk * jnp.exp(cum - logw)).astype(bf16)
    rh = (r * jnp.exp(cum)).astype(bf16)
    bh = (kka * e_neg).astype(bf16)
    kh = (k * e_neg).astype(bf16)
    last = cum[c - 1:c, :]
    dec = jnp.exp(last - cum)
    bd = (kka * dec).astype(bf16)
    kd = (k * dec).astype(bf16)
    g_last = jnp.exp(last)
    vb = v.astype(bf16)

    r2 = lax.broadcasted_iota(jnp.int32, (2 * c, 2 * c), 0)
    c2 = lax.broadcasted_iota(jnp.int32, (2 * c, 2 * c), 1)
    rr = jnp.where(r2 >= c, r2 - c, r2)
    cc = jnp.where(c2 >= c, c2 - c, c2)
    tri_mask = cc < rr + jnp.where(r2 >= c, 1, 0)
    eye = jnp.where(row == col, 1.0, 0.0).astype(f32)

    heads = range(RW_HEADS)
    sls = [slice(h * hd, (h + 1) * hd) for h in heads]
    a1 = [jnp.concatenate([qh[:, sl], rh[:, sl]], axis=0) for sl in sls]
    b1 = [jnp.concatenate([bh[:, sl], kh[:, sl]], axis=0) for sl in sls]
    ss = [jnp.where(tri_mask, lax.dot_general(a1[h], b1[h], _NT, preferred_element_type=f32), 0.0)
          for h in heads]
    s_old = [s_ref[h] for h in heads]
    qr = [lax.dot_general(a1[h], s_old[h].astype(bf16), _NT, preferred_element_type=f32) for h in heads]
    lm = [_bdot(ss[h][:, c:2 * c], vb[:, sls[h]]) for h in heads]
    n = [-ss[h][0:c, 0:c] for h in heads]
    t = [eye + n[h] for h in heads]
    for _ in range(5):
        n = [_bdot(n[h], n[h]) for h in heads]
        t = [t[h] + _bdot(t[h], n[h]) for h in heads]
    u = [-_bdot(t[h], qr[h][0:c] + lm[h][0:c]) for h in heads]
    for h in heads:
        y_ref[:, sls[h]] = qr[h][c:2 * c] + lm[h][c:2 * c] + _bdot(ss[h][c:2 * c, 0:c], u[h])
    for h in heads:
        zt = jnp.concatenate([u[h], v[:, sls[h]]], axis=0).T
        x = jnp.concatenate([bd[:, sls[h]], kd[:, sls[h]]], axis=0)
        s_ref[h] = s_old[h] * g_last[:, sls[h]] + _bdot(zt, x)

    y = y_ref[...]
    mean = _head_sum(y, head_bd) * (1.0 / hd)
    yc = y - mean
    var = _head_sum(yc * yc, head_bd) * (1.0 / hd)
    yn = yc * lax.rsqrt(var + RW_LN_EPS) * lng_ref[...] + lnb_ref[...]
    bonus = _head_sum(r * k * rk_ref[...], head_bd) * v
    o_ref[0] = ((yn + bonus) * g).astype(o_ref.dtype)


def _rwkv7_mixer(z, mu, w0, w_w2, a0, w_a2, w_g2, k_k, k_a, r_k, lnx_g, lnx_b):
    bsz, s, zw = z.shape
    w = RW_W
    n_lora = RW_LORA_W + RW_LORA_A + RW_LORA_G
    row = lambda t: t.reshape(1, -1)
    mu = jnp.pad(mu, (0, zw - mu.shape[0])).reshape(1, zw)
    o_w, o_a = RW_LORA_W, RW_LORA_W + RW_LORA_A
    pad_rows = lambda m, lo: jnp.pad(m, ((lo, w - lo - m.shape[0]), (0, 0))).astype(jnp.bfloat16)
    vec = pl.BlockSpec((1, w), lambda b, c: (0, 0))
    mat = pl.BlockSpec((w, w), lambda b, c: (0, 0))
    assert zw == 4 * w and n_lora <= w
    return pl.pallas_call(
        _rwkv7_kernel,
        out_shape=jax.ShapeDtypeStruct((bsz, s, w), jnp.bfloat16),
        grid=(bsz, s // WKV_CHUNK),
        in_specs=[pl.BlockSpec((1, WKV_CHUNK, zw), lambda b, c: (b, c, 0)),
                  pl.BlockSpec((1, zw), lambda b, c: (0, 0))] + [vec] * 7 + [mat] * 3,
        out_specs=pl.BlockSpec((1, WKV_CHUNK, w), lambda b, c: (b, c, 0)),
        scratch_shapes=[pltpu.VMEM((RW_HEADS, RW_HEAD_DIM, RW_HEAD_DIM), jnp.float32),
                        pltpu.VMEM((1, zw), jnp.float32),
                        pltpu.VMEM((WKV_CHUNK, w), jnp.float32)],
        compiler_params=pltpu.CompilerParams(dimension_semantics=("parallel", "arbitrary")),
        name="rwkv7",
    )(z, mu, row(w0), row(a0), row(k_k), row(k_a), row(r_k), row(lnx_g), row(lnx_b),
      pad_rows(w_w2, 0), pad_rows(w_a2, o_w), pad_rows(w_g2, o_a))


RET_LOG_DECAY = tuple(math.log(1.0 - 2.0 ** (-5.0 - h)) for h in range(RET_HEADS))


def _retention_kernel(q_ref, k_ref, v_ref, g_ref, cos_ref, sin_ref, o_ref, s_ref):
    c, dk, dv = RET_CHUNK, RET_DK, RET_DV
    f32 = jnp.float32

    @pl.when(pl.program_id(1) == 0)
    def _():
        s_ref[...] = jnp.zeros_like(s_ref)

    cos = cos_ref[0]
    sin = sin_ref[0]
    diff = (lax.broadcasted_iota(jnp.int32, (c, c), 0) - lax.broadcasted_iota(jnp.int32, (c, c), 1)).astype(f32)
    tok = lax.broadcasted_iota(jnp.int32, (c, 1), 0).astype(f32)
    rope = lambda x: x * cos + pltpu.roll(x, dk // 2, axis=1) * sin
    for h in range(RET_HEADS):
        lg = RET_LOG_DECAY[h]
        q = rope(q_ref[0, :, h * dk:(h + 1) * dk])
        k = rope(k_ref[0, :, h * dk:(h + 1) * dk]) * (dk ** -0.5)
        v = v_ref[0, :, h * dv:(h + 1) * dv]
        decay_in = jnp.where(diff >= 0, jnp.exp(jnp.maximum(diff, 0.0) * lg), 0.0)
        scores = _bdot_nt(q, k) * decay_in
        state = s_ref[h]
        o = _bdot(scores, v) + _bdot(q * jnp.exp((tok + 1.0) * lg), state)
        s_ref[h] = state * math.exp(c * lg) + _bdot((k * jnp.exp((c - 1.0 - tok) * lg)).T, v)
        mu = jnp.mean(o, axis=-1, keepdims=True)
        oc = o - mu
        var = jnp.mean(oc * oc, axis=-1, keepdims=True)
        gate = g_ref[0, :, h * dv:(h + 1) * dv]
        o_ref[0, :, h * dv:(h + 1) * dv] = (oc * lax.rsqrt(var + RET_LN_EPS) * gate * jax.nn.sigmoid(gate)
                                            ).astype(o_ref.dtype)


def _retention_mixer(z, cos, sin):
    bsz, s, _ = z.shape
    qk_w, v_w = RET_HEADS * RET_DK, RET_HEADS * RET_DV
    c = RET_CHUNK
    tab = pl.BlockSpec((1, c, RET_DK), lambda b, i: (b, i, 0))
    return pl.pallas_call(
        _retention_kernel,
        out_shape=jax.ShapeDtypeStruct((bsz, s, v_w), jnp.bfloat16),
        grid=(bsz, s // c),
        in_specs=[pl.BlockSpec((1, c, qk_w), lambda b, i: (b, i, 0)),
                  pl.BlockSpec((1, c, qk_w), lambda b, i: (b, i, 1)),
                  pl.BlockSpec((1, c, v_w), lambda b, i: (b, i, 1)),
                  pl.BlockSpec((1, c, v_w), lambda b, i: (b, i, 2)), tab, tab],
        out_specs=pl.BlockSpec((1, c, v_w), lambda b, i: (b, i, 0)),
        scratch_shapes=[pltpu.VMEM((RET_HEADS, RET_DK, RET_DV), jnp.float32)],
        compiler_params=pltpu.CompilerParams(dimension_semantics=("parallel", "arbitrary")),
        name="retention",
    )(z, z, z, z, cos, sin)


def _rope_tables(positions, inv_freq):
    ang = positions.astype(jnp.float32)[..., None] * inv_freq
    cos, sin = jnp.cos(ang), jnp.sin(ang)
    return jnp.concatenate([cos, cos], axis=-1), jnp.concatenate([-sin, sin], axis=-1)


SC_TM = 512


def _short_conv_kernel(seq_len, b_ref, c_ref, x_ref, ch_ref, xh_ref, w_ref, o_ref):
    at_start = (pl.program_id(0) * SC_TM) % seq_len == 0
    u = c_ref[...] * x_ref[...]
    up = jnp.where(at_start, 0.0, ch_ref[...] * xh_ref[...])
    w = w_ref[...]
    y = w[0:1] * _shift_rows(u, up, 2) + w[1:2] * _shift_rows(u, up, 1) + w[2:3] * u
    o_ref[...] = (b_ref[...] * y).astype(o_ref.dtype)


def _short_conv_mixer(z, w_conv, seq_len):
    m = z.shape[0]
    w = SC_W
    per_halo = SC_TM // HALO
    tile = lambda j: pl.BlockSpec((SC_TM, w), lambda i: (i, j))
    halo = lambda j: pl.BlockSpec((HALO, w), lambda i: (jnp.maximum(i * per_halo - 1, 0), j))
    return pl.pallas_call(
        functools.partial(_short_conv_kernel, seq_len),
        out_shape=jax.ShapeDtypeStruct((m, w), jnp.bfloat16),
        grid=(m // SC_TM,),
        in_specs=[tile(0), tile(1), tile(2), halo(1), halo(2), pl.BlockSpec((SC_K, w), lambda i: (0, 0))],
        out_specs=pl.BlockSpec((SC_TM, w), lambda i: (i, 0)),
        compiler_params=pltpu.CompilerParams(dimension_semantics=("parallel",)),
        name="short_conv",
    )(z, z, z, z, z, w_conv)


NSA_KEY_TILE = 512
NSA_WIN_SPAN = WINDOW + Q_BLOCK
NSA_PREP_TM = 512
NEG_BIG = -1e30
NSA_Q_W = NSA_HEADS * NSA_DH
NSA_KV_W = NSA_KV_HEADS * NSA_DH
NSA_GATE_OFF = NSA_Q_W + 6 * NSA_KV_W


def _rope_lanes(x, cos, sin):
    width = x.shape[1]
    half = NSA_DH // 2
    lane = lax.broadcasted_iota(jnp.int32, x.shape, 1)
    other = jnp.where(lane % NSA_DH < half, pltpu.roll(x, width - half, axis=1), pltpu.roll(x, half, axis=1))
    return x * cos + other * sin


def _nsa_prep_kernel(z_ref, cos_ref, sin_ref, q_ref, ks_ref, vs_ref, kw_ref, vw_ref, kc_ref, vc_ref,
                     gc_ref, gs_ref, gw_ref):
    f32 = jnp.float32
    cos, sin = cos_ref[0], sin_ref[0]
    z = z_ref[0]
    q_ref[0] = (_rope_lanes(z[:, 0:NSA_Q_W], cos, sin) * (NSA_DH ** -0.5)).astype(q_ref.dtype)
    kv = lambda i: z[:, NSA_Q_W + i * NSA_KV_W:NSA_Q_W + (i + 1) * NSA_KV_W]
    cos_kv, sin_kv = cos[:, 0:NSA_KV_W], sin[:, 0:NSA_KV_W]
    pieces = ((kc_ref, kv(0)), (vc_ref, kv(1)), (ks_ref, _rope_lanes(kv(2), cos_kv, sin_kv)), (vs_ref, kv(3)),
              (kw_ref, _rope_lanes(kv(4), cos_kv, sin_kv)), (vw_ref, kv(5)))
    for ref, val in pieces:
        for g in range(NSA_KV_HEADS):
            ref[0, g] = val[:, g * NSA_DH:(g + 1) * NSA_DH].astype(ref.dtype)
    gate = jax.nn.sigmoid(z[:, NSA_GATE_OFF:NSA_GATE_OFF + LANES])
    src = lax.broadcasted_iota(jnp.int32, (LANES, NSA_Q_W), 0)
    head = lax.broadcasted_iota(jnp.int32, (LANES, NSA_Q_W), 1) // NSA_DH
    for j, ref in enumerate((gc_ref, gs_ref, gw_ref)):
        expand = jnp.where(src == 3 * head + j, 1.0, 0.0).astype(jnp.bfloat16)
        ref[0] = _head_sum(gate, expand)


def _nsa_prep(z, cos, sin):
    bsz, s, zw = z.shape
    tm = NSA_PREP_TM
    wide = pl.BlockSpec((1, tm, NSA_Q_W), lambda b, i: (b, i, 0))
    grp = pl.BlockSpec((1, NSA_KV_HEADS, tm, NSA_DH), lambda b, i: (b, 0, i, 0))
    grp_sd = lambda dt: jax.ShapeDtypeStruct((bsz, NSA_KV_HEADS, s, NSA_DH), dt)
    wide_sd = lambda dt: jax.ShapeDtypeStruct((bsz, s, NSA_Q_W), dt)
    bf16, f32 = jnp.bfloat16, jnp.float32
    return pl.pallas_call(
        _nsa_prep_kernel,
        out_shape=(wide_sd(bf16), grp_sd(bf16), grp_sd(bf16), grp_sd(bf16), grp_sd(bf16), grp_sd(f32), grp_sd(f32),
                   wide_sd(f32), wide_sd(f32), wide_sd(f32)),
        grid=(bsz, s // tm),
        in_specs=[pl.BlockSpec((1, tm, zw), lambda b, i: (b, i, 0)), wide, wide],
        out_specs=(wide, grp, grp, grp, grp, grp, grp, wide, wide, wide),
        compiler_params=pltpu.CompilerParams(dimension_semantics=("parallel", "parallel")),
        name="nsa_prep",
    )(z, cos, sin)


def _nsa_compress_kernel(hk_ref, hv_ref, pos_ref, w1_ref, w2_ref, cos_ref, sin_ref, kc_ref, vc_ref):
    half = w1_ref.shape[1] // 2
    n = hk_ref.shape[1]
    for i, (h_ref, o_ref) in enumerate(((hk_ref, kc_ref), (hv_ref, vc_ref))):
        h = h_ref[0]
        first = _bdot(h, w1_ref[i, 0:half, :])
        second = pltpu.roll(_bdot(h, w1_ref[i, half:2 * half, :]), n - 1, axis=0)
        bias = _bdot(jnp.broadcast_to(pos_ref[i], (8, 2 * half)), w1_ref[i])[0:1]
        out = _bdot(jax.nn.gelu(first + second + bias, approximate=True), w2_ref[i])
        if i == 0:
            src = lax.broadcasted_iota(jnp.int32, (NSA_DH, NSA_DH), 0)
            dst = lax.broadcasted_iota(jnp.int32, (NSA_DH, NSA_DH), 1)
            swap = jnp.where(src == (dst + NSA_DH // 2) % NSA_DH, 1.0, 0.0).astype(jnp.bfloat16)
            out = out * cos_ref[0] + _head_sum(out, swap) * sin_ref[0]
        o_ref[0] = out.astype(o_ref.dtype)


def _nsa_compress(hk, hv, pos_flat, w1, w2, cos_c, sin_c):
    bg, n, hw = hk.shape
    ng = NSA_KV_HEADS
    blk = pl.BlockSpec((1, n, hw), lambda i: (i, 0, 0))
    tab = pl.BlockSpec((1, n, NSA_DH), lambda i: (i // ng, 0, 0))
    out = pl.BlockSpec((1, n, NSA_DH), lambda i: (i, 0, 0))
    whole = lambda a: pl.BlockSpec(a.shape, lambda i: (0,) * a.ndim)
    sd = jax.ShapeDtypeStruct((bg, n, NSA_DH), jnp.bfloat16)
    return pl.pallas_call(
        _nsa_compress_kernel,
        out_shape=(sd, sd),
        grid=(bg,),
        in_specs=[blk, blk, whole(pos_flat), whole(w1), whole(w2), tab, tab],
        out_specs=(out, out),
        compiler_params=pltpu.CompilerParams(
            dimension_semantics=("parallel",), vmem_limit_bytes=V7X_VMEM_LIMIT_BYTES),
        name="nsa_compress",
    )(hk, hv, pos_flat, w1, w2, cos_c, sin_c)


def _nsa_attn_kernel(q_ref, kc_ref, vc_ref, ks_ref, vs_ref, kw_ref, vw_ref, gc_ref, gs_ref, gw_ref, o_ref):
    f32, bf16 = jnp.float32, jnp.bfloat16
    qi = pl.program_id(2)
    q0 = qi * Q_BLOCK
    hpg, dh, tk = NSA_GROUP, NSA_DH, NSA_KEY_TILE
    rows = hpg * Q_BLOCK
    s_len = ks_ref.shape[2]
    n_sel = s_len // SEL_BLOCK
    n_cmp = kc_ref.shape[2]
    q_all = q_ref[0]
    q = jnp.concatenate([q_all[:, h * dh:(h + 1) * dh] for h in range(hpg)], axis=0)
    by_head = lambda o: jnp.concatenate([o[h * Q_BLOCK:(h + 1) * Q_BLOCK] for h in range(hpg)], axis=1)

    tq_c = q0 + lax.broadcasted_iota(jnp.int32, (Q_BLOCK, n_cmp), 0)
    c_end = lax.broadcasted_iota(jnp.int32, (Q_BLOCK, n_cmp), 1) * CMP_STRIDE + (CMP_LEN - 1)
    c_ok = jnp.concatenate([jnp.where(c_end <= tq_c, 1.0, 0.0)] * hpg, axis=0) > 0.5
    s = jnp.where(c_ok, lax.dot_general(q, kc_ref[0, 0], _NT, preferred_element_type=f32), NEG_BIG)
    m = jnp.max(s, axis=-1, keepdims=True)
    e = jnp.where(c_ok, jnp.exp(s - m), 0.0)
    p = e / jnp.maximum(jnp.sum(e, axis=-1, keepdims=True), 1e-30)
    o_cmp = jnp.dot(p.astype(bf16), vc_ref[0, 0], preferred_element_type=f32)

    p_sum = p[0:Q_BLOCK]
    for h in range(1, hpg):
        p_sum = p_sum + p[h * Q_BLOCK:(h + 1) * Q_BLOCK]
    c_start = lax.broadcasted_iota(jnp.int32, (n_sel, n_cmp), 1) * CMP_STRIDE
    j_blk = lax.broadcasted_iota(jnp.int32, (n_sel, n_cmp), 0)
    overlap_t = jnp.where((c_start < (j_blk + 1) * SEL_BLOCK) & (c_start + CMP_LEN > j_blk * SEL_BLOCK),
                          1.0, 0.0).astype(bf16)
    p_hi = p_sum.astype(bf16)
    p_lo = (p_sum - p_hi.astype(f32)).astype(bf16)
    imp = (lax.dot_general(overlap_t, p_hi, _NT, preferred_element_type=f32)
           + lax.dot_general(overlap_t, p_lo, _NT, preferred_element_type=f32))
    jb = lax.broadcasted_iota(jnp.int32, (n_sel, Q_BLOCK), 0)
    cur = (q0 + lax.broadcasted_iota(jnp.int32, (n_sel, Q_BLOCK), 1)) // SEL_BLOCK
    forced = (jb == 0) | (jb == cur) | (jb == cur - 1)
    imp = jnp.where(forced, -NEG_BIG, jnp.where(jb <= cur, imp, NEG_BIG))
    rank = jnp.zeros((n_sel, Q_BLOCK), f32)
    for i in range(n_sel):
        row_i = imp[i:i + 1, :]
        ahead = (row_i > imp) | ((row_i == imp) & (jb > i))
        rank = rank + jnp.where(ahead, 1.0, 0.0)
    sel = jnp.where(rank < float(min(SEL_TOP, n_sel)), 1.0, 0.0).T.astype(bf16)

    t_pos = q0 + lax.broadcasted_iota(jnp.int32, (Q_BLOCK, tk), 0)
    col = lax.broadcasted_iota(jnp.int32, (Q_BLOCK, tk), 1)
    blk_row = lax.broadcasted_iota(jnp.int32, (n_sel, tk), 0)
    blk_col = lax.broadcasted_iota(jnp.int32, (n_sel, tk), 1) // SEL_BLOCK

    def sweep(kt, carry):
        m, l, acc = carry
        k0 = pl.multiple_of(kt * tk, tk)
        kt_keys = ks_ref[0, 0, pl.ds(k0, tk), :]
        kt_vals = vs_ref[0, 0, pl.ds(k0, tk), :]
        expand = jnp.where(blk_row == blk_col + kt * (tk // SEL_BLOCK), 1.0, 0.0).astype(bf16)
        picked = jnp.dot(sel, expand, preferred_element_type=f32)
        ok = (picked > 0.5) & (k0 + col <= t_pos)
        bias = jnp.where(ok, 0.0, NEG_BIG).astype(f32)
        bias = jnp.concatenate([bias] * hpg, axis=0)
        s = lax.dot_general(q, kt_keys, _NT, preferred_element_type=f32) + bias
        m_new = jnp.maximum(m, jnp.max(s, axis=-1, keepdims=True))
        alpha = jnp.exp(m - m_new)
        p = jnp.exp(s - m_new)
        l = alpha * l + jnp.sum(p, axis=-1, keepdims=True)
        acc = alpha * acc + jnp.dot(p.astype(bf16), kt_vals, preferred_element_type=f32)
        return m_new, l, acc

    init = (jnp.full((rows, 1), NEG_BIG, f32), jnp.zeros((rows, 1), f32), jnp.zeros((rows, dh), f32))
    n_tiles = (q0 + Q_BLOCK + tk - 1) // tk
    m, l, acc = lax.fori_loop(0, n_tiles, sweep, init)
    o_sel = acc / jnp.maximum(l, 1e-30)

    w0 = pl.multiple_of(jnp.maximum(q0 - WINDOW, 0), Q_BLOCK)
    kwin = kw_ref[0, 0, pl.ds(w0, NSA_WIN_SPAN), :]
    vwin = vw_ref[0, 0, pl.ds(w0, NSA_WIN_SPAN), :]
    tq = q0 + lax.broadcasted_iota(jnp.int32, (Q_BLOCK, NSA_WIN_SPAN), 0)
    kp = w0 + lax.broadcasted_iota(jnp.int32, (Q_BLOCK, NSA_WIN_SPAN), 1)
    dlt = tq - kp
    wbias = jnp.where((dlt >= 0) & (dlt < WINDOW), 0.0, NEG_BIG).astype(f32)
    wbias = jnp.concatenate([wbias] * hpg, axis=0)
    s = lax.dot_general(q, kwin, _NT, preferred_element_type=f32) + wbias
    mw = jnp.max(s, axis=-1, keepdims=True)
    p = jnp.exp(s - mw)
    lw = jnp.sum(p, axis=-1, keepdims=True)
    o_win = jnp.dot(p.astype(bf16), vwin, preferred_element_type=f32) / jnp.maximum(lw, 1e-30)

    o_ref[0] = (gc_ref[0] * by_head(o_cmp) + gs_ref[0] * by_head(o_sel) + gw_ref[0] * by_head(o_win)
                ).astype(o_ref.dtype)


def _nsa_attn(q, k_cmp, v_cmp, ks, vs, kw, vw, g_cmp, g_sel, g_win):
    bsz, s, _ = q.shape
    ng, dh = NSA_KV_HEADS, NSA_DH
    gw = NSA_GROUP * dh
    kv_spec = pl.BlockSpec((1, 1, s, dh), lambda b, g, i: (b, g, 0, 0))
    cmp_spec = pl.BlockSpec((1, 1, k_cmp.shape[2], dh), lambda b, g, i: (b, g, 0, 0))
    q_spec = pl.BlockSpec((1, Q_BLOCK, gw), lambda b, g, i: (b, i, g))
    return pl.pallas_call(
        _nsa_attn_kernel,
        out_shape=jax.ShapeDtypeStruct(q.shape, jnp.bfloat16),
        grid=(bsz, ng, s // Q_BLOCK),
        in_specs=[q_spec, cmp_spec, cmp_spec, kv_spec, kv_spec, kv_spec, kv_spec, q_spec, q_spec, q_spec],
        out_specs=q_spec,
        compiler_params=pltpu.CompilerParams(
            dimension_semantics=("parallel", "parallel", "arbitrary"),
            vmem_limit_bytes=V7X_VMEM_LIMIT_BYTES),
        name="nsa_attn",
    )(q, k_cmp, v_cmp, ks, vs, kw, vw, g_cmp, g_sel, g_win)


def _nsa_mixer(z, nsa_tabs, cmp_pos, cmp_w1, cmp_w2):
    bsz, s, _ = z.shape
    ng, dh = NSA_KV_HEADS, NSA_DH
    cos, sin, cos_c, sin_c = nsa_tabs
    q, ks, vs, kw, vw, kc, vc, g_cmp, g_sel, g_win = _nsa_prep(z, cos, sin)
    n_blk = s // CMP_STRIDE
    flat = lambda t: t.reshape(bsz * ng, n_blk, CMP_STRIDE * dh)
    k_cmp, v_cmp = _nsa_compress(flat(kc), flat(vc), cmp_pos.reshape(2, 1, CMP_LEN * dh), cmp_w1, cmp_w2,
                                 cos_c, sin_c)
    grouped = lambda t: t.reshape(bsz, ng, n_blk, dh)
    return _nsa_attn(q, grouped(k_cmp), grouped(v_cmp), ks, vs, kw, vw, g_cmp, g_sel, g_win)


def _nsa_tables(positions):
    cos, sin = _rope_tables(positions, _rope_freqs(NSA_DH))
    end = CMP_LEN - 1
    pad = lambda t: jnp.pad(t[:, end::CMP_STRIDE], ((0, 0), (0, 1), (0, 0)))
    tile = lambda t: jnp.tile(t, (1, 1, NSA_HEADS))
    return tile(cos), tile(sin), pad(cos), pad(sin)


def _hybrid_mixer(h, bsz, ret_tabs, nsa_tabs, w_in, sc_conv, rw_mu, rw_w0, rw_w2, rw_a0, rw_a2, rw_g2,
                  rw_k_k, rw_k_a, rw_r_k, rw_lnx_g, rw_lnx_b, nsa_cmp_pos, nsa_cmp_w1, nsa_cmp_w2,
                  w_branch):
    m = h.shape[0]
    s = m // bsz
    offs = np.cumsum((0,) + GROUP_SIZES)
    group = lambda i: _mm(h, w_in[:, offs[i]:offs[i + 1]], keep_pad=True)
    per_batch = lambda z: z.reshape(bsz, s, z.shape[1])
    o_a = _short_conv_mixer(group(0), sc_conv, s)
    o_b = _rwkv7_mixer(per_batch(group(1)), rw_mu, rw_w0, rw_w2, rw_a0, rw_a2, rw_g2, rw_k_k, rw_k_a, rw_r_k,
                       rw_lnx_g, rw_lnx_b)
    o_c = _retention_mixer(per_batch(group(2)), *ret_tabs)
    o_d = _nsa_mixer(per_batch(group(3)), nsa_tabs, nsa_cmp_pos, nsa_cmp_w1, nsa_cmp_w2)
    outs = [o.reshape(m, o.shape[-1]) for o in (o_a, o_b, o_c, o_d)]
    w_gate = w_in[:, offs[4]:offs[5]].astype(jnp.bfloat16)
    return _merge(h, w_gate, outs, _split(w_branch, BR_SIZES, axis=0))


def kernel(x, mem, positions, ln_mix_pre, ln_mix_post, ln_mem_q, ln_mem_kv, ln_mem_post, ln_ffn_pre, ln_ffn_post, w_in, sc_conv, rw_mu, rw_w0, rw_w2, rw_a0, rw_a2, rw_g2, rw_k_k, rw_k_a, rw_r_k, rw_lnx_g, rw_lnx_b, nsa_cmp_pos, nsa_cmp_w1, nsa_cmp_w2, w_branch, w_out, mem_wq, mem_wkv, mem_wo, ffn_w_up, ffn_conv_w, ffn_conv_b, ffn_w_down):
    bsz, s, d = x.shape
    mem_len = mem.shape[1]
    xf = x.reshape(bsz * s, d)
    h = _prenorm(xf, ln_mix_pre[0])
    ret_tabs = _rope_tables(positions, _retnet_freqs(RET_DK))
    nsa_tabs = _nsa_tables(positions)
    for l in range(DEPTH):
        merged = _hybrid_mixer(h, bsz, ret_tabs, nsa_tabs, w_in[l], sc_conv[l], rw_mu[l], rw_w0[l], rw_w2[l], rw_a0[l],
                               rw_a2[l], rw_g2[l], rw_k_k[l], rw_k_a[l], rw_r_k[l], rw_lnx_g[l], rw_lnx_b[l],
                               nsa_cmp_pos[l], nsa_cmp_w1[l], nsa_cmp_w2[l], w_branch[l])
        xf, h = _mm_norm_res(merged, w_out[l], xf, ln_mix_post[l], ln_mem_q[l])
        mem_n = _prenorm(mem.reshape(bsz * mem_len, d), ln_mem_kv[l])
        kv = _mm(mem_n, mem_wkv[l]).astype(jnp.bfloat16)
        k_mem = kv[:, :d].reshape(bsz, mem_len, d)
        v_mem = kv[:, d:].reshape(bsz, mem_len, d)
        o = _xattn(h, mem_wq[l], k_mem, v_mem, s)
        xf, h = _mm_norm_res(o, mem_wo[l], xf, ln_mem_post[l], ln_ffn_pre[l])
        act = _ffn_up(h, ffn_w_up[l], ffn_conv_w[l], ffn_conv_b[l], s)
        g_next = ln_mix_pre[l + 1] if l + 1 < DEPTH else None
        xf, h = _mm_norm_res(act, ffn_w_down[l], xf, ln_ffn_post[l], g_next)
    return xf.reshape(bsz, s, d)
```

```python
import functools
import math

import jax
import jax.numpy as jnp
import numpy as np
from jax import lax
from jax.experimental import pallas as pl
from jax.experimental.pallas import tpu as pltpu

D_MODEL = 2048
DEPTH = 2
MEM_HEADS = 4
MEM_HEAD_DIM = D_MODEL // MEM_HEADS
SC_W = D_MODEL // 4
SC_K = 3
RW_HEAD_DIM = 64
RW_W = D_MODEL // 4
RW_HEADS = RW_W // RW_HEAD_DIM
RW_LORA_W = 96
RW_LORA_A = 96
RW_LORA_G = 256
RW_DECAY_SCALE = math.exp(-0.5)
RW_LN_EPS = 64e-5
RET_HEADS = 4
RET_DK = 128
RET_DV = 2 * RET_DK
RET_CHUNK = 128
RET_LN_EPS = 1e-5
NSA_HEADS = 8
NSA_KV_HEADS = 2
NSA_DH = 64
NSA_GROUP = NSA_HEADS // NSA_KV_HEADS
CMP_LEN = 32
CMP_STRIDE = 16
SEL_BLOCK = 64
SEL_TOP = 16
WINDOW = 512
Q_BLOCK = 128
ROPE_THETA = 10000.0
EPS = 1e-6
N_BRANCH = 4

SC_SIZES = (SC_W, SC_W, SC_W)
RW_SIZES = (RW_W, RW_W, RW_W, RW_LORA_W, RW_LORA_A, RW_LORA_G)
RET_SIZES = (RET_HEADS * RET_DK, RET_HEADS * RET_DK, RET_HEADS * RET_DV, RET_HEADS * RET_DV)
NSA_SIZES = (NSA_HEADS * NSA_DH,) + (NSA_KV_HEADS * NSA_DH,) * 6 + (NSA_HEADS * 3,)
GATE_SIZES = (D_MODEL,) * N_BRANCH
GROUP_SIZES = (sum(SC_SIZES), sum(RW_SIZES), sum(RET_SIZES), sum(NSA_SIZES), sum(GATE_SIZES))
BR_SIZES = (SC_W, RW_W, RET_HEADS * RET_DV, NSA_HEADS * NSA_DH)

V7X_VMEM_LIMIT_BYTES = 48 * 1024 * 1024
LANES = 128
_NT = (((1,), (1,)), ((), ()))


def _split(x, sizes, axis=-1):
    return jnp.split(x, [int(c) for c in np.cumsum(sizes)[:-1]], axis=axis)


def _pick_tile(n, candidates):
    for c in candidates:
        if n % c == 0:
            return c
    return n


def _bdot(a, b):
    return jnp.dot(a.astype(jnp.bfloat16), b.astype(jnp.bfloat16), preferred_element_type=jnp.float32)


def _bdot_nt(a, b):
    return lax.dot_general(a.astype(jnp.bfloat16), b.astype(jnp.bfloat16), _NT,
                           preferred_element_type=jnp.float32)


def _head_sum(x, bd):
    hi = x.astype(jnp.bfloat16)
    lo = (x - hi.astype(jnp.float32)).astype(jnp.bfloat16)
    return (jnp.dot(hi, bd, preferred_element_type=jnp.float32)
            + jnp.dot(lo, bd, preferred_element_type=jnp.float32))


def _mm_kernel(a_ref, w_ref, o_ref, acc_ref):
    k = pl.program_id(2)

    @pl.when(k == 0)
    def _():
        acc_ref[...] = jnp.zeros_like(acc_ref)

    acc_ref[...] += jnp.dot(a_ref[...], w_ref[...], preferred_element_type=jnp.float32)

    @pl.when(k == pl.num_programs(2) - 1)
    def _():
        o_ref[...] = acc_ref[...]


def _mm(a, w, keep_pad=False):
    m, k = a.shape
    n = w.shape[1]
    a = a.astype(jnp.bfloat16)
    w = w.astype(jnp.bfloat16)
    n_pad = LANES * ((n + LANES - 1) // LANES)
    tn = n_pad if n_pad <= 1536 else _pick_tile(n_pad, (1024, 768, 512, 256, 128))
    if n_pad != n:
        w = jnp.pad(w, ((0, 0), (0, n_pad - n)))
    tm = _pick_tile(m, (1024, 512, 256, 128, 8))
    tk = _pick_tile(k, (2048, 1280, 1408, 1024, 512, 256, 128))
    out = pl.pallas_call(
        _mm_kernel,
        out_shape=jax.ShapeDtypeStruct((m, n_pad), jnp.float32),
        grid=(m // tm, n_pad // tn, k // tk),
        in_specs=[pl.BlockSpec((tm, tk), lambda i, j, kk: (i, kk)),
                  pl.BlockSpec((tk, tn), lambda i, j, kk: (kk, j))],
        out_specs=pl.BlockSpec((tm, tn), lambda i, j, kk: (i, j)),
        scratch_shapes=[pltpu.VMEM((tm, tn), jnp.float32)],
        compiler_params=pltpu.CompilerParams(
            dimension_semantics=("parallel", "parallel", "arbitrary"),
            vmem_limit_bytes=V7X_VMEM_LIMIT_BYTES),
        name="mm",
    )(a, w)
    return out if keep_pad or n_pad == n else out[:, :n]


def _rms(y, g):
    return y * lax.rsqrt(jnp.mean(y * y, axis=-1, keepdims=True) + EPS) * g


def _prenorm_kernel(x_ref, g_ref, h_ref):
    h_ref[...] = _rms(x_ref[...], g_ref[...]).astype(h_ref.dtype)


def _prenorm(x, g):
    m, d = x.shape
    tm = _pick_tile(m, (512, 256, 128, 8))
    return pl.pallas_call(
        _prenorm_kernel,
        out_shape=jax.ShapeDtypeStruct((m, d), jnp.bfloat16),
        grid=(m // tm,),
        in_specs=[pl.BlockSpec((tm, d), lambda i: (i, 0)), pl.BlockSpec((1, d), lambda i: (0, 0))],
        out_specs=pl.BlockSpec((tm, d), lambda i: (i, 0)),
        compiler_params=pltpu.CompilerParams(dimension_semantics=("parallel",)),
        name="prenorm",
    )(x, g.reshape(1, d))


def _mm_norm_res_kernel(emit_h, whole_k, a_ref, w_ref, res_ref, g_ref, *rest):
    if emit_h:
        g2_ref, x_ref, h_ref = rest[:3]
    else:
        x_ref = rest[0]

    def finish(y):
        x_new = res_ref[...] + _rms(y, g_ref[...])
        x_ref[...] = x_new
        if emit_h:
            h_ref[...] = _rms(x_new, g2_ref[...]).astype(h_ref.dtype)

    if whole_k:
        finish(jnp.dot(a_ref[...], w_ref[...], preferred_element_type=jnp.float32))
        return
    acc_ref = rest[-1]
    k = pl.program_id(1)

    @pl.when(k == 0)
    def _():
        acc_ref[...] = jnp.zeros_like(acc_ref)

    acc_ref[...] += jnp.dot(a_ref[...], w_ref[...], preferred_element_type=jnp.float32)

    @pl.when(k == pl.num_programs(1) - 1)
    def _():
        finish(acc_ref[...])


def _mm_norm_res(a, w, res, g, g_next):
    m, k = a.shape
    d = w.shape[1]
    tk = _pick_tile(k, (2048, 1408, 1024, 512))
    tm = 512
    whole_k = tk == k
    emit_h = g_next is not None
    row = pl.BlockSpec((tm, d), lambda i, kk: (i, 0))
    vec = pl.BlockSpec((1, d), lambda i, kk: (0, 0))
    w_spec = (pl.BlockSpec((tk, d), lambda i, kk: (0, 0), pipeline_mode=pl.Buffered(1)) if whole_k
              else pl.BlockSpec((tk, d), lambda i, kk: (kk, 0)))
    x_sd = jax.ShapeDtypeStruct((m, d), jnp.float32)
    h_sd = jax.ShapeDtypeStruct((m, d), jnp.bfloat16)
    out = pl.pallas_call(
        functools.partial(_mm_norm_res_kernel, emit_h, whole_k),
        out_shape=(x_sd, h_sd) if emit_h else x_sd,
        grid=(m // tm, k // tk),
        in_specs=[pl.BlockSpec((tm, tk), lambda i, kk: (i, kk)), w_spec, row, vec] + ([vec] if emit_h else []),
        out_specs=(row, row) if emit_h else row,
        scratch_shapes=[] if whole_k else [pltpu.VMEM((tm, d), jnp.float32)],
        compiler_params=pltpu.CompilerParams(
            dimension_semantics=("parallel", "arbitrary"), vmem_limit_bytes=V7X_VMEM_LIMIT_BYTES),
        name="mm_norm_res",
    )(a.astype(jnp.bfloat16), w.astype(jnp.bfloat16), res, g.reshape(1, d),
      *([g_next.reshape(1, d)] if emit_h else []))
    return out if emit_h else (out, None)


FFN_TM = 1024
FFN_TN = 512
HALO = 8


def _shift_rows(u, prev, n):
    rolled = pltpu.roll(u, n, axis=0)
    rows = lax.broadcasted_iota(jnp.int32, u.shape, 0)
    for r in range(n):
        rolled = jnp.where(rows == r, prev[HALO - n + r:HALO - n + r + 1, :], rolled)
    return rolled


def _ffn_up_kernel(seq_len, h_ref, halo_ref, wa_ref, wb_ref, cwa_ref, cwb_ref, ba_ref, bb_ref, o_ref):
    i = pl.program_id(0)
    h = h_ref[...]
    at_start = (i * FFN_TM) % seq_len == 0
    halo = jnp.where(at_start, jnp.zeros_like(halo_ref[...]), halo_ref[...])

    def conv(w_ref, cw_ref, b_ref):
        u = jnp.dot(h, w_ref[...], preferred_element_type=jnp.float32)
        up = jnp.dot(halo, w_ref[...], preferred_element_type=jnp.float32)
        cw = cw_ref[...]
        y = cw[0:1] * _shift_rows(u, up, 2)
        y = y + cw[1:2] * _shift_rows(u, up, 1)
        y = y + cw[2:3] * u
        return y + b_ref[...]

    a = conv(wa_ref, cwa_ref, ba_ref)
    b = conv(wb_ref, cwb_ref, bb_ref)
    o_ref[...] = (jax.nn.gelu(a, approximate=True) * b).astype(o_ref.dtype)


def _ffn_up(h, w_up, w_conv, b_conv, seq_len):
    m, d = h.shape
    f = w_up.shape[1] // 2
    nj = f // FFN_TN
    w_up = w_up.astype(jnp.bfloat16)
    b_conv = b_conv.reshape(1, 2 * f)
    tiles_per_halo = FFN_TM // HALO
    return pl.pallas_call(
        functools.partial(_ffn_up_kernel, seq_len),
        out_shape=jax.ShapeDtypeStruct((m, f), jnp.bfloat16),
        grid=(m // FFN_TM, nj),
        in_specs=[pl.BlockSpec((FFN_TM, d), lambda i, j: (i, 0)),
                  pl.BlockSpec((HALO, d), lambda i, j: (jnp.maximum(i * tiles_per_halo - 1, 0), 0)),
                  pl.BlockSpec((d, FFN_TN), lambda i, j: (0, j)),
                  pl.BlockSpec((d, FFN_TN), lambda i, j: (0, j + nj)),
                  pl.BlockSpec((3, FFN_TN), lambda i, j: (0, j)),
                  pl.BlockSpec((3, FFN_TN), lambda i, j: (0, j + nj)),
                  pl.BlockSpec((1, FFN_TN), lambda i, j: (0, j)),
                  pl.BlockSpec((1, FFN_TN), lambda i, j: (0, j + nj))],
        out_specs=pl.BlockSpec((FFN_TM, FFN_TN), lambda i, j: (i, j)),
        compiler_params=pltpu.CompilerParams(
            dimension_semantics=("parallel", "arbitrary"), vmem_limit_bytes=V7X_VMEM_LIMIT_BYTES),
        name="ffn_up",
    )(h, h, w_up, w_up, w_conv, w_conv, b_conv, b_conv)


MERGE_TM = 512
MERGE_TN = 512


def _merge_kernel(h_ref, wg0, wg1, wg2, wg3, o0, o1, o2, o3, wb0, wb1, wb2, wb3, out_ref):
    h = h_ref[...]
    acc = None
    for wg, o, wb in ((wg0, o0, wb0), (wg1, o1, wb1), (wg2, o2, wb2), (wg3, o3, wb3)):
        gate = jax.nn.sigmoid(jnp.dot(h, wg[...], preferred_element_type=jnp.float32))
        term = gate * jnp.dot(o[...], wb[...], preferred_element_type=jnp.float32)
        acc = term if acc is None else acc + term
    out_ref[...] = acc.astype(out_ref.dtype)


def _merge(h, w_gate, outs, w_branches):
    m, d = h.shape
    nj = d // MERGE_TN
    gate_specs = [pl.BlockSpec((d, MERGE_TN), functools.partial(lambda b, i, j: (0, b * nj + j), b))
                  for b in range(N_BRANCH)]
    o_specs = [pl.BlockSpec((MERGE_TM, o.shape[1]), lambda i, j: (i, 0)) for o in outs]
    wb_specs = [pl.BlockSpec((wb.shape[0], MERGE_TN), lambda i, j: (0, j)) for wb in w_branches]
    return pl.pallas_call(
        _merge_kernel,
        out_shape=jax.ShapeDtypeStruct((m, d), jnp.bfloat16),
        grid=(m // MERGE_TM, nj),
        in_specs=[pl.BlockSpec((MERGE_TM, d), lambda i, j: (i, 0))] + gate_specs + o_specs + wb_specs,
        out_specs=pl.BlockSpec((MERGE_TM, MERGE_TN), lambda i, j: (i, j)),
        compiler_params=pltpu.CompilerParams(
            dimension_semantics=("parallel", "arbitrary"), vmem_limit_bytes=V7X_VMEM_LIMIT_BYTES),
        name="merge",
    )(h, *([w_gate] * N_BRANCH), *[o.astype(jnp.bfloat16) for o in outs],
      *[wb.astype(jnp.bfloat16) for wb in w_branches])


XATT_TM = 512


def _xattn_kernel(h_ref, wq_ref, k_ref, v_ref, o_ref):
    q = jnp.dot(h_ref[...], wq_ref[...], preferred_element_type=jnp.float32)
    q = (q * (MEM_HEAD_DIM ** -0.5)).astype(jnp.bfloat16)
    for hh in range(MEM_HEADS):
        sl = slice(hh * MEM_HEAD_DIM, (hh + 1) * MEM_HEAD_DIM)
        s = lax.dot_general(q[:, sl], k_ref[0, :, sl], _NT, preferred_element_type=jnp.float32)
        p = jnp.exp(s - jnp.max(s, axis=-1, keepdims=True))
        p = p / jnp.sum(p, axis=-1, keepdims=True)
        o_ref[:, sl] = jnp.dot(p.astype(jnp.bfloat16), v_ref[0, :, sl],
                               preferred_element_type=jnp.float32).astype(o_ref.dtype)


def _xattn(h, wq, k, v, seq_len):
    m, d = h.shape
    mem_len = k.shape[1]
    per_batch = seq_len // XATT_TM
    kv_spec = pl.BlockSpec((1, mem_len, d), lambda i: (i // per_batch, 0, 0))
    return pl.pallas_call(
        _xattn_kernel,
        out_shape=jax.ShapeDtypeStruct((m, d), jnp.bfloat16),
        grid=(m // XATT_TM,),
        in_specs=[pl.BlockSpec((XATT_TM, d), lambda i: (i, 0)),
                  pl.BlockSpec((d, d), lambda i: (0, 0)), kv_spec, kv_spec],
        out_specs=pl.BlockSpec((XATT_TM, d), lambda i: (i, 0)),
        compiler_params=pltpu.CompilerParams(
            dimension_semantics=("parallel",), vmem_limit_bytes=V7X_VMEM_LIMIT_BYTES),
        name="xattn",
    )(h, wq.astype(jnp.bfloat16), k, v)


def _rope_freqs(d):
    return ROPE_THETA ** (-jnp.arange(0, d, 2, dtype=jnp.float32) / d)


def _retnet_freqs(d):
    return 1.0 / (ROPE_THETA ** jnp.linspace(0.0, 1.0, d // 2, dtype=jnp.float32))


WKV_CHUNK = 64


def _rwkv7_kernel(z_ref, mu_ref, w0_ref, a0_ref, kk_ref, ka_ref, rk_ref, lng_ref, lnb_ref,
                  ww2_ref, wa2_ref, wg2_ref, o_ref, s_ref, prev_ref, y_ref):
    c = WKV_CHUNK
    hd = RW_HEAD_DIM
    w = RW_W
    nb = z_ref.shape[0]
    f32, bf16 = jnp.float32, jnp.bfloat16

    @pl.when(pl.program_id(0) == 0)
    def _():
        s_ref[...] = jnp.zeros_like(s_ref)
        prev_ref[...] = jnp.zeros_like(prev_ref)

    lane_h = lax.broadcasted_iota(jnp.int32, (w, w), 0) // hd
    lane_w = lax.broadcasted_iota(jnp.int32, (w, w), 1) // hd
    head_bd = jnp.where(lane_h == lane_w, 1.0, 0.0).astype(bf16)
    row = lax.broadcasted_iota(jnp.int32, (c, c), 0)
    col = lax.broadcasted_iota(jnp.int32, (c, c), 1)
    tri = jnp.where(row >= col, 1.0, 0.0).astype(bf16)
    eye = jnp.where(row == col, 1.0, 0.0).astype(f32)
    r2 = lax.broadcasted_iota(jnp.int32, (2 * c, 2 * c), 0)
    c2 = lax.broadcasted_iota(jnp.int32, (2 * c, 2 * c), 1)
    rr = jnp.where(r2 >= c, r2 - c, r2)
    cc = jnp.where(c2 >= c, c2 - c, c2)
    tri_mask = cc < rr + jnp.where(r2 >= c, 1, 0)

    def prepare(b):
        z = z_ref[b]
        rows = lax.broadcasted_iota(jnp.int32, z.shape, 0)
        z_prev = jnp.where(rows == 0, prev_ref[b], pltpu.roll(z, 1, axis=0))
        prev_ref[b] = z[c - 1:c, :]
        z = z + (z_prev - z) * mu_ref[...]
        r, k, v, lora = z[:, 0:w], z[:, w:2 * w], z[:, 2 * w:3 * w], z[:, 3 * w:4 * w]
        logw = -RW_DECAY_SCALE * jax.nn.sigmoid(w0_ref[...] + _bdot(jnp.tanh(lora), ww2_ref[...]))
        a = jax.nn.sigmoid(a0_ref[...] + _bdot(lora, wa2_ref[...]))
        g = _bdot(jax.nn.sigmoid(lora), wg2_ref[...])
        kk = k * kk_ref[...]
        kk = kk / jnp.maximum(jnp.sqrt(_head_sum(kk * kk, head_bd)), 1e-12)
        k = k * (1.0 + (a - 1.0) * ka_ref[...])
        hi = logw.astype(bf16)
        rem = logw - hi.astype(f32)
        mid = rem.astype(bf16)
        lo = (rem - mid.astype(f32)).astype(bf16)
        cum = (jnp.dot(tri, hi, preferred_element_type=f32) + jnp.dot(tri, mid, preferred_element_type=f32)
               + jnp.dot(tri, lo, preferred_element_type=f32))
        kka = kk * a
        e_neg = jnp.exp(-cum)
        last = cum[c - 1:c, :]
        dec = jnp.exp(last - cum)
        return dict(r=r, k=k, v=v, g=g, vb=v.astype(bf16), g_last=jnp.exp(last),
                    qh=(kk * jnp.exp(cum - logw)).astype(bf16), rh=(r * jnp.exp(cum)).astype(bf16),
                    bh=(kka * e_neg).astype(bf16), kh=(k * e_neg).astype(bf16),
                    bd=(kka * dec).astype(bf16), kd=(k * dec).astype(bf16))

    pre = [prepare(b) for b in range(nb)]

    units = [(b, h) for b in range(nb) for h in range(RW_HEADS)]
    n_units = range(len(units))
    part = lambda name, i: pre[units[i][0]][name][:, units[i][1] * hd:(units[i][1] + 1) * hd]
    a1 = [jnp.concatenate([part("qh", i), part("rh", i)], axis=0) for i in n_units]
    b1 = [jnp.concatenate([part("bh", i), part("kh", i)], axis=0) for i in n_units]
    ss = [jnp.where(tri_mask, lax.dot_general(a1[i], b1[i], _NT, preferred_element_type=f32), 0.0)
          for i in n_units]
    s_old = [s_ref[i] for i in n_units]
    qr = [lax.dot_general(a1[i], s_old[i].astype(bf16), _NT, preferred_element_type=f32) for i in n_units]
    lm = [_bdot(ss[i][:, c:2 * c], part("vb", i)) for i in n_units]
    n = [-ss[i][0:c, 0:c] for i in n_units]
    t = [eye + n[i] for i in n_units]
    for _ in range(5):
        n = [_bdot(n[i], n[i]) for i in n_units]
        t = [t[i] + _bdot(t[i], n[i]) for i in n_units]
    u = [-_bdot(t[i], qr[i][0:c] + lm[i][0:c]) for i in n_units]
    for i in n_units:
        b, h = units[i]
        y_ref[b, :, h * hd:(h + 1) * hd] = qr[i][c:2 * c] + lm[i][c:2 * c] + _bdot(ss[i][c:2 * c, 0:c], u[i])
    for i in n_units:
        zt = jnp.concatenate([u[i], part("v", i)], axis=0).T
        x = jnp.concatenate([part("bd", i), part("kd", i)], axis=0)
        s_ref[i] = s_old[i] * part("g_last", i) + _bdot(zt, x)

    for b in range(nb):
        p = pre[b]
        y = y_ref[b]
        mean = _head_sum(y, head_bd) * (1.0 / hd)
        yc = y - mean
        var = _head_sum(yc * yc, head_bd) * (1.0 / hd)
        yn = yc * lax.rsqrt(var + RW_LN_EPS) * lng_ref[...] + lnb_ref[...]
        bonus = _head_sum(p["r"] * p["k"] * rk_ref[...], head_bd) * p["v"]
        o_ref[b] = ((yn + bonus) * p["g"]).astype(o_ref.dtype)


def _rwkv7_mixer(z, mu, w0, w_w2, a0, w_a2, w_g2, k_k, k_a, r_k, lnx_g, lnx_b):
    bsz, s, zw = z.shape
    w = RW_W
    n_lora = RW_LORA_W + RW_LORA_A + RW_LORA_G
    row = lambda t: t.reshape(1, -1)
    mu = jnp.pad(mu, (0, zw - mu.shape[0])).reshape(1, zw)
    o_w, o_a = RW_LORA_W, RW_LORA_W + RW_LORA_A
    pad_rows = lambda m, lo: jnp.pad(m, ((lo, w - lo - m.shape[0]), (0, 0))).astype(jnp.bfloat16)
    vec = pl.BlockSpec((1, w), lambda c: (0, 0))
    mat = pl.BlockSpec((w, w), lambda c: (0, 0))
    assert zw == 4 * w and n_lora <= w
    return pl.pallas_call(
        _rwkv7_kernel,
        out_shape=jax.ShapeDtypeStruct((bsz, s, w), jnp.bfloat16),
        grid=(s // WKV_CHUNK,),
        in_specs=[pl.BlockSpec((bsz, WKV_CHUNK, zw), lambda c: (0, c, 0)),
                  pl.BlockSpec((1, zw), lambda c: (0, 0))] + [vec] * 7 + [mat] * 3,
        out_specs=pl.BlockSpec((bsz, WKV_CHUNK, w), lambda c: (0, c, 0)),
        scratch_shapes=[pltpu.VMEM((bsz * RW_HEADS, RW_HEAD_DIM, RW_HEAD_DIM), jnp.float32),
                        pltpu.VMEM((bsz, 1, zw), jnp.float32),
                        pltpu.VMEM((bsz, WKV_CHUNK, w), jnp.float32)],
        compiler_params=pltpu.CompilerParams(dimension_semantics=("arbitrary",)),
        name="rwkv7",
    )(z, mu, row(w0), row(a0), row(k_k), row(k_a), row(r_k), row(lnx_g), row(lnx_b),
      pad_rows(w_w2, 0), pad_rows(w_a2, o_w), pad_rows(w_g2, o_a))


RET_LOG_DECAY = tuple(math.log(1.0 - 2.0 ** (-5.0 - h)) for h in range(RET_HEADS))


def _retention_kernel(q_ref, k_ref, v_ref, g_ref, cos_ref, sin_ref, o_ref, s_ref):
    c, dk, dv = RET_CHUNK, RET_DK, RET_DV
    f32 = jnp.float32

    @pl.when(pl.program_id(1) == 0)
    def _():
        s_ref[...] = jnp.zeros_like(s_ref)

    cos = cos_ref[0]
    sin = sin_ref[0]
    diff = (lax.broadcasted_iota(jnp.int32, (c, c), 0) - lax.broadcasted_iota(jnp.int32, (c, c), 1)).astype(f32)
    tok = lax.broadcasted_iota(jnp.int32, (c, 1), 0).astype(f32)
    rope = lambda x: x * cos + pltpu.roll(x, dk // 2, axis=1) * sin
    for h in range(RET_HEADS):
        lg = RET_LOG_DECAY[h]
        q = rope(q_ref[0, :, h * dk:(h + 1) * dk])
        k = rope(k_ref[0, :, h * dk:(h + 1) * dk]) * (dk ** -0.5)
        v = v_ref[0, :, h * dv:(h + 1) * dv]
        decay_in = jnp.where(diff >= 0, jnp.exp(jnp.maximum(diff, 0.0) * lg), 0.0)
        scores = _bdot_nt(q, k) * decay_in
        state = s_ref[h]
        o = _bdot(scores, v) + _bdot(q * jnp.exp((tok + 1.0) * lg), state)
        s_ref[h] = state * math.exp(c * lg) + _bdot((k * jnp.exp((c - 1.0 - tok) * lg)).T, v)
        mu = jnp.mean(o, axis=-1, keepdims=True)
        oc = o - mu
        var = jnp.mean(oc * oc, axis=-1, keepdims=True)
        gate = g_ref[0, :, h * dv:(h + 1) * dv]
        o_ref[0, :, h * dv:(h + 1) * dv] = (oc * lax.rsqrt(var + RET_LN_EPS) * gate * jax.nn.sigmoid(gate)
                                            ).astype(o_ref.dtype)


def _retention_mixer(z, cos, sin):
    bsz, s, _ = z.shape
    qk_w, v_w = RET_HEADS * RET_DK, RET_HEADS * RET_DV
    c = RET_CHUNK
    tab = pl.BlockSpec((1, c, RET_DK), lambda b, i: (b, i, 0))
    return pl.pallas_call(
        _retention_kernel,
        out_shape=jax.ShapeDtypeStruct((bsz, s, v_w), jnp.bfloat16),
        grid=(bsz, s // c),
        in_specs=[pl.BlockSpec((1, c, qk_w), lambda b, i: (b, i, 0)),
                  pl.BlockSpec((1, c, qk_w), lambda b, i: (b, i, 1)),
                  pl.BlockSpec((1, c, v_w), lambda b, i: (b, i, 1)),
                  pl.BlockSpec((1, c, v_w), lambda b, i: (b, i, 2)), tab, tab],
        out_specs=pl.BlockSpec((1, c, v_w), lambda b, i: (b, i, 0)),
        scratch_shapes=[pltpu.VMEM((RET_HEADS, RET_DK, RET_DV), jnp.float32)],
        compiler_params=pltpu.CompilerParams(dimension_semantics=("parallel", "arbitrary")),
        name="retention",
    )(z, z, z, z, cos, sin)


TRIG_TM = 1024


def _trig_kernel(pos_ref, freq_ref, rc_ref, rs_ref, nc_ref, ns_ref):
    ang = pos_ref[...] * freq_ref[...]
    c, sn = jnp.cos(ang), jnp.sin(ang)
    n_ret, n_nsa = RET_DK // 2, NSA_DH // 2
    rc_ref[...] = jnp.concatenate([c[:, :n_ret]] * 2, axis=1)
    rs_ref[...] = jnp.concatenate([-sn[:, :n_ret], sn[:, :n_ret]], axis=1)
    nsa_c, nsa_s = c[:, n_ret:n_ret + n_nsa], sn[:, n_ret:n_ret + n_nsa]
    nc_ref[...] = jnp.concatenate([nsa_c, nsa_c] * NSA_HEADS, axis=1)
    ns_ref[...] = jnp.concatenate([-nsa_s, nsa_s] * NSA_HEADS, axis=1)


def _rope_tables(positions):
    bsz, s = positions.shape
    m = bsz * s
    n_ret, n_nsa = RET_DK // 2, NSA_DH // 2
    assert n_ret + n_nsa <= LANES
    freq = jnp.concatenate([_retnet_freqs(RET_DK), _rope_freqs(NSA_DH),
                            jnp.zeros((LANES - n_ret - n_nsa,), jnp.float32)]).reshape(1, LANES)
    tm = _pick_tile(m, (TRIG_TM, 512, 256, 128, 8))
    spec = lambda w: pl.BlockSpec((tm, w), lambda i: (i, 0))
    sd = lambda w: jax.ShapeDtypeStruct((m, w), jnp.float32)
    nsa_w = NSA_HEADS * NSA_DH
    tabs = pl.pallas_call(
        _trig_kernel,
        out_shape=(sd(RET_DK), sd(RET_DK), sd(nsa_w), sd(nsa_w)),
        grid=(m // tm,),
        in_specs=[spec(1), pl.BlockSpec((1, LANES), lambda i: (0, 0))],
        out_specs=(spec(RET_DK), spec(RET_DK), spec(nsa_w), spec(nsa_w)),
        compiler_params=pltpu.CompilerParams(dimension_semantics=("parallel",)),
        name="rope_tables",
    )(positions.astype(jnp.float32).reshape(m, 1), freq)
    return [t.reshape(bsz, s, t.shape[1]) for t in tabs]


SC_TM = 512


def _short_conv_kernel(seq_len, b_ref, c_ref, x_ref, ch_ref, xh_ref, w_ref, o_ref):
    at_start = (pl.program_id(0) * SC_TM) % seq_len == 0
    u = c_ref[...] * x_ref[...]
    up = jnp.where(at_start, 0.0, ch_ref[...] * xh_ref[...])
    w = w_ref[...]
    y = w[0:1] * _shift_rows(u, up, 2) + w[1:2] * _shift_rows(u, up, 1) + w[2:3] * u
    o_ref[...] = (b_ref[...] * y).astype(o_ref.dtype)


def _short_conv_mixer(z, w_conv, seq_len):
    m = z.shape[0]
    w = SC_W
    per_halo = SC_TM // HALO
    tile = lambda j: pl.BlockSpec((SC_TM, w), lambda i: (i, j))
    halo = lambda j: pl.BlockSpec((HALO, w), lambda i: (jnp.maximum(i * per_halo - 1, 0), j))
    return pl.pallas_call(
        functools.partial(_short_conv_kernel, seq_len),
        out_shape=jax.ShapeDtypeStruct((m, w), jnp.bfloat16),
        grid=(m // SC_TM,),
        in_specs=[tile(0), tile(1), tile(2), halo(1), halo(2), pl.BlockSpec((SC_K, w), lambda i: (0, 0))],
        out_specs=pl.BlockSpec((SC_TM, w), lambda i: (i, 0)),
        compiler_params=pltpu.CompilerParams(dimension_semantics=("parallel",)),
        name="short_conv",
    )(z, z, z, z, z, w_conv)


NSA_KEY_TILE = 512
NSA_WIN_SPAN = WINDOW + Q_BLOCK
NSA_PREP_TM = 512
NEG_BIG = -1e30
NSA_Q_W = NSA_HEADS * NSA_DH
NSA_KV_W = NSA_KV_HEADS * NSA_DH
NSA_GATE_OFF = NSA_Q_W + 6 * NSA_KV_W


def _rope_lanes(x, cos, sin):
    width = x.shape[1]
    half = NSA_DH // 2
    lane = lax.broadcasted_iota(jnp.int32, x.shape, 1)
    other = jnp.where(lane % NSA_DH < half, pltpu.roll(x, width - half, axis=1), pltpu.roll(x, half, axis=1))
    return x * cos + other * sin


def _nsa_prep_kernel(z_ref, cos_ref, sin_ref, q_ref, ks_ref, vs_ref, kw_ref, vw_ref, kc_ref, vc_ref,
                     gc_ref, gs_ref, gw_ref):
    cos, sin = cos_ref[0], sin_ref[0]
    z = z_ref[0]
    q_ref[0] = (_rope_lanes(z[:, 0:NSA_Q_W], cos, sin) * (NSA_DH ** -0.5)).astype(q_ref.dtype)
    kv = lambda i: z[:, NSA_Q_W + i * NSA_KV_W:NSA_Q_W + (i + 1) * NSA_KV_W]
    cos_kv, sin_kv = cos[:, 0:NSA_KV_W], sin[:, 0:NSA_KV_W]
    pieces = ((kc_ref, kv(0)), (vc_ref, kv(1)), (ks_ref, _rope_lanes(kv(2), cos_kv, sin_kv)), (vs_ref, kv(3)),
              (kw_ref, _rope_lanes(kv(4), cos_kv, sin_kv)), (vw_ref, kv(5)))
    n_sel = ks_ref.shape[3] - NSA_DH
    tok = pl.program_id(1) * NSA_PREP_TM + lax.broadcasted_iota(jnp.int32, (NSA_PREP_TM, n_sel), 0)
    block_1hot = jnp.where(tok // SEL_BLOCK == lax.broadcasted_iota(jnp.int32, (NSA_PREP_TM, n_sel), 1), 1.0, 0.0)
    for ref, val in pieces:
        for g in range(NSA_KV_HEADS):
            val_g = val[:, g * NSA_DH:(g + 1) * NSA_DH]
            if ref is ks_ref:
                val_g = jnp.concatenate([val_g, block_1hot], axis=1)
            ref[0, g] = val_g.astype(ref.dtype)
    gate = jax.nn.sigmoid(z[:, NSA_GATE_OFF:NSA_GATE_OFF + LANES])
    src = lax.broadcasted_iota(jnp.int32, (LANES, NSA_Q_W), 0)
    head = lax.broadcasted_iota(jnp.int32, (LANES, NSA_Q_W), 1) // NSA_DH
    for j, ref in enumerate((gc_ref, gs_ref, gw_ref)):
        expand = jnp.where(src == 3 * head + j, 1.0, 0.0).astype(jnp.bfloat16)
        ref[0] = _head_sum(gate, expand)


def _nsa_prep(z, cos, sin):
    bsz, s, zw = z.shape
    tm = NSA_PREP_TM
    wide = pl.BlockSpec((1, tm, NSA_Q_W), lambda b, i: (b, i, 0))
    grp = pl.BlockSpec((1, NSA_KV_HEADS, tm, NSA_DH), lambda b, i: (b, 0, i, 0))
    grp_sd = lambda dt: jax.ShapeDtypeStruct((bsz, NSA_KV_HEADS, s, NSA_DH), dt)
    ks_w = NSA_DH + s // SEL_BLOCK
    ks_spec = pl.BlockSpec((1, NSA_KV_HEADS, tm, ks_w), lambda b, i: (b, 0, i, 0))
    ks_sd = jax.ShapeDtypeStruct((bsz, NSA_KV_HEADS, s, ks_w), jnp.bfloat16)
    wide_sd = lambda dt: jax.ShapeDtypeStruct((bsz, s, NSA_Q_W), dt)
    bf16, f32 = jnp.bfloat16, jnp.float32
    return pl.pallas_call(
        _nsa_prep_kernel,
        out_shape=(wide_sd(bf16), ks_sd, grp_sd(bf16), grp_sd(bf16), grp_sd(bf16), grp_sd(f32), grp_sd(f32),
                   wide_sd(f32), wide_sd(f32), wide_sd(f32)),
        grid=(bsz, s // tm),
        in_specs=[pl.BlockSpec((1, tm, zw), lambda b, i: (b, i, 0)), wide, wide],
        out_specs=(wide, ks_spec, grp, grp, grp, grp, grp, wide, wide, wide),
        compiler_params=pltpu.CompilerParams(dimension_semantics=("parallel", "parallel")),
        name="nsa_prep",
    )(z, cos, sin)


def _nsa_compress_kernel(hk_ref, hv_ref, pos_ref, w1_ref, w2_ref, cos_ref, sin_ref, kc_ref, vc_ref):
    half = w1_ref.shape[1] // 2
    n = hk_ref.shape[1]
    for i, (h_ref, o_ref) in enumerate(((hk_ref, kc_ref), (hv_ref, vc_ref))):
        h = h_ref[0]
        first = _bdot(h, w1_ref[i, 0:half, :])
        second = pltpu.roll(_bdot(h, w1_ref[i, half:2 * half, :]), n - 1, axis=0)
        bias = _bdot(jnp.broadcast_to(pos_ref[i], (8, 2 * half)), w1_ref[i])[0:1]
        out = _bdot(jax.nn.gelu(first + second + bias, approximate=True), w2_ref[i])
        if i == 0:
            src = lax.broadcasted_iota(jnp.int32, (NSA_DH, NSA_DH), 0)
            dst = lax.broadcasted_iota(jnp.int32, (NSA_DH, NSA_DH), 1)
            swap = jnp.where(src == (dst + NSA_DH // 2) % NSA_DH, 1.0, 0.0).astype(jnp.bfloat16)
            out = out * cos_ref[0] + _head_sum(out, swap) * sin_ref[0]
        o_ref[0] = out.astype(o_ref.dtype)


def _nsa_compress(hk, hv, pos_flat, w1, w2, cos_c, sin_c):
    bg, n, hw = hk.shape
    ng = NSA_KV_HEADS
    blk = pl.BlockSpec((1, n, hw), lambda i: (i, 0, 0))
    tab = pl.BlockSpec((1, n, NSA_DH), lambda i: (i // ng, 0, 0))
    out = pl.BlockSpec((1, n, NSA_DH), lambda i: (i, 0, 0))
    whole = lambda a: pl.BlockSpec(a.shape, lambda i: (0,) * a.ndim)
    sd = jax.ShapeDtypeStruct((bg, n, NSA_DH), jnp.bfloat16)
    return pl.pallas_call(
        _nsa_compress_kernel,
        out_shape=(sd, sd),
        grid=(bg,),
        in_specs=[blk, blk, whole(pos_flat), whole(w1), whole(w2), tab, tab],
        out_specs=(out, out),
        compiler_params=pltpu.CompilerParams(
            dimension_semantics=("parallel",), vmem_limit_bytes=V7X_VMEM_LIMIT_BYTES),
        name="nsa_compress",
    )(hk, hv, pos_flat, w1, w2, cos_c, sin_c)


def _nsa_attn_kernel(q_ref, kc_ref, vc_ref, ks_ref, vs_ref, kw_ref, vw_ref, gc_ref, gs_ref, gw_ref, o_ref):
    f32, bf16 = jnp.float32, jnp.bfloat16
    qi = pl.program_id(2)
    q0 = qi * Q_BLOCK
    hpg, dh, tk = NSA_GROUP, NSA_DH, NSA_KEY_TILE
    rows = hpg * Q_BLOCK
    n_sel = ks_ref.shape[3] - dh
    n_cmp = kc_ref.shape[2]
    q_all = q_ref[0]
    q = jnp.concatenate([q_all[:, h * dh:(h + 1) * dh] for h in range(hpg)], axis=0)
    by_head = lambda o: jnp.concatenate([o[h * Q_BLOCK:(h + 1) * Q_BLOCK] for h in range(hpg)], axis=1)

    tq_c = q0 + lax.broadcasted_iota(jnp.int32, (Q_BLOCK, n_cmp), 0)
    c_end = lax.broadcasted_iota(jnp.int32, (Q_BLOCK, n_cmp), 1) * CMP_STRIDE + (CMP_LEN - 1)
    c_ok = jnp.concatenate([jnp.where(c_end <= tq_c, 1.0, 0.0)] * hpg, axis=0) > 0.5
    s = jnp.where(c_ok, lax.dot_general(q, kc_ref[0, 0], _NT, preferred_element_type=f32), NEG_BIG)
    m = jnp.max(s, axis=-1, keepdims=True)
    e = jnp.where(c_ok, jnp.exp(s - m), 0.0)
    p = e / jnp.maximum(jnp.sum(e, axis=-1, keepdims=True), 1e-30)
    o_cmp = jnp.dot(p.astype(bf16), vc_ref[0, 0], preferred_element_type=f32)

    p_sum = p[0:Q_BLOCK]
    for h in range(1, hpg):
        p_sum = p_sum + p[h * Q_BLOCK:(h + 1) * Q_BLOCK]
    c_start = lax.broadcasted_iota(jnp.int32, (n_sel, n_cmp), 1) * CMP_STRIDE
    j_blk = lax.broadcasted_iota(jnp.int32, (n_sel, n_cmp), 0)
    overlap_t = jnp.where((c_start < (j_blk + 1) * SEL_BLOCK) & (c_start + CMP_LEN > j_blk * SEL_BLOCK),
                          1.0, 0.0).astype(bf16)
    p_hi = p_sum.astype(bf16)
    p_lo = (p_sum - p_hi.astype(f32)).astype(bf16)
    imp = (lax.dot_general(overlap_t, p_hi, _NT, preferred_element_type=f32)
           + lax.dot_general(overlap_t, p_lo, _NT, preferred_element_type=f32))
    jb = lax.broadcasted_iota(jnp.int32, (n_sel, Q_BLOCK), 0)
    cur = (q0 + lax.broadcasted_iota(jnp.int32, (n_sel, Q_BLOCK), 1)) // SEL_BLOCK
    forced = (jb == 0) | (jb == cur) | (jb == cur - 1)
    imp = jnp.where(forced, -NEG_BIG, jnp.where(jb <= cur, imp, NEG_BIG))
    rank = jnp.zeros((n_sel, Q_BLOCK), f32)
    for i in range(n_sel):
        row_i = imp[i:i + 1, :]
        ahead = (row_i > imp) | ((row_i == imp) & (jb > i))
        rank = rank + jnp.where(ahead, 1.0, 0.0)
    sel_bias = jnp.where(rank < float(min(SEL_TOP, n_sel)), 0.0, NEG_BIG).T.astype(bf16)

    q_sel = jnp.concatenate([q, jnp.concatenate([sel_bias] * hpg, axis=0)], axis=1)

    def tile_update(kt, carry, on_diagonal):
        m, l, acc = carry
        k0 = pl.multiple_of(kt * tk, tk)
        s = lax.dot_general(q_sel, ks_ref[0, 0, pl.ds(k0, tk), :], _NT, preferred_element_type=f32)
        if on_diagonal:
            t_pos = q0 + lax.broadcasted_iota(jnp.int32, (Q_BLOCK, tk), 0)
            key = k0 + lax.broadcasted_iota(jnp.int32, (Q_BLOCK, tk), 1)
            causal = jnp.concatenate([jnp.where(key <= t_pos, 0.0, NEG_BIG)] * hpg, axis=0)
            s = s + causal
        m_new = jnp.maximum(m, jnp.max(s, axis=-1, keepdims=True))
        alpha = jnp.exp(m - m_new)
        p = jnp.exp(s - m_new)
        l = alpha * l + jnp.sum(p, axis=-1, keepdims=True)
        acc = alpha * acc + jnp.dot(p.astype(bf16), vs_ref[0, 0, pl.ds(k0, tk), :], preferred_element_type=f32)
        return m_new, l, acc

    init = (jnp.full((rows, 1), NEG_BIG, f32), jnp.zeros((rows, 1), f32), jnp.zeros((rows, dh), f32))
    last_tile = (q0 + Q_BLOCK - 1) // tk
    carry = lax.fori_loop(0, last_tile, lambda kt, c: tile_update(kt, c, False), init)
    m, l, acc = tile_update(last_tile, carry, True)
    o_sel = acc / jnp.maximum(l, 1e-30)

    w0 = pl.multiple_of(jnp.maximum(q0 - WINDOW, 0), Q_BLOCK)
    kwin = kw_ref[0, 0, pl.ds(w0, NSA_WIN_SPAN), :]
    vwin = vw_ref[0, 0, pl.ds(w0, NSA_WIN_SPAN), :]
    tq = q0 + lax.broadcasted_iota(jnp.int32, (Q_BLOCK, NSA_WIN_SPAN), 0)
    kp = w0 + lax.broadcasted_iota(jnp.int32, (Q_BLOCK, NSA_WIN_SPAN), 1)
    dlt = tq - kp
    wbias = jnp.where((dlt >= 0) & (dlt < WINDOW), 0.0, NEG_BIG).astype(f32)
    wbias = jnp.concatenate([wbias] * hpg, axis=0)
    s = lax.dot_general(q, kwin, _NT, preferred_element_type=f32) + wbias
    mw = jnp.max(s, axis=-1, keepdims=True)
    p = jnp.exp(s - mw)
    lw = jnp.sum(p, axis=-1, keepdims=True)
    o_win = jnp.dot(p.astype(bf16), vwin, preferred_element_type=f32) / jnp.maximum(lw, 1e-30)

    o_ref[0] = (gc_ref[0] * by_head(o_cmp) + gs_ref[0] * by_head(o_sel) + gw_ref[0] * by_head(o_win)
                ).astype(o_ref.dtype)


def _nsa_attn(q, k_cmp, v_cmp, ks, vs, kw, vw, g_cmp, g_sel, g_win):
    bsz, s, _ = q.shape
    ng, dh = NSA_KV_HEADS, NSA_DH
    gw = NSA_GROUP * dh
    kv_spec = pl.BlockSpec((1, 1, s, dh), lambda b, g, i: (b, g, 0, 0))
    ks_spec = pl.BlockSpec((1, 1, s, ks.shape[3]), lambda b, g, i: (b, g, 0, 0))
    cmp_spec = pl.BlockSpec((1, 1, k_cmp.shape[2], dh), lambda b, g, i: (b, g, 0, 0))
    q_spec = pl.BlockSpec((1, Q_BLOCK, gw), lambda b, g, i: (b, i, g))
    return pl.pallas_call(
        _nsa_attn_kernel,
        out_shape=jax.ShapeDtypeStruct(q.shape, jnp.bfloat16),
        grid=(bsz, ng, s // Q_BLOCK),
        in_specs=[q_spec, cmp_spec, cmp_spec, ks_spec, kv_spec, kv_spec, kv_spec, q_spec, q_spec, q_spec],
        out_specs=q_spec,
        compiler_params=pltpu.CompilerParams(
            dimension_semantics=("parallel", "parallel", "arbitrary"),
            vmem_limit_bytes=V7X_VMEM_LIMIT_BYTES),
        name="nsa_attn",
    )(q, k_cmp, v_cmp, ks, vs, kw, vw, g_cmp, g_sel, g_win)


def _nsa_mixer(z, nsa_tabs, cmp_pos, cmp_w1, cmp_w2):
    bsz, s, _ = z.shape
    ng, dh = NSA_KV_HEADS, NSA_DH
    cos, sin, cos_c, sin_c = nsa_tabs
    q, ks, vs, kw, vw, kc, vc, g_cmp, g_sel, g_win = _nsa_prep(z, cos, sin)
    n_blk = s // CMP_STRIDE
    flat = lambda t: t.reshape(bsz * ng, n_blk, CMP_STRIDE * dh)
    k_cmp, v_cmp = _nsa_compress(flat(kc), flat(vc), cmp_pos.reshape(2, 1, CMP_LEN * dh), cmp_w1, cmp_w2,
                                 cos_c, sin_c)
    grouped = lambda t: t.reshape(bsz, ng, n_blk, dh)
    return _nsa_attn(q, grouped(k_cmp), grouped(v_cmp), ks, vs, kw, vw, g_cmp, g_sel, g_win)


def _nsa_tables(cos, sin):
    end = CMP_LEN - 1
    at_ends = lambda t: jnp.pad(t[:, end::CMP_STRIDE, :NSA_DH], ((0, 0), (0, 1), (0, 0)))
    return cos, sin, at_ends(cos), at_ends(sin)


def _hybrid_mixer(h, bsz, ret_tabs, nsa_tabs, w_in, sc_conv, rw_mu, rw_w0, rw_w2, rw_a0, rw_a2, rw_g2,
                  rw_k_k, rw_k_a, rw_r_k, rw_lnx_g, rw_lnx_b, nsa_cmp_pos, nsa_cmp_w1, nsa_cmp_w2,
                  w_branch):
    m = h.shape[0]
    s = m // bsz
    offs = np.cumsum((0,) + GROUP_SIZES)
    group = lambda i: _mm(h, w_in[:, offs[i]:offs[i + 1]], keep_pad=True)
    per_batch = lambda z: z.reshape(bsz, s, z.shape[1])
    o_a = _short_conv_mixer(group(0), sc_conv, s)
    o_b = _rwkv7_mixer(per_batch(group(1)), rw_mu, rw_w0, rw_w2, rw_a0, rw_a2, rw_g2, rw_k_k, rw_k_a, rw_r_k,
                       rw_lnx_g, rw_lnx_b)
    o_c = _retention_mixer(per_batch(group(2)), *ret_tabs)
    o_d = _nsa_mixer(per_batch(group(3)), nsa_tabs, nsa_cmp_pos, nsa_cmp_w1, nsa_cmp_w2)
    outs = [o.reshape(m, o.shape[-1]) for o in (o_a, o_b, o_c, o_d)]
    w_gate = w_in[:, offs[4]:offs[5]].astype(jnp.bfloat16)
    return _merge(h, w_gate, outs, _split(w_branch, BR_SIZES, axis=0))


def kernel(x, mem, positions, ln_mix_pre, ln_mix_post, ln_mem_q, ln_mem_kv, ln_mem_post, ln_ffn_pre, ln_ffn_post, w_in, sc_conv, rw_mu, rw_w0, rw_w2, rw_a0, rw_a2, rw_g2, rw_k_k, rw_k_a, rw_r_k, rw_lnx_g, rw_lnx_b, nsa_cmp_pos, nsa_cmp_w1, nsa_cmp_w2, w_branch, w_out, mem_wq, mem_wkv, mem_wo, ffn_w_up, ffn_conv_w, ffn_conv_b, ffn_w_down):
    bsz, s, d = x.shape
    mem_len = mem.shape[1]
    xf = x.reshape(bsz * s, d)
    h = _prenorm(xf, ln_mix_pre[0])
    ret_cos, ret_sin, nsa_cos, nsa_sin = _rope_tables(positions)
    ret_tabs = (ret_cos, ret_sin)
    nsa_tabs = _nsa_tables(nsa_cos, nsa_sin)
    for l in range(DEPTH):
        merged = _hybrid_mixer(h, bsz, ret_tabs, nsa_tabs, w_in[l], sc_conv[l], rw_mu[l], rw_w0[l], rw_w2[l], rw_a0[l],
                               rw_a2[l], rw_g2[l], rw_k_k[l], rw_k_a[l], rw_r_k[l], rw_lnx_g[l], rw_lnx_b[l],
                               nsa_cmp_pos[l], nsa_cmp_w1[l], nsa_cmp_w2[l], w_branch[l])
        xf, h = _mm_norm_res(merged, w_out[l], xf, ln_mix_post[l], ln_mem_q[l])
        mem_n = _prenorm(mem.reshape(bsz * mem_len, d), ln_mem_kv[l])
        kv = _mm(mem_n, mem_wkv[l]).astype(jnp.bfloat16)
        k_mem = kv[:, :d].reshape(bsz, mem_len, d)
        v_mem = kv[:, d:].reshape(bsz, mem_len, d)
        o = _xattn(h, mem_wq[l], k_mem, v_mem, s)
        xf, h = _mm_norm_res(o, mem_wo[l], xf, ln_mem_post[l], ln_ffn_pre[l])
        act = _ffn_up(h, ffn_w_up[l], ffn_conv_w[l], ffn_conv_b[l], s)
        g_next = ln_mix_pre[l + 1] if l + 1 < DEPTH else None
        xf, h = _mm_norm_res(act, ffn_w_down[l], xf, ln_ffn_post[l], g_next)
    return xf.reshape(bsz, s, d)
```

```python
import functools
import math

import jax
import jax.numpy as jnp
import numpy as np
from jax import lax
from jax.experimental import pallas as pl
from jax.experimental.pallas import tpu as pltpu

D_MODEL = 2048
DEPTH = 2
MEM_HEADS = 4
MEM_HEAD_DIM = D_MODEL // MEM_HEADS
SC_W = D_MODEL // 4
SC_K = 3
RW_HEAD_DIM = 64
RW_W = D_MODEL // 4
RW_HEADS = RW_W // RW_HEAD_DIM
RW_LORA_W = 96
RW_LORA_A = 96
RW_LORA_G = 256
RW_DECAY_SCALE = math.exp(-0.5)
RW_LN_EPS = 64e-5
RET_HEADS = 4
RET_DK = 128
RET_DV = 2 * RET_DK
RET_CHUNK = 128
RET_LN_EPS = 1e-5
NSA_HEADS = 8
NSA_KV_HEADS = 2
NSA_DH = 64
NSA_GROUP = NSA_HEADS // NSA_KV_HEADS
CMP_LEN = 32
CMP_STRIDE = 16
SEL_BLOCK = 64
SEL_TOP = 16
WINDOW = 512
Q_BLOCK = 128
ROPE_THETA = 10000.0
EPS = 1e-6
N_BRANCH = 4

SC_SIZES = (SC_W, SC_W, SC_W)
RW_SIZES = (RW_W, RW_W, RW_W, RW_LORA_W, RW_LORA_A, RW_LORA_G)
RET_SIZES = (RET_HEADS * RET_DK, RET_HEADS * RET_DK, RET_HEADS * RET_DV, RET_HEADS * RET_DV)
NSA_SIZES = (NSA_HEADS * NSA_DH,) + (NSA_KV_HEADS * NSA_DH,) * 6 + (NSA_HEADS * 3,)
GATE_SIZES = (D_MODEL,) * N_BRANCH
GROUP_SIZES = (sum(SC_SIZES), sum(RW_SIZES), sum(RET_SIZES), sum(NSA_SIZES), sum(GATE_SIZES))
BR_SIZES = (SC_W, RW_W, RET_HEADS * RET_DV, NSA_HEADS * NSA_DH)

V7X_VMEM_LIMIT_BYTES = 48 * 1024 * 1024
LANES = 128
_NT = (((1,), (1,)), ((), ()))


def _split(x, sizes, axis=-1):
    return jnp.split(x, [int(c) for c in np.cumsum(sizes)[:-1]], axis=axis)


def _pick_tile(n, candidates):
    for c in candidates:
        if n % c == 0:
            return c
    return n


def _bdot(a, b):
    return jnp.dot(a.astype(jnp.bfloat16), b.astype(jnp.bfloat16), preferred_element_type=jnp.float32)


def _bdot_nt(a, b):
    return lax.dot_general(a.astype(jnp.bfloat16), b.astype(jnp.bfloat16), _NT,
                           preferred_element_type=jnp.float32)


def _head_sum(x, bd):
    hi = x.astype(jnp.bfloat16)
    lo = (x - hi.astype(jnp.float32)).astype(jnp.bfloat16)
    return (jnp.dot(hi, bd, preferred_element_type=jnp.float32)
            + jnp.dot(lo, bd, preferred_element_type=jnp.float32))


def _mm_kernel(a_ref, w_ref, o_ref, acc_ref):
    k = pl.program_id(2)

    @pl.when(k == 0)
    def _():
        acc_ref[...] = jnp.zeros_like(acc_ref)

    acc_ref[...] += jnp.dot(a_ref[...], w_ref[...], preferred_element_type=jnp.float32)

    @pl.when(k == pl.num_programs(2) - 1)
    def _():
        o_ref[...] = acc_ref[...]


def _mm(a, w, keep_pad=False):
    n = w.shape[1]
    w = w.astype(jnp.bfloat16)
    n_pad = LANES * ((n + LANES - 1) // LANES)
    if n_pad != n:
        w = jnp.pad(w, ((0, 0), (0, n_pad - n)))
    out = _mm_cols(a, w[None], 0, 0, n_pad)
    return out if keep_pad or n_pad == n else out[:, :n]


def _mm_cols(a, w_stack, layer, col0, n):
    m, k = a.shape
    tn = next(c for c in (n if n <= 1536 else 1024, 1024, 768, 512, 256, 128) if n % c == 0 and col0 % c == 0)
    tm = _pick_tile(m, (1024, 512, 256, 128, 8))
    tk = _pick_tile(k, (2048, 1280, 1408, 1024, 512, 256, 128))
    j0 = col0 // tn
    return pl.pallas_call(
        _mm_kernel,
        out_shape=jax.ShapeDtypeStruct((m, n), jnp.float32),
        grid=(m // tm, n // tn, k // tk),
        in_specs=[pl.BlockSpec((tm, tk), lambda i, j, kk: (i, kk)),
                  pl.BlockSpec((None, tk, tn), lambda i, j, kk: (layer, kk, j0 + j))],
        out_specs=pl.BlockSpec((tm, tn), lambda i, j, kk: (i, j)),
        scratch_shapes=[pltpu.VMEM((tm, tn), jnp.float32)],
        compiler_params=pltpu.CompilerParams(
            dimension_semantics=("parallel", "parallel", "arbitrary"),
            vmem_limit_bytes=V7X_VMEM_LIMIT_BYTES),
        name="mm",
    )(a.astype(jnp.bfloat16), w_stack)


def _rms(y, g):
    return y * lax.rsqrt(jnp.mean(y * y, axis=-1, keepdims=True) + EPS) * g


def _prenorm_kernel(x_ref, g_ref, h_ref):
    h_ref[...] = _rms(x_ref[...], g_ref[...]).astype(h_ref.dtype)


def _prenorm(x, g):
    m, d = x.shape
    tm = _pick_tile(m, (512, 256, 128, 8))
    return pl.pallas_call(
        _prenorm_kernel,
        out_shape=jax.ShapeDtypeStruct((m, d), jnp.bfloat16),
        grid=(m // tm,),
        in_specs=[pl.BlockSpec((tm, d), lambda i: (i, 0)), pl.BlockSpec((1, d), lambda i: (0, 0))],
        out_specs=pl.BlockSpec((tm, d), lambda i: (i, 0)),
        compiler_params=pltpu.CompilerParams(dimension_semantics=("parallel",)),
        name="prenorm",
    )(x, g.reshape(1, d))


def _mm_norm_res_kernel(emit_h, whole_k, a_ref, w_ref, res_ref, g_ref, *rest):
    if emit_h:
        g2_ref, x_ref, h_ref = rest[:3]
    else:
        x_ref = rest[0]

    def finish(y):
        x_new = res_ref[...] + _rms(y, g_ref[...])
        x_ref[...] = x_new
        if emit_h:
            h_ref[...] = _rms(x_new, g2_ref[...]).astype(h_ref.dtype)

    if whole_k:
        finish(jnp.dot(a_ref[...], w_ref[...], preferred_element_type=jnp.float32))
        return
    acc_ref = rest[-1]
    k = pl.program_id(1)

    @pl.when(k == 0)
    def _():
        acc_ref[...] = jnp.zeros_like(acc_ref)

    acc_ref[...] += jnp.dot(a_ref[...], w_ref[...], preferred_element_type=jnp.float32)

    @pl.when(k == pl.num_programs(1) - 1)
    def _():
        finish(acc_ref[...])


def _mm_norm_res(a, w, res, g, g_next):
    m, k = a.shape
    d = w.shape[1]
    tk = _pick_tile(k, (2048, 1408, 1024, 512))
    tm = 512
    whole_k = tk == k
    emit_h = g_next is not None
    row = pl.BlockSpec((tm, d), lambda i, kk: (i, 0))
    vec = pl.BlockSpec((1, d), lambda i, kk: (0, 0))
    w_spec = (pl.BlockSpec((tk, d), lambda i, kk: (0, 0), pipeline_mode=pl.Buffered(1)) if whole_k
              else pl.BlockSpec((tk, d), lambda i, kk: (kk, 0)))
    x_sd = jax.ShapeDtypeStruct((m, d), jnp.float32)
    h_sd = jax.ShapeDtypeStruct((m, d), jnp.bfloat16)
    out = pl.pallas_call(
        functools.partial(_mm_norm_res_kernel, emit_h, whole_k),
        out_shape=(x_sd, h_sd) if emit_h else x_sd,
        grid=(m // tm, k // tk),
        in_specs=[pl.BlockSpec((tm, tk), lambda i, kk: (i, kk)), w_spec, row, vec] + ([vec] if emit_h else []),
        out_specs=(row, row) if emit_h else row,
        scratch_shapes=[] if whole_k else [pltpu.VMEM((tm, d), jnp.float32)],
        compiler_params=pltpu.CompilerParams(
            dimension_semantics=("parallel", "arbitrary"), vmem_limit_bytes=V7X_VMEM_LIMIT_BYTES),
        name="mm_norm_res",
    )(a.astype(jnp.bfloat16), w.astype(jnp.bfloat16), res, g.reshape(1, d),
      *([g_next.reshape(1, d)] if emit_h else []))
    return out if emit_h else (out, None)


FFN_TM = 1024
FFN_TN = 512
HALO = 8


def _shift_rows(u, prev, n):
    rolled = pltpu.roll(u, n, axis=0)
    rows = lax.broadcasted_iota(jnp.int32, u.shape, 0)
    for r in range(n):
        rolled = jnp.where(rows == r, prev[HALO - n + r:HALO - n + r + 1, :], rolled)
    return rolled


def _ffn_up_kernel(seq_len, h_ref, halo_ref, wa_ref, wb_ref, cwa_ref, cwb_ref, ba_ref, bb_ref, o_ref):
    i = pl.program_id(0)
    h = h_ref[...]
    at_start = (i * FFN_TM) % seq_len == 0
    halo = jnp.where(at_start, jnp.zeros_like(halo_ref[...]), halo_ref[...])

    def conv(w_ref, cw_ref, b_ref):
        u = jnp.dot(h, w_ref[...], preferred_element_type=jnp.float32)
        up = jnp.dot(halo, w_ref[...], preferred_element_type=jnp.float32)
        cw = cw_ref[...]
        y = cw[0:1] * _shift_rows(u, up, 2)
        y = y + cw[1:2] * _shift_rows(u, up, 1)
        y = y + cw[2:3] * u
        return y + b_ref[...]

    a = conv(wa_ref, cwa_ref, ba_ref)
    b = conv(wb_ref, cwb_ref, bb_ref)
    o_ref[...] = (jax.nn.gelu(a, approximate=True) * b).astype(o_ref.dtype)


def _ffn_up(h, w_up, w_conv, b_conv, seq_len):
    m, d = h.shape
    f = w_up.shape[1] // 2
    nj = f // FFN_TN
    w_up = w_up.astype(jnp.bfloat16)
    b_conv = b_conv.reshape(1, 2 * f)
    tiles_per_halo = FFN_TM // HALO
    return pl.pallas_call(
        functools.partial(_ffn_up_kernel, seq_len),
        out_shape=jax.ShapeDtypeStruct((m, f), jnp.bfloat16),
        grid=(m // FFN_TM, nj),
        in_specs=[pl.BlockSpec((FFN_TM, d), lambda i, j: (i, 0)),
                  pl.BlockSpec((HALO, d), lambda i, j: (jnp.maximum(i * tiles_per_halo - 1, 0), 0)),
                  pl.BlockSpec((d, FFN_TN), lambda i, j: (0, j)),
                  pl.BlockSpec((d, FFN_TN), lambda i, j: (0, j + nj)),
                  pl.BlockSpec((3, FFN_TN), lambda i, j: (0, j)),
                  pl.BlockSpec((3, FFN_TN), lambda i, j: (0, j + nj)),
                  pl.BlockSpec((1, FFN_TN), lambda i, j: (0, j)),
                  pl.BlockSpec((1, FFN_TN), lambda i, j: (0, j + nj))],
        out_specs=pl.BlockSpec((FFN_TM, FFN_TN), lambda i, j: (i, j)),
        compiler_params=pltpu.CompilerParams(
            dimension_semantics=("parallel", "arbitrary"), vmem_limit_bytes=V7X_VMEM_LIMIT_BYTES),
        name="ffn_up",
    )(h, h, w_up, w_up, w_conv, w_conv, b_conv, b_conv)


MERGE_TM = 512
MERGE_TN = 512


def _merge_kernel(h_ref, wg0, wg1, wg2, wg3, o0, o1, o2, o3, wb0, wb1, wb2, wb3, out_ref):
    h = h_ref[...]
    acc = None
    for wg, o, wb in ((wg0, o0, wb0), (wg1, o1, wb1), (wg2, o2, wb2), (wg3, o3, wb3)):
        gate = jax.nn.sigmoid(jnp.dot(h, wg[...], preferred_element_type=jnp.float32))
        term = gate * jnp.dot(o[...], wb[...], preferred_element_type=jnp.float32)
        acc = term if acc is None else acc + term
    out_ref[...] = acc.astype(out_ref.dtype)


def _merge(h, w_stack, layer, gate_col0, outs, w_branches):
    m, d = h.shape
    nj = d // MERGE_TN
    j0 = gate_col0 // MERGE_TN
    assert gate_col0 % MERGE_TN == 0
    gate_specs = [pl.BlockSpec((None, d, MERGE_TN), functools.partial(lambda b, i, j: (layer, 0, j0 + b * nj + j), b))
                  for b in range(N_BRANCH)]
    o_specs = [pl.BlockSpec((MERGE_TM, o.shape[1]), lambda i, j: (i, 0)) for o in outs]
    wb_specs = [pl.BlockSpec((wb.shape[0], MERGE_TN), lambda i, j: (0, j)) for wb in w_branches]
    return pl.pallas_call(
        _merge_kernel,
        out_shape=jax.ShapeDtypeStruct((m, d), jnp.bfloat16),
        grid=(m // MERGE_TM, nj),
        in_specs=[pl.BlockSpec((MERGE_TM, d), lambda i, j: (i, 0))] + gate_specs + o_specs + wb_specs,
        out_specs=pl.BlockSpec((MERGE_TM, MERGE_TN), lambda i, j: (i, j)),
        compiler_params=pltpu.CompilerParams(
            dimension_semantics=("parallel", "arbitrary"), vmem_limit_bytes=V7X_VMEM_LIMIT_BYTES),
        name="merge",
    )(h, *([w_stack] * N_BRANCH), *[o.astype(jnp.bfloat16) for o in outs],
      *[wb.astype(jnp.bfloat16) for wb in w_branches])


XATT_TM = 512


def _xattn_kernel(h_ref, wq_ref, k_ref, v_ref, o_ref):
    q = jnp.dot(h_ref[...], wq_ref[...], preferred_element_type=jnp.float32)
    q = (q * (MEM_HEAD_DIM ** -0.5)).astype(jnp.bfloat16)
    for hh in range(MEM_HEADS):
        sl = slice(hh * MEM_HEAD_DIM, (hh + 1) * MEM_HEAD_DIM)
        s = lax.dot_general(q[:, sl], k_ref[0, :, sl], _NT, preferred_element_type=jnp.float32)
        p = jnp.exp(s - jnp.max(s, axis=-1, keepdims=True))
        p = p / jnp.sum(p, axis=-1, keepdims=True)
        o_ref[:, sl] = jnp.dot(p.astype(jnp.bfloat16), v_ref[0, :, sl],
                               preferred_element_type=jnp.float32).astype(o_ref.dtype)


def _xattn(h, wq, k, v, seq_len):
    m, d = h.shape
    mem_len = k.shape[1]
    per_batch = seq_len // XATT_TM
    kv_spec = pl.BlockSpec((1, mem_len, d), lambda i: (i // per_batch, 0, 0))
    return pl.pallas_call(
        _xattn_kernel,
        out_shape=jax.ShapeDtypeStruct((m, d), jnp.bfloat16),
        grid=(m // XATT_TM,),
        in_specs=[pl.BlockSpec((XATT_TM, d), lambda i: (i, 0)),
                  pl.BlockSpec((d, d), lambda i: (0, 0)), kv_spec, kv_spec],
        out_specs=pl.BlockSpec((XATT_TM, d), lambda i: (i, 0)),
        compiler_params=pltpu.CompilerParams(
            dimension_semantics=("parallel",), vmem_limit_bytes=V7X_VMEM_LIMIT_BYTES),
        name="xattn",
    )(h, wq.astype(jnp.bfloat16), k, v)


def _rope_freqs(d):
    return ROPE_THETA ** (-jnp.arange(0, d, 2, dtype=jnp.float32) / d)


def _retnet_freqs(d):
    return 1.0 / (ROPE_THETA ** jnp.linspace(0.0, 1.0, d // 2, dtype=jnp.float32))


WKV_CHUNK = 64


def _rwkv7_kernel(z_ref, mu_ref, w0_ref, a0_ref, kk_ref, ka_ref, rk_ref, lng_ref, lnb_ref,
                  ww2_ref, wa2_ref, wg2_ref, o_ref, s_ref, prev_ref, y_ref):
    c = WKV_CHUNK
    hd = RW_HEAD_DIM
    w = RW_W
    nb = z_ref.shape[0]
    f32, bf16 = jnp.float32, jnp.bfloat16

    @pl.when(pl.program_id(0) == 0)
    def _():
        s_ref[...] = jnp.zeros_like(s_ref)
        prev_ref[...] = jnp.zeros_like(prev_ref)

    lane_h = lax.broadcasted_iota(jnp.int32, (w, w), 0) // hd
    lane_w = lax.broadcasted_iota(jnp.int32, (w, w), 1) // hd
    head_bd = jnp.where(lane_h == lane_w, 1.0, 0.0).astype(bf16)
    row = lax.broadcasted_iota(jnp.int32, (c, c), 0)
    col = lax.broadcasted_iota(jnp.int32, (c, c), 1)
    tri = jnp.where(row >= col, 1.0, 0.0).astype(bf16)
    eye = jnp.where(row == col, 1.0, 0.0).astype(f32)
    r2 = lax.broadcasted_iota(jnp.int32, (2 * c, 2 * c), 0)
    c2 = lax.broadcasted_iota(jnp.int32, (2 * c, 2 * c), 1)
    rr = jnp.where(r2 >= c, r2 - c, r2)
    cc = jnp.where(c2 >= c, c2 - c, c2)
    tri_mask = cc < rr + jnp.where(r2 >= c, 1, 0)

    def prepare(b):
        z = z_ref[b]
        rows = lax.broadcasted_iota(jnp.int32, z.shape, 0)
        z_prev = jnp.where(rows == 0, prev_ref[b], pltpu.roll(z, 1, axis=0))
        prev_ref[b] = z[c - 1:c, :]
        z = z + (z_prev - z) * mu_ref[...]
        r, k, v, lora = z[:, 0:w], z[:, w:2 * w], z[:, 2 * w:3 * w], z[:, 3 * w:4 * w]
        logw = -RW_DECAY_SCALE * jax.nn.sigmoid(w0_ref[...] + _bdot(jnp.tanh(lora), ww2_ref[...]))
        a = jax.nn.sigmoid(a0_ref[...] + _bdot(lora, wa2_ref[...]))
        g = _bdot(jax.nn.sigmoid(lora), wg2_ref[...])
        kk = k * kk_ref[...]
        kk = kk / jnp.maximum(jnp.sqrt(_head_sum(kk * kk, head_bd)), 1e-12)
        k = k * (1.0 + (a - 1.0) * ka_ref[...])
        hi = logw.astype(bf16)
        rem = logw - hi.astype(f32)
        mid = rem.astype(bf16)
        lo = (rem - mid.astype(f32)).astype(bf16)
        cum = (jnp.dot(tri, hi, preferred_element_type=f32) + jnp.dot(tri, mid, preferred_element_type=f32)
               + jnp.dot(tri, lo, preferred_element_type=f32))
        kka = kk * a
        e_neg = jnp.exp(-cum)
        last = cum[c - 1:c, :]
        dec = jnp.exp(last - cum)
        return dict(r=r, k=k, v=v, g=g, vb=v.astype(bf16), g_last=jnp.exp(last),
                    qh=(kk * jnp.exp(cum - logw)).astype(bf16), rh=(r * jnp.exp(cum)).astype(bf16),
                    bh=(kka * e_neg).astype(bf16), kh=(k * e_neg).astype(bf16),
                    bd=(kka * dec).astype(bf16), kd=(k * dec).astype(bf16))

    pre = [prepare(b) for b in range(nb)]

    units = [(b, h) for b in range(nb) for h in range(RW_HEADS)]
    n_units = range(len(units))
    part = lambda name, i: pre[units[i][0]][name][:, units[i][1] * hd:(units[i][1] + 1) * hd]
    a1 = [jnp.concatenate([part("qh", i), part("rh", i)], axis=0) for i in n_units]
    b1 = [jnp.concatenate([part("bh", i), part("kh", i)], axis=0) for i in n_units]
    ss = [jnp.where(tri_mask, lax.dot_general(a1[i], b1[i], _NT, preferred_element_type=f32), 0.0)
          for i in n_units]
    s_old = [s_ref[i] for i in n_units]
    qr = [lax.dot_general(a1[i], s_old[i].astype(bf16), _NT, preferred_element_type=f32) for i in n_units]
    lm = [_bdot(ss[i][:, c:2 * c], part("vb", i)) for i in n_units]
    n = [-ss[i][0:c, 0:c] for i in n_units]
    t = [eye + n[i] for i in n_units]
    for _ in range(5):
        n = [_bdot(n[i], n[i]) for i in n_units]
        t = [t[i] + _bdot(t[i], n[i]) for i in n_units]
    u = [-_bdot(t[i], qr[i][0:c] + lm[i][0:c]) for i in n_units]
    for i in n_units:
        b, h = units[i]
        y_ref[b, :, h * hd:(h + 1) * hd] = qr[i][c:2 * c] + lm[i][c:2 * c] + _bdot(ss[i][c:2 * c, 0:c], u[i])
    for i in n_units:
        zt = jnp.concatenate([u[i], part("v", i)], axis=0).T
        x = jnp.concatenate([part("bd", i), part("kd", i)], axis=0)
        s_ref[i] = s_old[i] * part("g_last", i) + _bdot(zt, x)

    for b in range(nb):
        p = pre[b]
        y = y_ref[b]
        mean = _head_sum(y, head_bd) * (1.0 / hd)
        yc = y - mean
        var = _head_sum(yc * yc, head_bd) * (1.0 / hd)
        yn = yc * lax.rsqrt(var + RW_LN_EPS) * lng_ref[...] + lnb_ref[...]
        bonus = _head_sum(p["r"] * p["k"] * rk_ref[...], head_bd) * p["v"]
        o_ref[b] = ((yn + bonus) * p["g"]).astype(o_ref.dtype)


def _rwkv7_mixer(z, mu, w0, w_w2, a0, w_a2, w_g2, k_k, k_a, r_k, lnx_g, lnx_b):
    bsz, s, zw = z.shape
    w = RW_W
    n_lora = RW_LORA_W + RW_LORA_A + RW_LORA_G
    row = lambda t: t.reshape(1, -1)
    mu = jnp.pad(mu, (0, zw - mu.shape[0])).reshape(1, zw)
    o_w, o_a = RW_LORA_W, RW_LORA_W + RW_LORA_A
    pad_rows = lambda m, lo: jnp.pad(m, ((lo, w - lo - m.shape[0]), (0, 0))).astype(jnp.bfloat16)
    vec = pl.BlockSpec((1, w), lambda c: (0, 0))
    mat = pl.BlockSpec((w, w), lambda c: (0, 0))
    assert zw == 4 * w and n_lora <= w
    return pl.pallas_call(
        _rwkv7_kernel,
        out_shape=jax.ShapeDtypeStruct((bsz, s, w), jnp.bfloat16),
        grid=(s // WKV_CHUNK,),
        in_specs=[pl.BlockSpec((bsz, WKV_CHUNK, zw), lambda c: (0, c, 0)),
                  pl.BlockSpec((1, zw), lambda c: (0, 0))] + [vec] * 7 + [mat] * 3,
        out_specs=pl.BlockSpec((bsz, WKV_CHUNK, w), lambda c: (0, c, 0)),
        scratch_shapes=[pltpu.VMEM((bsz * RW_HEADS, RW_HEAD_DIM, RW_HEAD_DIM), jnp.float32),
                        pltpu.VMEM((bsz, 1, zw), jnp.float32),
                        pltpu.VMEM((bsz, WKV_CHUNK, w), jnp.float32)],
        compiler_params=pltpu.CompilerParams(dimension_semantics=("arbitrary",)),
        name="rwkv7",
    )(z, mu, row(w0), row(a0), row(k_k), row(k_a), row(r_k), row(lnx_g), row(lnx_b),
      pad_rows(w_w2, 0), pad_rows(w_a2, o_w), pad_rows(w_g2, o_a))


RET_LOG_DECAY = tuple(math.log(1.0 - 2.0 ** (-5.0 - h)) for h in range(RET_HEADS))


def _retention_kernel(q_ref, k_ref, v_ref, g_ref, cos_ref, sin_ref, o_ref, s_ref):
    c, dk, dv = RET_CHUNK, RET_DK, RET_DV
    f32 = jnp.float32

    @pl.when(pl.program_id(1) == 0)
    def _():
        s_ref[...] = jnp.zeros_like(s_ref)

    cos = cos_ref[0]
    sin = sin_ref[0]
    diff = (lax.broadcasted_iota(jnp.int32, (c, c), 0) - lax.broadcasted_iota(jnp.int32, (c, c), 1)).astype(f32)
    tok = lax.broadcasted_iota(jnp.int32, (c, 1), 0).astype(f32)
    rope = lambda x: x * cos + pltpu.roll(x, dk // 2, axis=1) * sin
    for h in range(RET_HEADS):
        lg = RET_LOG_DECAY[h]
        q = rope(q_ref[0, :, h * dk:(h + 1) * dk])
        k = rope(k_ref[0, :, h * dk:(h + 1) * dk]) * (dk ** -0.5)
        v = v_ref[0, :, h * dv:(h + 1) * dv]
        decay_in = jnp.where(diff >= 0, jnp.exp(jnp.maximum(diff, 0.0) * lg), 0.0)
        scores = _bdot_nt(q, k) * decay_in
        state = s_ref[h]
        o = _bdot(scores, v) + _bdot(q * jnp.exp((tok + 1.0) * lg), state)
        s_ref[h] = state * math.exp(c * lg) + _bdot((k * jnp.exp((c - 1.0 - tok) * lg)).T, v)
        mu = jnp.mean(o, axis=-1, keepdims=True)
        oc = o - mu
        var = jnp.mean(oc * oc, axis=-1, keepdims=True)
        gate = g_ref[0, :, h * dv:(h + 1) * dv]
        o_ref[0, :, h * dv:(h + 1) * dv] = (oc * lax.rsqrt(var + RET_LN_EPS) * gate * jax.nn.sigmoid(gate)
                                            ).astype(o_ref.dtype)


def _retention_mixer(z, cos, sin):
    bsz, s, _ = z.shape
    qk_w, v_w = RET_HEADS * RET_DK, RET_HEADS * RET_DV
    c = RET_CHUNK
    tab = pl.BlockSpec((1, c, RET_DK), lambda b, i: (b, i, 0))
    return pl.pallas_call(
        _retention_kernel,
        out_shape=jax.ShapeDtypeStruct((bsz, s, v_w), jnp.bfloat16),
        grid=(bsz, s // c),
        in_specs=[pl.BlockSpec((1, c, qk_w), lambda b, i: (b, i, 0)),
                  pl.BlockSpec((1, c, qk_w), lambda b, i: (b, i, 1)),
                  pl.BlockSpec((1, c, v_w), lambda b, i: (b, i, 1)),
                  pl.BlockSpec((1, c, v_w), lambda b, i: (b, i, 2)), tab, tab],
        out_specs=pl.BlockSpec((1, c, v_w), lambda b, i: (b, i, 0)),
        scratch_shapes=[pltpu.VMEM((RET_HEADS, RET_DK, RET_DV), jnp.float32)],
        compiler_params=pltpu.CompilerParams(dimension_semantics=("parallel", "arbitrary")),
        name="retention",
    )(z, z, z, z, cos, sin)


TRIG_TM = 1024


def _trig_kernel(pos_ref, freq_ref, rc_ref, rs_ref, nc_ref, ns_ref):
    ang = pos_ref[...] * freq_ref[...]
    c, sn = jnp.cos(ang), jnp.sin(ang)
    n_ret, n_nsa = RET_DK // 2, NSA_DH // 2
    rc_ref[...] = jnp.concatenate([c[:, :n_ret]] * 2, axis=1)
    rs_ref[...] = jnp.concatenate([-sn[:, :n_ret], sn[:, :n_ret]], axis=1)
    nsa_c, nsa_s = c[:, n_ret:n_ret + n_nsa], sn[:, n_ret:n_ret + n_nsa]
    nc_ref[...] = jnp.concatenate([nsa_c, nsa_c] * NSA_HEADS, axis=1)
    ns_ref[...] = jnp.concatenate([-nsa_s, nsa_s] * NSA_HEADS, axis=1)


def _rope_tables(positions):
    bsz, s = positions.shape
    m = bsz * s
    n_ret, n_nsa = RET_DK // 2, NSA_DH // 2
    assert n_ret + n_nsa <= LANES
    freq = jnp.concatenate([_retnet_freqs(RET_DK), _rope_freqs(NSA_DH),
                            jnp.zeros((LANES - n_ret - n_nsa,), jnp.float32)]).reshape(1, LANES)
    tm = _pick_tile(m, (TRIG_TM, 512, 256, 128, 8))
    spec = lambda w: pl.BlockSpec((tm, w), lambda i: (i, 0))
    sd = lambda w: jax.ShapeDtypeStruct((m, w), jnp.float32)
    nsa_w = NSA_HEADS * NSA_DH
    tabs = pl.pallas_call(
        _trig_kernel,
        out_shape=(sd(RET_DK), sd(RET_DK), sd(nsa_w), sd(nsa_w)),
        grid=(m // tm,),
        in_specs=[spec(1), pl.BlockSpec((1, LANES), lambda i: (0, 0))],
        out_specs=(spec(RET_DK), spec(RET_DK), spec(nsa_w), spec(nsa_w)),
        compiler_params=pltpu.CompilerParams(dimension_semantics=("parallel",)),
        name="rope_tables",
    )(positions.astype(jnp.float32).reshape(m, 1), freq)
    return [t.reshape(bsz, s, t.shape[1]) for t in tabs]


SC_TM = 512


def _short_conv_kernel(seq_len, b_ref, c_ref, x_ref, ch_ref, xh_ref, w_ref, o_ref):
    at_start = (pl.program_id(0) * SC_TM) % seq_len == 0
    u = c_ref[...] * x_ref[...]
    up = jnp.where(at_start, 0.0, ch_ref[...] * xh_ref[...])
    w = w_ref[...]
    y = w[0:1] * _shift_rows(u, up, 2) + w[1:2] * _shift_rows(u, up, 1) + w[2:3] * u
    o_ref[...] = (b_ref[...] * y).astype(o_ref.dtype)


def _short_conv_mixer(z, w_conv, seq_len):
    m = z.shape[0]
    w = SC_W
    per_halo = SC_TM // HALO
    tile = lambda j: pl.BlockSpec((SC_TM, w), lambda i: (i, j))
    halo = lambda j: pl.BlockSpec((HALO, w), lambda i: (jnp.maximum(i * per_halo - 1, 0), j))
    return pl.pallas_call(
        functools.partial(_short_conv_kernel, seq_len),
        out_shape=jax.ShapeDtypeStruct((m, w), jnp.bfloat16),
        grid=(m // SC_TM,),
        in_specs=[tile(0), tile(1), tile(2), halo(1), halo(2), pl.BlockSpec((SC_K, w), lambda i: (0, 0))],
        out_specs=pl.BlockSpec((SC_TM, w), lambda i: (i, 0)),
        compiler_params=pltpu.CompilerParams(dimension_semantics=("parallel",)),
        name="short_conv",
    )(z, z, z, z, z, w_conv)


NSA_KEY_TILE = 512
NSA_WIN_SPAN = WINDOW + Q_BLOCK
NSA_PREP_TM = 512
NEG_BIG = -1e30
NSA_Q_W = NSA_HEADS * NSA_DH
NSA_KV_W = NSA_KV_HEADS * NSA_DH
NSA_GATE_OFF = NSA_Q_W + 6 * NSA_KV_W


def _rope_lanes(x, cos, sin):
    width = x.shape[1]
    half = NSA_DH // 2
    lane = lax.broadcasted_iota(jnp.int32, x.shape, 1)
    other = jnp.where(lane % NSA_DH < half, pltpu.roll(x, width - half, axis=1), pltpu.roll(x, half, axis=1))
    return x * cos + other * sin


def _nsa_prep_kernel(z_ref, cos_ref, sin_ref, q_ref, ks_ref, vs_ref, kw_ref, vw_ref, kc_ref, vc_ref,
                     gc_ref, gs_ref, gw_ref):
    cos, sin = cos_ref[0], sin_ref[0]
    z = z_ref[0]
    q_ref[0] = (_rope_lanes(z[:, 0:NSA_Q_W], cos, sin) * (NSA_DH ** -0.5)).astype(q_ref.dtype)
    kv = lambda i: z[:, NSA_Q_W + i * NSA_KV_W:NSA_Q_W + (i + 1) * NSA_KV_W]
    cos_kv, sin_kv = cos[:, 0:NSA_KV_W], sin[:, 0:NSA_KV_W]
    pieces = ((kc_ref, kv(0)), (vc_ref, kv(1)), (ks_ref, _rope_lanes(kv(2), cos_kv, sin_kv)), (vs_ref, kv(3)),
              (kw_ref, _rope_lanes(kv(4), cos_kv, sin_kv)), (vw_ref, kv(5)))
    n_sel = ks_ref.shape[3] - NSA_DH
    tok = pl.program_id(1) * NSA_PREP_TM + lax.broadcasted_iota(jnp.int32, (NSA_PREP_TM, n_sel), 0)
    block_1hot = jnp.where(tok // SEL_BLOCK == lax.broadcasted_iota(jnp.int32, (NSA_PREP_TM, n_sel), 1), 1.0, 0.0)
    for ref, val in pieces:
        for g in range(NSA_KV_HEADS):
            val_g = val[:, g * NSA_DH:(g + 1) * NSA_DH]
            if ref is ks_ref:
                val_g = jnp.concatenate([val_g, block_1hot], axis=1)
            ref[0, g] = val_g.astype(ref.dtype)
    gate = jax.nn.sigmoid(z[:, NSA_GATE_OFF:NSA_GATE_OFF + LANES])
    src = lax.broadcasted_iota(jnp.int32, (LANES, NSA_Q_W), 0)
    head = lax.broadcasted_iota(jnp.int32, (LANES, NSA_Q_W), 1) // NSA_DH
    for j, ref in enumerate((gc_ref, gs_ref, gw_ref)):
        expand = jnp.where(src == 3 * head + j, 1.0, 0.0).astype(jnp.bfloat16)
        ref[0] = _head_sum(gate, expand)


def _nsa_prep(z, cos, sin):
    bsz, s, zw = z.shape
    tm = NSA_PREP_TM
    wide = pl.BlockSpec((1, tm, NSA_Q_W), lambda b, i: (b, i, 0))
    grp = pl.BlockSpec((1, NSA_KV_HEADS, tm, NSA_DH), lambda b, i: (b, 0, i, 0))
    grp_sd = lambda dt: jax.ShapeDtypeStruct((bsz, NSA_KV_HEADS, s, NSA_DH), dt)
    ks_w = NSA_DH + s // SEL_BLOCK
    ks_spec = pl.BlockSpec((1, NSA_KV_HEADS, tm, ks_w), lambda b, i: (b, 0, i, 0))
    ks_sd = jax.ShapeDtypeStruct((bsz, NSA_KV_HEADS, s, ks_w), jnp.bfloat16)
    wide_sd = lambda dt: jax.ShapeDtypeStruct((bsz, s, NSA_Q_W), dt)
    bf16, f32 = jnp.bfloat16, jnp.float32
    return pl.pallas_call(
        _nsa_prep_kernel,
        out_shape=(wide_sd(bf16), ks_sd, grp_sd(bf16), grp_sd(bf16), grp_sd(bf16), grp_sd(f32), grp_sd(f32),
                   wide_sd(f32), wide_sd(f32), wide_sd(f32)),
        grid=(bsz, s // tm),
        in_specs=[pl.BlockSpec((1, tm, zw), lambda b, i: (b, i, 0)), wide, wide],
        out_specs=(wide, ks_spec, grp, grp, grp, grp, grp, wide, wide, wide),
        compiler_params=pltpu.CompilerParams(dimension_semantics=("parallel", "parallel")),
        name="nsa_prep",
    )(z, cos, sin)


def _nsa_compress_kernel(hk_ref, hv_ref, pos_ref, w1_ref, w2_ref, cos_ref, sin_ref, kc_ref, vc_ref):
    half = w1_ref.shape[1] // 2
    n = hk_ref.shape[1]
    for i, (h_ref, o_ref) in enumerate(((hk_ref, kc_ref), (hv_ref, vc_ref))):
        h = h_ref[0]
        first = _bdot(h, w1_ref[i, 0:half, :])
        second = pltpu.roll(_bdot(h, w1_ref[i, half:2 * half, :]), n - 1, axis=0)
        bias = _bdot(jnp.broadcast_to(pos_ref[i], (8, 2 * half)), w1_ref[i])[0:1]
        out = _bdot(jax.nn.gelu(first + second + bias, approximate=True), w2_ref[i])
        if i == 0:
            src = lax.broadcasted_iota(jnp.int32, (NSA_DH, NSA_DH), 0)
            dst = lax.broadcasted_iota(jnp.int32, (NSA_DH, NSA_DH), 1)
            swap = jnp.where(src == (dst + NSA_DH // 2) % NSA_DH, 1.0, 0.0).astype(jnp.bfloat16)
            out = out * cos_ref[0] + _head_sum(out, swap) * sin_ref[0]
        o_ref[0] = out.astype(o_ref.dtype)


def _nsa_compress(hk, hv, pos_flat, w1, w2, cos_c, sin_c):
    bg, n, hw = hk.shape
    ng = NSA_KV_HEADS
    blk = pl.BlockSpec((1, n, hw), lambda i: (i, 0, 0))
    tab = pl.BlockSpec((1, n, NSA_DH), lambda i: (i // ng, 0, 0))
    out = pl.BlockSpec((1, n, NSA_DH), lambda i: (i, 0, 0))
    whole = lambda a: pl.BlockSpec(a.shape, lambda i: (0,) * a.ndim)
    sd = jax.ShapeDtypeStruct((bg, n, NSA_DH), jnp.bfloat16)
    return pl.pallas_call(
        _nsa_compress_kernel,
        out_shape=(sd, sd),
        grid=(bg,),
        in_specs=[blk, blk, whole(pos_flat), whole(w1), whole(w2), tab, tab],
        out_specs=(out, out),
        compiler_params=pltpu.CompilerParams(
            dimension_semantics=("parallel",), vmem_limit_bytes=V7X_VMEM_LIMIT_BYTES),
        name="nsa_compress",
    )(hk, hv, pos_flat, w1, w2, cos_c, sin_c)


def _nsa_attn_kernel(q_ref, kc_ref, vc_ref, ks_ref, vs_ref, kw_ref, vw_ref, gc_ref, gs_ref, gw_ref, o_ref):
    f32, bf16 = jnp.float32, jnp.bfloat16
    qi = pl.program_id(2)
    q0 = qi * Q_BLOCK
    hpg, dh, tk = NSA_GROUP, NSA_DH, NSA_KEY_TILE
    rows = hpg * Q_BLOCK
    n_sel = ks_ref.shape[3] - dh
    n_cmp = kc_ref.shape[2]
    q_all = q_ref[0]
    q = jnp.concatenate([q_all[:, h * dh:(h + 1) * dh] for h in range(hpg)], axis=0)
    by_head = lambda o: jnp.concatenate([o[h * Q_BLOCK:(h + 1) * Q_BLOCK] for h in range(hpg)], axis=1)

    tq_c = q0 + lax.broadcasted_iota(jnp.int32, (Q_BLOCK, n_cmp), 0)
    c_end = lax.broadcasted_iota(jnp.int32, (Q_BLOCK, n_cmp), 1) * CMP_STRIDE + (CMP_LEN - 1)
    c_ok = jnp.concatenate([jnp.where(c_end <= tq_c, 1.0, 0.0)] * hpg, axis=0) > 0.5
    s = jnp.where(c_ok, lax.dot_general(q, kc_ref[0, 0], _NT, preferred_element_type=f32), NEG_BIG)
    m = jnp.max(s, axis=-1, keepdims=True)
    e = jnp.where(c_ok, jnp.exp(s - m), 0.0)
    p = e / jnp.maximum(jnp.sum(e, axis=-1, keepdims=True), 1e-30)
    o_cmp = jnp.dot(p.astype(bf16), vc_ref[0, 0], preferred_element_type=f32)

    p_sum = p[0:Q_BLOCK]
    for h in range(1, hpg):
        p_sum = p_sum + p[h * Q_BLOCK:(h + 1) * Q_BLOCK]
    c_start = lax.broadcasted_iota(jnp.int32, (n_sel, n_cmp), 1) * CMP_STRIDE
    j_blk = lax.broadcasted_iota(jnp.int32, (n_sel, n_cmp), 0)
    overlap_t = jnp.where((c_start < (j_blk + 1) * SEL_BLOCK) & (c_start + CMP_LEN > j_blk * SEL_BLOCK),
                          1.0, 0.0).astype(bf16)
    p_hi = p_sum.astype(bf16)
    p_lo = (p_sum - p_hi.astype(f32)).astype(bf16)
    imp = (lax.dot_general(overlap_t, p_hi, _NT, preferred_element_type=f32)
           + lax.dot_general(overlap_t, p_lo, _NT, preferred_element_type=f32))
    jb = lax.broadcasted_iota(jnp.int32, (n_sel, Q_BLOCK), 0)
    cur = (q0 + lax.broadcasted_iota(jnp.int32, (n_sel, Q_BLOCK), 1)) // SEL_BLOCK
    forced = (jb == 0) | (jb == cur) | (jb == cur - 1)
    imp = jnp.where(forced, -NEG_BIG, jnp.where(jb <= cur, imp, NEG_BIG))
    rank = jnp.zeros((n_sel, Q_BLOCK), f32)
    for i in range(n_sel):
        row_i = imp[i:i + 1, :]
        ahead = (row_i > imp) | ((row_i == imp) & (jb > i))
        rank = rank + jnp.where(ahead, 1.0, 0.0)
    sel_bias = jnp.where(rank < float(min(SEL_TOP, n_sel)), 0.0, NEG_BIG).T.astype(bf16)

    q_sel = jnp.concatenate([q, jnp.concatenate([sel_bias] * hpg, axis=0)], axis=1)

    def tile_update(kt, carry, on_diagonal):
        m, l, acc = carry
        k0 = pl.multiple_of(kt * tk, tk)
        s = lax.dot_general(q_sel, ks_ref[0, 0, pl.ds(k0, tk), :], _NT, preferred_element_type=f32)
        if on_diagonal:
            t_pos = q0 + lax.broadcasted_iota(jnp.int32, (Q_BLOCK, tk), 0)
            key = k0 + lax.broadcasted_iota(jnp.int32, (Q_BLOCK, tk), 1)
            causal = jnp.concatenate([jnp.where(key <= t_pos, 0.0, NEG_BIG)] * hpg, axis=0)
            s = s + causal
        m_new = jnp.maximum(m, jnp.max(s, axis=-1, keepdims=True))
        alpha = jnp.exp(m - m_new)
        p = jnp.exp(s - m_new)
        l = alpha * l + jnp.sum(p, axis=-1, keepdims=True)
        acc = alpha * acc + jnp.dot(p.astype(bf16), vs_ref[0, 0, pl.ds(k0, tk), :], preferred_element_type=f32)
        return m_new, l, acc

    init = (jnp.full((rows, 1), NEG_BIG, f32), jnp.zeros((rows, 1), f32), jnp.zeros((rows, dh), f32))
    last_tile = (q0 + Q_BLOCK - 1) // tk
    carry = lax.fori_loop(0, last_tile, lambda kt, c: tile_update(kt, c, False), init)
    m, l, acc = tile_update(last_tile, carry, True)
    o_sel = acc / jnp.maximum(l, 1e-30)

    w0 = pl.multiple_of(jnp.maximum(q0 - WINDOW, 0), Q_BLOCK)
    kwin = kw_ref[0, 0, pl.ds(w0, NSA_WIN_SPAN), :]
    vwin = vw_ref[0, 0, pl.ds(w0, NSA_WIN_SPAN), :]
    tq = q0 + lax.broadcasted_iota(jnp.int32, (Q_BLOCK, NSA_WIN_SPAN), 0)
    kp = w0 + lax.broadcasted_iota(jnp.int32, (Q_BLOCK, NSA_WIN_SPAN), 1)
    dlt = tq - kp
    wbias = jnp.where((dlt >= 0) & (dlt < WINDOW), 0.0, NEG_BIG).astype(f32)
    wbias = jnp.concatenate([wbias] * hpg, axis=0)
    s = lax.dot_general(q, kwin, _NT, preferred_element_type=f32) + wbias
    mw = jnp.max(s, axis=-1, keepdims=True)
    p = jnp.exp(s - mw)
    lw = jnp.sum(p, axis=-1, keepdims=True)
    o_win = jnp.dot(p.astype(bf16), vwin, preferred_element_type=f32) / jnp.maximum(lw, 1e-30)

    o_ref[0] = (gc_ref[0] * by_head(o_cmp) + gs_ref[0] * by_head(o_sel) + gw_ref[0] * by_head(o_win)
                ).astype(o_ref.dtype)


def _nsa_attn(q, k_cmp, v_cmp, ks, vs, kw, vw, g_cmp, g_sel, g_win):
    bsz, s, _ = q.shape
    ng, dh = NSA_KV_HEADS, NSA_DH
    gw = NSA_GROUP * dh
    kv_spec = pl.BlockSpec((1, 1, s, dh), lambda b, g, i: (b, g, 0, 0))
    ks_spec = pl.BlockSpec((1, 1, s, ks.shape[3]), lambda b, g, i: (b, g, 0, 0))
    cmp_spec = pl.BlockSpec((1, 1, k_cmp.shape[2], dh), lambda b, g, i: (b, g, 0, 0))
    q_spec = pl.BlockSpec((1, Q_BLOCK, gw), lambda b, g, i: (b, i, g))
    return pl.pallas_call(
        _nsa_attn_kernel,
        out_shape=jax.ShapeDtypeStruct(q.shape, jnp.bfloat16),
        grid=(bsz, ng, s // Q_BLOCK),
        in_specs=[q_spec, cmp_spec, cmp_spec, ks_spec, kv_spec, kv_spec, kv_spec, q_spec, q_spec, q_spec],
        out_specs=q_spec,
        compiler_params=pltpu.CompilerParams(
            dimension_semantics=("parallel", "parallel", "arbitrary"),
            vmem_limit_bytes=V7X_VMEM_LIMIT_BYTES),
        name="nsa_attn",
    )(q, k_cmp, v_cmp, ks, vs, kw, vw, g_cmp, g_sel, g_win)


def _nsa_mixer(z, nsa_tabs, cmp_pos, cmp_w1, cmp_w2):
    bsz, s, _ = z.shape
    ng, dh = NSA_KV_HEADS, NSA_DH
    cos, sin, cos_c, sin_c = nsa_tabs
    q, ks, vs, kw, vw, kc, vc, g_cmp, g_sel, g_win = _nsa_prep(z, cos, sin)
    n_blk = s // CMP_STRIDE
    flat = lambda t: t.reshape(bsz * ng, n_blk, CMP_STRIDE * dh)
    k_cmp, v_cmp = _nsa_compress(flat(kc), flat(vc), cmp_pos.reshape(2, 1, CMP_LEN * dh), cmp_w1, cmp_w2,
                                 cos_c, sin_c)
    grouped = lambda t: t.reshape(bsz, ng, n_blk, dh)
    return _nsa_attn(q, grouped(k_cmp), grouped(v_cmp), ks, vs, kw, vw, g_cmp, g_sel, g_win)


def _nsa_tables(cos, sin):
    end = CMP_LEN - 1
    at_ends = lambda t: jnp.pad(t[:, end::CMP_STRIDE, :NSA_DH], ((0, 0), (0, 1), (0, 0)))
    return cos, sin, at_ends(cos), at_ends(sin)


IN_GROUPS = (("ret", GROUP_SIZES[0] + GROUP_SIZES[1], GROUP_SIZES[2], 3072),
             ("rw", GROUP_SIZES[0], GROUP_SIZES[1], 2048),
             ("sc", 0, GROUP_SIZES[0], 1536),
             ("nsa", sum(GROUP_SIZES[:3]), GROUP_SIZES[3], 1536),
             ("gate", sum(GROUP_SIZES[:4]), GROUP_SIZES[4], 8192))
IN_COL0 = {name: sum(g[3] for g in IN_GROUPS[:i]) for i, (name, _, _, _) in enumerate(IN_GROUPS)}
IN_WIDTH = {name: padded for name, _, _, padded in IN_GROUPS}


def _aligned_in_proj(w_in):
    parts = [jnp.pad(w_in[:, :, start:start + width], ((0, 0), (0, 0), (0, padded - width)))
             for _, start, width, padded in IN_GROUPS]
    return jnp.concatenate(parts, axis=-1).astype(jnp.bfloat16)


def _hybrid_mixer(h, bsz, layer, ret_tabs, nsa_tabs, w_all, sc_conv, rw_mu, rw_w0, rw_w2, rw_a0, rw_a2, rw_g2,
                  rw_k_k, rw_k_a, rw_r_k, rw_lnx_g, rw_lnx_b, nsa_cmp_pos, nsa_cmp_w1, nsa_cmp_w2,
                  w_branch):
    m = h.shape[0]
    s = m // bsz
    group = lambda name: _mm_cols(h, w_all, layer, IN_COL0[name], IN_WIDTH[name])
    per_batch = lambda z: z.reshape(bsz, s, z.shape[1])
    o_a = _short_conv_mixer(group("sc"), sc_conv, s)
    o_b = _rwkv7_mixer(per_batch(group("rw")), rw_mu, rw_w0, rw_w2, rw_a0, rw_a2, rw_g2, rw_k_k, rw_k_a, rw_r_k,
                       rw_lnx_g, rw_lnx_b)
    o_c = _retention_mixer(per_batch(group("ret")), *ret_tabs)
    o_d = _nsa_mixer(per_batch(group("nsa")), nsa_tabs, nsa_cmp_pos, nsa_cmp_w1, nsa_cmp_w2)
    outs = [o.reshape(m, o.shape[-1]) for o in (o_a, o_b, o_c, o_d)]
    return _merge(h, w_all, layer, IN_COL0["gate"], outs, _split(w_branch, BR_SIZES, axis=0))


def kernel(x, mem, positions, ln_mix_pre, ln_mix_post, ln_mem_q, ln_mem_kv, ln_mem_post, ln_ffn_pre, ln_ffn_post, w_in, sc_conv, rw_mu, rw_w0, rw_w2, rw_a0, rw_a2, rw_g2, rw_k_k, rw_k_a, rw_r_k, rw_lnx_g, rw_lnx_b, nsa_cmp_pos, nsa_cmp_w1, nsa_cmp_w2, w_branch, w_out, mem_wq, mem_wkv, mem_wo, ffn_w_up, ffn_conv_w, ffn_conv_b, ffn_w_down):
    bsz, s, d = x.shape
    mem_len = mem.shape[1]
    xf = x.reshape(bsz * s, d)
    h = _prenorm(xf, ln_mix_pre[0])
    ret_cos, ret_sin, nsa_cos, nsa_sin = _rope_tables(positions)
    ret_tabs = (ret_cos, ret_sin)
    nsa_tabs = _nsa_tables(nsa_cos, nsa_sin)
    w_all = _aligned_in_proj(w_in)
    for l in range(DEPTH):
        merged = _hybrid_mixer(h, bsz, l, ret_tabs, nsa_tabs, w_all, sc_conv[l], rw_mu[l], rw_w0[l], rw_w2[l], rw_a0[l],
                               rw_a2[l], rw_g2[l], rw_k_k[l], rw_k_a[l], rw_r_k[l], rw_lnx_g[l], rw_lnx_b[l],
                               nsa_cmp_pos[l], nsa_cmp_w1[l], nsa_cmp_w2[l], w_branch[l])
        xf, h = _mm_norm_res(merged, w_out[l], xf, ln_mix_post[l], ln_mem_q[l])
        mem_n = _prenorm(mem.reshape(bsz * mem_len, d), ln_mem_kv[l])
        kv = _mm(mem_n, mem_wkv[l]).astype(jnp.bfloat16)
        k_mem = kv[:, :d].reshape(bsz, mem_len, d)
        v_mem = kv[:, d:].reshape(bsz, mem_len, d)
        o = _xattn(h, mem_wq[l], k_mem, v_mem, s)
        xf, h = _mm_norm_res(o, mem_wo[l], xf, ln_mem_post[l], ln_ffn_pre[l])
        act = _ffn_up(h, ffn_w_up[l], ffn_conv_w[l], ffn_conv_b[l], s)
        g_next = ln_mix_pre[l + 1] if l + 1 < DEPTH else None
        xf, h = _mm_norm_res(act, ffn_w_down[l], xf, ln_ffn_post[l], g_next)
    return xf.reshape(bsz, s, d)
```

```python
import functools
import math

import jax
import jax.numpy as jnp
import numpy as np
from jax import lax
from jax.experimental import pallas as pl
from jax.experimental.pallas import tpu as pltpu

D_MODEL = 2048
DEPTH = 2
MEM_HEADS = 4
MEM_HEAD_DIM = D_MODEL // MEM_HEADS
SC_W = D_MODEL // 4
SC_K = 3
RW_HEAD_DIM = 64
RW_W = D_MODEL // 4
RW_HEADS = RW_W // RW_HEAD_DIM
RW_LORA_W = 96
RW_LORA_A = 96
RW_LORA_G = 256
RW_DECAY_SCALE = math.exp(-0.5)
RW_LN_EPS = 64e-5
RET_HEADS = 4
RET_DK = 128
RET_DV = 2 * RET_DK
RET_CHUNK = 128
RET_LN_EPS = 1e-5
NSA_HEADS = 8
NSA_KV_HEADS = 2
NSA_DH = 64
NSA_GROUP = NSA_HEADS // NSA_KV_HEADS
CMP_LEN = 32
CMP_STRIDE = 16
SEL_BLOCK = 64
SEL_TOP = 16
WINDOW = 512
Q_BLOCK = 128
ROPE_THETA = 10000.0
EPS = 1e-6
N_BRANCH = 4

SC_SIZES = (SC_W, SC_W, SC_W)
RW_SIZES = (RW_W, RW_W, RW_W, RW_LORA_W, RW_LORA_A, RW_LORA_G)
RET_SIZES = (RET_HEADS * RET_DK, RET_HEADS * RET_DK, RET_HEADS * RET_DV, RET_HEADS * RET_DV)
NSA_SIZES = (NSA_HEADS * NSA_DH,) + (NSA_KV_HEADS * NSA_DH,) * 6 + (NSA_HEADS * 3,)
GATE_SIZES = (D_MODEL,) * N_BRANCH
GROUP_SIZES = (sum(SC_SIZES), sum(RW_SIZES), sum(RET_SIZES), sum(NSA_SIZES), sum(GATE_SIZES))
BR_SIZES = (SC_W, RW_W, RET_HEADS * RET_DV, NSA_HEADS * NSA_DH)

V7X_VMEM_LIMIT_BYTES = 48 * 1024 * 1024
LANES = 128
_NT = (((1,), (1,)), ((), ()))


def _pick_tile(n, candidates):
    for c in candidates:
        if n % c == 0:
            return c
    return n


def _bdot(a, b):
    return jnp.dot(a.astype(jnp.bfloat16), b.astype(jnp.bfloat16), preferred_element_type=jnp.float32)


def _bdot_nt(a, b):
    return lax.dot_general(a.astype(jnp.bfloat16), b.astype(jnp.bfloat16), _NT,
                           preferred_element_type=jnp.float32)


def _head_sum(x, bd):
    hi = x.astype(jnp.bfloat16)
    lo = (x - hi.astype(jnp.float32)).astype(jnp.bfloat16)
    return (jnp.dot(hi, bd, preferred_element_type=jnp.float32)
            + jnp.dot(lo, bd, preferred_element_type=jnp.float32))


def _mm_kernel(a_ref, w_ref, o_ref, acc_ref):
    k = pl.program_id(2)

    @pl.when(k == 0)
    def _():
        acc_ref[...] = jnp.zeros_like(acc_ref)

    acc_ref[...] += jnp.dot(a_ref[...], w_ref[...], preferred_element_type=jnp.float32)

    @pl.when(k == pl.num_programs(2) - 1)
    def _():
        o_ref[...] = acc_ref[...]


def _mm_cols(a, w_stack, layer, col0, n):
    m, k = a.shape
    tn = next(c for c in (n if n <= 1536 else 1024, 1024, 768, 512, 256, 128) if n % c == 0 and col0 % c == 0)
    tm = _pick_tile(m, (1024, 512, 256, 128, 8))
    tk = _pick_tile(k, (2048, 1280, 1408, 1024, 512, 256, 128))
    j0 = col0 // tn
    return pl.pallas_call(
        _mm_kernel,
        out_shape=jax.ShapeDtypeStruct((m, n), jnp.float32),
        grid=(m // tm, n // tn, k // tk),
        in_specs=[pl.BlockSpec((tm, tk), lambda i, j, kk: (i, kk)),
                  pl.BlockSpec((None, tk, tn), lambda i, j, kk: (layer, kk, j0 + j))],
        out_specs=pl.BlockSpec((tm, tn), lambda i, j, kk: (i, j)),
        scratch_shapes=[pltpu.VMEM((tm, tn), jnp.float32)],
        compiler_params=pltpu.CompilerParams(
            dimension_semantics=("parallel", "parallel", "arbitrary"),
            vmem_limit_bytes=V7X_VMEM_LIMIT_BYTES),
        name="mm",
    )(a.astype(jnp.bfloat16), w_stack)


def _rms(y, g):
    return y * lax.rsqrt(jnp.mean(y * y, axis=-1, keepdims=True) + EPS) * g


def _prenorm_kernel(x_ref, g_ref, h_ref):
    h_ref[...] = _rms(x_ref[...], g_ref[...]).astype(h_ref.dtype)


def _prenorm(x, g):
    m, d = x.shape
    tm = _pick_tile(m, (512, 256, 128, 8))
    return pl.pallas_call(
        _prenorm_kernel,
        out_shape=jax.ShapeDtypeStruct((m, d), jnp.bfloat16),
        grid=(m // tm,),
        in_specs=[pl.BlockSpec((tm, d), lambda i: (i, 0)), pl.BlockSpec((1, d), lambda i: (0, 0))],
        out_specs=pl.BlockSpec((tm, d), lambda i: (i, 0)),
        compiler_params=pltpu.CompilerParams(dimension_semantics=("parallel",)),
        name="prenorm",
    )(x, g.reshape(1, d))


def _mm_norm_res_kernel(emit_h, whole_k, a_ref, w_ref, res_ref, g_ref, *rest):
    if emit_h:
        g2_ref, x_ref, h_ref = rest[:3]
    else:
        x_ref = rest[0]

    def finish(y):
        x_new = res_ref[...] + _rms(y, g_ref[...])
        x_ref[...] = x_new
        if emit_h:
            h_ref[...] = _rms(x_new, g2_ref[...]).astype(h_ref.dtype)

    if whole_k:
        finish(jnp.dot(a_ref[...], w_ref[...], preferred_element_type=jnp.float32))
        return
    acc_ref = rest[-1]
    k = pl.program_id(1)

    @pl.when(k == 0)
    def _():
        acc_ref[...] = jnp.zeros_like(acc_ref)

    acc_ref[...] += jnp.dot(a_ref[...], w_ref[...], preferred_element_type=jnp.float32)

    @pl.when(k == pl.num_programs(1) - 1)
    def _():
        finish(acc_ref[...])


def _mm_norm_res(a, w_stack, layer, res, g, g_next):
    m, k = a.shape
    d = w_stack.shape[2]
    tk = _pick_tile(k, (2048, 1408, 1024, 512))
    tm = 512
    whole_k = tk == k
    emit_h = g_next is not None
    row = pl.BlockSpec((tm, d), lambda i, kk: (i, 0))
    vec = pl.BlockSpec((1, d), lambda i, kk: (0, 0))
    w_spec = (pl.BlockSpec((None, tk, d), lambda i, kk: (layer, 0, 0), pipeline_mode=pl.Buffered(1)) if whole_k
              else pl.BlockSpec((None, tk, d), lambda i, kk: (layer, kk, 0)))
    x_sd = jax.ShapeDtypeStruct((m, d), jnp.float32)
    h_sd = jax.ShapeDtypeStruct((m, d), jnp.bfloat16)
    out = pl.pallas_call(
        functools.partial(_mm_norm_res_kernel, emit_h, whole_k),
        out_shape=(x_sd, h_sd) if emit_h else x_sd,
        grid=(m // tm, k // tk),
        in_specs=[pl.BlockSpec((tm, tk), lambda i, kk: (i, kk)), w_spec, row, vec] + ([vec] if emit_h else []),
        out_specs=(row, row) if emit_h else row,
        scratch_shapes=[] if whole_k else [pltpu.VMEM((tm, d), jnp.float32)],
        compiler_params=pltpu.CompilerParams(
            dimension_semantics=("parallel", "arbitrary"), vmem_limit_bytes=V7X_VMEM_LIMIT_BYTES),
        name="mm_norm_res",
    )(a.astype(jnp.bfloat16), w_stack, res, g.reshape(1, d), *([g_next.reshape(1, d)] if emit_h else []))
    return out if emit_h else (out, None)


FFN_TM = 1024
FFN_TN = 512
HALO = 8


def _shift_rows(u, prev, n):
    rolled = pltpu.roll(u, n, axis=0)
    rows = lax.broadcasted_iota(jnp.int32, u.shape, 0)
    for r in range(n):
        rolled = jnp.where(rows == r, prev[HALO - n + r:HALO - n + r + 1, :], rolled)
    return rolled


def _ffn_up_kernel(seq_len, h_ref, halo_ref, wa_ref, wb_ref, cwa_ref, cwb_ref, ba_ref, bb_ref, o_ref):
    i = pl.program_id(0)
    h = h_ref[...]
    at_start = (i * FFN_TM) % seq_len == 0
    halo = jnp.where(at_start, jnp.zeros_like(halo_ref[...]), halo_ref[...])

    def conv(w_ref, cw_ref, b_ref):
        w = w_ref[...].astype(jnp.bfloat16)
        u = jnp.dot(h, w, preferred_element_type=jnp.float32)
        up = jnp.dot(halo, w, preferred_element_type=jnp.float32)
        cw = cw_ref[...]
        y = cw[0:1] * _shift_rows(u, up, 2)
        y = y + cw[1:2] * _shift_rows(u, up, 1)
        y = y + cw[2:3] * u
        return y + b_ref[...]

    a = conv(wa_ref, cwa_ref, ba_ref)
    b = conv(wb_ref, cwb_ref, bb_ref)
    o_ref[...] = (jax.nn.gelu(a, approximate=True) * b).astype(o_ref.dtype)


def _ffn_up(h, w_up_stack, layer, w_conv, b_conv, seq_len):
    m, d = h.shape
    f = w_up_stack.shape[2] // 2
    nj = f // FFN_TN
    b_conv = b_conv.reshape(1, 2 * f)
    tiles_per_halo = FFN_TM // HALO
    return pl.pallas_call(
        functools.partial(_ffn_up_kernel, seq_len),
        out_shape=jax.ShapeDtypeStruct((m, f), jnp.bfloat16),
        grid=(m // FFN_TM, nj),
        in_specs=[pl.BlockSpec((FFN_TM, d), lambda i, j: (i, 0)),
                  pl.BlockSpec((HALO, d), lambda i, j: (jnp.maximum(i * tiles_per_halo - 1, 0), 0)),
                  pl.BlockSpec((None, d, FFN_TN), lambda i, j: (layer, 0, j)),
                  pl.BlockSpec((None, d, FFN_TN), lambda i, j: (layer, 0, j + nj)),
                  pl.BlockSpec((3, FFN_TN), lambda i, j: (0, j)),
                  pl.BlockSpec((3, FFN_TN), lambda i, j: (0, j + nj)),
                  pl.BlockSpec((1, FFN_TN), lambda i, j: (0, j)),
                  pl.BlockSpec((1, FFN_TN), lambda i, j: (0, j + nj))],
        out_specs=pl.BlockSpec((FFN_TM, FFN_TN), lambda i, j: (i, j)),
        compiler_params=pltpu.CompilerParams(
            dimension_semantics=("parallel", "arbitrary"), vmem_limit_bytes=V7X_VMEM_LIMIT_BYTES),
        name="ffn_up",
    )(h, h, w_up_stack, w_up_stack, w_conv, w_conv, b_conv, b_conv)


MERGE_TM = 512
MERGE_TN = 512


def _merge_kernel(h_ref, wg0, wg1, wg2, wg3, o0, o1, o2, o3, wb0, wb1, wb2, wb3, out_ref):
    h = h_ref[...]
    acc = None
    for wg, o, wb in ((wg0, o0, wb0), (wg1, o1, wb1), (wg2, o2, wb2), (wg3, o3, wb3)):
        gate = jax.nn.sigmoid(jnp.dot(h, wg[...], preferred_element_type=jnp.float32))
        term = gate * jnp.dot(o[...], wb[...], preferred_element_type=jnp.float32)
        acc = term if acc is None else acc + term
    out_ref[...] = acc.astype(out_ref.dtype)


def _merge(h, w_stack, layer, gate_col0, outs, wb_stack):
    m, d = h.shape
    nj = d // MERGE_TN
    j0 = gate_col0 // MERGE_TN
    assert gate_col0 % MERGE_TN == 0
    gate_specs = [pl.BlockSpec((None, d, MERGE_TN), functools.partial(lambda b, i, j: (layer, 0, j0 + b * nj + j), b))
                  for b in range(N_BRANCH)]
    o_specs = [pl.BlockSpec((MERGE_TM, o.shape[1]), lambda i, j: (i, 0)) for o in outs]
    row0 = np.cumsum([0] + [o.shape[1] for o in outs])
    assert all(r % o.shape[1] == 0 for r, o in zip(row0, outs))
    wb_specs = [pl.BlockSpec((None, o.shape[1], MERGE_TN),
                             functools.partial(lambda rb, i, j: (layer, rb, j), int(r) // o.shape[1]))
                for r, o in zip(row0, outs)]
    return pl.pallas_call(
        _merge_kernel,
        out_shape=jax.ShapeDtypeStruct((m, d), jnp.bfloat16),
        grid=(m // MERGE_TM, nj),
        in_specs=[pl.BlockSpec((MERGE_TM, d), lambda i, j: (i, 0))] + gate_specs + o_specs + wb_specs,
        out_specs=pl.BlockSpec((MERGE_TM, MERGE_TN), lambda i, j: (i, j)),
        compiler_params=pltpu.CompilerParams(
            dimension_semantics=("parallel", "arbitrary"), vmem_limit_bytes=V7X_VMEM_LIMIT_BYTES),
        name="merge",
    )(h, *([w_stack] * N_BRANCH), *[o.astype(jnp.bfloat16) for o in outs], *([wb_stack] * N_BRANCH))


XATT_TM = 512


def _xattn_kernel(h_ref, wq_ref, k_ref, v_ref, o_ref):
    q = jnp.dot(h_ref[...], wq_ref[...], preferred_element_type=jnp.float32)
    q = (q * (MEM_HEAD_DIM ** -0.5)).astype(jnp.bfloat16)
    for hh in range(MEM_HEADS):
        sl = slice(hh * MEM_HEAD_DIM, (hh + 1) * MEM_HEAD_DIM)
        s = lax.dot_general(q[:, sl], k_ref[0, :, sl], _NT, preferred_element_type=jnp.float32)
        p = jnp.exp(s - jnp.max(s, axis=-1, keepdims=True))
        p = p / jnp.sum(p, axis=-1, keepdims=True)
        o_ref[:, sl] = jnp.dot(p.astype(jnp.bfloat16), v_ref[0, :, sl],
                               preferred_element_type=jnp.float32).astype(o_ref.dtype)


def _xattn(h, wq_stack, layer, k, v, seq_len):
    m, d = h.shape
    mem_len = k.shape[1]
    per_batch = seq_len // XATT_TM
    kv_spec = pl.BlockSpec((1, mem_len, d), lambda i: (i // per_batch, 0, 0))
    return pl.pallas_call(
        _xattn_kernel,
        out_shape=jax.ShapeDtypeStruct((m, d), jnp.bfloat16),
        grid=(m // XATT_TM,),
        in_specs=[pl.BlockSpec((XATT_TM, d), lambda i: (i, 0)),
                  pl.BlockSpec((None, d, d), lambda i: (layer, 0, 0)), kv_spec, kv_spec],
        out_specs=pl.BlockSpec((XATT_TM, d), lambda i: (i, 0)),
        compiler_params=pltpu.CompilerParams(
            dimension_semantics=("parallel",), vmem_limit_bytes=V7X_VMEM_LIMIT_BYTES),
        name="xattn",
    )(h, wq_stack, k, v)


def _rope_freqs(d):
    return ROPE_THETA ** (-jnp.arange(0, d, 2, dtype=jnp.float32) / d)


def _retnet_freqs(d):
    return 1.0 / (ROPE_THETA ** jnp.linspace(0.0, 1.0, d // 2, dtype=jnp.float32))


WKV_CHUNK = 64


def _rwkv7_kernel(z_ref, mu_ref, w0_ref, a0_ref, kk_ref, ka_ref, rk_ref, lng_ref, lnb_ref,
                  ww2_ref, wa2_ref, wg2_ref, o_ref, s_ref, prev_ref, y_ref):
    c = WKV_CHUNK
    hd = RW_HEAD_DIM
    w = RW_W
    nb = z_ref.shape[0]
    f32, bf16 = jnp.float32, jnp.bfloat16

    @pl.when(pl.program_id(0) == 0)
    def _():
        s_ref[...] = jnp.zeros_like(s_ref)
        prev_ref[...] = jnp.zeros_like(prev_ref)

    lane_h = lax.broadcasted_iota(jnp.int32, (w, w), 0) // hd
    lane_w = lax.broadcasted_iota(jnp.int32, (w, w), 1) // hd
    head_bd = jnp.where(lane_h == lane_w, 1.0, 0.0).astype(bf16)
    row = lax.broadcasted_iota(jnp.int32, (c, c), 0)
    col = lax.broadcasted_iota(jnp.int32, (c, c), 1)
    tri = jnp.where(row >= col, 1.0, 0.0).astype(bf16)
    eye = jnp.where(row == col, 1.0, 0.0).astype(f32)
    r2 = lax.broadcasted_iota(jnp.int32, (2 * c, 2 * c), 0)
    c2 = lax.broadcasted_iota(jnp.int32, (2 * c, 2 * c), 1)
    rr = jnp.where(r2 >= c, r2 - c, r2)
    cc = jnp.where(c2 >= c, c2 - c, c2)
    tri_mask = cc < rr + jnp.where(r2 >= c, 1, 0)

    def prepare(b):
        z = z_ref[b]
        rows = lax.broadcasted_iota(jnp.int32, z.shape, 0)
        z_prev = jnp.where(rows == 0, prev_ref[b], pltpu.roll(z, 1, axis=0))
        prev_ref[b] = z[c - 1:c, :]
        z = z + (z_prev - z) * mu_ref[...]
        r, k, v, lora = z[:, 0:w], z[:, w:2 * w], z[:, 2 * w:3 * w], z[:, 3 * w:4 * w]
        logw = -RW_DECAY_SCALE * jax.nn.sigmoid(w0_ref[...] + _bdot(jnp.tanh(lora), ww2_ref[...]))
        a = jax.nn.sigmoid(a0_ref[...] + _bdot(lora, wa2_ref[...]))
        g = _bdot(jax.nn.sigmoid(lora), wg2_ref[...])
        kk = k * kk_ref[...]
        kk = kk / jnp.maximum(jnp.sqrt(_head_sum(kk * kk, head_bd)), 1e-12)
        k = k * (1.0 + (a - 1.0) * ka_ref[...])
        hi = logw.astype(bf16)
        rem = logw - hi.astype(f32)
        mid = rem.astype(bf16)
        lo = (rem - mid.astype(f32)).astype(bf16)
        cum = (jnp.dot(tri, hi, preferred_element_type=f32) + jnp.dot(tri, mid, preferred_element_type=f32)
               + jnp.dot(tri, lo, preferred_element_type=f32))
        kka = kk * a
        e_neg = jnp.exp(-cum)
        last = cum[c - 1:c, :]
        dec = jnp.exp(last - cum)
        return dict(r=r, k=k, v=v, g=g, vb=v.astype(bf16), g_last=jnp.exp(last),
                    qh=(kk * jnp.exp(cum - logw)).astype(bf16), rh=(r * jnp.exp(cum)).astype(bf16),
                    bh=(kka * e_neg).astype(bf16), kh=(k * e_neg).astype(bf16),
                    bd=(kka * dec).astype(bf16), kd=(k * dec).astype(bf16))

    pre = [prepare(b) for b in range(nb)]

    units = [(b, h) for b in range(nb) for h in range(RW_HEADS)]
    n_units = range(len(units))
    part = lambda name, i: pre[units[i][0]][name][:, units[i][1] * hd:(units[i][1] + 1) * hd]
    a1 = [jnp.concatenate([part("qh", i), part("rh", i)], axis=0) for i in n_units]
    b1 = [jnp.concatenate([part("bh", i), part("kh", i)], axis=0) for i in n_units]
    ss = [jnp.where(tri_mask, lax.dot_general(a1[i], b1[i], _NT, preferred_element_type=f32), 0.0)
          for i in n_units]
    s_old = [s_ref[i] for i in n_units]
    qr = [lax.dot_general(a1[i], s_old[i].astype(bf16), _NT, preferred_element_type=f32) for i in n_units]
    lm = [_bdot(ss[i][:, c:2 * c], part("vb", i)) for i in n_units]
    n = [-ss[i][0:c, 0:c] for i in n_units]
    t = [eye + n[i] for i in n_units]
    for _ in range(5):
        n = [_bdot(n[i], n[i]) for i in n_units]
        t = [t[i] + _bdot(t[i], n[i]) for i in n_units]
    u = [-_bdot(t[i], qr[i][0:c] + lm[i][0:c]) for i in n_units]
    for i in n_units:
        b, h = units[i]
        y_ref[b, :, h * hd:(h + 1) * hd] = qr[i][c:2 * c] + lm[i][c:2 * c] + _bdot(ss[i][c:2 * c, 0:c], u[i])
    for i in n_units:
        zt = jnp.concatenate([u[i], part("v", i)], axis=0).T
        x = jnp.concatenate([part("bd", i), part("kd", i)], axis=0)
        s_ref[i] = s_old[i] * part("g_last", i) + _bdot(zt, x)

    for b in range(nb):
        p = pre[b]
        y = y_ref[b]
        mean = _head_sum(y, head_bd) * (1.0 / hd)
        yc = y - mean
        var = _head_sum(yc * yc, head_bd) * (1.0 / hd)
        yn = yc * lax.rsqrt(var + RW_LN_EPS) * lng_ref[...] + lnb_ref[...]
        bonus = _head_sum(p["r"] * p["k"] * rk_ref[...], head_bd) * p["v"]
        o_ref[b] = ((yn + bonus) * p["g"]).astype(o_ref.dtype)


def _rwkv7_mixer(z, mu, w0, w_w2, a0, w_a2, w_g2, k_k, k_a, r_k, lnx_g, lnx_b):
    bsz, s, zw = z.shape
    w = RW_W
    n_lora = RW_LORA_W + RW_LORA_A + RW_LORA_G
    row = lambda t: t.reshape(1, -1)
    mu = jnp.pad(mu, (0, zw - mu.shape[0])).reshape(1, zw)
    o_w, o_a = RW_LORA_W, RW_LORA_W + RW_LORA_A
    pad_rows = lambda m, lo: jnp.pad(m, ((lo, w - lo - m.shape[0]), (0, 0))).astype(jnp.bfloat16)
    vec = pl.BlockSpec((1, w), lambda c: (0, 0))
    mat = pl.BlockSpec((w, w), lambda c: (0, 0))
    assert zw == 4 * w and n_lora <= w
    return pl.pallas_call(
        _rwkv7_kernel,
        out_shape=jax.ShapeDtypeStruct((bsz, s, w), jnp.bfloat16),
        grid=(s // WKV_CHUNK,),
        in_specs=[pl.BlockSpec((bsz, WKV_CHUNK, zw), lambda c: (0, c, 0)),
                  pl.BlockSpec((1, zw), lambda c: (0, 0))] + [vec] * 7 + [mat] * 3,
        out_specs=pl.BlockSpec((bsz, WKV_CHUNK, w), lambda c: (0, c, 0)),
        scratch_shapes=[pltpu.VMEM((bsz * RW_HEADS, RW_HEAD_DIM, RW_HEAD_DIM), jnp.float32),
                        pltpu.VMEM((bsz, 1, zw), jnp.float32),
                        pltpu.VMEM((bsz, WKV_CHUNK, w), jnp.float32)],
        compiler_params=pltpu.CompilerParams(dimension_semantics=("arbitrary",)),
        name="rwkv7",
    )(z, mu, row(w0), row(a0), row(k_k), row(k_a), row(r_k), row(lnx_g), row(lnx_b),
      pad_rows(w_w2, 0), pad_rows(w_a2, o_w), pad_rows(w_g2, o_a))


RET_LOG_DECAY = tuple(math.log(1.0 - 2.0 ** (-5.0 - h)) for h in range(RET_HEADS))


def _retention_kernel(q_ref, k_ref, v_ref, g_ref, cos_ref, sin_ref, o_ref, s_ref):
    c, dk, dv = RET_CHUNK, RET_DK, RET_DV
    f32 = jnp.float32

    @pl.when(pl.program_id(1) == 0)
    def _():
        s_ref[...] = jnp.zeros_like(s_ref)

    cos = cos_ref[0]
    sin = sin_ref[0]
    diff = (lax.broadcasted_iota(jnp.int32, (c, c), 0) - lax.broadcasted_iota(jnp.int32, (c, c), 1)).astype(f32)
    tok = lax.broadcasted_iota(jnp.int32, (c, 1), 0).astype(f32)
    rope = lambda x: x * cos + pltpu.roll(x, dk // 2, axis=1) * sin
    for h in range(RET_HEADS):
        lg = RET_LOG_DECAY[h]
        q = rope(q_ref[0, :, h * dk:(h + 1) * dk])
        k = rope(k_ref[0, :, h * dk:(h + 1) * dk]) * (dk ** -0.5)
        v = v_ref[0, :, h * dv:(h + 1) * dv]
        decay_in = jnp.where(diff >= 0, jnp.exp(jnp.maximum(diff, 0.0) * lg), 0.0)
        scores = _bdot_nt(q, k) * decay_in
        state = s_ref[h]
        o = _bdot(scores, v) + _bdot(q * jnp.exp((tok + 1.0) * lg), state)
        s_ref[h] = state * math.exp(c * lg) + _bdot((k * jnp.exp((c - 1.0 - tok) * lg)).T, v)
        mu = jnp.mean(o, axis=-1, keepdims=True)
        oc = o - mu
        var = jnp.mean(oc * oc, axis=-1, keepdims=True)
        gate = g_ref[0, :, h * dv:(h + 1) * dv]
        o_ref[0, :, h * dv:(h + 1) * dv] = (oc * lax.rsqrt(var + RET_LN_EPS) * gate * jax.nn.sigmoid(gate)
                                            ).astype(o_ref.dtype)


def _retention_mixer(z, cos, sin):
    bsz, s, _ = z.shape
    qk_w, v_w = RET_HEADS * RET_DK, RET_HEADS * RET_DV
    c = RET_CHUNK
    tab = pl.BlockSpec((1, c, RET_DK), lambda b, i: (b, i, 0))
    return pl.pallas_call(
        _retention_kernel,
        out_shape=jax.ShapeDtypeStruct((bsz, s, v_w), jnp.bfloat16),
        grid=(bsz, s // c),
        in_specs=[pl.BlockSpec((1, c, qk_w), lambda b, i: (b, i, 0)),
                  pl.BlockSpec((1, c, qk_w), lambda b, i: (b, i, 1)),
                  pl.BlockSpec((1, c, v_w), lambda b, i: (b, i, 1)),
                  pl.BlockSpec((1, c, v_w), lambda b, i: (b, i, 2)), tab, tab],
        out_specs=pl.BlockSpec((1, c, v_w), lambda b, i: (b, i, 0)),
        scratch_shapes=[pltpu.VMEM((RET_HEADS, RET_DK, RET_DV), jnp.float32)],
        compiler_params=pltpu.CompilerParams(dimension_semantics=("parallel", "arbitrary")),
        name="retention",
    )(z, z, z, z, cos, sin)


TRIG_TM = 1024


def _trig_kernel(pos_ref, freq_ref, rc_ref, rs_ref, nc_ref, ns_ref):
    ang = pos_ref[...] * freq_ref[...]
    c, sn = jnp.cos(ang), jnp.sin(ang)
    n_ret, n_nsa = RET_DK // 2, NSA_DH // 2
    rc_ref[...] = jnp.concatenate([c[:, :n_ret]] * 2, axis=1)
    rs_ref[...] = jnp.concatenate([-sn[:, :n_ret], sn[:, :n_ret]], axis=1)
    nsa_c, nsa_s = c[:, n_ret:n_ret + n_nsa], sn[:, n_ret:n_ret + n_nsa]
    nc_ref[...] = jnp.concatenate([nsa_c, nsa_c] * NSA_HEADS, axis=1)
    ns_ref[...] = jnp.concatenate([-nsa_s, nsa_s] * NSA_HEADS, axis=1)


def _rope_tables(positions):
    bsz, s = positions.shape
    m = bsz * s
    n_ret, n_nsa = RET_DK // 2, NSA_DH // 2
    assert n_ret + n_nsa <= LANES
    freq = jnp.concatenate([_retnet_freqs(RET_DK), _rope_freqs(NSA_DH),
                            jnp.zeros((LANES - n_ret - n_nsa,), jnp.float32)]).reshape(1, LANES)
    tm = _pick_tile(m, (TRIG_TM, 512, 256, 128, 8))
    spec = lambda w: pl.BlockSpec((tm, w), lambda i: (i, 0))
    sd = lambda w: jax.ShapeDtypeStruct((m, w), jnp.float32)
    nsa_w = NSA_HEADS * NSA_DH
    tabs = pl.pallas_call(
        _trig_kernel,
        out_shape=(sd(RET_DK), sd(RET_DK), sd(nsa_w), sd(nsa_w)),
        grid=(m // tm,),
        in_specs=[spec(1), pl.BlockSpec((1, LANES), lambda i: (0, 0))],
        out_specs=(spec(RET_DK), spec(RET_DK), spec(nsa_w), spec(nsa_w)),
        compiler_params=pltpu.CompilerParams(dimension_semantics=("parallel",)),
        name="rope_tables",
    )(positions.astype(jnp.float32).reshape(m, 1), freq)
    return [t.reshape(bsz, s, t.shape[1]) for t in tabs]


SC_TM = 512


def _short_conv_kernel(seq_len, b_ref, c_ref, x_ref, ch_ref, xh_ref, w_ref, o_ref):
    at_start = (pl.program_id(0) * SC_TM) % seq_len == 0
    u = c_ref[...] * x_ref[...]
    up = jnp.where(at_start, 0.0, ch_ref[...] * xh_ref[...])
    w = w_ref[...]
    y = w[0:1] * _shift_rows(u, up, 2) + w[1:2] * _shift_rows(u, up, 1) + w[2:3] * u
    o_ref[...] = (b_ref[...] * y).astype(o_ref.dtype)


def _short_conv_mixer(z, w_conv, seq_len):
    m = z.shape[0]
    w = SC_W
    per_halo = SC_TM // HALO
    tile = lambda j: pl.BlockSpec((SC_TM, w), lambda i: (i, j))
    halo = lambda j: pl.BlockSpec((HALO, w), lambda i: (jnp.maximum(i * per_halo - 1, 0), j))
    return pl.pallas_call(
        functools.partial(_short_conv_kernel, seq_len),
        out_shape=jax.ShapeDtypeStruct((m, w), jnp.bfloat16),
        grid=(m // SC_TM,),
        in_specs=[tile(0), tile(1), tile(2), halo(1), halo(2), pl.BlockSpec((SC_K, w), lambda i: (0, 0))],
        out_specs=pl.BlockSpec((SC_TM, w), lambda i: (i, 0)),
        compiler_params=pltpu.CompilerParams(dimension_semantics=("parallel",)),
        name="short_conv",
    )(z, z, z, z, z, w_conv)


NSA_KEY_TILE = 512
NSA_WIN_SPAN = WINDOW + Q_BLOCK
NSA_PREP_TM = 512
NEG_BIG = -1e30
NSA_Q_W = NSA_HEADS * NSA_DH
NSA_KV_W = NSA_KV_HEADS * NSA_DH
NSA_GATE_OFF = NSA_Q_W + 6 * NSA_KV_W


def _rope_lanes(x, cos, sin):
    width = x.shape[1]
    half = NSA_DH // 2
    lane = lax.broadcasted_iota(jnp.int32, x.shape, 1)
    other = jnp.where(lane % NSA_DH < half, pltpu.roll(x, width - half, axis=1), pltpu.roll(x, half, axis=1))
    return x * cos + other * sin


def _nsa_prep_kernel(z_ref, cos_ref, sin_ref, q_ref, ks_ref, vs_ref, kw_ref, vw_ref, kc_ref, vc_ref,
                     gc_ref, gs_ref, gw_ref):
    cos, sin = cos_ref[0], sin_ref[0]
    z = z_ref[0]
    q_ref[0] = (_rope_lanes(z[:, 0:NSA_Q_W], cos, sin) * (NSA_DH ** -0.5)).astype(q_ref.dtype)
    kv = lambda i: z[:, NSA_Q_W + i * NSA_KV_W:NSA_Q_W + (i + 1) * NSA_KV_W]
    cos_kv, sin_kv = cos[:, 0:NSA_KV_W], sin[:, 0:NSA_KV_W]
    pieces = ((kc_ref, kv(0)), (vc_ref, kv(1)), (ks_ref, _rope_lanes(kv(2), cos_kv, sin_kv)), (vs_ref, kv(3)),
              (kw_ref, _rope_lanes(kv(4), cos_kv, sin_kv)), (vw_ref, kv(5)))
    n_sel = ks_ref.shape[3] - NSA_DH
    tok = pl.program_id(1) * NSA_PREP_TM + lax.broadcasted_iota(jnp.int32, (NSA_PREP_TM, n_sel), 0)
    block_1hot = jnp.where(tok // SEL_BLOCK == lax.broadcasted_iota(jnp.int32, (NSA_PREP_TM, n_sel), 1), 1.0, 0.0)
    for ref, val in pieces:
        for g in range(NSA_KV_HEADS):
            val_g = val[:, g * NSA_DH:(g + 1) * NSA_DH]
            if ref is ks_ref:
                val_g = jnp.concatenate([val_g, block_1hot], axis=1)
            ref[0, g] = val_g.astype(ref.dtype)
    gate = jax.nn.sigmoid(z[:, NSA_GATE_OFF:NSA_GATE_OFF + LANES])
    src = lax.broadcasted_iota(jnp.int32, (LANES, NSA_Q_W), 0)
    head = lax.broadcasted_iota(jnp.int32, (LANES, NSA_Q_W), 1) // NSA_DH
    for j, ref in enumerate((gc_ref, gs_ref, gw_ref)):
        expand = jnp.where(src == 3 * head + j, 1.0, 0.0).astype(jnp.bfloat16)
        ref[0] = _head_sum(gate, expand)


def _nsa_prep(z, cos, sin):
    bsz, s, zw = z.shape
    tm = NSA_PREP_TM
    wide = pl.BlockSpec((1, tm, NSA_Q_W), lambda b, i: (b, i, 0))
    grp = pl.BlockSpec((1, NSA_KV_HEADS, tm, NSA_DH), lambda b, i: (b, 0, i, 0))
    grp_sd = lambda dt: jax.ShapeDtypeStruct((bsz, NSA_KV_HEADS, s, NSA_DH), dt)
    ks_w = NSA_DH + s // SEL_BLOCK
    ks_spec = pl.BlockSpec((1, NSA_KV_HEADS, tm, ks_w), lambda b, i: (b, 0, i, 0))
    ks_sd = jax.ShapeDtypeStruct((bsz, NSA_KV_HEADS, s, ks_w), jnp.bfloat16)
    wide_sd = lambda dt: jax.ShapeDtypeStruct((bsz, s, NSA_Q_W), dt)
    bf16, f32 = jnp.bfloat16, jnp.float32
    return pl.pallas_call(
        _nsa_prep_kernel,
        out_shape=(wide_sd(bf16), ks_sd, grp_sd(bf16), grp_sd(bf16), grp_sd(bf16), grp_sd(f32), grp_sd(f32),
                   wide_sd(f32), wide_sd(f32), wide_sd(f32)),
        grid=(bsz, s // tm),
        in_specs=[pl.BlockSpec((1, tm, zw), lambda b, i: (b, i, 0)), wide, wide],
        out_specs=(wide, ks_spec, grp, grp, grp, grp, grp, wide, wide, wide),
        compiler_params=pltpu.CompilerParams(dimension_semantics=("parallel", "parallel")),
        name="nsa_prep",
    )(z, cos, sin)


def _nsa_compress_kernel(hk_ref, hv_ref, pos_ref, w1_ref, w2_ref, cos_ref, sin_ref, kc_ref, vc_ref):
    half = w1_ref.shape[1] // 2
    n = hk_ref.shape[1]
    for i, (h_ref, o_ref) in enumerate(((hk_ref, kc_ref), (hv_ref, vc_ref))):
        h = h_ref[0]
        first = _bdot(h, w1_ref[i, 0:half, :])
        second = pltpu.roll(_bdot(h, w1_ref[i, half:2 * half, :]), n - 1, axis=0)
        bias = _bdot(jnp.broadcast_to(pos_ref[i], (8, 2 * half)), w1_ref[i])[0:1]
        out = _bdot(jax.nn.gelu(first + second + bias, approximate=True), w2_ref[i])
        if i == 0:
            src = lax.broadcasted_iota(jnp.int32, (NSA_DH, NSA_DH), 0)
            dst = lax.broadcasted_iota(jnp.int32, (NSA_DH, NSA_DH), 1)
            swap = jnp.where(src == (dst + NSA_DH // 2) % NSA_DH, 1.0, 0.0).astype(jnp.bfloat16)
            out = out * cos_ref[0] + _head_sum(out, swap) * sin_ref[0]
        o_ref[0] = out.astype(o_ref.dtype)


def _nsa_compress(hk, hv, pos_flat, w1, w2, cos_c, sin_c):
    bg, n, hw = hk.shape
    ng = NSA_KV_HEADS
    blk = pl.BlockSpec((1, n, hw), lambda i: (i, 0, 0))
    tab = pl.BlockSpec((1, n, NSA_DH), lambda i: (i // ng, 0, 0))
    out = pl.BlockSpec((1, n, NSA_DH), lambda i: (i, 0, 0))
    whole = lambda a: pl.BlockSpec(a.shape, lambda i: (0,) * a.ndim)
    sd = jax.ShapeDtypeStruct((bg, n, NSA_DH), jnp.bfloat16)
    return pl.pallas_call(
        _nsa_compress_kernel,
        out_shape=(sd, sd),
        grid=(bg,),
        in_specs=[blk, blk, whole(pos_flat), whole(w1), whole(w2), tab, tab],
        out_specs=(out, out),
        compiler_params=pltpu.CompilerParams(
            dimension_semantics=("parallel",), vmem_limit_bytes=V7X_VMEM_LIMIT_BYTES),
        name="nsa_compress",
    )(hk, hv, pos_flat, w1, w2, cos_c, sin_c)


def _nsa_attn_kernel(q_ref, kc_ref, vc_ref, ks_ref, vs_ref, kw_ref, vw_ref, gc_ref, gs_ref, gw_ref, o_ref):
    f32, bf16 = jnp.float32, jnp.bfloat16
    qi = pl.program_id(2)
    q0 = qi * Q_BLOCK
    hpg, dh, tk = NSA_GROUP, NSA_DH, NSA_KEY_TILE
    rows = hpg * Q_BLOCK
    n_sel = ks_ref.shape[3] - dh
    n_cmp = kc_ref.shape[2]
    q_all = q_ref[0]
    q = jnp.concatenate([q_all[:, h * dh:(h + 1) * dh] for h in range(hpg)], axis=0)
    by_head = lambda o: jnp.concatenate([o[h * Q_BLOCK:(h + 1) * Q_BLOCK] for h in range(hpg)], axis=1)

    tq_c = q0 + lax.broadcasted_iota(jnp.int32, (Q_BLOCK, n_cmp), 0)
    c_end = lax.broadcasted_iota(jnp.int32, (Q_BLOCK, n_cmp), 1) * CMP_STRIDE + (CMP_LEN - 1)
    c_ok = jnp.concatenate([jnp.where(c_end <= tq_c, 1.0, 0.0)] * hpg, axis=0) > 0.5
    s = jnp.where(c_ok, lax.dot_general(q, kc_ref[0, 0], _NT, preferred_element_type=f32), NEG_BIG)
    m = jnp.max(s, axis=-1, keepdims=True)
    e = jnp.where(c_ok, jnp.exp(s - m), 0.0)
    p = e / jnp.maximum(jnp.sum(e, axis=-1, keepdims=True), 1e-30)
    o_cmp = jnp.dot(p.astype(bf16), vc_ref[0, 0], preferred_element_type=f32)

    p_sum = p[0:Q_BLOCK]
    for h in range(1, hpg):
        p_sum = p_sum + p[h * Q_BLOCK:(h + 1) * Q_BLOCK]
    c_start = lax.broadcasted_iota(jnp.int32, (n_sel, n_cmp), 1) * CMP_STRIDE
    j_blk = lax.broadcasted_iota(jnp.int32, (n_sel, n_cmp), 0)
    overlap_t = jnp.where((c_start < (j_blk + 1) * SEL_BLOCK) & (c_start + CMP_LEN > j_blk * SEL_BLOCK),
                          1.0, 0.0).astype(bf16)
    p_hi = p_sum.astype(bf16)
    p_lo = (p_sum - p_hi.astype(f32)).astype(bf16)
    imp = (lax.dot_general(overlap_t, p_hi, _NT, preferred_element_type=f32)
           + lax.dot_general(overlap_t, p_lo, _NT, preferred_element_type=f32))
    jb = lax.broadcasted_iota(jnp.int32, (n_sel, Q_BLOCK), 0)
    cur = (q0 + lax.broadcasted_iota(jnp.int32, (n_sel, Q_BLOCK), 1)) // SEL_BLOCK
    forced = (jb == 0) | (jb == cur) | (jb == cur - 1)
    imp = jnp.where(forced, -NEG_BIG, jnp.where(jb <= cur, imp, NEG_BIG))
    rank = jnp.zeros((n_sel, Q_BLOCK), f32)
    for i in range(n_sel):
        row_i = imp[i:i + 1, :]
        ahead = (row_i > imp) | ((row_i == imp) & (jb > i))
        rank = rank + jnp.where(ahead, 1.0, 0.0)
    sel_bias = jnp.where(rank < float(min(SEL_TOP, n_sel)), 0.0, NEG_BIG).T.astype(bf16)

    q_sel = jnp.concatenate([q, jnp.concatenate([sel_bias] * hpg, axis=0)], axis=1)

    def tile_update(kt, carry, on_diagonal):
        m, l, acc = carry
        k0 = pl.multiple_of(kt * tk, tk)
        s = lax.dot_general(q_sel, ks_ref[0, 0, pl.ds(k0, tk), :], _NT, preferred_element_type=f32)
        if on_diagonal:
            t_pos = q0 + lax.broadcasted_iota(jnp.int32, (Q_BLOCK, tk), 0)
            key = k0 + lax.broadcasted_iota(jnp.int32, (Q_BLOCK, tk), 1)
            causal = jnp.concatenate([jnp.where(key <= t_pos, 0.0, NEG_BIG)] * hpg, axis=0)
            s = s + causal
        m_new = jnp.maximum(m, jnp.max(s, axis=-1, keepdims=True))
        alpha = jnp.exp(m - m_new)
        p = jnp.exp(s - m_new)
        l = alpha * l + jnp.sum(p, axis=-1, keepdims=True)
        acc = alpha * acc + jnp.dot(p.astype(bf16), vs_ref[0, 0, pl.ds(k0, tk), :], preferred_element_type=f32)
        return m_new, l, acc

    init = (jnp.full((rows, 1), NEG_BIG, f32), jnp.zeros((rows, 1), f32), jnp.zeros((rows, dh), f32))
    last_tile = (q0 + Q_BLOCK - 1) // tk
    carry = lax.fori_loop(0, last_tile, lambda kt, c: tile_update(kt, c, False), init)
    m, l, acc = tile_update(last_tile, carry, True)
    o_sel = acc / jnp.maximum(l, 1e-30)

    w0 = pl.multiple_of(jnp.maximum(q0 - WINDOW, 0), Q_BLOCK)
    kwin = kw_ref[0, 0, pl.ds(w0, NSA_WIN_SPAN), :]
    vwin = vw_ref[0, 0, pl.ds(w0, NSA_WIN_SPAN), :]
    tq = q0 + lax.broadcasted_iota(jnp.int32, (Q_BLOCK, NSA_WIN_SPAN), 0)
    kp = w0 + lax.broadcasted_iota(jnp.int32, (Q_BLOCK, NSA_WIN_SPAN), 1)
    dlt = tq - kp
    wbias = jnp.where((dlt >= 0) & (dlt < WINDOW), 0.0, NEG_BIG).astype(f32)
    wbias = jnp.concatenate([wbias] * hpg, axis=0)
    s = lax.dot_general(q, kwin, _NT, preferred_element_type=f32) + wbias
    mw = jnp.max(s, axis=-1, keepdims=True)
    p = jnp.exp(s - mw)
    lw = jnp.sum(p, axis=-1, keepdims=True)
    o_win = jnp.dot(p.astype(bf16), vwin, preferred_element_type=f32) / jnp.maximum(lw, 1e-30)

    o_ref[0] = (gc_ref[0] * by_head(o_cmp) + gs_ref[0] * by_head(o_sel) + gw_ref[0] * by_head(o_win)
                ).astype(o_ref.dtype)


def _nsa_attn(q, k_cmp, v_cmp, ks, vs, kw, vw, g_cmp, g_sel, g_win):
    bsz, s, _ = q.shape
    ng, dh = NSA_KV_HEADS, NSA_DH
    gw = NSA_GROUP * dh
    kv_spec = pl.BlockSpec((1, 1, s, dh), lambda b, g, i: (b, g, 0, 0))
    ks_spec = pl.BlockSpec((1, 1, s, ks.shape[3]), lambda b, g, i: (b, g, 0, 0))
    cmp_spec = pl.BlockSpec((1, 1, k_cmp.shape[2], dh), lambda b, g, i: (b, g, 0, 0))
    q_spec = pl.BlockSpec((1, Q_BLOCK, gw), lambda b, g, i: (b, i, g))
    return pl.pallas_call(
        _nsa_attn_kernel,
        out_shape=jax.ShapeDtypeStruct(q.shape, jnp.bfloat16),
        grid=(bsz, ng, s // Q_BLOCK),
        in_specs=[q_spec, cmp_spec, cmp_spec, ks_spec, kv_spec, kv_spec, kv_spec, q_spec, q_spec, q_spec],
        out_specs=q_spec,
        compiler_params=pltpu.CompilerParams(
            dimension_semantics=("parallel", "parallel", "arbitrary"),
            vmem_limit_bytes=V7X_VMEM_LIMIT_BYTES),
        name="nsa_attn",
    )(q, k_cmp, v_cmp, ks, vs, kw, vw, g_cmp, g_sel, g_win)


def _nsa_mixer(z, nsa_tabs, cmp_pos, cmp_w1, cmp_w2):
    bsz, s, _ = z.shape
    ng, dh = NSA_KV_HEADS, NSA_DH
    cos, sin, cos_c, sin_c = nsa_tabs
    q, ks, vs, kw, vw, kc, vc, g_cmp, g_sel, g_win = _nsa_prep(z, cos, sin)
    n_blk = s // CMP_STRIDE
    flat = lambda t: t.reshape(bsz * ng, n_blk, CMP_STRIDE * dh)
    k_cmp, v_cmp = _nsa_compress(flat(kc), flat(vc), cmp_pos.reshape(2, 1, CMP_LEN * dh), cmp_w1, cmp_w2,
                                 cos_c, sin_c)
    grouped = lambda t: t.reshape(bsz, ng, n_blk, dh)
    return _nsa_attn(q, grouped(k_cmp), grouped(v_cmp), ks, vs, kw, vw, g_cmp, g_sel, g_win)


def _nsa_tables(cos, sin):
    end = CMP_LEN - 1
    at_ends = lambda t: jnp.pad(t[:, end::CMP_STRIDE, :NSA_DH], ((0, 0), (0, 1), (0, 0)))
    return cos, sin, at_ends(cos), at_ends(sin)


IN_GROUPS = (("ret", GROUP_SIZES[0] + GROUP_SIZES[1], GROUP_SIZES[2], 3072),
             ("rw", GROUP_SIZES[0], GROUP_SIZES[1], 2048),
             ("sc", 0, GROUP_SIZES[0], 1536),
             ("nsa", sum(GROUP_SIZES[:3]), GROUP_SIZES[3], 1536),
             ("gate", sum(GROUP_SIZES[:4]), GROUP_SIZES[4], 8192))
IN_COL0 = {name: sum(g[3] for g in IN_GROUPS[:i]) for i, (name, _, _, _) in enumerate(IN_GROUPS)}
IN_WIDTH = {name: padded for name, _, _, padded in IN_GROUPS}


def _aligned_in_proj(w_in):
    parts = [jnp.pad(w_in[:, :, start:start + width], ((0, 0), (0, 0), (0, padded - width)))
             for _, start, width, padded in IN_GROUPS]
    return jnp.concatenate(parts, axis=-1).astype(jnp.bfloat16)


def _hybrid_mixer(h, bsz, layer, ret_tabs, nsa_tabs, w_all, sc_conv, rw_mu, rw_w0, rw_w2, rw_a0, rw_a2, rw_g2,
                  rw_k_k, rw_k_a, rw_r_k, rw_lnx_g, rw_lnx_b, nsa_cmp_pos, nsa_cmp_w1, nsa_cmp_w2,
                  wb_stack):
    m = h.shape[0]
    s = m // bsz
    group = lambda name: _mm_cols(h, w_all, layer, IN_COL0[name], IN_WIDTH[name])
    per_batch = lambda z: z.reshape(bsz, s, z.shape[1])
    o_a = _short_conv_mixer(group("sc"), sc_conv, s)
    o_b = _rwkv7_mixer(per_batch(group("rw")), rw_mu, rw_w0, rw_w2, rw_a0, rw_a2, rw_g2, rw_k_k, rw_k_a, rw_r_k,
                       rw_lnx_g, rw_lnx_b)
    o_c = _retention_mixer(per_batch(group("ret")), *ret_tabs)
    o_d = _nsa_mixer(per_batch(group("nsa")), nsa_tabs, nsa_cmp_pos, nsa_cmp_w1, nsa_cmp_w2)
    outs = [o.reshape(m, o.shape[-1]) for o in (o_a, o_b, o_c, o_d)]
    assert tuple(o.shape[1] for o in outs) == BR_SIZES
    return _merge(h, w_all, layer, IN_COL0["gate"], outs, wb_stack)


def kernel(x, mem, positions, ln_mix_pre, ln_mix_post, ln_mem_q, ln_mem_kv, ln_mem_post, ln_ffn_pre, ln_ffn_post, w_in, sc_conv, rw_mu, rw_w0, rw_w2, rw_a0, rw_a2, rw_g2, rw_k_k, rw_k_a, rw_r_k, rw_lnx_g, rw_lnx_b, nsa_cmp_pos, nsa_cmp_w1, nsa_cmp_w2, w_branch, w_out, mem_wq, mem_wkv, mem_wo, ffn_w_up, ffn_conv_w, ffn_conv_b, ffn_w_down):
    bsz, s, d = x.shape
    mem_len = mem.shape[1]
    xf = x.reshape(bsz * s, d)
    h = _prenorm(xf, ln_mix_pre[0])
    ret_cos, ret_sin, nsa_cos, nsa_sin = _rope_tables(positions)
    ret_tabs = (ret_cos, ret_sin)
    nsa_tabs = _nsa_tables(nsa_cos, nsa_sin)
    w_all = _aligned_in_proj(w_in)
    bf16 = lambda w: w.astype(jnp.bfloat16)
    wb_b, w_out_b, wq_b, wkv_b, wo_b, w_down_b = (bf16(w_branch), bf16(w_out), bf16(mem_wq), bf16(mem_wkv),
                                                   bf16(mem_wo), bf16(ffn_w_down))
    for l in range(DEPTH):
        merged = _hybrid_mixer(h, bsz, l, ret_tabs, nsa_tabs, w_all, sc_conv[l], rw_mu[l], rw_w0[l], rw_w2[l], rw_a0[l],
                               rw_a2[l], rw_g2[l], rw_k_k[l], rw_k_a[l], rw_r_k[l], rw_lnx_g[l], rw_lnx_b[l],
                               nsa_cmp_pos[l], nsa_cmp_w1[l], nsa_cmp_w2[l], wb_b)
        xf, h = _mm_norm_res(merged, w_out_b, l, xf, ln_mix_post[l], ln_mem_q[l])
        mem_n = _prenorm(mem.reshape(bsz * mem_len, d), ln_mem_kv[l])
        kv = _mm_cols(mem_n, wkv_b, l, 0, 2 * d).astype(jnp.bfloat16)
        k_mem = kv[:, :d].reshape(bsz, mem_len, d)
        v_mem = kv[:, d:].reshape(bsz, mem_len, d)
        o = _xattn(h, wq_b, l, k_mem, v_mem, s)
        xf, h = _mm_norm_res(o, wo_b, l, xf, ln_mem_post[l], ln_ffn_pre[l])
        act = _ffn_up(h, ffn_w_up, l, ffn_conv_w[l], ffn_conv_b[l], s)
        g_next = ln_mix_pre[l + 1] if l + 1 < DEPTH else None
        xf, h = _mm_norm_res(act, w_down_b, l, xf, ln_ffn_post[l], g_next)
    return xf.reshape(bsz, s, d)
```

```python
import functools
import math

import jax
import jax.numpy as jnp
import numpy as np
from jax import lax
from jax.experimental import pallas as pl
from jax.experimental.pallas import tpu as pltpu

D_MODEL = 2048
DEPTH = 2
MEM_HEADS = 4
MEM_HEAD_DIM = D_MODEL // MEM_HEADS
SC_W = D_MODEL // 4
SC_K = 3
RW_HEAD_DIM = 64
RW_W = D_MODEL // 4
RW_HEADS = RW_W // RW_HEAD_DIM
RW_LORA_W = 96
RW_LORA_A = 96
RW_LORA_G = 256
RW_DECAY_SCALE = math.exp(-0.5)
RW_LN_EPS = 64e-5
RET_HEADS = 4
RET_DK = 128
RET_DV = 2 * RET_DK
RET_CHUNK = 128
RET_LN_EPS = 1e-5
NSA_HEADS = 8
NSA_KV_HEADS = 2
NSA_DH = 64
NSA_GROUP = NSA_HEADS // NSA_KV_HEADS
CMP_LEN = 32
CMP_STRIDE = 16
SEL_BLOCK = 64
SEL_TOP = 16
WINDOW = 512
Q_BLOCK = 256
ROPE_THETA = 10000.0
EPS = 1e-6
N_BRANCH = 4

SC_SIZES = (SC_W, SC_W, SC_W)
RW_SIZES = (RW_W, RW_W, RW_W, RW_LORA_W, RW_LORA_A, RW_LORA_G)
RET_SIZES = (RET_HEADS * RET_DK, RET_HEADS * RET_DK, RET_HEADS * RET_DV, RET_HEADS * RET_DV)
NSA_SIZES = (NSA_HEADS * NSA_DH,) + (NSA_KV_HEADS * NSA_DH,) * 6 + (NSA_HEADS * 3,)
GATE_SIZES = (D_MODEL,) * N_BRANCH
GROUP_SIZES = (sum(SC_SIZES), sum(RW_SIZES), sum(RET_SIZES), sum(NSA_SIZES), sum(GATE_SIZES))
BR_SIZES = (SC_W, RW_W, RET_HEADS * RET_DV, NSA_HEADS * NSA_DH)

V7X_VMEM_LIMIT_BYTES = 48 * 1024 * 1024
LANES = 128
_NT = (((1,), (1,)), ((), ()))


def _pick_tile(n, candidates):
    for c in candidates:
        if n % c == 0:
            return c
    return n


def _bdot(a, b):
    return jnp.dot(a.astype(jnp.bfloat16), b.astype(jnp.bfloat16), preferred_element_type=jnp.float32)


def _bdot_nt(a, b):
    return lax.dot_general(a.astype(jnp.bfloat16), b.astype(jnp.bfloat16), _NT,
                           preferred_element_type=jnp.float32)


def _head_sum(x, bd):
    hi = x.astype(jnp.bfloat16)
    lo = (x - hi.astype(jnp.float32)).astype(jnp.bfloat16)
    return (jnp.dot(hi, bd, preferred_element_type=jnp.float32)
            + jnp.dot(lo, bd, preferred_element_type=jnp.float32))


def _mm_kernel(a_ref, w_ref, o_ref, acc_ref):
    k = pl.program_id(2)

    @pl.when(k == 0)
    def _():
        acc_ref[...] = jnp.zeros_like(acc_ref)

    acc_ref[...] += jnp.dot(a_ref[...], w_ref[...], preferred_element_type=jnp.float32)

    @pl.when(k == pl.num_programs(2) - 1)
    def _():
        o_ref[...] = acc_ref[...]


def _mm_cols(a, w_stack, layer, col0, n):
    m, k = a.shape
    tn = next(c for c in (n if n <= 1536 else 1024, 1024, 768, 512, 256, 128) if n % c == 0 and col0 % c == 0)
    tm = _pick_tile(m, (1024, 512, 256, 128, 8))
    tk = _pick_tile(k, (2048, 1280, 1408, 1024, 512, 256, 128))
    j0 = col0 // tn
    return pl.pallas_call(
        _mm_kernel,
        out_shape=jax.ShapeDtypeStruct((m, n), jnp.float32),
        grid=(m // tm, n // tn, k // tk),
        in_specs=[pl.BlockSpec((tm, tk), lambda i, j, kk: (i, kk)),
                  pl.BlockSpec((None, tk, tn), lambda i, j, kk: (layer, kk, j0 + j))],
        out_specs=pl.BlockSpec((tm, tn), lambda i, j, kk: (i, j)),
        scratch_shapes=[pltpu.VMEM((tm, tn), jnp.float32)],
        compiler_params=pltpu.CompilerParams(
            dimension_semantics=("parallel", "parallel", "arbitrary"),
            vmem_limit_bytes=V7X_VMEM_LIMIT_BYTES),
        name="mm",
    )(a.astype(jnp.bfloat16), w_stack)


def _rms(y, g):
    return y * lax.rsqrt(jnp.mean(y * y, axis=-1, keepdims=True) + EPS) * g


def _prenorm_kernel(x_ref, g_ref, h_ref):
    h_ref[...] = _rms(x_ref[...], g_ref[...]).astype(h_ref.dtype)


def _prenorm(x, g):
    m, d = x.shape
    tm = _pick_tile(m, (512, 256, 128, 8))
    return pl.pallas_call(
        _prenorm_kernel,
        out_shape=jax.ShapeDtypeStruct((m, d), jnp.bfloat16),
        grid=(m // tm,),
        in_specs=[pl.BlockSpec((tm, d), lambda i: (i, 0)), pl.BlockSpec((1, d), lambda i: (0, 0))],
        out_specs=pl.BlockSpec((tm, d), lambda i: (i, 0)),
        compiler_params=pltpu.CompilerParams(dimension_semantics=("parallel",)),
        name="prenorm",
    )(x, g.reshape(1, d))


def _mm_norm_res_kernel(emit_h, whole_k, a_ref, w_ref, res_ref, g_ref, *rest):
    if emit_h:
        g2_ref, x_ref, h_ref = rest[:3]
    else:
        x_ref = rest[0]

    def finish(y):
        x_new = res_ref[...] + _rms(y, g_ref[...])
        x_ref[...] = x_new
        if emit_h:
            h_ref[...] = _rms(x_new, g2_ref[...]).astype(h_ref.dtype)

    if whole_k:
        finish(jnp.dot(a_ref[...], w_ref[...], preferred_element_type=jnp.float32))
        return
    acc_ref = rest[-1]
    k = pl.program_id(1)

    @pl.when(k == 0)
    def _():
        acc_ref[...] = jnp.zeros_like(acc_ref)

    acc_ref[...] += jnp.dot(a_ref[...], w_ref[...], preferred_element_type=jnp.float32)

    @pl.when(k == pl.num_programs(1) - 1)
    def _():
        finish(acc_ref[...])


def _mm_norm_res(a, w_stack, layer, res, g, g_next):
    m, k = a.shape
    d = w_stack.shape[2]
    tk = _pick_tile(k, (2048, 1408, 1024, 512))
    tm = 512
    whole_k = tk == k
    emit_h = g_next is not None
    row = pl.BlockSpec((tm, d), lambda i, kk: (i, 0))
    vec = pl.BlockSpec((1, d), lambda i, kk: (0, 0))
    w_spec = (pl.BlockSpec((None, tk, d), lambda i, kk: (layer, 0, 0), pipeline_mode=pl.Buffered(1)) if whole_k
              else pl.BlockSpec((None, tk, d), lambda i, kk: (layer, kk, 0)))
    x_sd = jax.ShapeDtypeStruct((m, d), jnp.float32)
    h_sd = jax.ShapeDtypeStruct((m, d), jnp.bfloat16)
    out = pl.pallas_call(
        functools.partial(_mm_norm_res_kernel, emit_h, whole_k),
        out_shape=(x_sd, h_sd) if emit_h else x_sd,
        grid=(m // tm, k // tk),
        in_specs=[pl.BlockSpec((tm, tk), lambda i, kk: (i, kk)), w_spec, row, vec] + ([vec] if emit_h else []),
        out_specs=(row, row) if emit_h else row,
        scratch_shapes=[] if whole_k else [pltpu.VMEM((tm, d), jnp.float32)],
        compiler_params=pltpu.CompilerParams(
            dimension_semantics=("parallel", "arbitrary"), vmem_limit_bytes=V7X_VMEM_LIMIT_BYTES),
        name="mm_norm_res",
    )(a.astype(jnp.bfloat16), w_stack, res, g.reshape(1, d), *([g_next.reshape(1, d)] if emit_h else []))
    return out if emit_h else (out, None)


FFN_TM = 1024
FFN_TN = 512
HALO = 8


def _shift_rows(u, prev, n):
    rolled = pltpu.roll(u, n, axis=0)
    rows = lax.broadcasted_iota(jnp.int32, u.shape, 0)
    for r in range(n):
        rolled = jnp.where(rows == r, prev[HALO - n + r:HALO - n + r + 1, :], rolled)
    return rolled


def _ffn_up_kernel(seq_len, h_ref, halo_ref, wa_ref, wb_ref, cwa_ref, cwb_ref, ba_ref, bb_ref, o_ref):
    i = pl.program_id(0)
    h = h_ref[...]
    at_start = (i * FFN_TM) % seq_len == 0
    halo = jnp.where(at_start, jnp.zeros_like(halo_ref[...]), halo_ref[...])

    def conv(w_ref, cw_ref, b_ref):
        w = w_ref[...].astype(jnp.bfloat16)
        u = jnp.dot(h, w, preferred_element_type=jnp.float32)
        up = jnp.dot(halo, w, preferred_element_type=jnp.float32)
        cw = cw_ref[...]
        y = cw[0:1] * _shift_rows(u, up, 2)
        y = y + cw[1:2] * _shift_rows(u, up, 1)
        y = y + cw[2:3] * u
        return y + b_ref[...]

    a = conv(wa_ref, cwa_ref, ba_ref)
    b = conv(wb_ref, cwb_ref, bb_ref)
    o_ref[...] = (jax.nn.gelu(a, approximate=True) * b).astype(o_ref.dtype)


def _ffn_up(h, w_up_stack, layer, w_conv, b_conv, seq_len):
    m, d = h.shape
    f = w_up_stack.shape[2] // 2
    nj = f // FFN_TN
    b_conv = b_conv.reshape(1, 2 * f)
    tiles_per_halo = FFN_TM // HALO
    return pl.pallas_call(
        functools.partial(_ffn_up_kernel, seq_len),
        out_shape=jax.ShapeDtypeStruct((m, f), jnp.bfloat16),
        grid=(m // FFN_TM, nj),
        in_specs=[pl.BlockSpec((FFN_TM, d), lambda i, j: (i, 0)),
                  pl.BlockSpec((HALO, d), lambda i, j: (jnp.maximum(i * tiles_per_halo - 1, 0), 0)),
                  pl.BlockSpec((None, d, FFN_TN), lambda i, j: (layer, 0, j)),
                  pl.BlockSpec((None, d, FFN_TN), lambda i, j: (layer, 0, j + nj)),
                  pl.BlockSpec((3, FFN_TN), lambda i, j: (0, j)),
                  pl.BlockSpec((3, FFN_TN), lambda i, j: (0, j + nj)),
                  pl.BlockSpec((1, FFN_TN), lambda i, j: (0, j)),
                  pl.BlockSpec((1, FFN_TN), lambda i, j: (0, j + nj))],
        out_specs=pl.BlockSpec((FFN_TM, FFN_TN), lambda i, j: (i, j)),
        compiler_params=pltpu.CompilerParams(
            dimension_semantics=("parallel", "arbitrary"), vmem_limit_bytes=V7X_VMEM_LIMIT_BYTES),
        name="ffn_up",
    )(h, h, w_up_stack, w_up_stack, w_conv, w_conv, b_conv, b_conv)


MERGE_TM = 512
MERGE_TN = 512


def _merge_kernel(h_ref, wg0, wg1, wg2, wg3, o0, o1, o2, o3, wb0, wb1, wb2, wb3, out_ref):
    h = h_ref[...]
    acc = None
    for wg, o, wb in ((wg0, o0, wb0), (wg1, o1, wb1), (wg2, o2, wb2), (wg3, o3, wb3)):
        gate = jax.nn.sigmoid(jnp.dot(h, wg[...], preferred_element_type=jnp.float32))
        term = gate * jnp.dot(o[...], wb[...], preferred_element_type=jnp.float32)
        acc = term if acc is None else acc + term
    out_ref[...] = acc.astype(out_ref.dtype)


def _merge(h, w_stack, layer, gate_col0, outs, wb_stack):
    m, d = h.shape
    nj = d // MERGE_TN
    j0 = gate_col0 // MERGE_TN
    assert gate_col0 % MERGE_TN == 0
    gate_specs = [pl.BlockSpec((None, d, MERGE_TN), functools.partial(lambda b, i, j: (layer, 0, j0 + b * nj + j), b))
                  for b in range(N_BRANCH)]
    o_specs = [pl.BlockSpec((MERGE_TM, o.shape[1]), lambda i, j: (i, 0)) for o in outs]
    row0 = np.cumsum([0] + [o.shape[1] for o in outs])
    assert all(r % o.shape[1] == 0 for r, o in zip(row0, outs))
    wb_specs = [pl.BlockSpec((None, o.shape[1], MERGE_TN),
                             functools.partial(lambda rb, i, j: (layer, rb, j), int(r) // o.shape[1]))
                for r, o in zip(row0, outs)]
    return pl.pallas_call(
        _merge_kernel,
        out_shape=jax.ShapeDtypeStruct((m, d), jnp.bfloat16),
        grid=(m // MERGE_TM, nj),
        in_specs=[pl.BlockSpec((MERGE_TM, d), lambda i, j: (i, 0))] + gate_specs + o_specs + wb_specs,
        out_specs=pl.BlockSpec((MERGE_TM, MERGE_TN), lambda i, j: (i, j)),
        compiler_params=pltpu.CompilerParams(
            dimension_semantics=("parallel", "arbitrary"), vmem_limit_bytes=V7X_VMEM_LIMIT_BYTES),
        name="merge",
    )(h, *([w_stack] * N_BRANCH), *[o.astype(jnp.bfloat16) for o in outs], *([wb_stack] * N_BRANCH))


XATT_TM = 512


def _xattn_kernel(h_ref, wq_ref, k_ref, v_ref, o_ref):
    q = jnp.dot(h_ref[...], wq_ref[...], preferred_element_type=jnp.float32)
    q = (q * (MEM_HEAD_DIM ** -0.5)).astype(jnp.bfloat16)
    for hh in range(MEM_HEADS):
        sl = slice(hh * MEM_HEAD_DIM, (hh + 1) * MEM_HEAD_DIM)
        s = lax.dot_general(q[:, sl], k_ref[0, :, sl], _NT, preferred_element_type=jnp.float32)
        p = jnp.exp(s - jnp.max(s, axis=-1, keepdims=True))
        p = p / jnp.sum(p, axis=-1, keepdims=True)
        o_ref[:, sl] = jnp.dot(p.astype(jnp.bfloat16), v_ref[0, :, sl],
                               preferred_element_type=jnp.float32).astype(o_ref.dtype)


def _xattn(h, wq_stack, layer, k, v, seq_len):
    m, d = h.shape
    mem_len = k.shape[1]
    per_batch = seq_len // XATT_TM
    kv_spec = pl.BlockSpec((1, mem_len, d), lambda i: (i // per_batch, 0, 0))
    return pl.pallas_call(
        _xattn_kernel,
        out_shape=jax.ShapeDtypeStruct((m, d), jnp.bfloat16),
        grid=(m // XATT_TM,),
        in_specs=[pl.BlockSpec((XATT_TM, d), lambda i: (i, 0)),
                  pl.BlockSpec((None, d, d), lambda i: (layer, 0, 0)), kv_spec, kv_spec],
        out_specs=pl.BlockSpec((XATT_TM, d), lambda i: (i, 0)),
        compiler_params=pltpu.CompilerParams(
            dimension_semantics=("parallel",), vmem_limit_bytes=V7X_VMEM_LIMIT_BYTES),
        name="xattn",
    )(h, wq_stack, k, v)


def _rope_freqs(d):
    return ROPE_THETA ** (-jnp.arange(0, d, 2, dtype=jnp.float32) / d)


def _retnet_freqs(d):
    return 1.0 / (ROPE_THETA ** jnp.linspace(0.0, 1.0, d // 2, dtype=jnp.float32))


WKV_CHUNK = 64


def _rwkv7_kernel(z_ref, mu_ref, w0_ref, a0_ref, kk_ref, ka_ref, rk_ref, lng_ref, lnb_ref,
                  ww2_ref, wa2_ref, wg2_ref, o_ref, s_ref, prev_ref, y_ref):
    c = WKV_CHUNK
    hd = RW_HEAD_DIM
    w = RW_W
    nb = z_ref.shape[0]
    f32, bf16 = jnp.float32, jnp.bfloat16

    @pl.when(pl.program_id(0) == 0)
    def _():
        s_ref[...] = jnp.zeros_like(s_ref)
        prev_ref[...] = jnp.zeros_like(prev_ref)

    lane_h = lax.broadcasted_iota(jnp.int32, (w, w), 0) // hd
    lane_w = lax.broadcasted_iota(jnp.int32, (w, w), 1) // hd
    head_bd = jnp.where(lane_h == lane_w, 1.0, 0.0).astype(bf16)
    row = lax.broadcasted_iota(jnp.int32, (c, c), 0)
    col = lax.broadcasted_iota(jnp.int32, (c, c), 1)
    tri = jnp.where(row >= col, 1.0, 0.0).astype(bf16)
    eye = jnp.where(row == col, 1.0, 0.0).astype(f32)
    r2 = lax.broadcasted_iota(jnp.int32, (2 * c, 2 * c), 0)
    c2 = lax.broadcasted_iota(jnp.int32, (2 * c, 2 * c), 1)
    rr = jnp.where(r2 >= c, r2 - c, r2)
    cc = jnp.where(c2 >= c, c2 - c, c2)
    tri_mask = cc < rr + jnp.where(r2 >= c, 1, 0)

    def prepare(b):
        z = z_ref[b]
        rows = lax.broadcasted_iota(jnp.int32, z.shape, 0)
        z_prev = jnp.where(rows == 0, prev_ref[b], pltpu.roll(z, 1, axis=0))
        prev_ref[b] = z[c - 1:c, :]
        z = z + (z_prev - z) * mu_ref[...]
        r, k, v, lora = z[:, 0:w], z[:, w:2 * w], z[:, 2 * w:3 * w], z[:, 3 * w:4 * w]
        logw = -RW_DECAY_SCALE * jax.nn.sigmoid(w0_ref[...] + _bdot(jnp.tanh(lora), ww2_ref[...]))
        a = jax.nn.sigmoid(a0_ref[...] + _bdot(lora, wa2_ref[...]))
        g = _bdot(jax.nn.sigmoid(lora), wg2_ref[...])
        kk = k * kk_ref[...]
        kk = kk / jnp.maximum(jnp.sqrt(_head_sum(kk * kk, head_bd)), 1e-12)
        k = k * (1.0 + (a - 1.0) * ka_ref[...])
        hi = logw.astype(bf16)
        rem = logw - hi.astype(f32)
        mid = rem.astype(bf16)
        lo = (rem - mid.astype(f32)).astype(bf16)
        cum = (jnp.dot(tri, hi, preferred_element_type=f32) + jnp.dot(tri, mid, preferred_element_type=f32)
               + jnp.dot(tri, lo, preferred_element_type=f32))
        kka = kk * a
        e_neg = jnp.exp(-cum)
        last = cum[c - 1:c, :]
        dec = jnp.exp(last - cum)
        return dict(r=r, k=k, v=v, g=g, vb=v.astype(bf16), g_last=jnp.exp(last),
                    qh=(kk * jnp.exp(cum - logw)).astype(bf16), rh=(r * jnp.exp(cum)).astype(bf16),
                    bh=(kka * e_neg).astype(bf16), kh=(k * e_neg).astype(bf16),
                    bd=(kka * dec).astype(bf16), kd=(k * dec).astype(bf16))

    pre = [prepare(b) for b in range(nb)]

    units = [(b, h) for b in range(nb) for h in range(RW_HEADS)]
    n_units = range(len(units))
    part = lambda name, i: pre[units[i][0]][name][:, units[i][1] * hd:(units[i][1] + 1) * hd]
    a1 = [jnp.concatenate([part("qh", i), part("rh", i)], axis=0) for i in n_units]
    b1 = [jnp.concatenate([part("bh", i), part("kh", i)], axis=0) for i in n_units]
    ss = [jnp.where(tri_mask, lax.dot_general(a1[i], b1[i], _NT, preferred_element_type=f32), 0.0)
          for i in n_units]
    s_old = [s_ref[i] for i in n_units]
    qr = [lax.dot_general(a1[i], s_old[i].astype(bf16), _NT, preferred_element_type=f32) for i in n_units]
    lm = [_bdot(ss[i][:, c:2 * c], part("vb", i)) for i in n_units]
    n = [-ss[i][0:c, 0:c] for i in n_units]
    t = [eye + n[i] for i in n_units]
    for _ in range(5):
        n = [_bdot(n[i], n[i]) for i in n_units]
        t = [t[i] + _bdot(t[i], n[i]) for i in n_units]
    u = [-_bdot(t[i], qr[i][0:c] + lm[i][0:c]) for i in n_units]
    for i in n_units:
        b, h = units[i]
        y_ref[b, :, h * hd:(h + 1) * hd] = qr[i][c:2 * c] + lm[i][c:2 * c] + _bdot(ss[i][c:2 * c, 0:c], u[i])
    for i in n_units:
        zt = jnp.concatenate([u[i], part("v", i)], axis=0).T
        x = jnp.concatenate([part("bd", i), part("kd", i)], axis=0)
        s_ref[i] = s_old[i] * part("g_last", i) + _bdot(zt, x)

    for b in range(nb):
        p = pre[b]
        y = y_ref[b]
        mean = _head_sum(y, head_bd) * (1.0 / hd)
        yc = y - mean
        var = _head_sum(yc * yc, head_bd) * (1.0 / hd)
        yn = yc * lax.rsqrt(var + RW_LN_EPS) * lng_ref[...] + lnb_ref[...]
        bonus = _head_sum(p["r"] * p["k"] * rk_ref[...], head_bd) * p["v"]
        o_ref[b] = ((yn + bonus) * p["g"]).astype(o_ref.dtype)


def _rwkv7_mixer(z, mu, w0, w_w2, a0, w_a2, w_g2, k_k, k_a, r_k, lnx_g, lnx_b):
    bsz, s, zw = z.shape
    w = RW_W
    n_lora = RW_LORA_W + RW_LORA_A + RW_LORA_G
    row = lambda t: t.reshape(1, -1)
    mu = jnp.pad(mu, (0, zw - mu.shape[0])).reshape(1, zw)
    o_w, o_a = RW_LORA_W, RW_LORA_W + RW_LORA_A
    pad_rows = lambda m, lo: jnp.pad(m, ((lo, w - lo - m.shape[0]), (0, 0))).astype(jnp.bfloat16)
    vec = pl.BlockSpec((1, w), lambda c: (0, 0))
    mat = pl.BlockSpec((w, w), lambda c: (0, 0))
    assert zw == 4 * w and n_lora <= w
    return pl.pallas_call(
        _rwkv7_kernel,
        out_shape=jax.ShapeDtypeStruct((bsz, s, w), jnp.bfloat16),
        grid=(s // WKV_CHUNK,),
        in_specs=[pl.BlockSpec((bsz, WKV_CHUNK, zw), lambda c: (0, c, 0)),
                  pl.BlockSpec((1, zw), lambda c: (0, 0))] + [vec] * 7 + [mat] * 3,
        out_specs=pl.BlockSpec((bsz, WKV_CHUNK, w), lambda c: (0, c, 0)),
        scratch_shapes=[pltpu.VMEM((bsz * RW_HEADS, RW_HEAD_DIM, RW_HEAD_DIM), jnp.float32),
                        pltpu.VMEM((bsz, 1, zw), jnp.float32),
                        pltpu.VMEM((bsz, WKV_CHUNK, w), jnp.float32)],
        compiler_params=pltpu.CompilerParams(dimension_semantics=("arbitrary",)),
        name="rwkv7",
    )(z, mu, row(w0), row(a0), row(k_k), row(k_a), row(r_k), row(lnx_g), row(lnx_b),
      pad_rows(w_w2, 0), pad_rows(w_a2, o_w), pad_rows(w_g2, o_a))


RET_LOG_DECAY = tuple(math.log(1.0 - 2.0 ** (-5.0 - h)) for h in range(RET_HEADS))


def _retention_kernel(q_ref, k_ref, v_ref, g_ref, cos_ref, sin_ref, o_ref, s_ref):
    c, dk, dv = RET_CHUNK, RET_DK, RET_DV
    f32 = jnp.float32

    @pl.when(pl.program_id(1) == 0)
    def _():
        s_ref[...] = jnp.zeros_like(s_ref)

    cos = cos_ref[0]
    sin = sin_ref[0]
    diff = (lax.broadcasted_iota(jnp.int32, (c, c), 0) - lax.broadcasted_iota(jnp.int32, (c, c), 1)).astype(f32)
    tok = lax.broadcasted_iota(jnp.int32, (c, 1), 0).astype(f32)
    rope = lambda x: x * cos + pltpu.roll(x, dk // 2, axis=1) * sin
    for h in range(RET_HEADS):
        lg = RET_LOG_DECAY[h]
        q = rope(q_ref[0, :, h * dk:(h + 1) * dk])
        k = rope(k_ref[0, :, h * dk:(h + 1) * dk]) * (dk ** -0.5)
        v = v_ref[0, :, h * dv:(h + 1) * dv]
        decay_in = jnp.where(diff >= 0, jnp.exp(jnp.maximum(diff, 0.0) * lg), 0.0)
        scores = _bdot_nt(q, k) * decay_in
        state = s_ref[h]
        o = _bdot(scores, v) + _bdot(q * jnp.exp((tok + 1.0) * lg), state)
        s_ref[h] = state * math.exp(c * lg) + _bdot((k * jnp.exp((c - 1.0 - tok) * lg)).T, v)
        mu = jnp.mean(o, axis=-1, keepdims=True)
        oc = o - mu
        var = jnp.mean(oc * oc, axis=-1, keepdims=True)
        gate = g_ref[0, :, h * dv:(h + 1) * dv]
        o_ref[0, :, h * dv:(h + 1) * dv] = (oc * lax.rsqrt(var + RET_LN_EPS) * gate * jax.nn.sigmoid(gate)
                                            ).astype(o_ref.dtype)


def _retention_mixer(z, cos, sin):
    bsz, s, _ = z.shape
    qk_w, v_w = RET_HEADS * RET_DK, RET_HEADS * RET_DV
    c = RET_CHUNK
    tab = pl.BlockSpec((1, c, RET_DK), lambda b, i: (b, i, 0))
    return pl.pallas_call(
        _retention_kernel,
        out_shape=jax.ShapeDtypeStruct((bsz, s, v_w), jnp.bfloat16),
        grid=(bsz, s // c),
        in_specs=[pl.BlockSpec((1, c, qk_w), lambda b, i: (b, i, 0)),
                  pl.BlockSpec((1, c, qk_w), lambda b, i: (b, i, 1)),
                  pl.BlockSpec((1, c, v_w), lambda b, i: (b, i, 1)),
                  pl.BlockSpec((1, c, v_w), lambda b, i: (b, i, 2)), tab, tab],
        out_specs=pl.BlockSpec((1, c, v_w), lambda b, i: (b, i, 0)),
        scratch_shapes=[pltpu.VMEM((RET_HEADS, RET_DK, RET_DV), jnp.float32)],
        compiler_params=pltpu.CompilerParams(dimension_semantics=("parallel", "arbitrary")),
        name="retention",
    )(z, z, z, z, cos, sin)


TRIG_TM = 1024


def _trig_kernel(pos_ref, freq_ref, rc_ref, rs_ref, nc_ref, ns_ref):
    ang = pos_ref[...] * freq_ref[...]
    c, sn = jnp.cos(ang), jnp.sin(ang)
    n_ret, n_nsa = RET_DK // 2, NSA_DH // 2
    rc_ref[...] = jnp.concatenate([c[:, :n_ret]] * 2, axis=1)
    rs_ref[...] = jnp.concatenate([-sn[:, :n_ret], sn[:, :n_ret]], axis=1)
    nsa_c, nsa_s = c[:, n_ret:n_ret + n_nsa], sn[:, n_ret:n_ret + n_nsa]
    nc_ref[...] = jnp.concatenate([nsa_c, nsa_c] * NSA_HEADS, axis=1)
    ns_ref[...] = jnp.concatenate([-nsa_s, nsa_s] * NSA_HEADS, axis=1)


def _rope_tables(positions):
    bsz, s = positions.shape
    m = bsz * s
    n_ret, n_nsa = RET_DK // 2, NSA_DH // 2
    assert n_ret + n_nsa <= LANES
    freq = jnp.concatenate([_retnet_freqs(RET_DK), _rope_freqs(NSA_DH),
                            jnp.zeros((LANES - n_ret - n_nsa,), jnp.float32)]).reshape(1, LANES)
    tm = _pick_tile(m, (TRIG_TM, 512, 256, 128, 8))
    spec = lambda w: pl.BlockSpec((tm, w), lambda i: (i, 0))
    sd = lambda w: jax.ShapeDtypeStruct((m, w), jnp.float32)
    nsa_w = NSA_HEADS * NSA_DH
    tabs = pl.pallas_call(
        _trig_kernel,
        out_shape=(sd(RET_DK), sd(RET_DK), sd(nsa_w), sd(nsa_w)),
        grid=(m // tm,),
        in_specs=[spec(1), pl.BlockSpec((1, LANES), lambda i: (0, 0))],
        out_specs=(spec(RET_DK), spec(RET_DK), spec(nsa_w), spec(nsa_w)),
        compiler_params=pltpu.CompilerParams(dimension_semantics=("parallel",)),
        name="rope_tables",
    )(positions.astype(jnp.float32).reshape(m, 1), freq)
    return [t.reshape(bsz, s, t.shape[1]) for t in tabs]


SC_TM = 512


def _short_conv_kernel(seq_len, b_ref, c_ref, x_ref, ch_ref, xh_ref, w_ref, o_ref):
    at_start = (pl.program_id(0) * SC_TM) % seq_len == 0
    u = c_ref[...] * x_ref[...]
    up = jnp.where(at_start, 0.0, ch_ref[...] * xh_ref[...])
    w = w_ref[...]
    y = w[0:1] * _shift_rows(u, up, 2) + w[1:2] * _shift_rows(u, up, 1) + w[2:3] * u
    o_ref[...] = (b_ref[...] * y).astype(o_ref.dtype)


def _short_conv_mixer(z, w_conv, seq_len):
    m = z.shape[0]
    w = SC_W
    per_halo = SC_TM // HALO
    tile = lambda j: pl.BlockSpec((SC_TM, w), lambda i: (i, j))
    halo = lambda j: pl.BlockSpec((HALO, w), lambda i: (jnp.maximum(i * per_halo - 1, 0), j))
    return pl.pallas_call(
        functools.partial(_short_conv_kernel, seq_len),
        out_shape=jax.ShapeDtypeStruct((m, w), jnp.bfloat16),
        grid=(m // SC_TM,),
        in_specs=[tile(0), tile(1), tile(2), halo(1), halo(2), pl.BlockSpec((SC_K, w), lambda i: (0, 0))],
        out_specs=pl.BlockSpec((SC_TM, w), lambda i: (i, 0)),
        compiler_params=pltpu.CompilerParams(dimension_semantics=("parallel",)),
        name="short_conv",
    )(z, z, z, z, z, w_conv)


NSA_KEY_TILE = 512
NSA_WIN_SPAN = WINDOW + Q_BLOCK
NSA_PREP_TM = 512
NEG_BIG = -1e30
NSA_Q_W = NSA_HEADS * NSA_DH
NSA_KV_W = NSA_KV_HEADS * NSA_DH
NSA_GATE_OFF = NSA_Q_W + 6 * NSA_KV_W


def _rope_lanes(x, cos, sin):
    width = x.shape[1]
    half = NSA_DH // 2
    lane = lax.broadcasted_iota(jnp.int32, x.shape, 1)
    other = jnp.where(lane % NSA_DH < half, pltpu.roll(x, width - half, axis=1), pltpu.roll(x, half, axis=1))
    return x * cos + other * sin


def _nsa_prep_kernel(z_ref, cos_ref, sin_ref, q_ref, ks_ref, vs_ref, kw_ref, vw_ref, kc_ref, vc_ref,
                     gc_ref, gs_ref, gw_ref):
    cos, sin = cos_ref[0], sin_ref[0]
    z = z_ref[0]
    q_ref[0] = (_rope_lanes(z[:, 0:NSA_Q_W], cos, sin) * (NSA_DH ** -0.5)).astype(q_ref.dtype)
    kv = lambda i: z[:, NSA_Q_W + i * NSA_KV_W:NSA_Q_W + (i + 1) * NSA_KV_W]
    cos_kv, sin_kv = cos[:, 0:NSA_KV_W], sin[:, 0:NSA_KV_W]
    pieces = ((kc_ref, kv(0)), (vc_ref, kv(1)), (ks_ref, _rope_lanes(kv(2), cos_kv, sin_kv)), (vs_ref, kv(3)),
              (kw_ref, _rope_lanes(kv(4), cos_kv, sin_kv)), (vw_ref, kv(5)))
    n_sel = ks_ref.shape[3] - NSA_DH
    tok = pl.program_id(1) * NSA_PREP_TM + lax.broadcasted_iota(jnp.int32, (NSA_PREP_TM, n_sel), 0)
    block_1hot = jnp.where(tok // SEL_BLOCK == lax.broadcasted_iota(jnp.int32, (NSA_PREP_TM, n_sel), 1), 1.0, 0.0)
    for ref, val in pieces:
        for g in range(NSA_KV_HEADS):
            val_g = val[:, g * NSA_DH:(g + 1) * NSA_DH]
            if ref is ks_ref:
                val_g = jnp.concatenate([val_g, block_1hot], axis=1)
            ref[0, g] = val_g.astype(ref.dtype)
    gate = jax.nn.sigmoid(z[:, NSA_GATE_OFF:NSA_GATE_OFF + LANES])
    src = lax.broadcasted_iota(jnp.int32, (LANES, NSA_Q_W), 0)
    head = lax.broadcasted_iota(jnp.int32, (LANES, NSA_Q_W), 1) // NSA_DH
    for j, ref in enumerate((gc_ref, gs_ref, gw_ref)):
        expand = jnp.where(src == 3 * head + j, 1.0, 0.0).astype(jnp.bfloat16)
        ref[0] = _head_sum(gate, expand)


def _nsa_prep(z, cos, sin):
    bsz, s, zw = z.shape
    tm = NSA_PREP_TM
    wide = pl.BlockSpec((1, tm, NSA_Q_W), lambda b, i: (b, i, 0))
    grp = pl.BlockSpec((1, NSA_KV_HEADS, tm, NSA_DH), lambda b, i: (b, 0, i, 0))
    grp_sd = lambda dt: jax.ShapeDtypeStruct((bsz, NSA_KV_HEADS, s, NSA_DH), dt)
    ks_w = NSA_DH + s // SEL_BLOCK
    ks_spec = pl.BlockSpec((1, NSA_KV_HEADS, tm, ks_w), lambda b, i: (b, 0, i, 0))
    ks_sd = jax.ShapeDtypeStruct((bsz, NSA_KV_HEADS, s, ks_w), jnp.bfloat16)
    wide_sd = lambda dt: jax.ShapeDtypeStruct((bsz, s, NSA_Q_W), dt)
    bf16, f32 = jnp.bfloat16, jnp.float32
    return pl.pallas_call(
        _nsa_prep_kernel,
        out_shape=(wide_sd(bf16), ks_sd, grp_sd(bf16), grp_sd(bf16), grp_sd(bf16), grp_sd(f32), grp_sd(f32),
                   wide_sd(f32), wide_sd(f32), wide_sd(f32)),
        grid=(bsz, s // tm),
        in_specs=[pl.BlockSpec((1, tm, zw), lambda b, i: (b, i, 0)), wide, wide],
        out_specs=(wide, ks_spec, grp, grp, grp, grp, grp, wide, wide, wide),
        compiler_params=pltpu.CompilerParams(dimension_semantics=("parallel", "parallel")),
        name="nsa_prep",
    )(z, cos, sin)


def _nsa_compress_kernel(hk_ref, hv_ref, pos_ref, w1_ref, w2_ref, cos_ref, sin_ref, kc_ref, vc_ref):
    half = w1_ref.shape[1] // 2
    n = hk_ref.shape[1]
    for i, (h_ref, o_ref) in enumerate(((hk_ref, kc_ref), (hv_ref, vc_ref))):
        h = h_ref[0]
        first = _bdot(h, w1_ref[i, 0:half, :])
        second = pltpu.roll(_bdot(h, w1_ref[i, half:2 * half, :]), n - 1, axis=0)
        bias = _bdot(jnp.broadcast_to(pos_ref[i], (8, 2 * half)), w1_ref[i])[0:1]
        out = _bdot(jax.nn.gelu(first + second + bias, approximate=True), w2_ref[i])
        if i == 0:
            src = lax.broadcasted_iota(jnp.int32, (NSA_DH, NSA_DH), 0)
            dst = lax.broadcasted_iota(jnp.int32, (NSA_DH, NSA_DH), 1)
            swap = jnp.where(src == (dst + NSA_DH // 2) % NSA_DH, 1.0, 0.0).astype(jnp.bfloat16)
            out = out * cos_ref[0] + _head_sum(out, swap) * sin_ref[0]
        o_ref[0] = out.astype(o_ref.dtype)


def _nsa_compress(hk, hv, pos_flat, w1, w2, cos_c, sin_c):
    bg, n, hw = hk.shape
    ng = NSA_KV_HEADS
    blk = pl.BlockSpec((1, n, hw), lambda i: (i, 0, 0))
    tab = pl.BlockSpec((1, n, NSA_DH), lambda i: (i // ng, 0, 0))
    out = pl.BlockSpec((1, n, NSA_DH), lambda i: (i, 0, 0))
    whole = lambda a: pl.BlockSpec(a.shape, lambda i: (0,) * a.ndim)
    sd = jax.ShapeDtypeStruct((bg, n, NSA_DH), jnp.bfloat16)
    return pl.pallas_call(
        _nsa_compress_kernel,
        out_shape=(sd, sd),
        grid=(bg,),
        in_specs=[blk, blk, whole(pos_flat), whole(w1), whole(w2), tab, tab],
        out_specs=(out, out),
        compiler_params=pltpu.CompilerParams(
            dimension_semantics=("parallel",), vmem_limit_bytes=V7X_VMEM_LIMIT_BYTES),
        name="nsa_compress",
    )(hk, hv, pos_flat, w1, w2, cos_c, sin_c)


def _nsa_attn_kernel(q_ref, kc_ref, vc_ref, ks_ref, vs_ref, kw_ref, vw_ref, gc_ref, gs_ref, gw_ref, o_ref):
    f32, bf16 = jnp.float32, jnp.bfloat16
    qi = pl.program_id(2)
    q0 = qi * Q_BLOCK
    hpg, dh, tk = NSA_GROUP, NSA_DH, NSA_KEY_TILE
    rows = hpg * Q_BLOCK
    n_sel = ks_ref.shape[3] - dh
    n_cmp = kc_ref.shape[2]
    q_all = q_ref[0]
    q = jnp.concatenate([q_all[:, h * dh:(h + 1) * dh] for h in range(hpg)], axis=0)
    by_head = lambda o: jnp.concatenate([o[h * Q_BLOCK:(h + 1) * Q_BLOCK] for h in range(hpg)], axis=1)

    tq_c = q0 + lax.broadcasted_iota(jnp.int32, (Q_BLOCK, n_cmp), 0)
    c_end = lax.broadcasted_iota(jnp.int32, (Q_BLOCK, n_cmp), 1) * CMP_STRIDE + (CMP_LEN - 1)
    c_ok = jnp.concatenate([jnp.where(c_end <= tq_c, 1.0, 0.0)] * hpg, axis=0) > 0.5
    s = jnp.where(c_ok, lax.dot_general(q, kc_ref[0, 0], _NT, preferred_element_type=f32), NEG_BIG)
    m = jnp.max(s, axis=-1, keepdims=True)
    e = jnp.where(c_ok, jnp.exp(s - m), 0.0)
    p = e / jnp.maximum(jnp.sum(e, axis=-1, keepdims=True), 1e-30)
    o_cmp = jnp.dot(p.astype(bf16), vc_ref[0, 0], preferred_element_type=f32)

    p_sum = p[0:Q_BLOCK]
    for h in range(1, hpg):
        p_sum = p_sum + p[h * Q_BLOCK:(h + 1) * Q_BLOCK]
    c_start = lax.broadcasted_iota(jnp.int32, (n_sel, n_cmp), 1) * CMP_STRIDE
    j_blk = lax.broadcasted_iota(jnp.int32, (n_sel, n_cmp), 0)
    overlap_t = jnp.where((c_start < (j_blk + 1) * SEL_BLOCK) & (c_start + CMP_LEN > j_blk * SEL_BLOCK),
                          1.0, 0.0).astype(bf16)
    p_hi = p_sum.astype(bf16)
    p_lo = (p_sum - p_hi.astype(f32)).astype(bf16)
    imp = (lax.dot_general(overlap_t, p_hi, _NT, preferred_element_type=f32)
           + lax.dot_general(overlap_t, p_lo, _NT, preferred_element_type=f32))
    jb = lax.broadcasted_iota(jnp.int32, (n_sel, Q_BLOCK), 0)
    cur = (q0 + lax.broadcasted_iota(jnp.int32, (n_sel, Q_BLOCK), 1)) // SEL_BLOCK
    forced = (jb == 0) | (jb == cur) | (jb == cur - 1)
    imp = jnp.where(forced, -NEG_BIG, jnp.where(jb <= cur, imp, NEG_BIG))
    rank = jnp.zeros((n_sel, Q_BLOCK), f32)
    for i in range(n_sel):
        row_i = imp[i:i + 1, :]
        ahead = (row_i > imp) | ((row_i == imp) & (jb > i))
        rank = rank + jnp.where(ahead, 1.0, 0.0)
    sel_bias = jnp.where(rank < float(min(SEL_TOP, n_sel)), 0.0, NEG_BIG).T.astype(bf16)

    q_sel = jnp.concatenate([q, jnp.concatenate([sel_bias] * hpg, axis=0)], axis=1)

    def tile_update(kt, carry, on_diagonal):
        m, l, acc = carry
        k0 = pl.multiple_of(kt * tk, tk)
        s = lax.dot_general(q_sel, ks_ref[0, 0, pl.ds(k0, tk), :], _NT, preferred_element_type=f32)
        if on_diagonal:
            t_pos = q0 + lax.broadcasted_iota(jnp.int32, (Q_BLOCK, tk), 0)
            key = k0 + lax.broadcasted_iota(jnp.int32, (Q_BLOCK, tk), 1)
            causal = jnp.concatenate([jnp.where(key <= t_pos, 0.0, NEG_BIG)] * hpg, axis=0)
            s = s + causal
        m_new = jnp.maximum(m, jnp.max(s, axis=-1, keepdims=True))
        alpha = jnp.exp(m - m_new)
        p = jnp.exp(s - m_new)
        l = alpha * l + jnp.sum(p, axis=-1, keepdims=True)
        acc = alpha * acc + jnp.dot(p.astype(bf16), vs_ref[0, 0, pl.ds(k0, tk), :], preferred_element_type=f32)
        return m_new, l, acc

    init = (jnp.full((rows, 1), NEG_BIG, f32), jnp.zeros((rows, 1), f32), jnp.zeros((rows, dh), f32))
    last_tile = (q0 + Q_BLOCK - 1) // tk
    carry = lax.fori_loop(0, last_tile, lambda kt, c: tile_update(kt, c, False), init)
    m, l, acc = tile_update(last_tile, carry, True)
    o_sel = acc / jnp.maximum(l, 1e-30)

    w0 = pl.multiple_of(jnp.maximum(q0 - WINDOW, 0), Q_BLOCK)
    kwin = kw_ref[0, 0, pl.ds(w0, NSA_WIN_SPAN), :]
    vwin = vw_ref[0, 0, pl.ds(w0, NSA_WIN_SPAN), :]
    tq = q0 + lax.broadcasted_iota(jnp.int32, (Q_BLOCK, NSA_WIN_SPAN), 0)
    kp = w0 + lax.broadcasted_iota(jnp.int32, (Q_BLOCK, NSA_WIN_SPAN), 1)
    dlt = tq - kp
    wbias = jnp.where((dlt >= 0) & (dlt < WINDOW), 0.0, NEG_BIG).astype(f32)
    wbias = jnp.concatenate([wbias] * hpg, axis=0)
    s = lax.dot_general(q, kwin, _NT, preferred_element_type=f32) + wbias
    mw = jnp.max(s, axis=-1, keepdims=True)
    p = jnp.exp(s - mw)
    lw = jnp.sum(p, axis=-1, keepdims=True)
    o_win = jnp.dot(p.astype(bf16), vwin, preferred_element_type=f32) / jnp.maximum(lw, 1e-30)

    o_ref[0] = (gc_ref[0] * by_head(o_cmp) + gs_ref[0] * by_head(o_sel) + gw_ref[0] * by_head(o_win)
                ).astype(o_ref.dtype)


def _nsa_attn(q, k_cmp, v_cmp, ks, vs, kw, vw, g_cmp, g_sel, g_win):
    bsz, s, _ = q.shape
    ng, dh = NSA_KV_HEADS, NSA_DH
    gw = NSA_GROUP * dh
    kv_spec = pl.BlockSpec((1, 1, s, dh), lambda b, g, i: (b, g, 0, 0))
    ks_spec = pl.BlockSpec((1, 1, s, ks.shape[3]), lambda b, g, i: (b, g, 0, 0))
    cmp_spec = pl.BlockSpec((1, 1, k_cmp.shape[2], dh), lambda b, g, i: (b, g, 0, 0))
    q_spec = pl.BlockSpec((1, Q_BLOCK, gw), lambda b, g, i: (b, i, g))
    return pl.pallas_call(
        _nsa_attn_kernel,
        out_shape=jax.ShapeDtypeStruct(q.shape, jnp.bfloat16),
        grid=(bsz, ng, s // Q_BLOCK),
        in_specs=[q_spec, cmp_spec, cmp_spec, ks_spec, kv_spec, kv_spec, kv_spec, q_spec, q_spec, q_spec],
        out_specs=q_spec,
        compiler_params=pltpu.CompilerParams(
            dimension_semantics=("parallel", "parallel", "arbitrary"),
            vmem_limit_bytes=V7X_VMEM_LIMIT_BYTES),
        name="nsa_attn",
    )(q, k_cmp, v_cmp, ks, vs, kw, vw, g_cmp, g_sel, g_win)


def _nsa_mixer(z, nsa_tabs, cmp_pos, cmp_w1, cmp_w2):
    bsz, s, _ = z.shape
    ng, dh = NSA_KV_HEADS, NSA_DH
    cos, sin, cos_c, sin_c = nsa_tabs
    q, ks, vs, kw, vw, kc, vc, g_cmp, g_sel, g_win = _nsa_prep(z, cos, sin)
    n_blk = s // CMP_STRIDE
    flat = lambda t: t.reshape(bsz * ng, n_blk, CMP_STRIDE * dh)
    k_cmp, v_cmp = _nsa_compress(flat(kc), flat(vc), cmp_pos.reshape(2, 1, CMP_LEN * dh), cmp_w1, cmp_w2,
                                 cos_c, sin_c)
    grouped = lambda t: t.reshape(bsz, ng, n_blk, dh)
    return _nsa_attn(q, grouped(k_cmp), grouped(v_cmp), ks, vs, kw, vw, g_cmp, g_sel, g_win)


def _nsa_tables(cos, sin):
    end = CMP_LEN - 1
    at_ends = lambda t: jnp.pad(t[:, end::CMP_STRIDE, :NSA_DH], ((0, 0), (0, 1), (0, 0)))
    return cos, sin, at_ends(cos), at_ends(sin)


IN_GROUPS = (("ret", GROUP_SIZES[0] + GROUP_SIZES[1], GROUP_SIZES[2], 3072),
             ("rw", GROUP_SIZES[0], GROUP_SIZES[1], 2048),
             ("sc", 0, GROUP_SIZES[0], 1536),
             ("nsa", sum(GROUP_SIZES[:3]), GROUP_SIZES[3], 1536),
             ("gate", sum(GROUP_SIZES[:4]), GROUP_SIZES[4], 8192))
IN_COL0 = {name: sum(g[3] for g in IN_GROUPS[:i]) for i, (name, _, _, _) in enumerate(IN_GROUPS)}
IN_WIDTH = {name: padded for name, _, _, padded in IN_GROUPS}


def _aligned_in_proj(w_in):
    parts = [jnp.pad(w_in[:, :, start:start + width], ((0, 0), (0, 0), (0, padded - width)))
             for _, start, width, padded in IN_GROUPS]
    return jnp.concatenate(parts, axis=-1).astype(jnp.bfloat16)


def _hybrid_mixer(h, bsz, layer, ret_tabs, nsa_tabs, w_all, sc_conv, rw_mu, rw_w0, rw_w2, rw_a0, rw_a2, rw_g2,
                  rw_k_k, rw_k_a, rw_r_k, rw_lnx_g, rw_lnx_b, nsa_cmp_pos, nsa_cmp_w1, nsa_cmp_w2,
                  wb_stack):
    m = h.shape[0]
    s = m // bsz
    group = lambda name: _mm_cols(h, w_all, layer, IN_COL0[name], IN_WIDTH[name])
    per_batch = lambda z: z.reshape(bsz, s, z.shape[1])
    o_a = _short_conv_mixer(group("sc"), sc_conv, s)
    o_b = _rwkv7_mixer(per_batch(group("rw")), rw_mu, rw_w0, rw_w2, rw_a0, rw_a2, rw_g2, rw_k_k, rw_k_a, rw_r_k,
                       rw_lnx_g, rw_lnx_b)
    o_c = _retention_mixer(per_batch(group("ret")), *ret_tabs)
    o_d = _nsa_mixer(per_batch(group("nsa")), nsa_tabs, nsa_cmp_pos, nsa_cmp_w1, nsa_cmp_w2)
    outs = [o.reshape(m, o.shape[-1]) for o in (o_a, o_b, o_c, o_d)]
    assert tuple(o.shape[1] for o in outs) == BR_SIZES
    return _merge(h, w_all, layer, IN_COL0["gate"], outs, wb_stack)


def kernel(x, mem, positions, ln_mix_pre, ln_mix_post, ln_mem_q, ln_mem_kv, ln_mem_post, ln_ffn_pre, ln_ffn_post, w_in, sc_conv, rw_mu, rw_w0, rw_w2, rw_a0, rw_a2, rw_g2, rw_k_k, rw_k_a, rw_r_k, rw_lnx_g, rw_lnx_b, nsa_cmp_pos, nsa_cmp_w1, nsa_cmp_w2, w_branch, w_out, mem_wq, mem_wkv, mem_wo, ffn_w_up, ffn_conv_w, ffn_conv_b, ffn_w_down):
    bsz, s, d = x.shape
    mem_len = mem.shape[1]
    xf = x.reshape(bsz * s, d)
    h = _prenorm(xf, ln_mix_pre[0])
    ret_cos, ret_sin, nsa_cos, nsa_sin = _rope_tables(positions)
    ret_tabs = (ret_cos, ret_sin)
    nsa_tabs = _nsa_tables(nsa_cos, nsa_sin)
    w_all = _aligned_in_proj(w_in)
    bf16 = lambda w: w.astype(jnp.bfloat16)
    wb_b, w_out_b, wq_b, wkv_b, wo_b, w_down_b = (bf16(w_branch), bf16(w_out), bf16(mem_wq), bf16(mem_wkv),
                                                   bf16(mem_wo), bf16(ffn_w_down))
    for l in range(DEPTH):
        merged = _hybrid_mixer(h, bsz, l, ret_tabs, nsa_tabs, w_all, sc_conv[l], rw_mu[l], rw_w0[l], rw_w2[l], rw_a0[l],
                               rw_a2[l], rw_g2[l], rw_k_k[l], rw_k_a[l], rw_r_k[l], rw_lnx_g[l], rw_lnx_b[l],
                               nsa_cmp_pos[l], nsa_cmp_w1[l], nsa_cmp_w2[l], wb_b)
        xf, h = _mm_norm_res(merged, w_out_b, l, xf, ln_mix_post[l], ln_mem_q[l])
        mem_n = _prenorm(mem.reshape(bsz * mem_len, d), ln_mem_kv[l])
        kv = _mm_cols(mem_n, wkv_b, l, 0, 2 * d).astype(jnp.bfloat16)
        k_mem = kv[:, :d].reshape(bsz, mem_len, d)
        v_mem = kv[:, d:].reshape(bsz, mem_len, d)
        o = _xattn(h, wq_b, l, k_mem, v_mem, s)
        xf, h = _mm_norm_res(o, wo_b, l, xf, ln_mem_post[l], ln_ffn_pre[l])
        act = _ffn_up(h, ffn_w_up, l, ffn_conv_w[l], ffn_conv_b[l], s)
        g_next = ln_mix_pre[l + 1] if l + 1 < DEPTH else None
        xf, h = _mm_norm_res(act, w_down_b, l, xf, ln_ffn_post[l], g_next)
    return xf.reshape(bsz, s, d)
```

```python
import functools
import math

import jax
import jax.numpy as jnp
import numpy as np
from jax import lax
from jax.experimental import pallas as pl
from jax.experimental.pallas import tpu as pltpu

D_MODEL = 2048
DEPTH = 2
MEM_HEADS = 4
MEM_HEAD_DIM = D_MODEL // MEM_HEADS
SC_W = D_MODEL // 4
SC_K = 3
RW_HEAD_DIM = 64
RW_W = D_MODEL // 4
RW_HEADS = RW_W // RW_HEAD_DIM
RW_LORA_W = 96
RW_LORA_A = 96
RW_LORA_G = 256
RW_DECAY_SCALE = math.exp(-0.5)
RW_LN_EPS = 64e-5
RET_HEADS = 4
RET_DK = 128
RET_DV = 2 * RET_DK
RET_CHUNK = 128
RET_LN_EPS = 1e-5
NSA_HEADS = 8
NSA_KV_HEADS = 2
NSA_DH = 64
NSA_GROUP = NSA_HEADS // NSA_KV_HEADS
CMP_LEN = 32
CMP_STRIDE = 16
SEL_BLOCK = 64
SEL_TOP = 16
WINDOW = 512
Q_BLOCK = 256
ROPE_THETA = 10000.0
EPS = 1e-6
N_BRANCH = 4

SC_SIZES = (SC_W, SC_W, SC_W)
RW_SIZES = (RW_W, RW_W, RW_W, RW_LORA_W, RW_LORA_A, RW_LORA_G)
RET_SIZES = (RET_HEADS * RET_DK, RET_HEADS * RET_DK, RET_HEADS * RET_DV, RET_HEADS * RET_DV)
NSA_SIZES = (NSA_HEADS * NSA_DH,) + (NSA_KV_HEADS * NSA_DH,) * 6 + (NSA_HEADS * 3,)
GATE_SIZES = (D_MODEL,) * N_BRANCH
GROUP_SIZES = (sum(SC_SIZES), sum(RW_SIZES), sum(RET_SIZES), sum(NSA_SIZES), sum(GATE_SIZES))
BR_SIZES = (SC_W, RW_W, RET_HEADS * RET_DV, NSA_HEADS * NSA_DH)

V7X_VMEM_LIMIT_BYTES = 48 * 1024 * 1024
LANES = 128
_NT = (((1,), (1,)), ((), ()))


def _pick_tile(n, candidates):
    for c in candidates:
        if n % c == 0:
            return c
    return n


def _bdot(a, b):
    return jnp.dot(a.astype(jnp.bfloat16), b.astype(jnp.bfloat16), preferred_element_type=jnp.float32)


def _bdot_nt(a, b):
    return lax.dot_general(a.astype(jnp.bfloat16), b.astype(jnp.bfloat16), _NT,
                           preferred_element_type=jnp.float32)


def _head_sum(x, bd):
    hi = x.astype(jnp.bfloat16)
    lo = (x - hi.astype(jnp.float32)).astype(jnp.bfloat16)
    return (jnp.dot(hi, bd, preferred_element_type=jnp.float32)
            + jnp.dot(lo, bd, preferred_element_type=jnp.float32))


def _mm_kernel(a_ref, w_ref, o_ref, acc_ref):
    k = pl.program_id(2)

    @pl.when(k == 0)
    def _():
        acc_ref[...] = jnp.zeros_like(acc_ref)

    acc_ref[...] += jnp.dot(a_ref[...], w_ref[...], preferred_element_type=jnp.float32)

    @pl.when(k == pl.num_programs(2) - 1)
    def _():
        o_ref[...] = acc_ref[...]


def _mm_cols(a, w_stack, layer, col0, n):
    m, k = a.shape
    tn = next(c for c in (n if n <= 1536 else 1024, 1024, 768, 512, 256, 128) if n % c == 0 and col0 % c == 0)
    tm = _pick_tile(m, (1024, 512, 256, 128, 8))
    tk = _pick_tile(k, (2048, 1280, 1408, 1024, 512, 256, 128))
    j0 = col0 // tn
    return pl.pallas_call(
        _mm_kernel,
        out_shape=jax.ShapeDtypeStruct((m, n), jnp.float32),
        grid=(m // tm, n // tn, k // tk),
        in_specs=[pl.BlockSpec((tm, tk), lambda i, j, kk: (i, kk)),
                  pl.BlockSpec((None, tk, tn), lambda i, j, kk: (layer, kk, j0 + j))],
        out_specs=pl.BlockSpec((tm, tn), lambda i, j, kk: (i, j)),
        scratch_shapes=[pltpu.VMEM((tm, tn), jnp.float32)],
        compiler_params=pltpu.CompilerParams(
            dimension_semantics=("parallel", "parallel", "arbitrary"),
            vmem_limit_bytes=V7X_VMEM_LIMIT_BYTES),
        name="mm",
    )(a.astype(jnp.bfloat16), w_stack)


def _rms(y, g):
    return y * lax.rsqrt(jnp.mean(y * y, axis=-1, keepdims=True) + EPS) * g


def _prenorm_kernel(x_ref, g_ref, h_ref):
    h_ref[...] = _rms(x_ref[...], g_ref[...]).astype(h_ref.dtype)


def _prenorm(x, g):
    m, d = x.shape
    tm = _pick_tile(m, (512, 256, 128, 8))
    return pl.pallas_call(
        _prenorm_kernel,
        out_shape=jax.ShapeDtypeStruct((m, d), jnp.bfloat16),
        grid=(m // tm,),
        in_specs=[pl.BlockSpec((tm, d), lambda i: (i, 0)), pl.BlockSpec((1, d), lambda i: (0, 0))],
        out_specs=pl.BlockSpec((tm, d), lambda i: (i, 0)),
        compiler_params=pltpu.CompilerParams(dimension_semantics=("parallel",)),
        name="prenorm",
    )(x, g.reshape(1, d))


def _mm_norm_res_kernel(emit_h, whole_k, a_ref, w_ref, res_ref, g_ref, *rest):
    if emit_h:
        g2_ref, x_ref, h_ref = rest[:3]
    else:
        x_ref = rest[0]

    def finish(y):
        x_new = res_ref[...] + _rms(y, g_ref[...])
        x_ref[...] = x_new
        if emit_h:
            h_ref[...] = _rms(x_new, g2_ref[...]).astype(h_ref.dtype)

    if whole_k:
        finish(jnp.dot(a_ref[...], w_ref[...], preferred_element_type=jnp.float32))
        return
    acc_ref = rest[-1]
    k = pl.program_id(1)

    @pl.when(k == 0)
    def _():
        acc_ref[...] = jnp.zeros_like(acc_ref)

    acc_ref[...] += jnp.dot(a_ref[...], w_ref[...], preferred_element_type=jnp.float32)

    @pl.when(k == pl.num_programs(1) - 1)
    def _():
        finish(acc_ref[...])


def _mm_norm_res(a, w_stack, layer, res, g, g_next):
    m, k = a.shape
    d = w_stack.shape[2]
    tk = _pick_tile(k, (2048, 1408, 1024, 512))
    tm = 512
    whole_k = tk == k
    emit_h = g_next is not None
    row = pl.BlockSpec((tm, d), lambda i, kk: (i, 0))
    vec = pl.BlockSpec((1, d), lambda i, kk: (0, 0))
    w_spec = (pl.BlockSpec((None, tk, d), lambda i, kk: (layer, 0, 0), pipeline_mode=pl.Buffered(1)) if whole_k
              else pl.BlockSpec((None, tk, d), lambda i, kk: (layer, kk, 0)))
    x_sd = jax.ShapeDtypeStruct((m, d), jnp.float32)
    h_sd = jax.ShapeDtypeStruct((m, d), jnp.bfloat16)
    out = pl.pallas_call(
        functools.partial(_mm_norm_res_kernel, emit_h, whole_k),
        out_shape=(x_sd, h_sd) if emit_h else x_sd,
        grid=(m // tm, k // tk),
        in_specs=[pl.BlockSpec((tm, tk), lambda i, kk: (i, kk)), w_spec, row, vec] + ([vec] if emit_h else []),
        out_specs=(row, row) if emit_h else row,
        scratch_shapes=[] if whole_k else [pltpu.VMEM((tm, d), jnp.float32)],
        compiler_params=pltpu.CompilerParams(
            dimension_semantics=("parallel", "arbitrary"), vmem_limit_bytes=V7X_VMEM_LIMIT_BYTES),
        name="mm_norm_res",
    )(a.astype(jnp.bfloat16), w_stack, res, g.reshape(1, d), *([g_next.reshape(1, d)] if emit_h else []))
    return out if emit_h else (out, None)


FFN_TM = 1024
FFN_TN = 512
HALO = 8


def _shift_rows(u, prev, n):
    rolled = pltpu.roll(u, n, axis=0)
    rows = lax.broadcasted_iota(jnp.int32, u.shape, 0)
    for r in range(n):
        rolled = jnp.where(rows == r, prev[HALO - n + r:HALO - n + r + 1, :], rolled)
    return rolled


def _ffn_up_kernel(seq_len, h_ref, halo_ref, wa_ref, wb_ref, cwa_ref, cwb_ref, ba_ref, bb_ref, o_ref):
    i = pl.program_id(0)
    h = h_ref[...]
    at_start = (i * FFN_TM) % seq_len == 0
    halo = jnp.where(at_start, jnp.zeros_like(halo_ref[...]), halo_ref[...])

    def conv(w_ref, cw_ref, b_ref):
        w = w_ref[...].astype(jnp.bfloat16)
        u = jnp.dot(h, w, preferred_element_type=jnp.float32)
        up = jnp.dot(halo, w, preferred_element_type=jnp.float32)
        cw = cw_ref[...]
        y = cw[0:1] * _shift_rows(u, up, 2)
        y = y + cw[1:2] * _shift_rows(u, up, 1)
        y = y + cw[2:3] * u
        return y + b_ref[...]

    a = conv(wa_ref, cwa_ref, ba_ref)
    b = conv(wb_ref, cwb_ref, bb_ref)
    o_ref[...] = (jax.nn.gelu(a, approximate=True) * b).astype(o_ref.dtype)


def _ffn_up(h, w_up_stack, layer, w_conv, b_conv, seq_len):
    m, d = h.shape
    f = w_up_stack.shape[2] // 2
    nj = f // FFN_TN
    b_conv = b_conv.reshape(1, 2 * f)
    tiles_per_halo = FFN_TM // HALO
    return pl.pallas_call(
        functools.partial(_ffn_up_kernel, seq_len),
        out_shape=jax.ShapeDtypeStruct((m, f), jnp.bfloat16),
        grid=(m // FFN_TM, nj),
        in_specs=[pl.BlockSpec((FFN_TM, d), lambda i, j: (i, 0)),
                  pl.BlockSpec((HALO, d), lambda i, j: (jnp.maximum(i * tiles_per_halo - 1, 0), 0)),
                  pl.BlockSpec((None, d, FFN_TN), lambda i, j: (layer, 0, j)),
                  pl.BlockSpec((None, d, FFN_TN), lambda i, j: (layer, 0, j + nj)),
                  pl.BlockSpec((3, FFN_TN), lambda i, j: (0, j)),
                  pl.BlockSpec((3, FFN_TN), lambda i, j: (0, j + nj)),
                  pl.BlockSpec((1, FFN_TN), lambda i, j: (0, j)),
                  pl.BlockSpec((1, FFN_TN), lambda i, j: (0, j + nj))],
        out_specs=pl.BlockSpec((FFN_TM, FFN_TN), lambda i, j: (i, j)),
        compiler_params=pltpu.CompilerParams(
            dimension_semantics=("parallel", "arbitrary"), vmem_limit_bytes=V7X_VMEM_LIMIT_BYTES),
        name="ffn_up",
    )(h, h, w_up_stack, w_up_stack, w_conv, w_conv, b_conv, b_conv)


MERGE_TM = 512
MERGE_TN = 512


def _merge_kernel(h_ref, wg0, wg1, wg2, wg3, o0, o1, o2, o3, wb0, wb1, wb2, wb3, out_ref):
    h = h_ref[...]
    acc = None
    for wg, o, wb in ((wg0, o0, wb0), (wg1, o1, wb1), (wg2, o2, wb2), (wg3, o3, wb3)):
        gate = jax.nn.sigmoid(jnp.dot(h, wg[...], preferred_element_type=jnp.float32))
        term = gate * jnp.dot(o[...], wb[...], preferred_element_type=jnp.float32)
        acc = term if acc is None else acc + term
    out_ref[...] = acc.astype(out_ref.dtype)


def _merge(h, w_stack, layer, gate_col0, outs, wb_stack):
    m, d = h.shape
    nj = d // MERGE_TN
    j0 = gate_col0 // MERGE_TN
    assert gate_col0 % MERGE_TN == 0
    gate_specs = [pl.BlockSpec((None, d, MERGE_TN), functools.partial(lambda b, i, j: (layer, 0, j0 + b * nj + j), b))
                  for b in range(N_BRANCH)]
    o_specs = [pl.BlockSpec((MERGE_TM, o.shape[1]), lambda i, j: (i, 0)) for o in outs]
    row0 = np.cumsum([0] + [o.shape[1] for o in outs])
    assert all(r % o.shape[1] == 0 for r, o in zip(row0, outs))
    wb_specs = [pl.BlockSpec((None, o.shape[1], MERGE_TN),
                             functools.partial(lambda rb, i, j: (layer, rb, j), int(r) // o.shape[1]))
                for r, o in zip(row0, outs)]
    return pl.pallas_call(
        _merge_kernel,
        out_shape=jax.ShapeDtypeStruct((m, d), jnp.bfloat16),
        grid=(m // MERGE_TM, nj),
        in_specs=[pl.BlockSpec((MERGE_TM, d), lambda i, j: (i, 0))] + gate_specs + o_specs + wb_specs,
        out_specs=pl.BlockSpec((MERGE_TM, MERGE_TN), lambda i, j: (i, j)),
        compiler_params=pltpu.CompilerParams(
            dimension_semantics=("parallel", "arbitrary"), vmem_limit_bytes=V7X_VMEM_LIMIT_BYTES),
        name="merge",
    )(h, *([w_stack] * N_BRANCH), *[o.astype(jnp.bfloat16) for o in outs], *([wb_stack] * N_BRANCH))


XATT_TM = 512


def _xattn_kernel(h_ref, wq_ref, k_ref, v_ref, o_ref):
    q = jnp.dot(h_ref[...], wq_ref[...], preferred_element_type=jnp.float32)
    q = (q * (MEM_HEAD_DIM ** -0.5)).astype(jnp.bfloat16)
    for hh in range(MEM_HEADS):
        sl = slice(hh * MEM_HEAD_DIM, (hh + 1) * MEM_HEAD_DIM)
        s = lax.dot_general(q[:, sl], k_ref[0, :, sl], _NT, preferred_element_type=jnp.float32)
        p = jnp.exp(s - jnp.max(s, axis=-1, keepdims=True))
        p = p / jnp.sum(p, axis=-1, keepdims=True)
        o_ref[:, sl] = jnp.dot(p.astype(jnp.bfloat16), v_ref[0, :, sl],
                               preferred_element_type=jnp.float32).astype(o_ref.dtype)


def _xattn(h, wq_stack, layer, k, v, seq_len):
    m, d = h.shape
    mem_len = k.shape[1]
    per_batch = seq_len // XATT_TM
    kv_spec = pl.BlockSpec((1, mem_len, d), lambda i: (i // per_batch, 0, 0))
    return pl.pallas_call(
        _xattn_kernel,
        out_shape=jax.ShapeDtypeStruct((m, d), jnp.bfloat16),
        grid=(m // XATT_TM,),
        in_specs=[pl.BlockSpec((XATT_TM, d), lambda i: (i, 0)),
                  pl.BlockSpec((None, d, d), lambda i: (layer, 0, 0)), kv_spec, kv_spec],
        out_specs=pl.BlockSpec((XATT_TM, d), lambda i: (i, 0)),
        compiler_params=pltpu.CompilerParams(
            dimension_semantics=("parallel",), vmem_limit_bytes=V7X_VMEM_LIMIT_BYTES),
        name="xattn",
    )(h, wq_stack, k, v)


def _rope_freqs(d):
    return ROPE_THETA ** (-jnp.arange(0, d, 2, dtype=jnp.float32) / d)


def _retnet_freqs(d):
    return 1.0 / (ROPE_THETA ** jnp.linspace(0.0, 1.0, d // 2, dtype=jnp.float32))


WKV_CHUNK = 64


def _rwkv7_kernel(z_ref, mu_ref, w0_ref, a0_ref, kk_ref, ka_ref, rk_ref, lng_ref, lnb_ref,
                  ww2_ref, wa2_ref, wg2_ref, o_ref, s_ref, prev_ref, y_ref):
    c = WKV_CHUNK
    hd = RW_HEAD_DIM
    w = RW_W
    nb = z_ref.shape[0]
    f32, bf16 = jnp.float32, jnp.bfloat16

    @pl.when(pl.program_id(0) == 0)
    def _():
        s_ref[...] = jnp.zeros_like(s_ref)
        prev_ref[...] = jnp.zeros_like(prev_ref)

    lane_h = lax.broadcasted_iota(jnp.int32, (w, w), 0) // hd
    lane_w = lax.broadcasted_iota(jnp.int32, (w, w), 1) // hd
    head_bd = jnp.where(lane_h == lane_w, 1.0, 0.0).astype(bf16)
    row = lax.broadcasted_iota(jnp.int32, (c, c), 0)
    col = lax.broadcasted_iota(jnp.int32, (c, c), 1)
    tri = jnp.where(row >= col, 1.0, 0.0).astype(bf16)
    eye = jnp.where(row == col, 1.0, 0.0).astype(f32)
    r2 = lax.broadcasted_iota(jnp.int32, (2 * c, 2 * c), 0)
    c2 = lax.broadcasted_iota(jnp.int32, (2 * c, 2 * c), 1)
    rr = jnp.where(r2 >= c, r2 - c, r2)
    cc = jnp.where(c2 >= c, c2 - c, c2)
    tri_mask = cc < rr + jnp.where(r2 >= c, 1, 0)

    def prepare(b):
        z = z_ref[b]
        rows = lax.broadcasted_iota(jnp.int32, z.shape, 0)
        z_prev = jnp.where(rows == 0, prev_ref[b], pltpu.roll(z, 1, axis=0))
        prev_ref[b] = z[c - 1:c, :]
        z = z + (z_prev - z) * mu_ref[...]
        r, k, v, lora = z[:, 0:w], z[:, w:2 * w], z[:, 2 * w:3 * w], z[:, 3 * w:4 * w]
        logw = -RW_DECAY_SCALE * jax.nn.sigmoid(w0_ref[...] + _bdot(jnp.tanh(lora), ww2_ref[...]))
        a = jax.nn.sigmoid(a0_ref[...] + _bdot(lora, wa2_ref[...]))
        g = _bdot(jax.nn.sigmoid(lora), wg2_ref[...])
        kk = k * kk_ref[...]
        kk = kk / jnp.maximum(jnp.sqrt(_head_sum(kk * kk, head_bd)), 1e-12)
        k = k * (1.0 + (a - 1.0) * ka_ref[...])
        hi = logw.astype(bf16)
        rem = logw - hi.astype(f32)
        mid = rem.astype(bf16)
        lo = (rem - mid.astype(f32)).astype(bf16)
        cum = (jnp.dot(tri, hi, preferred_element_type=f32) + jnp.dot(tri, mid, preferred_element_type=f32)
               + jnp.dot(tri, lo, preferred_element_type=f32))
        kka = kk * a
        e_neg = jnp.exp(-cum)
        last = cum[c - 1:c, :]
        dec = jnp.exp(last - cum)
        return dict(r=r, k=k, v=v, g=g, vb=v.astype(bf16), g_last=jnp.exp(last),
                    qh=(kk * jnp.exp(cum - logw)).astype(bf16), rh=(r * jnp.exp(cum)).astype(bf16),
                    bh=(kka * e_neg).astype(bf16), kh=(k * e_neg).astype(bf16),
                    bd=(kka * dec).astype(bf16), kd=(k * dec).astype(bf16))

    pre = [prepare(b) for b in range(nb)]

    units = [(b, h) for b in range(nb) for h in range(RW_HEADS)]
    n_units = range(len(units))
    part = lambda name, i: pre[units[i][0]][name][:, units[i][1] * hd:(units[i][1] + 1) * hd]
    a1 = [jnp.concatenate([part("qh", i), part("rh", i)], axis=0) for i in n_units]
    b1 = [jnp.concatenate([part("bh", i), part("kh", i)], axis=0) for i in n_units]
    ss = [jnp.where(tri_mask, lax.dot_general(a1[i], b1[i], _NT, preferred_element_type=f32), 0.0)
          for i in n_units]
    s_old = [s_ref[i] for i in n_units]
    qr = [lax.dot_general(a1[i], s_old[i].astype(bf16), _NT, preferred_element_type=f32) for i in n_units]
    lm = [_bdot(ss[i][:, c:2 * c], part("vb", i)) for i in n_units]
    n = [-ss[i][0:c, 0:c] for i in n_units]
    t = [eye + n[i] for i in n_units]
    for _ in range(5):
        n = [_bdot(n[i], n[i]) for i in n_units]
        t = [t[i] + _bdot(t[i], n[i]) for i in n_units]
    u = [-_bdot(t[i], qr[i][0:c] + lm[i][0:c]) for i in n_units]
    for i in n_units:
        b, h = units[i]
        y_ref[b, :, h * hd:(h + 1) * hd] = qr[i][c:2 * c] + lm[i][c:2 * c] + _bdot(ss[i][c:2 * c, 0:c], u[i])
    for i in n_units:
        zt = jnp.concatenate([u[i], part("v", i)], axis=0).T
        x = jnp.concatenate([part("bd", i), part("kd", i)], axis=0)
        s_ref[i] = s_old[i] * part("g_last", i) + _bdot(zt, x)

    for b in range(nb):
        p = pre[b]
        y = y_ref[b]
        mean = _head_sum(y, head_bd) * (1.0 / hd)
        yc = y - mean
        var = _head_sum(yc * yc, head_bd) * (1.0 / hd)
        yn = yc * lax.rsqrt(var + RW_LN_EPS) * lng_ref[...] + lnb_ref[...]
        bonus = _head_sum(p["r"] * p["k"] * rk_ref[...], head_bd) * p["v"]
        o_ref[b] = ((yn + bonus) * p["g"]).astype(o_ref.dtype)


def _rwkv7_mixer(z, mu, w0, w_w2, a0, w_a2, w_g2, k_k, k_a, r_k, lnx_g, lnx_b):
    bsz, s, zw = z.shape
    w = RW_W
    n_lora = RW_LORA_W + RW_LORA_A + RW_LORA_G
    row = lambda t: t.reshape(1, -1)
    mu = jnp.pad(mu, (0, zw - mu.shape[0])).reshape(1, zw)
    o_w, o_a = RW_LORA_W, RW_LORA_W + RW_LORA_A
    pad_rows = lambda m, lo: jnp.pad(m, ((lo, w - lo - m.shape[0]), (0, 0))).astype(jnp.bfloat16)
    vec = pl.BlockSpec((1, w), lambda c: (0, 0))
    mat = pl.BlockSpec((w, w), lambda c: (0, 0))
    assert zw == 4 * w and n_lora <= w
    return pl.pallas_call(
        _rwkv7_kernel,
        out_shape=jax.ShapeDtypeStruct((bsz, s, w), jnp.bfloat16),
        grid=(s // WKV_CHUNK,),
        in_specs=[pl.BlockSpec((bsz, WKV_CHUNK, zw), lambda c: (0, c, 0)),
                  pl.BlockSpec((1, zw), lambda c: (0, 0))] + [vec] * 7 + [mat] * 3,
        out_specs=pl.BlockSpec((bsz, WKV_CHUNK, w), lambda c: (0, c, 0)),
        scratch_shapes=[pltpu.VMEM((bsz * RW_HEADS, RW_HEAD_DIM, RW_HEAD_DIM), jnp.float32),
                        pltpu.VMEM((bsz, 1, zw), jnp.float32),
                        pltpu.VMEM((bsz, WKV_CHUNK, w), jnp.float32)],
        compiler_params=pltpu.CompilerParams(dimension_semantics=("arbitrary",)),
        name="rwkv7",
    )(z, mu, row(w0), row(a0), row(k_k), row(k_a), row(r_k), row(lnx_g), row(lnx_b),
      pad_rows(w_w2, 0), pad_rows(w_a2, o_w), pad_rows(w_g2, o_a))


RET_LOG_DECAY = tuple(math.log(1.0 - 2.0 ** (-5.0 - h)) for h in range(RET_HEADS))


def _retention_kernel(q_ref, k_ref, v_ref, g_ref, cos_ref, sin_ref, o_ref, s_ref):
    c, dk, dv = RET_CHUNK, RET_DK, RET_DV
    f32 = jnp.float32

    @pl.when(pl.program_id(1) == 0)
    def _():
        s_ref[...] = jnp.zeros_like(s_ref)

    cos = cos_ref[0]
    sin = sin_ref[0]
    diff = (lax.broadcasted_iota(jnp.int32, (c, c), 0) - lax.broadcasted_iota(jnp.int32, (c, c), 1)).astype(f32)
    tok = lax.broadcasted_iota(jnp.int32, (c, 1), 0).astype(f32)
    rope = lambda x: x * cos + pltpu.roll(x, dk // 2, axis=1) * sin
    for h in range(RET_HEADS):
        lg = RET_LOG_DECAY[h]
        q = rope(q_ref[0, :, h * dk:(h + 1) * dk])
        k = rope(k_ref[0, :, h * dk:(h + 1) * dk]) * (dk ** -0.5)
        v = v_ref[0, :, h * dv:(h + 1) * dv]
        decay_in = jnp.where(diff >= 0, jnp.exp(jnp.maximum(diff, 0.0) * lg), 0.0)
        scores = _bdot_nt(q, k) * decay_in
        state = s_ref[h]
        o = _bdot(scores, v) + _bdot(q * jnp.exp((tok + 1.0) * lg), state)
        s_ref[h] = state * math.exp(c * lg) + _bdot((k * jnp.exp((c - 1.0 - tok) * lg)).T, v)
        mu = jnp.mean(o, axis=-1, keepdims=True)
        oc = o - mu
        var = jnp.mean(oc * oc, axis=-1, keepdims=True)
        gate = g_ref[0, :, h * dv:(h + 1) * dv]
        o_ref[0, :, h * dv:(h + 1) * dv] = (oc * lax.rsqrt(var + RET_LN_EPS) * gate * jax.nn.sigmoid(gate)
                                            ).astype(o_ref.dtype)


def _retention_mixer(z, cos, sin):
    bsz, s, _ = z.shape
    qk_w, v_w = RET_HEADS * RET_DK, RET_HEADS * RET_DV
    c = RET_CHUNK
    tab = pl.BlockSpec((1, c, RET_DK), lambda b, i: (b, i, 0))
    return pl.pallas_call(
        _retention_kernel,
        out_shape=jax.ShapeDtypeStruct((bsz, s, v_w), jnp.bfloat16),
        grid=(bsz, s // c),
        in_specs=[pl.BlockSpec((1, c, qk_w), lambda b, i: (b, i, 0)),
                  pl.BlockSpec((1, c, qk_w), lambda b, i: (b, i, 1)),
                  pl.BlockSpec((1, c, v_w), lambda b, i: (b, i, 1)),
                  pl.BlockSpec((1, c, v_w), lambda b, i: (b, i, 2)), tab, tab],
        out_specs=pl.BlockSpec((1, c, v_w), lambda b, i: (b, i, 0)),
        scratch_shapes=[pltpu.VMEM((RET_HEADS, RET_DK, RET_DV), jnp.float32)],
        compiler_params=pltpu.CompilerParams(dimension_semantics=("parallel", "arbitrary")),
        name="retention",
    )(z, z, z, z, cos, sin)


TRIG_TM = 1024


def _trig_kernel(pos_ref, freq_ref, rc_ref, rs_ref, nc_ref, ns_ref):
    ang = pos_ref[...] * freq_ref[...]
    c, sn = jnp.cos(ang), jnp.sin(ang)
    n_ret, n_nsa = RET_DK // 2, NSA_DH // 2
    rc_ref[...] = jnp.concatenate([c[:, :n_ret]] * 2, axis=1)
    rs_ref[...] = jnp.concatenate([-sn[:, :n_ret], sn[:, :n_ret]], axis=1)
    nsa_c, nsa_s = c[:, n_ret:n_ret + n_nsa], sn[:, n_ret:n_ret + n_nsa]
    nc_ref[...] = jnp.concatenate([nsa_c, nsa_c] * NSA_HEADS, axis=1)
    ns_ref[...] = jnp.concatenate([-nsa_s, nsa_s] * NSA_HEADS, axis=1)


def _rope_tables(positions):
    bsz, s = positions.shape
    m = bsz * s
    n_ret, n_nsa = RET_DK // 2, NSA_DH // 2
    assert n_ret + n_nsa <= LANES
    freq = jnp.concatenate([_retnet_freqs(RET_DK), _rope_freqs(NSA_DH),
                            jnp.zeros((LANES - n_ret - n_nsa,), jnp.float32)]).reshape(1, LANES)
    tm = _pick_tile(m, (TRIG_TM, 512, 256, 128, 8))
    spec = lambda w: pl.BlockSpec((tm, w), lambda i: (i, 0))
    sd = lambda w: jax.ShapeDtypeStruct((m, w), jnp.float32)
    nsa_w = NSA_HEADS * NSA_DH
    tabs = pl.pallas_call(
        _trig_kernel,
        out_shape=(sd(RET_DK), sd(RET_DK), sd(nsa_w), sd(nsa_w)),
        grid=(m // tm,),
        in_specs=[spec(1), pl.BlockSpec((1, LANES), lambda i: (0, 0))],
        out_specs=(spec(RET_DK), spec(RET_DK), spec(nsa_w), spec(nsa_w)),
        compiler_params=pltpu.CompilerParams(dimension_semantics=("parallel",)),
        name="rope_tables",
    )(positions.astype(jnp.float32).reshape(m, 1), freq)
    return [t.reshape(bsz, s, t.shape[1]) for t in tabs]


SC_TM = 512


def _short_conv_kernel(seq_len, b_ref, c_ref, x_ref, ch_ref, xh_ref, w_ref, o_ref):
    at_start = (pl.program_id(0) * SC_TM) % seq_len == 0
    u = c_ref[...] * x_ref[...]
    up = jnp.where(at_start, 0.0, ch_ref[...] * xh_ref[...])
    w = w_ref[...]
    y = w[0:1] * _shift_rows(u, up, 2) + w[1:2] * _shift_rows(u, up, 1) + w[2:3] * u
    o_ref[...] = (b_ref[...] * y).astype(o_ref.dtype)


def _short_conv_mixer(z, w_conv, seq_len):
    m = z.shape[0]
    w = SC_W
    per_halo = SC_TM // HALO
    tile = lambda j: pl.BlockSpec((SC_TM, w), lambda i: (i, j))
    halo = lambda j: pl.BlockSpec((HALO, w), lambda i: (jnp.maximum(i * per_halo - 1, 0), j))
    return pl.pallas_call(
        functools.partial(_short_conv_kernel, seq_len),
        out_shape=jax.ShapeDtypeStruct((m, w), jnp.bfloat16),
        grid=(m // SC_TM,),
        in_specs=[tile(0), tile(1), tile(2), halo(1), halo(2), pl.BlockSpec((SC_K, w), lambda i: (0, 0))],
        out_specs=pl.BlockSpec((SC_TM, w), lambda i: (i, 0)),
        compiler_params=pltpu.CompilerParams(dimension_semantics=("parallel",)),
        name="short_conv",
    )(z, z, z, z, z, w_conv)


NSA_KEY_TILE = 512
NSA_WIN_SPAN = WINDOW + Q_BLOCK
NSA_PREP_TM = 512
NEG_BIG = -1e30
NSA_Q_W = NSA_HEADS * NSA_DH
NSA_KV_W = NSA_KV_HEADS * NSA_DH
NSA_GATE_OFF = NSA_Q_W + 6 * NSA_KV_W
NSA_VAL_W = LANES


def _rope_lanes(x, cos, sin):
    width = x.shape[1]
    half = NSA_DH // 2
    lane = lax.broadcasted_iota(jnp.int32, x.shape, 1)
    other = jnp.where(lane % NSA_DH < half, pltpu.roll(x, width - half, axis=1), pltpu.roll(x, half, axis=1))
    return x * cos + other * sin


def _nsa_prep_kernel(z_ref, cos_ref, sin_ref, q_ref, ks_ref, vs_ref, kw_ref, vw_ref, kc_ref, vc_ref,
                     gc_ref, gs_ref, gw_ref):
    cos, sin = cos_ref[0], sin_ref[0]
    z = z_ref[0]
    q_ref[0] = (_rope_lanes(z[:, 0:NSA_Q_W], cos, sin) * (NSA_DH ** -0.5)).astype(q_ref.dtype)
    kv = lambda i: z[:, NSA_Q_W + i * NSA_KV_W:NSA_Q_W + (i + 1) * NSA_KV_W]
    cos_kv, sin_kv = cos[:, 0:NSA_KV_W], sin[:, 0:NSA_KV_W]
    pieces = ((kc_ref, kv(0)), (vc_ref, kv(1)), (ks_ref, _rope_lanes(kv(2), cos_kv, sin_kv)), (vs_ref, kv(3)),
              (kw_ref, _rope_lanes(kv(4), cos_kv, sin_kv)), (vw_ref, kv(5)))
    n_sel = ks_ref.shape[3] - NSA_DH
    tok = pl.program_id(1) * NSA_PREP_TM + lax.broadcasted_iota(jnp.int32, (NSA_PREP_TM, n_sel), 0)
    block_1hot = jnp.where(tok // SEL_BLOCK == lax.broadcasted_iota(jnp.int32, (NSA_PREP_TM, n_sel), 1), 1.0, 0.0)
    ones_col = jnp.where(lax.broadcasted_iota(jnp.int32, (NSA_PREP_TM, NSA_VAL_W - NSA_DH), 1) == 0, 1.0, 0.0)
    for ref, val in pieces:
        for g in range(NSA_KV_HEADS):
            val_g = val[:, g * NSA_DH:(g + 1) * NSA_DH]
            if ref is ks_ref:
                val_g = jnp.concatenate([val_g, block_1hot], axis=1)
            elif ref is vs_ref or ref is vw_ref:
                val_g = jnp.concatenate([val_g, ones_col], axis=1)
            ref[0, g] = val_g.astype(ref.dtype)
    gate = jax.nn.sigmoid(z[:, NSA_GATE_OFF:NSA_GATE_OFF + LANES])
    src = lax.broadcasted_iota(jnp.int32, (LANES, NSA_Q_W), 0)
    head = lax.broadcasted_iota(jnp.int32, (LANES, NSA_Q_W), 1) // NSA_DH
    for j, ref in enumerate((gc_ref, gs_ref, gw_ref)):
        expand = jnp.where(src == 3 * head + j, 1.0, 0.0).astype(jnp.bfloat16)
        ref[0] = _head_sum(gate, expand)


def _nsa_prep(z, cos, sin):
    bsz, s, zw = z.shape
    tm = NSA_PREP_TM
    wide = pl.BlockSpec((1, tm, NSA_Q_W), lambda b, i: (b, i, 0))
    grp = pl.BlockSpec((1, NSA_KV_HEADS, tm, NSA_DH), lambda b, i: (b, 0, i, 0))
    grp_sd = lambda dt: jax.ShapeDtypeStruct((bsz, NSA_KV_HEADS, s, NSA_DH), dt)
    ks_w = NSA_DH + s // SEL_BLOCK
    ks_spec = pl.BlockSpec((1, NSA_KV_HEADS, tm, ks_w), lambda b, i: (b, 0, i, 0))
    ks_sd = jax.ShapeDtypeStruct((bsz, NSA_KV_HEADS, s, ks_w), jnp.bfloat16)
    val_spec = pl.BlockSpec((1, NSA_KV_HEADS, tm, NSA_VAL_W), lambda b, i: (b, 0, i, 0))
    val_sd = jax.ShapeDtypeStruct((bsz, NSA_KV_HEADS, s, NSA_VAL_W), jnp.bfloat16)
    wide_sd = lambda dt: jax.ShapeDtypeStruct((bsz, s, NSA_Q_W), dt)
    bf16, f32 = jnp.bfloat16, jnp.float32
    return pl.pallas_call(
        _nsa_prep_kernel,
        out_shape=(wide_sd(bf16), ks_sd, val_sd, grp_sd(bf16), val_sd, grp_sd(f32), grp_sd(f32),
                   wide_sd(f32), wide_sd(f32), wide_sd(f32)),
        grid=(bsz, s // tm),
        in_specs=[pl.BlockSpec((1, tm, zw), lambda b, i: (b, i, 0)), wide, wide],
        out_specs=(wide, ks_spec, val_spec, grp, val_spec, grp, grp, wide, wide, wide),
        compiler_params=pltpu.CompilerParams(dimension_semantics=("parallel", "parallel")),
        name="nsa_prep",
    )(z, cos, sin)


def _nsa_compress_kernel(hk_ref, hv_ref, pos_ref, w1_ref, w2_ref, cos_ref, sin_ref, kc_ref, vc_ref):
    half = w1_ref.shape[1] // 2
    n = hk_ref.shape[1]
    for i, (h_ref, o_ref) in enumerate(((hk_ref, kc_ref), (hv_ref, vc_ref))):
        h = h_ref[0]
        first = _bdot(h, w1_ref[i, 0:half, :])
        second = pltpu.roll(_bdot(h, w1_ref[i, half:2 * half, :]), n - 1, axis=0)
        bias = _bdot(jnp.broadcast_to(pos_ref[i], (8, 2 * half)), w1_ref[i])[0:1]
        out = _bdot(jax.nn.gelu(first + second + bias, approximate=True), w2_ref[i])
        if i == 0:
            src = lax.broadcasted_iota(jnp.int32, (NSA_DH, NSA_DH), 0)
            dst = lax.broadcasted_iota(jnp.int32, (NSA_DH, NSA_DH), 1)
            swap = jnp.where(src == (dst + NSA_DH // 2) % NSA_DH, 1.0, 0.0).astype(jnp.bfloat16)
            out = out * cos_ref[0] + _head_sum(out, swap) * sin_ref[0]
        o_ref[0] = out.astype(o_ref.dtype)


def _nsa_compress(hk, hv, pos_flat, w1, w2, cos_c, sin_c):
    bg, n, hw = hk.shape
    ng = NSA_KV_HEADS
    blk = pl.BlockSpec((1, n, hw), lambda i: (i, 0, 0))
    tab = pl.BlockSpec((1, n, NSA_DH), lambda i: (i // ng, 0, 0))
    out = pl.BlockSpec((1, n, NSA_DH), lambda i: (i, 0, 0))
    whole = lambda a: pl.BlockSpec(a.shape, lambda i: (0,) * a.ndim)
    sd = jax.ShapeDtypeStruct((bg, n, NSA_DH), jnp.bfloat16)
    return pl.pallas_call(
        _nsa_compress_kernel,
        out_shape=(sd, sd),
        grid=(bg,),
        in_specs=[blk, blk, whole(pos_flat), whole(w1), whole(w2), tab, tab],
        out_specs=(out, out),
        compiler_params=pltpu.CompilerParams(
            dimension_semantics=("parallel",), vmem_limit_bytes=V7X_VMEM_LIMIT_BYTES),
        name="nsa_compress",
    )(hk, hv, pos_flat, w1, w2, cos_c, sin_c)


def _nsa_attn_kernel(q_ref, kc_ref, vc_ref, ks_ref, vs_ref, kw_ref, vw_ref, gc_ref, gs_ref, gw_ref, o_ref):
    f32, bf16 = jnp.float32, jnp.bfloat16
    qi = pl.program_id(2)
    q0 = qi * Q_BLOCK
    hpg, dh, tk = NSA_GROUP, NSA_DH, NSA_KEY_TILE
    rows = hpg * Q_BLOCK
    n_sel = ks_ref.shape[3] - dh
    n_cmp = kc_ref.shape[2]
    q_all = q_ref[0]
    q = jnp.concatenate([q_all[:, h * dh:(h + 1) * dh] for h in range(hpg)], axis=0)
    by_head = lambda o: jnp.concatenate([o[h * Q_BLOCK:(h + 1) * Q_BLOCK] for h in range(hpg)], axis=1)

    tq_c = q0 + lax.broadcasted_iota(jnp.int32, (Q_BLOCK, n_cmp), 0)
    c_end = lax.broadcasted_iota(jnp.int32, (Q_BLOCK, n_cmp), 1) * CMP_STRIDE + (CMP_LEN - 1)
    c_ok = jnp.concatenate([jnp.where(c_end <= tq_c, 1.0, 0.0)] * hpg, axis=0) > 0.5
    s = jnp.where(c_ok, lax.dot_general(q, kc_ref[0, 0], _NT, preferred_element_type=f32), NEG_BIG)
    m = jnp.max(s, axis=-1, keepdims=True)
    e = jnp.where(c_ok, jnp.exp(s - m), 0.0)
    p = e / jnp.maximum(jnp.sum(e, axis=-1, keepdims=True), 1e-30)
    o_cmp = jnp.dot(p.astype(bf16), vc_ref[0, 0], preferred_element_type=f32)

    p_sum = p[0:Q_BLOCK]
    for h in range(1, hpg):
        p_sum = p_sum + p[h * Q_BLOCK:(h + 1) * Q_BLOCK]
    c_start = lax.broadcasted_iota(jnp.int32, (n_sel, n_cmp), 1) * CMP_STRIDE
    j_blk = lax.broadcasted_iota(jnp.int32, (n_sel, n_cmp), 0)
    overlap_t = jnp.where((c_start < (j_blk + 1) * SEL_BLOCK) & (c_start + CMP_LEN > j_blk * SEL_BLOCK),
                          1.0, 0.0).astype(bf16)
    p_hi = p_sum.astype(bf16)
    p_lo = (p_sum - p_hi.astype(f32)).astype(bf16)
    imp = (lax.dot_general(overlap_t, p_hi, _NT, preferred_element_type=f32)
           + lax.dot_general(overlap_t, p_lo, _NT, preferred_element_type=f32))
    jb = lax.broadcasted_iota(jnp.int32, (n_sel, Q_BLOCK), 0)
    cur = (q0 + lax.broadcasted_iota(jnp.int32, (n_sel, Q_BLOCK), 1)) // SEL_BLOCK
    forced = (jb == 0) | (jb == cur) | (jb == cur - 1)
    imp = jnp.where(forced, -NEG_BIG, jnp.where(jb <= cur, imp, NEG_BIG))
    rank = jnp.zeros((n_sel, Q_BLOCK), f32)
    for i in range(n_sel):
        row_i = imp[i:i + 1, :]
        ahead = (row_i > imp) | ((row_i == imp) & (jb > i))
        rank = rank + jnp.where(ahead, 1.0, 0.0)
    sel_bias = jnp.where(rank < float(min(SEL_TOP, n_sel)), 0.0, NEG_BIG).T.astype(bf16)

    q_sel = jnp.concatenate([q, jnp.concatenate([sel_bias] * hpg, axis=0)], axis=1)

    def tile_update(kt, carry, on_diagonal):
        m, acc = carry
        k0 = pl.multiple_of(kt * tk, tk)
        s = lax.dot_general(q_sel, ks_ref[0, 0, pl.ds(k0, tk), :], _NT, preferred_element_type=f32)
        if on_diagonal:
            t_pos = q0 + lax.broadcasted_iota(jnp.int32, (Q_BLOCK, tk), 0)
            key = k0 + lax.broadcasted_iota(jnp.int32, (Q_BLOCK, tk), 1)
            causal = jnp.concatenate([jnp.where(key <= t_pos, 0.0, NEG_BIG)] * hpg, axis=0)
            s = s + causal
        m_new = jnp.maximum(m, jnp.max(s, axis=-1, keepdims=True))
        p = jnp.exp(s - m_new).astype(bf16)
        acc = jnp.exp(m - m_new) * acc + jnp.dot(p, vs_ref[0, 0, pl.ds(k0, tk), :], preferred_element_type=f32)
        return m_new, acc

    init = (jnp.full((rows, 1), NEG_BIG, f32), jnp.zeros((rows, NSA_VAL_W), f32))
    last_tile = (q0 + Q_BLOCK - 1) // tk
    carry = lax.fori_loop(0, last_tile, lambda kt, c: tile_update(kt, c, False), init)
    _, acc = tile_update(last_tile, carry, True)
    o_sel = acc[:, 0:dh] / jnp.maximum(acc[:, dh:dh + 1], 1e-30)

    w0 = pl.multiple_of(jnp.maximum(q0 - WINDOW, 0), Q_BLOCK)
    kwin = kw_ref[0, 0, pl.ds(w0, NSA_WIN_SPAN), :]
    vwin = vw_ref[0, 0, pl.ds(w0, NSA_WIN_SPAN), :]
    tq = q0 + lax.broadcasted_iota(jnp.int32, (Q_BLOCK, NSA_WIN_SPAN), 0)
    kp = w0 + lax.broadcasted_iota(jnp.int32, (Q_BLOCK, NSA_WIN_SPAN), 1)
    dlt = tq - kp
    wbias = jnp.where((dlt >= 0) & (dlt < WINDOW), 0.0, NEG_BIG).astype(f32)
    wbias = jnp.concatenate([wbias] * hpg, axis=0)
    s = lax.dot_general(q, kwin, _NT, preferred_element_type=f32) + wbias
    p = jnp.exp(s - jnp.max(s, axis=-1, keepdims=True)).astype(bf16)
    pv = jnp.dot(p, vwin, preferred_element_type=f32)
    o_win = pv[:, 0:dh] / jnp.maximum(pv[:, dh:dh + 1], 1e-30)

    o_ref[0] = (gc_ref[0] * by_head(o_cmp) + gs_ref[0] * by_head(o_sel) + gw_ref[0] * by_head(o_win)
                ).astype(o_ref.dtype)


def _nsa_attn(q, k_cmp, v_cmp, ks, vs, kw, vw, g_cmp, g_sel, g_win):
    bsz, s, _ = q.shape
    ng, dh = NSA_KV_HEADS, NSA_DH
    gw = NSA_GROUP * dh
    kv_spec = pl.BlockSpec((1, 1, s, dh), lambda b, g, i: (b, g, 0, 0))
    val_spec = pl.BlockSpec((1, 1, s, NSA_VAL_W), lambda b, g, i: (b, g, 0, 0))
    ks_spec = pl.BlockSpec((1, 1, s, ks.shape[3]), lambda b, g, i: (b, g, 0, 0))
    cmp_spec = pl.BlockSpec((1, 1, k_cmp.shape[2], dh), lambda b, g, i: (b, g, 0, 0))
    q_spec = pl.BlockSpec((1, Q_BLOCK, gw), lambda b, g, i: (b, i, g))
    return pl.pallas_call(
        _nsa_attn_kernel,
        out_shape=jax.ShapeDtypeStruct(q.shape, jnp.bfloat16),
        grid=(bsz, ng, s // Q_BLOCK),
        in_specs=[q_spec, cmp_spec, cmp_spec, ks_spec, val_spec, kv_spec, val_spec, q_spec, q_spec, q_spec],
        out_specs=q_spec,
        compiler_params=pltpu.CompilerParams(
            dimension_semantics=("parallel", "parallel", "arbitrary"),
            vmem_limit_bytes=V7X_VMEM_LIMIT_BYTES),
        name="nsa_attn",
    )(q, k_cmp, v_cmp, ks, vs, kw, vw, g_cmp, g_sel, g_win)


def _nsa_mixer(z, nsa_tabs, cmp_pos, cmp_w1, cmp_w2):
    bsz, s, _ = z.shape
    ng, dh = NSA_KV_HEADS, NSA_DH
    cos, sin, cos_c, sin_c = nsa_tabs
    q, ks, vs, kw, vw, kc, vc, g_cmp, g_sel, g_win = _nsa_prep(z, cos, sin)
    n_blk = s // CMP_STRIDE
    flat = lambda t: t.reshape(bsz * ng, n_blk, CMP_STRIDE * dh)
    k_cmp, v_cmp = _nsa_compress(flat(kc), flat(vc), cmp_pos.reshape(2, 1, CMP_LEN * dh), cmp_w1, cmp_w2,
                                 cos_c, sin_c)
    grouped = lambda t: t.reshape(bsz, ng, n_blk, dh)
    return _nsa_attn(q, grouped(k_cmp), grouped(v_cmp), ks, vs, kw, vw, g_cmp, g_sel, g_win)


def _nsa_tables(cos, sin):
    end = CMP_LEN - 1
    at_ends = lambda t: jnp.pad(t[:, end::CMP_STRIDE, :NSA_DH], ((0, 0), (0, 1), (0, 0)))
    return cos, sin, at_ends(cos), at_ends(sin)


IN_GROUPS = (("ret", GROUP_SIZES[0] + GROUP_SIZES[1], GROUP_SIZES[2], 3072),
             ("rw", GROUP_SIZES[0], GROUP_SIZES[1], 2048),
             ("sc", 0, GROUP_SIZES[0], 1536),
             ("nsa", sum(GROUP_SIZES[:3]), GROUP_SIZES[3], 1536),
             ("gate", sum(GROUP_SIZES[:4]), GROUP_SIZES[4], 8192))
IN_COL0 = {name: sum(g[3] for g in IN_GROUPS[:i]) for i, (name, _, _, _) in enumerate(IN_GROUPS)}
IN_WIDTH = {name: padded for name, _, _, padded in IN_GROUPS}


def _aligned_in_proj(w_in):
    parts = [jnp.pad(w_in[:, :, start:start + width], ((0, 0), (0, 0), (0, padded - width)))
             for _, start, width, padded in IN_GROUPS]
    return jnp.concatenate(parts, axis=-1).astype(jnp.bfloat16)


def _hybrid_mixer(h, bsz, layer, ret_tabs, nsa_tabs, w_all, sc_conv, rw_mu, rw_w0, rw_w2, rw_a0, rw_a2, rw_g2,
                  rw_k_k, rw_k_a, rw_r_k, rw_lnx_g, rw_lnx_b, nsa_cmp_pos, nsa_cmp_w1, nsa_cmp_w2,
                  wb_stack):
    m = h.shape[0]
    s = m // bsz
    group = lambda name: _mm_cols(h, w_all, layer, IN_COL0[name], IN_WIDTH[name])
    per_batch = lambda z: z.reshape(bsz, s, z.shape[1])
    o_a = _short_conv_mixer(group("sc"), sc_conv, s)
    o_b = _rwkv7_mixer(per_batch(group("rw")), rw_mu, rw_w0, rw_w2, rw_a0, rw_a2, rw_g2, rw_k_k, rw_k_a, rw_r_k,
                       rw_lnx_g, rw_lnx_b)
    o_c = _retention_mixer(per_batch(group("ret")), *ret_tabs)
    o_d = _nsa_mixer(per_batch(group("nsa")), nsa_tabs, nsa_cmp_pos, nsa_cmp_w1, nsa_cmp_w2)
    outs = [o.reshape(m, o.shape[-1]) for o in (o_a, o_b, o_c, o_d)]
    assert tuple(o.shape[1] for o in outs) == BR_SIZES
    return _merge(h, w_all, layer, IN_COL0["gate"], outs, wb_stack)


def kernel(x, mem, positions, ln_mix_pre, ln_mix_post, ln_mem_q, ln_mem_kv, ln_mem_post, ln_ffn_pre, ln_ffn_post, w_in, sc_conv, rw_mu, rw_w0, rw_w2, rw_a0, rw_a2, rw_g2, rw_k_k, rw_k_a, rw_r_k, rw_lnx_g, rw_lnx_b, nsa_cmp_pos, nsa_cmp_w1, nsa_cmp_w2, w_branch, w_out, mem_wq, mem_wkv, mem_wo, ffn_w_up, ffn_conv_w, ffn_conv_b, ffn_w_down):
    bsz, s, d = x.shape
    mem_len = mem.shape[1]
    xf = x.reshape(bsz * s, d)
    h = _prenorm(xf, ln_mix_pre[0])
    ret_cos, ret_sin, nsa_cos, nsa_sin = _rope_tables(positions)
    ret_tabs = (ret_cos, ret_sin)
    nsa_tabs = _nsa_tables(nsa_cos, nsa_sin)
    w_all = _aligned_in_proj(w_in)
    bf16 = lambda w: w.astype(jnp.bfloat16)
    wb_b, w_out_b, wq_b, wkv_b, wo_b, w_down_b = (bf16(w_branch), bf16(w_out), bf16(mem_wq), bf16(mem_wkv),
                                                   bf16(mem_wo), bf16(ffn_w_down))
    for l in range(DEPTH):
        merged = _hybrid_mixer(h, bsz, l, ret_tabs, nsa_tabs, w_all, sc_conv[l], rw_mu[l], rw_w0[l], rw_w2[l], rw_a0[l],
                               rw_a2[l], rw_g2[l], rw_k_k[l], rw_k_a[l], rw_r_k[l], rw_lnx_g[l], rw_lnx_b[l],
                               nsa_cmp_pos[l], nsa_cmp_w1[l], nsa_cmp_w2[l], wb_b)
        xf, h = _mm_norm_res(merged, w_out_b, l, xf, ln_mix_post[l], ln_mem_q[l])
        mem_n = _prenorm(mem.reshape(bsz * mem_len, d), ln_mem_kv[l])
        kv = _mm_cols(mem_n, wkv_b, l, 0, 2 * d).astype(jnp.bfloat16)
        k_mem = kv[:, :d].reshape(bsz, mem_len, d)
        v_mem = kv[:, d:].reshape(bsz, mem_len, d)
        o = _xattn(h, wq_b, l, k_mem, v_mem, s)
        xf, h = _mm_norm_res(o, wo_b, l, xf, ln_mem_post[l], ln_ffn_pre[l])
        act = _ffn_up(h, ffn_w_up, l, ffn_conv_w[l], ffn_conv_b[l], s)
        g_next = ln_mix_pre[l + 1] if l + 1 < DEPTH else None
        xf, h = _mm_norm_res(act, w_down_b, l, xf, ln_ffn_post[l], g_next)
    return xf.reshape(bsz, s, d)
```

```python
import functools
import math

import jax
import jax.numpy as jnp
import numpy as np
from jax import lax
from jax.experimental import pallas as pl
from jax.experimental.pallas import tpu as pltpu

D_MODEL = 2048
DEPTH = 2
MEM_HEADS = 4
MEM_HEAD_DIM = D_MODEL // MEM_HEADS
SC_W = D_MODEL // 4
SC_K = 3
RW_HEAD_DIM = 64
RW_W = D_MODEL // 4
RW_HEADS = RW_W // RW_HEAD_DIM
RW_LORA_W = 96
RW_LORA_A = 96
RW_LORA_G = 256
RW_DECAY_SCALE = math.exp(-0.5)
RW_LN_EPS = 64e-5
RET_HEADS = 4
RET_DK = 128
RET_DV = 2 * RET_DK
RET_CHUNK = 128
RET_LN_EPS = 1e-5
NSA_HEADS = 8
NSA_KV_HEADS = 2
NSA_DH = 64
NSA_GROUP = NSA_HEADS // NSA_KV_HEADS
CMP_LEN = 32
CMP_STRIDE = 16
SEL_BLOCK = 64
SEL_TOP = 16
WINDOW = 512
Q_BLOCK = 256
ROPE_THETA = 10000.0
EPS = 1e-6
N_BRANCH = 4

SC_SIZES = (SC_W, SC_W, SC_W)
RW_SIZES = (RW_W, RW_W, RW_W, RW_LORA_W, RW_LORA_A, RW_LORA_G)
RET_SIZES = (RET_HEADS * RET_DK, RET_HEADS * RET_DK, RET_HEADS * RET_DV, RET_HEADS * RET_DV)
NSA_SIZES = (NSA_HEADS * NSA_DH,) + (NSA_KV_HEADS * NSA_DH,) * 6 + (NSA_HEADS * 3,)
GATE_SIZES = (D_MODEL,) * N_BRANCH
GROUP_SIZES = (sum(SC_SIZES), sum(RW_SIZES), sum(RET_SIZES), sum(NSA_SIZES), sum(GATE_SIZES))
BR_SIZES = (SC_W, RW_W, RET_HEADS * RET_DV, NSA_HEADS * NSA_DH)

V7X_VMEM_LIMIT_BYTES = 48 * 1024 * 1024
LANES = 128
SUBLANES = 8
_NT = (((1,), (1,)), ((), ()))


def _pick_tile(n, candidates):
    for c in candidates:
        if n % c == 0:
            return c
    return n


def _bdot(a, b):
    return jnp.dot(a.astype(jnp.bfloat16), b.astype(jnp.bfloat16), preferred_element_type=jnp.float32)


def _bdot_nt(a, b):
    return lax.dot_general(a.astype(jnp.bfloat16), b.astype(jnp.bfloat16), _NT,
                           preferred_element_type=jnp.float32)


def _head_sum(x, bd):
    hi = x.astype(jnp.bfloat16)
    lo = (x - hi.astype(jnp.float32)).astype(jnp.bfloat16)
    return (jnp.dot(hi, bd, preferred_element_type=jnp.float32)
            + jnp.dot(lo, bd, preferred_element_type=jnp.float32))


def _mm_kernel(a_ref, w_ref, o_ref, acc_ref):
    k = pl.program_id(2)

    @pl.when(k == 0)
    def _():
        acc_ref[...] = jnp.zeros_like(acc_ref)

    acc_ref[...] += jnp.dot(a_ref[...], w_ref[...], preferred_element_type=jnp.float32)

    @pl.when(k == pl.num_programs(2) - 1)
    def _():
        o_ref[...] = acc_ref[...]


def _mm_cols(a, w_stack, layer, col0, n):
    m, k = a.shape
    tn = next(c for c in (n if n <= 1536 else 1024, 1024, 768, 512, 256, 128) if n % c == 0 and col0 % c == 0)
    tm = _pick_tile(m, (1024, 512, 256, 128, 8))
    tk = _pick_tile(k, (2048, 1280, 1408, 1024, 512, 256, 128))
    j0 = col0 // tn
    return pl.pallas_call(
        _mm_kernel,
        out_shape=jax.ShapeDtypeStruct((m, n), jnp.float32),
        grid=(m // tm, n // tn, k // tk),
        in_specs=[pl.BlockSpec((tm, tk), lambda i, j, kk: (i, kk)),
                  pl.BlockSpec((None, tk, tn), lambda i, j, kk: (layer, kk, j0 + j))],
        out_specs=pl.BlockSpec((tm, tn), lambda i, j, kk: (i, j)),
        scratch_shapes=[pltpu.VMEM((tm, tn), jnp.float32)],
        compiler_params=pltpu.CompilerParams(
            dimension_semantics=("parallel", "parallel", "arbitrary"),
            vmem_limit_bytes=V7X_VMEM_LIMIT_BYTES),
        name="mm",
    )(a.astype(jnp.bfloat16), w_stack)


def _rms(y, g):
    return y * lax.rsqrt(jnp.mean(y * y, axis=-1, keepdims=True) + EPS) * g


def _prenorm_kernel(x_ref, g_ref, h_ref):
    h_ref[...] = _rms(x_ref[...], g_ref[...]).astype(h_ref.dtype)


def _prenorm(x, g):
    m, d = x.shape
    tm = _pick_tile(m, (512, 256, 128, 8))
    return pl.pallas_call(
        _prenorm_kernel,
        out_shape=jax.ShapeDtypeStruct((m, d), jnp.bfloat16),
        grid=(m // tm,),
        in_specs=[pl.BlockSpec((tm, d), lambda i: (i, 0)), pl.BlockSpec((1, d), lambda i: (0, 0))],
        out_specs=pl.BlockSpec((tm, d), lambda i: (i, 0)),
        compiler_params=pltpu.CompilerParams(dimension_semantics=("parallel",)),
        name="prenorm",
    )(x, g.reshape(1, d))


def _mm_norm_res_kernel(emit_h, whole_k, a_ref, w_ref, res_ref, g_ref, *rest):
    if emit_h:
        g2_ref, x_ref, h_ref = rest[:3]
    else:
        x_ref = rest[0]

    def finish(y):
        x_new = res_ref[...] + _rms(y, g_ref[...])
        x_ref[...] = x_new
        if emit_h:
            h_ref[...] = _rms(x_new, g2_ref[...]).astype(h_ref.dtype)

    if whole_k:
        finish(jnp.dot(a_ref[...], w_ref[...], preferred_element_type=jnp.float32))
        return
    acc_ref = rest[-1]
    k = pl.program_id(1)

    @pl.when(k == 0)
    def _():
        acc_ref[...] = jnp.zeros_like(acc_ref)

    acc_ref[...] += jnp.dot(a_ref[...], w_ref[...], preferred_element_type=jnp.float32)

    @pl.when(k == pl.num_programs(1) - 1)
    def _():
        finish(acc_ref[...])


def _mm_norm_res(a, w_stack, layer, res, g, g_next):
    m, k = a.shape
    d = w_stack.shape[2]
    tk = _pick_tile(k, (2048, 1408, 1024, 512))
    tm = 512
    whole_k = tk == k
    emit_h = g_next is not None
    row = pl.BlockSpec((tm, d), lambda i, kk: (i, 0))
    vec = pl.BlockSpec((1, d), lambda i, kk: (0, 0))
    w_spec = (pl.BlockSpec((None, tk, d), lambda i, kk: (layer, 0, 0), pipeline_mode=pl.Buffered(1)) if whole_k
              else pl.BlockSpec((None, tk, d), lambda i, kk: (layer, kk, 0)))
    x_sd = jax.ShapeDtypeStruct((m, d), jnp.float32)
    h_sd = jax.ShapeDtypeStruct((m, d), jnp.bfloat16)
    out = pl.pallas_call(
        functools.partial(_mm_norm_res_kernel, emit_h, whole_k),
        out_shape=(x_sd, h_sd) if emit_h else x_sd,
        grid=(m // tm, k // tk),
        in_specs=[pl.BlockSpec((tm, tk), lambda i, kk: (i, kk)), w_spec, row, vec] + ([vec] if emit_h else []),
        out_specs=(row, row) if emit_h else row,
        scratch_shapes=[] if whole_k else [pltpu.VMEM((tm, d), jnp.float32)],
        compiler_params=pltpu.CompilerParams(
            dimension_semantics=("parallel", "arbitrary"), vmem_limit_bytes=V7X_VMEM_LIMIT_BYTES),
        name="mm_norm_res",
    )(a.astype(jnp.bfloat16), w_stack, res, g.reshape(1, d), *([g_next.reshape(1, d)] if emit_h else []))
    return out if emit_h else (out, None)


FFN_TM = 1024
FFN_TN = 512
HALO = SUBLANES


def _shift_rows(u, prev, n):
    rolled = pltpu.roll(u, n, axis=0)
    rows = lax.broadcasted_iota(jnp.int32, u.shape, 0)
    for r in range(n):
        rolled = jnp.where(rows == r, prev[HALO - n + r:HALO - n + r + 1, :], rolled)
    return rolled


def _ffn_up_kernel(seq_len, h_ref, halo_ref, wa_ref, wb_ref, cwa_ref, cwb_ref, ba_ref, bb_ref, o_ref):
    i = pl.program_id(0)
    h = h_ref[...]
    at_start = (i * FFN_TM) % seq_len == 0
    halo = jnp.where(at_start, jnp.zeros_like(halo_ref[...]), halo_ref[...])

    def conv(w_ref, cw_ref, b_ref):
        w = w_ref[...].astype(jnp.bfloat16)
        u = jnp.dot(h, w, preferred_element_type=jnp.float32)
        up = jnp.dot(halo, w, preferred_element_type=jnp.float32)
        cw = cw_ref[...]
        y = cw[0:1] * _shift_rows(u, up, 2)
        y = y + cw[1:2] * _shift_rows(u, up, 1)
        y = y + cw[2:3] * u
        return y + b_ref[...]

    a = conv(wa_ref, cwa_ref, ba_ref)
    b = conv(wb_ref, cwb_ref, bb_ref)
    o_ref[...] = (jax.nn.gelu(a, approximate=True) * b).astype(o_ref.dtype)


def _ffn_up(h, w_up_stack, layer, w_conv, b_conv, seq_len):
    m, d = h.shape
    f = w_up_stack.shape[2] // 2
    nj = f // FFN_TN
    b_conv = b_conv.reshape(1, 2 * f)
    tiles_per_halo = FFN_TM // HALO
    return pl.pallas_call(
        functools.partial(_ffn_up_kernel, seq_len),
        out_shape=jax.ShapeDtypeStruct((m, f), jnp.bfloat16),
        grid=(m // FFN_TM, nj),
        in_specs=[pl.BlockSpec((FFN_TM, d), lambda i, j: (i, 0)),
                  pl.BlockSpec((HALO, d), lambda i, j: (jnp.maximum(i * tiles_per_halo - 1, 0), 0)),
                  pl.BlockSpec((None, d, FFN_TN), lambda i, j: (layer, 0, j)),
                  pl.BlockSpec((None, d, FFN_TN), lambda i, j: (layer, 0, j + nj)),
                  pl.BlockSpec((3, FFN_TN), lambda i, j: (0, j)),
                  pl.BlockSpec((3, FFN_TN), lambda i, j: (0, j + nj)),
                  pl.BlockSpec((1, FFN_TN), lambda i, j: (0, j)),
                  pl.BlockSpec((1, FFN_TN), lambda i, j: (0, j + nj))],
        out_specs=pl.BlockSpec((FFN_TM, FFN_TN), lambda i, j: (i, j)),
        compiler_params=pltpu.CompilerParams(
            dimension_semantics=("parallel", "arbitrary"), vmem_limit_bytes=V7X_VMEM_LIMIT_BYTES),
        name="ffn_up",
    )(h, h, w_up_stack, w_up_stack, w_conv, w_conv, b_conv, b_conv)


MERGE_TM = 512
MERGE_TN = 512


def _merge_kernel(h_ref, wg0, wg1, wg2, wg3, o0, o1, o2, o3, wb0, wb1, wb2, wb3, out_ref):
    h = h_ref[...]
    acc = None
    for wg, o, wb in ((wg0, o0, wb0), (wg1, o1, wb1), (wg2, o2, wb2), (wg3, o3, wb3)):
        gate = jax.nn.sigmoid(jnp.dot(h, wg[...], preferred_element_type=jnp.float32))
        term = gate * jnp.dot(o[...], wb[...], preferred_element_type=jnp.float32)
        acc = term if acc is None else acc + term
    out_ref[...] = acc.astype(out_ref.dtype)


def _merge(h, w_stack, layer, gate_col0, outs, wb_stack):
    m, d = h.shape
    nj = d // MERGE_TN
    j0 = gate_col0 // MERGE_TN
    assert gate_col0 % MERGE_TN == 0
    gate_specs = [pl.BlockSpec((None, d, MERGE_TN), functools.partial(lambda b, i, j: (layer, 0, j0 + b * nj + j), b))
                  for b in range(N_BRANCH)]
    o_specs = [pl.BlockSpec((MERGE_TM, o.shape[1]), lambda i, j: (i, 0)) for o in outs]
    row0 = np.cumsum([0] + [o.shape[1] for o in outs])
    assert all(r % o.shape[1] == 0 for r, o in zip(row0, outs))
    wb_specs = [pl.BlockSpec((None, o.shape[1], MERGE_TN),
                             functools.partial(lambda rb, i, j: (layer, rb, j), int(r) // o.shape[1]))
                for r, o in zip(row0, outs)]
    return pl.pallas_call(
        _merge_kernel,
        out_shape=jax.ShapeDtypeStruct((m, d), jnp.bfloat16),
        grid=(m // MERGE_TM, nj),
        in_specs=[pl.BlockSpec((MERGE_TM, d), lambda i, j: (i, 0))] + gate_specs + o_specs + wb_specs,
        out_specs=pl.BlockSpec((MERGE_TM, MERGE_TN), lambda i, j: (i, j)),
        compiler_params=pltpu.CompilerParams(
            dimension_semantics=("parallel", "arbitrary"), vmem_limit_bytes=V7X_VMEM_LIMIT_BYTES),
        name="merge",
    )(h, *([w_stack] * N_BRANCH), *[o.astype(jnp.bfloat16) for o in outs], *([wb_stack] * N_BRANCH))


XATT_TM = 512


def _xattn_kernel(h_ref, wq_ref, k_ref, v_ref, o_ref):
    q = jnp.dot(h_ref[...], wq_ref[...], preferred_element_type=jnp.float32)
    q = (q * (MEM_HEAD_DIM ** -0.5)).astype(jnp.bfloat16)
    for hh in range(MEM_HEADS):
        sl = slice(hh * MEM_HEAD_DIM, (hh + 1) * MEM_HEAD_DIM)
        s = lax.dot_general(q[:, sl], k_ref[0, :, sl], _NT, preferred_element_type=jnp.float32)
        p = jnp.exp(s - jnp.max(s, axis=-1, keepdims=True))
        p = p / jnp.sum(p, axis=-1, keepdims=True)
        o_ref[:, sl] = jnp.dot(p.astype(jnp.bfloat16), v_ref[0, :, sl],
                               preferred_element_type=jnp.float32).astype(o_ref.dtype)


def _xattn(h, wq_stack, layer, k, v, seq_len):
    m, d = h.shape
    mem_len = k.shape[1]
    per_batch = seq_len // XATT_TM
    kv_spec = pl.BlockSpec((1, mem_len, d), lambda i: (i // per_batch, 0, 0))
    return pl.pallas_call(
        _xattn_kernel,
        out_shape=jax.ShapeDtypeStruct((m, d), jnp.bfloat16),
        grid=(m // XATT_TM,),
        in_specs=[pl.BlockSpec((XATT_TM, d), lambda i: (i, 0)),
                  pl.BlockSpec((None, d, d), lambda i: (layer, 0, 0)), kv_spec, kv_spec],
        out_specs=pl.BlockSpec((XATT_TM, d), lambda i: (i, 0)),
        compiler_params=pltpu.CompilerParams(
            dimension_semantics=("parallel",), vmem_limit_bytes=V7X_VMEM_LIMIT_BYTES),
        name="xattn",
    )(h, wq_stack, k, v)


def _rope_freqs(d):
    return ROPE_THETA ** (-jnp.arange(0, d, 2, dtype=jnp.float32) / d)


def _retnet_freqs(d):
    return 1.0 / (ROPE_THETA ** jnp.linspace(0.0, 1.0, d // 2, dtype=jnp.float32))


WKV_CHUNK = 64


def _rwkv7_kernel(z_ref, mu_ref, w0_ref, a0_ref, kk_ref, ka_ref, rk_ref, lng_ref, lnb_ref,
                  ww2_ref, wa2_ref, wg2_ref, o_ref, s_ref, prev_ref, y_ref):
    c = WKV_CHUNK
    hd = RW_HEAD_DIM
    w = RW_W
    nb = z_ref.shape[0]
    f32, bf16 = jnp.float32, jnp.bfloat16

    @pl.when(pl.program_id(0) == 0)
    def _():
        s_ref[...] = jnp.zeros_like(s_ref)
        prev_ref[...] = jnp.zeros_like(prev_ref)

    lane_h = lax.broadcasted_iota(jnp.int32, (w, w), 0) // hd
    lane_w = lax.broadcasted_iota(jnp.int32, (w, w), 1) // hd
    head_bd = jnp.where(lane_h == lane_w, 1.0, 0.0).astype(bf16)
    row = lax.broadcasted_iota(jnp.int32, (c, c), 0)
    col = lax.broadcasted_iota(jnp.int32, (c, c), 1)
    tri = jnp.where(row >= col, 1.0, 0.0).astype(bf16)
    eye = jnp.where(row == col, 1.0, 0.0).astype(f32)
    r2 = lax.broadcasted_iota(jnp.int32, (2 * c, 2 * c), 0)
    c2 = lax.broadcasted_iota(jnp.int32, (2 * c, 2 * c), 1)
    rr = jnp.where(r2 >= c, r2 - c, r2)
    cc = jnp.where(c2 >= c, c2 - c, c2)
    tri_mask = cc < rr + jnp.where(r2 >= c, 1, 0)

    def prepare(b):
        z = z_ref[b]
        rows = lax.broadcasted_iota(jnp.int32, z.shape, 0)
        z_prev = jnp.where(rows == 0, prev_ref[b], pltpu.roll(z, 1, axis=0))
        prev_ref[b] = z[c - 1:c, :]
        z = z + (z_prev - z) * mu_ref[...]
        r, k, v, lora = z[:, 0:w], z[:, w:2 * w], z[:, 2 * w:3 * w], z[:, 3 * w:4 * w]
        logw = -RW_DECAY_SCALE * jax.nn.sigmoid(w0_ref[...] + _bdot(jnp.tanh(lora), ww2_ref[...]))
        a = jax.nn.sigmoid(a0_ref[...] + _bdot(lora, wa2_ref[...]))
        g = _bdot(jax.nn.sigmoid(lora), wg2_ref[...])
        kk = k * kk_ref[...]
        kk = kk / jnp.maximum(jnp.sqrt(_head_sum(kk * kk, head_bd)), 1e-12)
        k = k * (1.0 + (a - 1.0) * ka_ref[...])
        hi = logw.astype(bf16)
        rem = logw - hi.astype(f32)
        mid = rem.astype(bf16)
        lo = (rem - mid.astype(f32)).astype(bf16)
        cum = (jnp.dot(tri, hi, preferred_element_type=f32) + jnp.dot(tri, mid, preferred_element_type=f32)
               + jnp.dot(tri, lo, preferred_element_type=f32))
        kka = kk * a
        e_neg = jnp.exp(-cum)
        last = cum[c - 1:c, :]
        dec = jnp.exp(last - cum)
        return dict(r=r, k=k, v=v, g=g, vb=v.astype(bf16), g_last=jnp.exp(last),
                    qh=(kk * jnp.exp(cum - logw)).astype(bf16), rh=(r * jnp.exp(cum)).astype(bf16),
                    bh=(kka * e_neg).astype(bf16), kh=(k * e_neg).astype(bf16),
                    bd=(kka * dec).astype(bf16), kd=(k * dec).astype(bf16))

    pre = [prepare(b) for b in range(nb)]

    units = [(b, h) for b in range(nb) for h in range(RW_HEADS)]
    n_units = range(len(units))
    part = lambda name, i: pre[units[i][0]][name][:, units[i][1] * hd:(units[i][1] + 1) * hd]
    a1 = [jnp.concatenate([part("qh", i), part("rh", i)], axis=0) for i in n_units]
    b1 = [jnp.concatenate([part("bh", i), part("kh", i)], axis=0) for i in n_units]
    ss = [jnp.where(tri_mask, lax.dot_general(a1[i], b1[i], _NT, preferred_element_type=f32), 0.0)
          for i in n_units]
    s_old = [s_ref[i] for i in n_units]
    qr = [lax.dot_general(a1[i], s_old[i].astype(bf16), _NT, preferred_element_type=f32) for i in n_units]
    lm = [_bdot(ss[i][:, c:2 * c], part("vb", i)) for i in n_units]
    n = [-ss[i][0:c, 0:c] for i in n_units]
    t = [eye + n[i] for i in n_units]
    for _ in range(5):
        n = [_bdot(n[i], n[i]) for i in n_units]
        t = [t[i] + _bdot(t[i], n[i]) for i in n_units]
    u = [-_bdot(t[i], qr[i][0:c] + lm[i][0:c]) for i in n_units]
    for i in n_units:
        b, h = units[i]
        y_ref[b, :, h * hd:(h + 1) * hd] = qr[i][c:2 * c] + lm[i][c:2 * c] + _bdot(ss[i][c:2 * c, 0:c], u[i])
    for i in n_units:
        zt = jnp.concatenate([u[i], part("v", i)], axis=0).T
        x = jnp.concatenate([part("bd", i), part("kd", i)], axis=0)
        s_ref[i] = s_old[i] * part("g_last", i) + _bdot(zt, x)

    for b in range(nb):
        p = pre[b]
        y = y_ref[b]
        mean = _head_sum(y, head_bd) * (1.0 / hd)
        yc = y - mean
        var = _head_sum(yc * yc, head_bd) * (1.0 / hd)
        yn = yc * lax.rsqrt(var + RW_LN_EPS) * lng_ref[...] + lnb_ref[...]
        bonus = _head_sum(p["r"] * p["k"] * rk_ref[...], head_bd) * p["v"]
        o_ref[b] = ((yn + bonus) * p["g"]).astype(o_ref.dtype)


def _rwkv7_mixer(z, mu, w0, w_w2, a0, w_a2, w_g2, k_k, k_a, r_k, lnx_g, lnx_b):
    bsz, s, zw = z.shape
    w = RW_W
    n_lora = RW_LORA_W + RW_LORA_A + RW_LORA_G
    row = lambda t: t.reshape(1, -1)
    mu = jnp.pad(mu, (0, zw - mu.shape[0])).reshape(1, zw)
    o_w, o_a = RW_LORA_W, RW_LORA_W + RW_LORA_A
    pad_rows = lambda m, lo: jnp.pad(m, ((lo, w - lo - m.shape[0]), (0, 0))).astype(jnp.bfloat16)
    vec = pl.BlockSpec((1, w), lambda c: (0, 0))
    mat = pl.BlockSpec((w, w), lambda c: (0, 0))
    assert zw == 4 * w and n_lora <= w
    return pl.pallas_call(
        _rwkv7_kernel,
        out_shape=jax.ShapeDtypeStruct((bsz, s, w), jnp.bfloat16),
        grid=(s // WKV_CHUNK,),
        in_specs=[pl.BlockSpec((bsz, WKV_CHUNK, zw), lambda c: (0, c, 0)),
                  pl.BlockSpec((1, zw), lambda c: (0, 0))] + [vec] * 7 + [mat] * 3,
        out_specs=pl.BlockSpec((bsz, WKV_CHUNK, w), lambda c: (0, c, 0)),
        scratch_shapes=[pltpu.VMEM((bsz * RW_HEADS, RW_HEAD_DIM, RW_HEAD_DIM), jnp.float32),
                        pltpu.VMEM((bsz, 1, zw), jnp.float32),
                        pltpu.VMEM((bsz, WKV_CHUNK, w), jnp.float32)],
        compiler_params=pltpu.CompilerParams(dimension_semantics=("arbitrary",)),
        name="rwkv7",
    )(z, mu, row(w0), row(a0), row(k_k), row(k_a), row(r_k), row(lnx_g), row(lnx_b),
      pad_rows(w_w2, 0), pad_rows(w_a2, o_w), pad_rows(w_g2, o_a))


RET_LOG_DECAY = tuple(math.log(1.0 - 2.0 ** (-5.0 - h)) for h in range(RET_HEADS))


def _retention_kernel(q_ref, k_ref, v_ref, g_ref, cos_ref, sin_ref, o_ref, s_ref):
    c, dk, dv = RET_CHUNK, RET_DK, RET_DV
    f32 = jnp.float32

    @pl.when(pl.program_id(1) == 0)
    def _():
        s_ref[...] = jnp.zeros_like(s_ref)

    cos = cos_ref[0]
    sin = sin_ref[0]
    diff = (lax.broadcasted_iota(jnp.int32, (c, c), 0) - lax.broadcasted_iota(jnp.int32, (c, c), 1)).astype(f32)
    tok = lax.broadcasted_iota(jnp.int32, (c, 1), 0).astype(f32)
    rope = lambda x: x * cos + pltpu.roll(x, dk // 2, axis=1) * sin
    for h in range(RET_HEADS):
        lg = RET_LOG_DECAY[h]
        q = rope(q_ref[0, :, h * dk:(h + 1) * dk])
        k = rope(k_ref[0, :, h * dk:(h + 1) * dk]) * (dk ** -0.5)
        v = v_ref[0, :, h * dv:(h + 1) * dv]
        decay_in = jnp.where(diff >= 0, jnp.exp(jnp.maximum(diff, 0.0) * lg), 0.0)
        scores = _bdot_nt(q, k) * decay_in
        state = s_ref[h]
        o = _bdot(scores, v) + _bdot(q * jnp.exp((tok + 1.0) * lg), state)
        s_ref[h] = state * math.exp(c * lg) + _bdot((k * jnp.exp((c - 1.0 - tok) * lg)).T, v)
        mu = jnp.mean(o, axis=-1, keepdims=True)
        oc = o - mu
        var = jnp.mean(oc * oc, axis=-1, keepdims=True)
        gate = g_ref[0, :, h * dv:(h + 1) * dv]
        o_ref[0, :, h * dv:(h + 1) * dv] = (oc * lax.rsqrt(var + RET_LN_EPS) * gate * jax.nn.sigmoid(gate)
                                            ).astype(o_ref.dtype)


def _retention_mixer(z, cos, sin):
    bsz, s, _ = z.shape
    qk_w, v_w = RET_HEADS * RET_DK, RET_HEADS * RET_DV
    c = RET_CHUNK
    tab = pl.BlockSpec((1, c, RET_DK), lambda b, i: (b, i, 0))
    return pl.pallas_call(
        _retention_kernel,
        out_shape=jax.ShapeDtypeStruct((bsz, s, v_w), jnp.bfloat16),
        grid=(bsz, s // c),
        in_specs=[pl.BlockSpec((1, c, qk_w), lambda b, i: (b, i, 0)),
                  pl.BlockSpec((1, c, qk_w), lambda b, i: (b, i, 1)),
                  pl.BlockSpec((1, c, v_w), lambda b, i: (b, i, 1)),
                  pl.BlockSpec((1, c, v_w), lambda b, i: (b, i, 2)), tab, tab],
        out_specs=pl.BlockSpec((1, c, v_w), lambda b, i: (b, i, 0)),
        scratch_shapes=[pltpu.VMEM((RET_HEADS, RET_DK, RET_DV), jnp.float32)],
        compiler_params=pltpu.CompilerParams(dimension_semantics=("parallel", "arbitrary")),
        name="retention",
    )(z, z, z, z, cos, sin)


TRIG_TM = 1024


def _trig_kernel(pos_ref, freq_ref, rc_ref, rs_ref, nc_ref, ns_ref):
    ang = pos_ref[...] * freq_ref[...]
    c, sn = jnp.cos(ang), jnp.sin(ang)
    n_ret, n_nsa = RET_DK // 2, NSA_DH // 2
    rc_ref[...] = jnp.concatenate([c[:, :n_ret]] * 2, axis=1)
    rs_ref[...] = jnp.concatenate([-sn[:, :n_ret], sn[:, :n_ret]], axis=1)
    nsa_c, nsa_s = c[:, n_ret:n_ret + n_nsa], sn[:, n_ret:n_ret + n_nsa]
    nc_ref[...] = jnp.concatenate([nsa_c, nsa_c] * NSA_HEADS, axis=1)
    ns_ref[...] = jnp.concatenate([-nsa_s, nsa_s] * NSA_HEADS, axis=1)


def _rope_tables(positions):
    bsz, s = positions.shape
    m = bsz * s
    n_ret, n_nsa = RET_DK // 2, NSA_DH // 2
    assert n_ret + n_nsa <= LANES
    freq = jnp.concatenate([_retnet_freqs(RET_DK), _rope_freqs(NSA_DH),
                            jnp.zeros((LANES - n_ret - n_nsa,), jnp.float32)]).reshape(1, LANES)
    tm = _pick_tile(m, (TRIG_TM, 512, 256, 128, 8))
    spec = lambda w: pl.BlockSpec((tm, w), lambda i: (i, 0))
    sd = lambda w: jax.ShapeDtypeStruct((m, w), jnp.float32)
    nsa_w = NSA_HEADS * NSA_DH
    tabs = pl.pallas_call(
        _trig_kernel,
        out_shape=(sd(RET_DK), sd(RET_DK), sd(nsa_w), sd(nsa_w)),
        grid=(m // tm,),
        in_specs=[spec(1), pl.BlockSpec((1, LANES), lambda i: (0, 0))],
        out_specs=(spec(RET_DK), spec(RET_DK), spec(nsa_w), spec(nsa_w)),
        compiler_params=pltpu.CompilerParams(dimension_semantics=("parallel",)),
        name="rope_tables",
    )(positions.astype(jnp.float32).reshape(m, 1), freq)
    return [t.reshape(bsz, s, t.shape[1]) for t in tabs]


SC_TM = 512


def _short_conv_kernel(seq_len, b_ref, c_ref, x_ref, ch_ref, xh_ref, w_ref, o_ref):
    at_start = (pl.program_id(0) * SC_TM) % seq_len == 0
    u = c_ref[...] * x_ref[...]
    up = jnp.where(at_start, 0.0, ch_ref[...] * xh_ref[...])
    w = w_ref[...]
    y = w[0:1] * _shift_rows(u, up, 2) + w[1:2] * _shift_rows(u, up, 1) + w[2:3] * u
    o_ref[...] = (b_ref[...] * y).astype(o_ref.dtype)


def _short_conv_mixer(z, w_conv, seq_len):
    m = z.shape[0]
    w = SC_W
    per_halo = SC_TM // HALO
    tile = lambda j: pl.BlockSpec((SC_TM, w), lambda i: (i, j))
    halo = lambda j: pl.BlockSpec((HALO, w), lambda i: (jnp.maximum(i * per_halo - 1, 0), j))
    return pl.pallas_call(
        functools.partial(_short_conv_kernel, seq_len),
        out_shape=jax.ShapeDtypeStruct((m, w), jnp.bfloat16),
        grid=(m // SC_TM,),
        in_specs=[tile(0), tile(1), tile(2), halo(1), halo(2), pl.BlockSpec((SC_K, w), lambda i: (0, 0))],
        out_specs=pl.BlockSpec((SC_TM, w), lambda i: (i, 0)),
        compiler_params=pltpu.CompilerParams(dimension_semantics=("parallel",)),
        name="short_conv",
    )(z, z, z, z, z, w_conv)


NSA_KEY_TILE = 512
NSA_WIN_SPAN = WINDOW + Q_BLOCK
NSA_PREP_TM = 512
NEG_BIG = -1e30
NSA_Q_W = NSA_HEADS * NSA_DH
NSA_KV_W = NSA_KV_HEADS * NSA_DH
NSA_GATE_OFF = NSA_Q_W + 6 * NSA_KV_W
NSA_VAL_W = LANES


def _rope_lanes(x, cos, sin):
    width = x.shape[1]
    half = NSA_DH // 2
    lane = lax.broadcasted_iota(jnp.int32, x.shape, 1)
    other = jnp.where(lane % NSA_DH < half, pltpu.roll(x, width - half, axis=1), pltpu.roll(x, half, axis=1))
    return x * cos + other * sin


def _nsa_prep_kernel(z_ref, cos_ref, sin_ref, q_ref, ks_ref, vs_ref, kw_ref, vw_ref, kc_ref, vc_ref,
                     gc_ref, gs_ref, gw_ref):
    cos, sin = cos_ref[0], sin_ref[0]
    z = z_ref[0]
    q_ref[0] = (_rope_lanes(z[:, 0:NSA_Q_W], cos, sin) * (NSA_DH ** -0.5)).astype(q_ref.dtype)
    kv = lambda i: z[:, NSA_Q_W + i * NSA_KV_W:NSA_Q_W + (i + 1) * NSA_KV_W]
    cos_kv, sin_kv = cos[:, 0:NSA_KV_W], sin[:, 0:NSA_KV_W]
    pieces = ((kc_ref, kv(0)), (vc_ref, kv(1)), (ks_ref, _rope_lanes(kv(2), cos_kv, sin_kv)), (vs_ref, kv(3)),
              (kw_ref, _rope_lanes(kv(4), cos_kv, sin_kv)), (vw_ref, kv(5)))
    n_sel = ks_ref.shape[3] - NSA_DH
    tok = pl.program_id(1) * NSA_PREP_TM + lax.broadcasted_iota(jnp.int32, (NSA_PREP_TM, n_sel), 0)
    block_1hot = jnp.where(tok // SEL_BLOCK == lax.broadcasted_iota(jnp.int32, (NSA_PREP_TM, n_sel), 1), 1.0, 0.0)
    ones_col = jnp.where(lax.broadcasted_iota(jnp.int32, (NSA_PREP_TM, NSA_VAL_W - NSA_DH), 1) == 0, 1.0, 0.0)
    for ref, val in pieces:
        for g in range(NSA_KV_HEADS):
            val_g = val[:, g * NSA_DH:(g + 1) * NSA_DH]
            if ref is ks_ref:
                val_g = jnp.concatenate([val_g, block_1hot], axis=1)
            elif ref is vs_ref or ref is vw_ref:
                val_g = jnp.concatenate([val_g, ones_col], axis=1)
            ref[0, g] = val_g.astype(ref.dtype)
    gate = jax.nn.sigmoid(z[:, NSA_GATE_OFF:NSA_GATE_OFF + LANES])
    src = lax.broadcasted_iota(jnp.int32, (LANES, NSA_Q_W), 0)
    head = lax.broadcasted_iota(jnp.int32, (LANES, NSA_Q_W), 1) // NSA_DH
    for j, ref in enumerate((gc_ref, gs_ref, gw_ref)):
        expand = jnp.where(src == 3 * head + j, 1.0, 0.0).astype(jnp.bfloat16)
        ref[0] = _head_sum(gate, expand)


def _nsa_prep(z, cos, sin):
    bsz, s, zw = z.shape
    tm = NSA_PREP_TM
    wide = pl.BlockSpec((1, tm, NSA_Q_W), lambda b, i: (b, i, 0))
    grp = pl.BlockSpec((1, NSA_KV_HEADS, tm, NSA_DH), lambda b, i: (b, 0, i, 0))
    grp_sd = lambda dt: jax.ShapeDtypeStruct((bsz, NSA_KV_HEADS, s, NSA_DH), dt)
    ks_w = NSA_DH + s // SEL_BLOCK
    ks_spec = pl.BlockSpec((1, NSA_KV_HEADS, tm, ks_w), lambda b, i: (b, 0, i, 0))
    ks_sd = jax.ShapeDtypeStruct((bsz, NSA_KV_HEADS, s, ks_w), jnp.bfloat16)
    val_spec = pl.BlockSpec((1, NSA_KV_HEADS, tm, NSA_VAL_W), lambda b, i: (b, 0, i, 0))
    val_sd = jax.ShapeDtypeStruct((bsz, NSA_KV_HEADS, s, NSA_VAL_W), jnp.bfloat16)
    wide_sd = lambda dt: jax.ShapeDtypeStruct((bsz, s, NSA_Q_W), dt)
    bf16, f32 = jnp.bfloat16, jnp.float32
    return pl.pallas_call(
        _nsa_prep_kernel,
        out_shape=(wide_sd(bf16), ks_sd, val_sd, grp_sd(bf16), val_sd, grp_sd(f32), grp_sd(f32),
                   wide_sd(f32), wide_sd(f32), wide_sd(f32)),
        grid=(bsz, s // tm),
        in_specs=[pl.BlockSpec((1, tm, zw), lambda b, i: (b, i, 0)), wide, wide],
        out_specs=(wide, ks_spec, val_spec, grp, val_spec, grp, grp, wide, wide, wide),
        compiler_params=pltpu.CompilerParams(dimension_semantics=("parallel", "parallel")),
        name="nsa_prep",
    )(z, cos, sin)


def _nsa_compress_kernel(hk_ref, hv_ref, pos_ref, w1_ref, w2_ref, cos_ref, sin_ref, kc_ref, vc_ref):
    half = w1_ref.shape[1] // 2
    n = hk_ref.shape[1]
    for i, (h_ref, o_ref) in enumerate(((hk_ref, kc_ref), (hv_ref, vc_ref))):
        h = h_ref[0]
        first = _bdot(h, w1_ref[i, 0:half, :])
        second = pltpu.roll(_bdot(h, w1_ref[i, half:2 * half, :]), n - 1, axis=0)
        bias = _bdot(jnp.broadcast_to(pos_ref[i], (SUBLANES, 2 * half)), w1_ref[i])[0:1]
        out = _bdot(jax.nn.gelu(first + second + bias, approximate=True), w2_ref[i])
        if i == 0:
            src = lax.broadcasted_iota(jnp.int32, (NSA_DH, NSA_DH), 0)
            dst = lax.broadcasted_iota(jnp.int32, (NSA_DH, NSA_DH), 1)
            swap = jnp.where(src == (dst + NSA_DH // 2) % NSA_DH, 1.0, 0.0).astype(jnp.bfloat16)
            out = out * cos_ref[0] + _head_sum(out, swap) * sin_ref[0]
        o_ref[0] = out.astype(o_ref.dtype)


def _nsa_compress(hk, hv, pos_flat, w1, w2, cos_c, sin_c):
    bg, n, hw = hk.shape
    ng = NSA_KV_HEADS
    blk = pl.BlockSpec((1, n, hw), lambda i: (i, 0, 0))
    tab = pl.BlockSpec((1, n, NSA_DH), lambda i: (i // ng, 0, 0))
    out = pl.BlockSpec((1, n, NSA_DH), lambda i: (i, 0, 0))
    whole = lambda a: pl.BlockSpec(a.shape, lambda i: (0,) * a.ndim)
    sd = jax.ShapeDtypeStruct((bg, n, NSA_DH), jnp.bfloat16)
    return pl.pallas_call(
        _nsa_compress_kernel,
        out_shape=(sd, sd),
        grid=(bg,),
        in_specs=[blk, blk, whole(pos_flat), whole(w1), whole(w2), tab, tab],
        out_specs=(out, out),
        compiler_params=pltpu.CompilerParams(
            dimension_semantics=("parallel",), vmem_limit_bytes=V7X_VMEM_LIMIT_BYTES),
        name="nsa_compress",
    )(hk, hv, pos_flat, w1, w2, cos_c, sin_c)


def _nsa_attn_kernel(q_ref, kc_ref, vc_ref, ks_ref, vs_ref, kw_ref, vw_ref, gc_ref, gs_ref, gw_ref, o_ref):
    f32, bf16 = jnp.float32, jnp.bfloat16
    qi = pl.program_id(2)
    q0 = qi * Q_BLOCK
    hpg, dh, tk = NSA_GROUP, NSA_DH, NSA_KEY_TILE
    rows = hpg * Q_BLOCK
    n_sel = ks_ref.shape[3] - dh
    n_cmp = kc_ref.shape[2]
    q_all = q_ref[0]
    q = jnp.concatenate([q_all[:, h * dh:(h + 1) * dh] for h in range(hpg)], axis=0)
    by_head = lambda o: jnp.concatenate([o[h * Q_BLOCK:(h + 1) * Q_BLOCK] for h in range(hpg)], axis=1)

    tq_c = q0 + lax.broadcasted_iota(jnp.int32, (Q_BLOCK, n_cmp), 0)
    c_end = lax.broadcasted_iota(jnp.int32, (Q_BLOCK, n_cmp), 1) * CMP_STRIDE + (CMP_LEN - 1)
    c_ok = jnp.concatenate([jnp.where(c_end <= tq_c, 1.0, 0.0)] * hpg, axis=0) > 0.5
    s = jnp.where(c_ok, lax.dot_general(q, kc_ref[0, 0], _NT, preferred_element_type=f32), NEG_BIG)
    m = jnp.max(s, axis=-1, keepdims=True)
    e = jnp.where(c_ok, jnp.exp(s - m), 0.0)
    p = e / jnp.maximum(jnp.sum(e, axis=-1, keepdims=True), 1e-30)
    o_cmp = jnp.dot(p.astype(bf16), vc_ref[0, 0], preferred_element_type=f32)

    p_sum = p[0:Q_BLOCK]
    for h in range(1, hpg):
        p_sum = p_sum + p[h * Q_BLOCK:(h + 1) * Q_BLOCK]
    c_start = lax.broadcasted_iota(jnp.int32, (n_sel, n_cmp), 1) * CMP_STRIDE
    j_blk = lax.broadcasted_iota(jnp.int32, (n_sel, n_cmp), 0)
    overlap_t = jnp.where((c_start < (j_blk + 1) * SEL_BLOCK) & (c_start + CMP_LEN > j_blk * SEL_BLOCK),
                          1.0, 0.0).astype(bf16)
    p_hi = p_sum.astype(bf16)
    p_lo = (p_sum - p_hi.astype(f32)).astype(bf16)
    imp = (lax.dot_general(overlap_t, p_hi, _NT, preferred_element_type=f32)
           + lax.dot_general(overlap_t, p_lo, _NT, preferred_element_type=f32))
    jb = lax.broadcasted_iota(jnp.int32, (n_sel, Q_BLOCK), 0)
    cur = (q0 + lax.broadcasted_iota(jnp.int32, (n_sel, Q_BLOCK), 1)) // SEL_BLOCK
    forced = (jb == 0) | (jb == cur) | (jb == cur - 1)
    imp = jnp.where(forced, -NEG_BIG, jnp.where(jb <= cur, imp, NEG_BIG))
    rank = jnp.zeros((n_sel, Q_BLOCK), f32)
    for i in range(n_sel):
        row_i = imp[i:i + 1, :]
        ahead = (row_i > imp) | ((row_i == imp) & (jb > i))
        rank = rank + jnp.where(ahead, 1.0, 0.0)
    sel_bias = jnp.where(rank < float(min(SEL_TOP, n_sel)), 0.0, NEG_BIG).T.astype(bf16)

    q_sel = jnp.concatenate([q, jnp.concatenate([sel_bias] * hpg, axis=0)], axis=1)

    def tile_update(kt, carry, on_diagonal):
        m, acc = carry
        k0 = pl.multiple_of(kt * tk, tk)
        s = lax.dot_general(q_sel, ks_ref[0, 0, pl.ds(k0, tk), :], _NT, preferred_element_type=f32)
        if on_diagonal:
            t_pos = q0 + lax.broadcasted_iota(jnp.int32, (Q_BLOCK, tk), 0)
            key = k0 + lax.broadcasted_iota(jnp.int32, (Q_BLOCK, tk), 1)
            causal = jnp.concatenate([jnp.where(key <= t_pos, 0.0, NEG_BIG)] * hpg, axis=0)
            s = s + causal
        m_new = jnp.maximum(m, jnp.max(s, axis=-1, keepdims=True))
        p = jnp.exp(s - m_new).astype(bf16)
        acc = jnp.exp(m - m_new) * acc + jnp.dot(p, vs_ref[0, 0, pl.ds(k0, tk), :], preferred_element_type=f32)
        return m_new, acc

    init = (jnp.full((rows, 1), NEG_BIG, f32), jnp.zeros((rows, NSA_VAL_W), f32))
    last_tile = (q0 + Q_BLOCK - 1) // tk
    carry = lax.fori_loop(0, last_tile, lambda kt, c: tile_update(kt, c, False), init)
    _, acc = tile_update(last_tile, carry, True)
    o_sel = acc[:, 0:dh] / jnp.maximum(acc[:, dh:dh + 1], 1e-30)

    w0 = pl.multiple_of(jnp.maximum(q0 - WINDOW, 0), Q_BLOCK)
    kwin = kw_ref[0, 0, pl.ds(w0, NSA_WIN_SPAN), :]
    vwin = vw_ref[0, 0, pl.ds(w0, NSA_WIN_SPAN), :]
    tq = q0 + lax.broadcasted_iota(jnp.int32, (Q_BLOCK, NSA_WIN_SPAN), 0)
    kp = w0 + lax.broadcasted_iota(jnp.int32, (Q_BLOCK, NSA_WIN_SPAN), 1)
    dlt = tq - kp
    wbias = jnp.where((dlt >= 0) & (dlt < WINDOW), 0.0, NEG_BIG).astype(f32)
    wbias = jnp.concatenate([wbias] * hpg, axis=0)
    s = lax.dot_general(q, kwin, _NT, preferred_element_type=f32) + wbias
    p = jnp.exp(s - jnp.max(s, axis=-1, keepdims=True)).astype(bf16)
    pv = jnp.dot(p, vwin, preferred_element_type=f32)
    o_win = pv[:, 0:dh] / jnp.maximum(pv[:, dh:dh + 1], 1e-30)

    o_ref[0] = (gc_ref[0] * by_head(o_cmp) + gs_ref[0] * by_head(o_sel) + gw_ref[0] * by_head(o_win)
                ).astype(o_ref.dtype)


def _nsa_attn(q, k_cmp, v_cmp, ks, vs, kw, vw, g_cmp, g_sel, g_win):
    bsz, s, _ = q.shape
    ng, dh = NSA_KV_HEADS, NSA_DH
    gw = NSA_GROUP * dh
    kv_spec = pl.BlockSpec((1, 1, s, dh), lambda b, g, i: (b, g, 0, 0))
    val_spec = pl.BlockSpec((1, 1, s, NSA_VAL_W), lambda b, g, i: (b, g, 0, 0))
    ks_spec = pl.BlockSpec((1, 1, s, ks.shape[3]), lambda b, g, i: (b, g, 0, 0))
    cmp_spec = pl.BlockSpec((1, 1, k_cmp.shape[2], dh), lambda b, g, i: (b, g, 0, 0))
    q_spec = pl.BlockSpec((1, Q_BLOCK, gw), lambda b, g, i: (b, i, g))
    return pl.pallas_call(
        _nsa_attn_kernel,
        out_shape=jax.ShapeDtypeStruct(q.shape, jnp.bfloat16),
        grid=(bsz, ng, s // Q_BLOCK),
        in_specs=[q_spec, cmp_spec, cmp_spec, ks_spec, val_spec, kv_spec, val_spec, q_spec, q_spec, q_spec],
        out_specs=q_spec,
        compiler_params=pltpu.CompilerParams(
            dimension_semantics=("parallel", "parallel", "arbitrary"),
            vmem_limit_bytes=V7X_VMEM_LIMIT_BYTES),
        name="nsa_attn",
    )(q, k_cmp, v_cmp, ks, vs, kw, vw, g_cmp, g_sel, g_win)


def _nsa_mixer(z, nsa_tabs, cmp_pos, cmp_w1, cmp_w2):
    bsz, s, _ = z.shape
    ng, dh = NSA_KV_HEADS, NSA_DH
    cos, sin, cos_c, sin_c = nsa_tabs
    q, ks, vs, kw, vw, kc, vc, g_cmp, g_sel, g_win = _nsa_prep(z, cos, sin)
    n_blk = s // CMP_STRIDE
    flat = lambda t: t.reshape(bsz * ng, n_blk, CMP_STRIDE * dh)
    k_cmp, v_cmp = _nsa_compress(flat(kc), flat(vc), cmp_pos.reshape(2, 1, CMP_LEN * dh), cmp_w1, cmp_w2,
                                 cos_c, sin_c)
    grouped = lambda t: t.reshape(bsz, ng, n_blk, dh)
    return _nsa_attn(q, grouped(k_cmp), grouped(v_cmp), ks, vs, kw, vw, g_cmp, g_sel, g_win)


def _nsa_tables(cos, sin):
    end = CMP_LEN - 1
    at_ends = lambda t: jnp.pad(t[:, end::CMP_STRIDE, :NSA_DH], ((0, 0), (0, 1), (0, 0)))
    return cos, sin, at_ends(cos), at_ends(sin)


IN_GROUPS = (("sc", 0, GROUP_SIZES[0], 1536),
             ("rw", GROUP_SIZES[0], GROUP_SIZES[1], 2048),
             ("ret", GROUP_SIZES[0] + GROUP_SIZES[1], GROUP_SIZES[2], 3072),
             ("nsa", sum(GROUP_SIZES[:3]), GROUP_SIZES[3], 1536),
             ("gate", sum(GROUP_SIZES[:4]), GROUP_SIZES[4], 8192))
REGROUP_TN = 512


def _regroup_kernel(shift, width, *refs):
    src, o_ref = refs[:-1], refs[-1]
    window = jnp.concatenate([r[...] for r in src], axis=1)
    tile = window[:, shift:shift + REGROUP_TN]
    col = pl.program_id(1) * REGROUP_TN + lax.broadcasted_iota(jnp.int32, tile.shape, 1)
    o_ref[...] = jnp.where(col < width, tile, 0.0).astype(o_ref.dtype)


def _regroup_in_proj(w_in):
    n_layers, d, n_in = w_in.shape
    last_block = (n_in - 1) // LANES
    out = {}
    for name, start, width, padded in IN_GROUPS:
        shift = start % LANES
        n_src = REGROUP_TN // LANES + (1 if shift else 0)
        first = start // LANES
        src_specs = [pl.BlockSpec((None, d, LANES), functools.partial(
            lambda k, l, j: (l, 0, jnp.minimum(first + (REGROUP_TN // LANES) * j + k, last_block)), k))
            for k in range(n_src)]
        out[name] = pl.pallas_call(
            functools.partial(_regroup_kernel, shift, width),
            out_shape=jax.ShapeDtypeStruct((n_layers, d, padded), jnp.bfloat16),
            grid=(n_layers, padded // REGROUP_TN),
            in_specs=src_specs,
            out_specs=pl.BlockSpec((None, d, REGROUP_TN), lambda l, j: (l, 0, j)),
            compiler_params=pltpu.CompilerParams(
                dimension_semantics=("parallel", "parallel"), vmem_limit_bytes=V7X_VMEM_LIMIT_BYTES),
            name="regroup_in_proj",
        )(*([w_in] * n_src))
    return out


def _hybrid_mixer(h, bsz, layer, ret_tabs, nsa_tabs, w_groups, sc_conv, rw_mu, rw_w0, rw_w2, rw_a0, rw_a2, rw_g2,
                  rw_k_k, rw_k_a, rw_r_k, rw_lnx_g, rw_lnx_b, nsa_cmp_pos, nsa_cmp_w1, nsa_cmp_w2,
                  wb_stack):
    m = h.shape[0]
    s = m // bsz
    group = lambda name: _mm_cols(h, w_groups[name], layer, 0, w_groups[name].shape[2])
    per_batch = lambda z: z.reshape(bsz, s, z.shape[1])
    o_a = _short_conv_mixer(group("sc"), sc_conv, s)
    o_b = _rwkv7_mixer(per_batch(group("rw")), rw_mu, rw_w0, rw_w2, rw_a0, rw_a2, rw_g2, rw_k_k, rw_k_a, rw_r_k,
                       rw_lnx_g, rw_lnx_b)
    o_c = _retention_mixer(per_batch(group("ret")), *ret_tabs)
    o_d = _nsa_mixer(per_batch(group("nsa")), nsa_tabs, nsa_cmp_pos, nsa_cmp_w1, nsa_cmp_w2)
    outs = [o.reshape(m, o.shape[-1]) for o in (o_a, o_b, o_c, o_d)]
    assert tuple(o.shape[1] for o in outs) == BR_SIZES
    return _merge(h, w_groups["gate"], layer, 0, outs, wb_stack)


def kernel(x, mem, positions, ln_mix_pre, ln_mix_post, ln_mem_q, ln_mem_kv, ln_mem_post, ln_ffn_pre, ln_ffn_post, w_in, sc_conv, rw_mu, rw_w0, rw_w2, rw_a0, rw_a2, rw_g2, rw_k_k, rw_k_a, rw_r_k, rw_lnx_g, rw_lnx_b, nsa_cmp_pos, nsa_cmp_w1, nsa_cmp_w2, w_branch, w_out, mem_wq, mem_wkv, mem_wo, ffn_w_up, ffn_conv_w, ffn_conv_b, ffn_w_down):
    bsz, s, d = x.shape
    mem_len = mem.shape[1]
    xf = x.reshape(bsz * s, d)
    h = _prenorm(xf, ln_mix_pre[0])
    ret_cos, ret_sin, nsa_cos, nsa_sin = _rope_tables(positions)
    ret_tabs = (ret_cos, ret_sin)
    nsa_tabs = _nsa_tables(nsa_cos, nsa_sin)
    w_groups = _regroup_in_proj(w_in)
    bf16 = lambda w: w.astype(jnp.bfloat16)
    wb_b, w_out_b, wq_b, wkv_b, wo_b, w_down_b = (bf16(w_branch), bf16(w_out), bf16(mem_wq), bf16(mem_wkv),
                                                   bf16(mem_wo), bf16(ffn_w_down))
    for l in range(DEPTH):
        merged = _hybrid_mixer(h, bsz, l, ret_tabs, nsa_tabs, w_groups, sc_conv[l], rw_mu[l], rw_w0[l], rw_w2[l], rw_a0[l],
                               rw_a2[l], rw_g2[l], rw_k_k[l], rw_k_a[l], rw_r_k[l], rw_lnx_g[l], rw_lnx_b[l],
                               nsa_cmp_pos[l], nsa_cmp_w1[l], nsa_cmp_w2[l], wb_b)
        xf, h = _mm_norm_res(merged, w_out_b, l, xf, ln_mix_post[l], ln_mem_q[l])
        mem_n = _prenorm(mem.reshape(bsz * mem_len, d), ln_mem_kv[l])
        kv = _mm_cols(mem_n, wkv_b, l, 0, 2 * d).astype(jnp.bfloat16)
        k_mem = kv[:, :d].reshape(bsz, mem_len, d)
        v_mem = kv[:, d:].reshape(bsz, mem_len, d)
        o = _xattn(h, wq_b, l, k_mem, v_mem, s)
        xf, h = _mm_norm_res(o, wo_b, l, xf, ln_mem_post[l], ln_ffn_pre[l])
        act = _ffn_up(h, ffn_w_up, l, ffn_conv_w[l], ffn_conv_b[l], s)
        g_next = ln_mix_pre[l + 1] if l + 1 < DEPTH else None
        xf, h = _mm_norm_res(act, w_down_b, l, xf, ln_ffn_post[l], g_next)
    return xf.reshape(bsz, s, d)
```

```python
import functools
import math

import jax
import jax.numpy as jnp
import numpy as np
from jax import lax
from jax.experimental import pallas as pl
from jax.experimental.pallas import tpu as pltpu

D_MODEL = 2048
DEPTH = 2
MEM_HEADS = 4
MEM_HEAD_DIM = D_MODEL // MEM_HEADS
SC_W = D_MODEL // 4
SC_K = 3
RW_HEAD_DIM = 64
RW_W = D_MODEL // 4
RW_HEADS = RW_W // RW_HEAD_DIM
RW_LORA_W = 96
RW_LORA_A = 96
RW_LORA_G = 256
RW_DECAY_SCALE = math.exp(-0.5)
RW_LN_EPS = 64e-5
RET_HEADS = 4
RET_DK = 128
RET_DV = 2 * RET_DK
RET_CHUNK = 128
RET_LN_EPS = 1e-5
NSA_HEADS = 8
NSA_KV_HEADS = 2
NSA_DH = 64
NSA_GROUP = NSA_HEADS // NSA_KV_HEADS
CMP_LEN = 32
CMP_STRIDE = 16
SEL_BLOCK = 64
SEL_TOP = 16
WINDOW = 512
Q_BLOCK = 256
ROPE_THETA = 10000.0
EPS = 1e-6
N_BRANCH = 4

SC_SIZES = (SC_W, SC_W, SC_W)
RW_SIZES = (RW_W, RW_W, RW_W, RW_LORA_W, RW_LORA_A, RW_LORA_G)
RET_SIZES = (RET_HEADS * RET_DK, RET_HEADS * RET_DK, RET_HEADS * RET_DV, RET_HEADS * RET_DV)
NSA_SIZES = (NSA_HEADS * NSA_DH,) + (NSA_KV_HEADS * NSA_DH,) * 6 + (NSA_HEADS * 3,)
GATE_SIZES = (D_MODEL,) * N_BRANCH
GROUP_SIZES = (sum(SC_SIZES), sum(RW_SIZES), sum(RET_SIZES), sum(NSA_SIZES), sum(GATE_SIZES))
BR_SIZES = (SC_W, RW_W, RET_HEADS * RET_DV, NSA_HEADS * NSA_DH)

V7X_VMEM_LIMIT_BYTES = 48 * 1024 * 1024
LANES = 128
SUBLANES = 8
_NT = (((1,), (1,)), ((), ()))


def _pick_tile(n, candidates):
    for c in candidates:
        if n % c == 0:
            return c
    return n


def _bdot(a, b):
    return jnp.dot(a.astype(jnp.bfloat16), b.astype(jnp.bfloat16), preferred_element_type=jnp.float32)


def _bdot_nt(a, b):
    return lax.dot_general(a.astype(jnp.bfloat16), b.astype(jnp.bfloat16), _NT,
                           preferred_element_type=jnp.float32)


def _head_sum(x, bd):
    hi = x.astype(jnp.bfloat16)
    lo = (x - hi.astype(jnp.float32)).astype(jnp.bfloat16)
    return (jnp.dot(hi, bd, preferred_element_type=jnp.float32)
            + jnp.dot(lo, bd, preferred_element_type=jnp.float32))


def _mm_kernel(a_ref, w_ref, o_ref, acc_ref):
    k = pl.program_id(2)

    @pl.when(k == 0)
    def _():
        acc_ref[...] = jnp.zeros_like(acc_ref)

    acc_ref[...] += lax.dot_general(a_ref[...], w_ref[...], _NT, preferred_element_type=jnp.float32)

    @pl.when(k == pl.num_programs(2) - 1)
    def _():
        o_ref[...] = acc_ref[...]


def _mm_cols(a, wt_stack, layer, col0, n):
    m, k = a.shape
    tn = next(c for c in (n if n <= 1536 else 1024, 1024, 768, 512, 256, 128) if n % c == 0 and col0 % c == 0)
    tm = _pick_tile(m, (1024, 512, 256, 128, 8))
    tk = _pick_tile(k, (2048, 1280, 1408, 1024, 512, 256, 128))
    j0 = col0 // tn
    return pl.pallas_call(
        _mm_kernel,
        out_shape=jax.ShapeDtypeStruct((m, n), jnp.float32),
        grid=(m // tm, n // tn, k // tk),
        in_specs=[pl.BlockSpec((tm, tk), lambda i, j, kk: (i, kk)),
                  pl.BlockSpec((None, tn, tk), lambda i, j, kk: (layer, j0 + j, kk))],
        out_specs=pl.BlockSpec((tm, tn), lambda i, j, kk: (i, j)),
        scratch_shapes=[pltpu.VMEM((tm, tn), jnp.float32)],
        compiler_params=pltpu.CompilerParams(
            dimension_semantics=("parallel", "parallel", "arbitrary"),
            vmem_limit_bytes=V7X_VMEM_LIMIT_BYTES),
        name="mm",
    )(a.astype(jnp.bfloat16), wt_stack)


def _rms(y, g):
    return y * lax.rsqrt(jnp.mean(y * y, axis=-1, keepdims=True) + EPS) * g


def _prenorm_kernel(x_ref, g_ref, h_ref):
    h_ref[...] = _rms(x_ref[...], g_ref[...]).astype(h_ref.dtype)


def _prenorm(x, g):
    m, d = x.shape
    tm = _pick_tile(m, (512, 256, 128, 8))
    return pl.pallas_call(
        _prenorm_kernel,
        out_shape=jax.ShapeDtypeStruct((m, d), jnp.bfloat16),
        grid=(m // tm,),
        in_specs=[pl.BlockSpec((tm, d), lambda i: (i, 0)), pl.BlockSpec((1, d), lambda i: (0, 0))],
        out_specs=pl.BlockSpec((tm, d), lambda i: (i, 0)),
        compiler_params=pltpu.CompilerParams(dimension_semantics=("parallel",)),
        name="prenorm",
    )(x, g.reshape(1, d))


def _mm_norm_res_kernel(emit_h, whole_k, a_ref, w_ref, res_ref, g_ref, *rest):
    if emit_h:
        g2_ref, x_ref, h_ref = rest[:3]
    else:
        x_ref = rest[0]

    def finish(y):
        x_new = res_ref[...] + _rms(y, g_ref[...])
        x_ref[...] = x_new
        if emit_h:
            h_ref[...] = _rms(x_new, g2_ref[...]).astype(h_ref.dtype)

    if whole_k:
        finish(jnp.dot(a_ref[...], w_ref[...], preferred_element_type=jnp.float32))
        return
    acc_ref = rest[-1]
    k = pl.program_id(1)

    @pl.when(k == 0)
    def _():
        acc_ref[...] = jnp.zeros_like(acc_ref)

    acc_ref[...] += jnp.dot(a_ref[...], w_ref[...], preferred_element_type=jnp.float32)

    @pl.when(k == pl.num_programs(1) - 1)
    def _():
        finish(acc_ref[...])


def _mm_norm_res(a, w_stack, layer, res, g, g_next):
    m, k = a.shape
    d = w_stack.shape[2]
    tk = _pick_tile(k, (2048, 1408, 1024, 512))
    tm = 512
    whole_k = tk == k
    emit_h = g_next is not None
    row = pl.BlockSpec((tm, d), lambda i, kk: (i, 0))
    vec = pl.BlockSpec((1, d), lambda i, kk: (0, 0))
    w_spec = (pl.BlockSpec((None, tk, d), lambda i, kk: (layer, 0, 0), pipeline_mode=pl.Buffered(1)) if whole_k
              else pl.BlockSpec((None, tk, d), lambda i, kk: (layer, kk, 0)))
    x_sd = jax.ShapeDtypeStruct((m, d), jnp.float32)
    h_sd = jax.ShapeDtypeStruct((m, d), jnp.bfloat16)
    out = pl.pallas_call(
        functools.partial(_mm_norm_res_kernel, emit_h, whole_k),
        out_shape=(x_sd, h_sd) if emit_h else x_sd,
        grid=(m // tm, k // tk),
        in_specs=[pl.BlockSpec((tm, tk), lambda i, kk: (i, kk)), w_spec, row, vec] + ([vec] if emit_h else []),
        out_specs=(row, row) if emit_h else row,
        scratch_shapes=[] if whole_k else [pltpu.VMEM((tm, d), jnp.float32)],
        compiler_params=pltpu.CompilerParams(
            dimension_semantics=("parallel", "arbitrary"), vmem_limit_bytes=V7X_VMEM_LIMIT_BYTES),
        name="mm_norm_res",
    )(a.astype(jnp.bfloat16), w_stack, res, g.reshape(1, d), *([g_next.reshape(1, d)] if emit_h else []))
    return out if emit_h else (out, None)


FFN_TM = 1024
FFN_TN = 512
HALO = SUBLANES


def _shift_rows(u, prev, n):
    rolled = pltpu.roll(u, n, axis=0)
    rows = lax.broadcasted_iota(jnp.int32, u.shape, 0)
    for r in range(n):
        rolled = jnp.where(rows == r, prev[HALO - n + r:HALO - n + r + 1, :], rolled)
    return rolled


def _ffn_up_kernel(seq_len, h_ref, halo_ref, wa_ref, wb_ref, cwa_ref, cwb_ref, ba_ref, bb_ref, o_ref):
    i = pl.program_id(0)
    h = h_ref[...]
    at_start = (i * FFN_TM) % seq_len == 0
    halo = jnp.where(at_start, jnp.zeros_like(halo_ref[...]), halo_ref[...])

    def conv(w_ref, cw_ref, b_ref):
        w = w_ref[...].astype(jnp.bfloat16)
        u = jnp.dot(h, w, preferred_element_type=jnp.float32)
        up = jnp.dot(halo, w, preferred_element_type=jnp.float32)
        cw = cw_ref[...]
        y = cw[0:1] * _shift_rows(u, up, 2)
        y = y + cw[1:2] * _shift_rows(u, up, 1)
        y = y + cw[2:3] * u
        return y + b_ref[...]

    a = conv(wa_ref, cwa_ref, ba_ref)
    b = conv(wb_ref, cwb_ref, bb_ref)
    o_ref[...] = (jax.nn.gelu(a, approximate=True) * b).astype(o_ref.dtype)


def _ffn_up(h, w_up_stack, layer, w_conv, b_conv, seq_len):
    m, d = h.shape
    f = w_up_stack.shape[2] // 2
    nj = f // FFN_TN
    b_conv = b_conv.reshape(1, 2 * f)
    tiles_per_halo = FFN_TM // HALO
    return pl.pallas_call(
        functools.partial(_ffn_up_kernel, seq_len),
        out_shape=jax.ShapeDtypeStruct((m, f), jnp.bfloat16),
        grid=(m // FFN_TM, nj),
        in_specs=[pl.BlockSpec((FFN_TM, d), lambda i, j: (i, 0)),
                  pl.BlockSpec((HALO, d), lambda i, j: (jnp.maximum(i * tiles_per_halo - 1, 0), 0)),
                  pl.BlockSpec((None, d, FFN_TN), lambda i, j: (layer, 0, j)),
                  pl.BlockSpec((None, d, FFN_TN), lambda i, j: (layer, 0, j + nj)),
                  pl.BlockSpec((3, FFN_TN), lambda i, j: (0, j)),
                  pl.BlockSpec((3, FFN_TN), lambda i, j: (0, j + nj)),
                  pl.BlockSpec((1, FFN_TN), lambda i, j: (0, j)),
                  pl.BlockSpec((1, FFN_TN), lambda i, j: (0, j + nj))],
        out_specs=pl.BlockSpec((FFN_TM, FFN_TN), lambda i, j: (i, j)),
        compiler_params=pltpu.CompilerParams(
            dimension_semantics=("parallel", "arbitrary"), vmem_limit_bytes=V7X_VMEM_LIMIT_BYTES),
        name="ffn_up",
    )(h, h, w_up_stack, w_up_stack, w_conv, w_conv, b_conv, b_conv)


MERGE_TM = 512
MERGE_TN = 512


def _merge_kernel(h_ref, wg0, wg1, wg2, wg3, o0, o1, o2, o3, wb0, wb1, wb2, wb3, out_ref):
    h = h_ref[...]
    acc = None
    for wg, o, wb in ((wg0, o0, wb0), (wg1, o1, wb1), (wg2, o2, wb2), (wg3, o3, wb3)):
        gate = jax.nn.sigmoid(lax.dot_general(h, wg[...], _NT, preferred_element_type=jnp.float32))
        term = gate * jnp.dot(o[...], wb[...], preferred_element_type=jnp.float32)
        acc = term if acc is None else acc + term
    out_ref[...] = acc.astype(out_ref.dtype)


def _merge(h, w_stack, layer, gate_col0, outs, wb_stack):
    m, d = h.shape
    nj = d // MERGE_TN
    j0 = gate_col0 // MERGE_TN
    assert gate_col0 % MERGE_TN == 0
    gate_specs = [pl.BlockSpec((None, MERGE_TN, d), functools.partial(lambda b, i, j: (layer, j0 + b * nj + j, 0), b))
                  for b in range(N_BRANCH)]
    o_specs = [pl.BlockSpec((MERGE_TM, o.shape[1]), lambda i, j: (i, 0)) for o in outs]
    row0 = np.cumsum([0] + [o.shape[1] for o in outs])
    assert all(r % o.shape[1] == 0 for r, o in zip(row0, outs))
    wb_specs = [pl.BlockSpec((None, o.shape[1], MERGE_TN),
                             functools.partial(lambda rb, i, j: (layer, rb, j), int(r) // o.shape[1]))
                for r, o in zip(row0, outs)]
    return pl.pallas_call(
        _merge_kernel,
        out_shape=jax.ShapeDtypeStruct((m, d), jnp.bfloat16),
        grid=(m // MERGE_TM, nj),
        in_specs=[pl.BlockSpec((MERGE_TM, d), lambda i, j: (i, 0))] + gate_specs + o_specs + wb_specs,
        out_specs=pl.BlockSpec((MERGE_TM, MERGE_TN), lambda i, j: (i, j)),
        compiler_params=pltpu.CompilerParams(
            dimension_semantics=("parallel", "arbitrary"), vmem_limit_bytes=V7X_VMEM_LIMIT_BYTES),
        name="merge",
    )(h, *([w_stack] * N_BRANCH), *[o.astype(jnp.bfloat16) for o in outs], *([wb_stack] * N_BRANCH))


XATT_TM = 512


def _xattn_kernel(h_ref, wq_ref, k_ref, v_ref, o_ref):
    q = jnp.dot(h_ref[...], wq_ref[...], preferred_element_type=jnp.float32)
    q = (q * (MEM_HEAD_DIM ** -0.5)).astype(jnp.bfloat16)
    for hh in range(MEM_HEADS):
        sl = slice(hh * MEM_HEAD_DIM, (hh + 1) * MEM_HEAD_DIM)
        s = lax.dot_general(q[:, sl], k_ref[0, :, sl], _NT, preferred_element_type=jnp.float32)
        p = jnp.exp(s - jnp.max(s, axis=-1, keepdims=True))
        p = p / jnp.sum(p, axis=-1, keepdims=True)
        o_ref[:, sl] = jnp.dot(p.astype(jnp.bfloat16), v_ref[0, :, sl],
                               preferred_element_type=jnp.float32).astype(o_ref.dtype)


def _xattn(h, wq_stack, layer, k, v, seq_len):
    m, d = h.shape
    mem_len = k.shape[1]
    per_batch = seq_len // XATT_TM
    kv_spec = pl.BlockSpec((1, mem_len, d), lambda i: (i // per_batch, 0, 0))
    return pl.pallas_call(
        _xattn_kernel,
        out_shape=jax.ShapeDtypeStruct((m, d), jnp.bfloat16),
        grid=(m // XATT_TM,),
        in_specs=[pl.BlockSpec((XATT_TM, d), lambda i: (i, 0)),
                  pl.BlockSpec((None, d, d), lambda i: (layer, 0, 0)), kv_spec, kv_spec],
        out_specs=pl.BlockSpec((XATT_TM, d), lambda i: (i, 0)),
        compiler_params=pltpu.CompilerParams(
            dimension_semantics=("parallel",), vmem_limit_bytes=V7X_VMEM_LIMIT_BYTES),
        name="xattn",
    )(h, wq_stack, k, v)


def _rope_freqs(d):
    return ROPE_THETA ** (-jnp.arange(0, d, 2, dtype=jnp.float32) / d)


def _retnet_freqs(d):
    return 1.0 / (ROPE_THETA ** jnp.linspace(0.0, 1.0, d // 2, dtype=jnp.float32))


WKV_CHUNK = 64


def _rwkv7_kernel(z_ref, mu_ref, w0_ref, a0_ref, kk_ref, ka_ref, rk_ref, lng_ref, lnb_ref,
                  ww2_ref, wa2_ref, wg2_ref, o_ref, s_ref, prev_ref, y_ref):
    c = WKV_CHUNK
    hd = RW_HEAD_DIM
    w = RW_W
    nb = z_ref.shape[0]
    f32, bf16 = jnp.float32, jnp.bfloat16

    @pl.when(pl.program_id(0) == 0)
    def _():
        s_ref[...] = jnp.zeros_like(s_ref)
        prev_ref[...] = jnp.zeros_like(prev_ref)

    lane_h = lax.broadcasted_iota(jnp.int32, (w, w), 0) // hd
    lane_w = lax.broadcasted_iota(jnp.int32, (w, w), 1) // hd
    head_bd = jnp.where(lane_h == lane_w, 1.0, 0.0).astype(bf16)
    row = lax.broadcasted_iota(jnp.int32, (c, c), 0)
    col = lax.broadcasted_iota(jnp.int32, (c, c), 1)
    tri = jnp.where(row >= col, 1.0, 0.0).astype(bf16)
    eye = jnp.where(row == col, 1.0, 0.0).astype(f32)
    r2 = lax.broadcasted_iota(jnp.int32, (2 * c, 2 * c), 0)
    c2 = lax.broadcasted_iota(jnp.int32, (2 * c, 2 * c), 1)
    rr = jnp.where(r2 >= c, r2 - c, r2)
    cc = jnp.where(c2 >= c, c2 - c, c2)
    tri_mask = cc < rr + jnp.where(r2 >= c, 1, 0)

    def prepare(b):
        z = z_ref[b]
        rows = lax.broadcasted_iota(jnp.int32, z.shape, 0)
        z_prev = jnp.where(rows == 0, prev_ref[b], pltpu.roll(z, 1, axis=0))
        prev_ref[b] = z[c - 1:c, :]
        z = z + (z_prev - z) * mu_ref[...]
        r, k, v, lora = z[:, 0:w], z[:, w:2 * w], z[:, 2 * w:3 * w], z[:, 3 * w:4 * w]
        logw = -RW_DECAY_SCALE * jax.nn.sigmoid(w0_ref[...] + _bdot(jnp.tanh(lora), ww2_ref[...]))
        a = jax.nn.sigmoid(a0_ref[...] + _bdot(lora, wa2_ref[...]))
        g = _bdot(jax.nn.sigmoid(lora), wg2_ref[...])
        kk = k * kk_ref[...]
        kk = kk / jnp.maximum(jnp.sqrt(_head_sum(kk * kk, head_bd)), 1e-12)
        k = k * (1.0 + (a - 1.0) * ka_ref[...])
        hi = logw.astype(bf16)
        rem = logw - hi.astype(f32)
        mid = rem.astype(bf16)
        lo = (rem - mid.astype(f32)).astype(bf16)
        cum = (jnp.dot(tri, hi, preferred_element_type=f32) + jnp.dot(tri, mid, preferred_element_type=f32)
               + jnp.dot(tri, lo, preferred_element_type=f32))
        kka = kk * a
        e_neg = jnp.exp(-cum)
        last = cum[c - 1:c, :]
        dec = jnp.exp(last - cum)
        return dict(r=r, k=k, v=v, g=g, vb=v.astype(bf16), g_last=jnp.exp(last),
                    qh=(kk * jnp.exp(cum - logw)).astype(bf16), rh=(r * jnp.exp(cum)).astype(bf16),
                    bh=(kka * e_neg).astype(bf16), kh=(k * e_neg).astype(bf16),
                    bd=(kka * dec).astype(bf16), kd=(k * dec).astype(bf16))

    pre = [prepare(b) for b in range(nb)]

    units = [(b, h) for b in range(nb) for h in range(RW_HEADS)]
    n_units = range(len(units))
    part = lambda name, i: pre[units[i][0]][name][:, units[i][1] * hd:(units[i][1] + 1) * hd]
    a1 = [jnp.concatenate([part("qh", i), part("rh", i)], axis=0) for i in n_units]
    b1 = [jnp.concatenate([part("bh", i), part("kh", i)], axis=0) for i in n_units]
    ss = [jnp.where(tri_mask, lax.dot_general(a1[i], b1[i], _NT, preferred_element_type=f32), 0.0)
          for i in n_units]
    s_old = [s_ref[i] for i in n_units]
    qr = [lax.dot_general(a1[i], s_old[i].astype(bf16), _NT, preferred_element_type=f32) for i in n_units]
    lm = [_bdot(ss[i][:, c:2 * c], part("vb", i)) for i in n_units]
    n = [-ss[i][0:c, 0:c] for i in n_units]
    t = [eye + n[i] for i in n_units]
    for _ in range(5):
        n = [_bdot(n[i], n[i]) for i in n_units]
        t = [t[i] + _bdot(t[i], n[i]) for i in n_units]
    u = [-_bdot(t[i], qr[i][0:c] + lm[i][0:c]) for i in n_units]
    for i in n_units:
        b, h = units[i]
        y_ref[b, :, h * hd:(h + 1) * hd] = qr[i][c:2 * c] + lm[i][c:2 * c] + _bdot(ss[i][c:2 * c, 0:c], u[i])
    for i in n_units:
        zt = jnp.concatenate([u[i], part("v", i)], axis=0).T
        x = jnp.concatenate([part("bd", i), part("kd", i)], axis=0)
        s_ref[i] = s_old[i] * part("g_last", i) + _bdot(zt, x)

    for b in range(nb):
        p = pre[b]
        y = y_ref[b]
        mean = _head_sum(y, head_bd) * (1.0 / hd)
        yc = y - mean
        var = _head_sum(yc * yc, head_bd) * (1.0 / hd)
        yn = yc * lax.rsqrt(var + RW_LN_EPS) * lng_ref[...] + lnb_ref[...]
        bonus = _head_sum(p["r"] * p["k"] * rk_ref[...], head_bd) * p["v"]
        o_ref[b] = ((yn + bonus) * p["g"]).astype(o_ref.dtype)


def _rwkv7_mixer(z, mu, w0, w_w2, a0, w_a2, w_g2, k_k, k_a, r_k, lnx_g, lnx_b):
    bsz, s, zw = z.shape
    w = RW_W
    n_lora = RW_LORA_W + RW_LORA_A + RW_LORA_G
    row = lambda t: t.reshape(1, -1)
    mu = jnp.pad(mu, (0, zw - mu.shape[0])).reshape(1, zw)
    o_w, o_a = RW_LORA_W, RW_LORA_W + RW_LORA_A
    pad_rows = lambda m, lo: jnp.pad(m, ((lo, w - lo - m.shape[0]), (0, 0))).astype(jnp.bfloat16)
    vec = pl.BlockSpec((1, w), lambda c: (0, 0))
    mat = pl.BlockSpec((w, w), lambda c: (0, 0))
    assert zw == 4 * w and n_lora <= w
    return pl.pallas_call(
        _rwkv7_kernel,
        out_shape=jax.ShapeDtypeStruct((bsz, s, w), jnp.bfloat16),
        grid=(s // WKV_CHUNK,),
        in_specs=[pl.BlockSpec((bsz, WKV_CHUNK, zw), lambda c: (0, c, 0)),
                  pl.BlockSpec((1, zw), lambda c: (0, 0))] + [vec] * 7 + [mat] * 3,
        out_specs=pl.BlockSpec((bsz, WKV_CHUNK, w), lambda c: (0, c, 0)),
        scratch_shapes=[pltpu.VMEM((bsz * RW_HEADS, RW_HEAD_DIM, RW_HEAD_DIM), jnp.float32),
                        pltpu.VMEM((bsz, 1, zw), jnp.float32),
                        pltpu.VMEM((bsz, WKV_CHUNK, w), jnp.float32)],
        compiler_params=pltpu.CompilerParams(dimension_semantics=("arbitrary",)),
        name="rwkv7",
    )(z, mu, row(w0), row(a0), row(k_k), row(k_a), row(r_k), row(lnx_g), row(lnx_b),
      pad_rows(w_w2, 0), pad_rows(w_a2, o_w), pad_rows(w_g2, o_a))


RET_LOG_DECAY = tuple(math.log(1.0 - 2.0 ** (-5.0 - h)) for h in range(RET_HEADS))


def _retention_kernel(q_ref, k_ref, v_ref, g_ref, cos_ref, sin_ref, o_ref, s_ref):
    c, dk, dv = RET_CHUNK, RET_DK, RET_DV
    f32 = jnp.float32

    @pl.when(pl.program_id(1) == 0)
    def _():
        s_ref[...] = jnp.zeros_like(s_ref)

    cos = cos_ref[0]
    sin = sin_ref[0]
    diff = (lax.broadcasted_iota(jnp.int32, (c, c), 0) - lax.broadcasted_iota(jnp.int32, (c, c), 1)).astype(f32)
    tok = lax.broadcasted_iota(jnp.int32, (c, 1), 0).astype(f32)
    rope = lambda x: x * cos + pltpu.roll(x, dk // 2, axis=1) * sin
    for h in range(RET_HEADS):
        lg = RET_LOG_DECAY[h]
        q = rope(q_ref[0, :, h * dk:(h + 1) * dk])
        k = rope(k_ref[0, :, h * dk:(h + 1) * dk]) * (dk ** -0.5)
        v = v_ref[0, :, h * dv:(h + 1) * dv]
        decay_in = jnp.where(diff >= 0, jnp.exp(jnp.maximum(diff, 0.0) * lg), 0.0)
        scores = _bdot_nt(q, k) * decay_in
        state = s_ref[h]
        o = _bdot(scores, v) + _bdot(q * jnp.exp((tok + 1.0) * lg), state)
        s_ref[h] = state * math.exp(c * lg) + _bdot((k * jnp.exp((c - 1.0 - tok) * lg)).T, v)
        mu = jnp.mean(o, axis=-1, keepdims=True)
        oc = o - mu
        var = jnp.mean(oc * oc, axis=-1, keepdims=True)
        gate = g_ref[0, :, h * dv:(h + 1) * dv]
        o_ref[0, :, h * dv:(h + 1) * dv] = (oc * lax.rsqrt(var + RET_LN_EPS) * gate * jax.nn.sigmoid(gate)
                                            ).astype(o_ref.dtype)


def _retention_mixer(z, cos, sin):
    bsz, s, _ = z.shape
    qk_w, v_w = RET_HEADS * RET_DK, RET_HEADS * RET_DV
    c = RET_CHUNK
    tab = pl.BlockSpec((1, c, RET_DK), lambda b, i: (b, i, 0))
    return pl.pallas_call(
        _retention_kernel,
        out_shape=jax.ShapeDtypeStruct((bsz, s, v_w), jnp.bfloat16),
        grid=(bsz, s // c),
        in_specs=[pl.BlockSpec((1, c, qk_w), lambda b, i: (b, i, 0)),
                  pl.BlockSpec((1, c, qk_w), lambda b, i: (b, i, 1)),
                  pl.BlockSpec((1, c, v_w), lambda b, i: (b, i, 1)),
                  pl.BlockSpec((1, c, v_w), lambda b, i: (b, i, 2)), tab, tab],
        out_specs=pl.BlockSpec((1, c, v_w), lambda b, i: (b, i, 0)),
        scratch_shapes=[pltpu.VMEM((RET_HEADS, RET_DK, RET_DV), jnp.float32)],
        compiler_params=pltpu.CompilerParams(dimension_semantics=("parallel", "arbitrary")),
        name="retention",
    )(z, z, z, z, cos, sin)


TRIG_TM = 1024


def _trig_kernel(pos_ref, freq_ref, rc_ref, rs_ref, nc_ref, ns_ref):
    ang = pos_ref[...] * freq_ref[...]
    c, sn = jnp.cos(ang), jnp.sin(ang)
    n_ret, n_nsa = RET_DK // 2, NSA_DH // 2
    rc_ref[...] = jnp.concatenate([c[:, :n_ret]] * 2, axis=1)
    rs_ref[...] = jnp.concatenate([-sn[:, :n_ret], sn[:, :n_ret]], axis=1)
    nsa_c, nsa_s = c[:, n_ret:n_ret + n_nsa], sn[:, n_ret:n_ret + n_nsa]
    nc_ref[...] = jnp.concatenate([nsa_c, nsa_c] * NSA_HEADS, axis=1)
    ns_ref[...] = jnp.concatenate([-nsa_s, nsa_s] * NSA_HEADS, axis=1)


def _rope_tables(positions):
    bsz, s = positions.shape
    m = bsz * s
    n_ret, n_nsa = RET_DK // 2, NSA_DH // 2
    assert n_ret + n_nsa <= LANES
    freq = jnp.concatenate([_retnet_freqs(RET_DK), _rope_freqs(NSA_DH),
                            jnp.zeros((LANES - n_ret - n_nsa,), jnp.float32)]).reshape(1, LANES)
    tm = _pick_tile(m, (TRIG_TM, 512, 256, 128, 8))
    spec = lambda w: pl.BlockSpec((tm, w), lambda i: (i, 0))
    sd = lambda w: jax.ShapeDtypeStruct((m, w), jnp.float32)
    nsa_w = NSA_HEADS * NSA_DH
    tabs = pl.pallas_call(
        _trig_kernel,
        out_shape=(sd(RET_DK), sd(RET_DK), sd(nsa_w), sd(nsa_w)),
        grid=(m // tm,),
        in_specs=[spec(1), pl.BlockSpec((1, LANES), lambda i: (0, 0))],
        out_specs=(spec(RET_DK), spec(RET_DK), spec(nsa_w), spec(nsa_w)),
        compiler_params=pltpu.CompilerParams(dimension_semantics=("parallel",)),
        name="rope_tables",
    )(positions.astype(jnp.float32).reshape(m, 1), freq)
    return [t.reshape(bsz, s, t.shape[1]) for t in tabs]


SC_TM = 512


def _short_conv_kernel(seq_len, b_ref, c_ref, x_ref, ch_ref, xh_ref, w_ref, o_ref):
    at_start = (pl.program_id(0) * SC_TM) % seq_len == 0
    u = c_ref[...] * x_ref[...]
    up = jnp.where(at_start, 0.0, ch_ref[...] * xh_ref[...])
    w = w_ref[...]
    y = w[0:1] * _shift_rows(u, up, 2) + w[1:2] * _shift_rows(u, up, 1) + w[2:3] * u
    o_ref[...] = (b_ref[...] * y).astype(o_ref.dtype)


def _short_conv_mixer(z, w_conv, seq_len):
    m = z.shape[0]
    w = SC_W
    per_halo = SC_TM // HALO
    tile = lambda j: pl.BlockSpec((SC_TM, w), lambda i: (i, j))
    halo = lambda j: pl.BlockSpec((HALO, w), lambda i: (jnp.maximum(i * per_halo - 1, 0), j))
    return pl.pallas_call(
        functools.partial(_short_conv_kernel, seq_len),
        out_shape=jax.ShapeDtypeStruct((m, w), jnp.bfloat16),
        grid=(m // SC_TM,),
        in_specs=[tile(0), tile(1), tile(2), halo(1), halo(2), pl.BlockSpec((SC_K, w), lambda i: (0, 0))],
        out_specs=pl.BlockSpec((SC_TM, w), lambda i: (i, 0)),
        compiler_params=pltpu.CompilerParams(dimension_semantics=("parallel",)),
        name="short_conv",
    )(z, z, z, z, z, w_conv)


NSA_KEY_TILE = 512
NSA_WIN_SPAN = WINDOW + Q_BLOCK
NSA_PREP_TM = 512
NEG_BIG = -1e30
NSA_Q_W = NSA_HEADS * NSA_DH
NSA_KV_W = NSA_KV_HEADS * NSA_DH
NSA_GATE_OFF = NSA_Q_W + 6 * NSA_KV_W
NSA_VAL_W = LANES


def _rope_lanes(x, cos, sin):
    width = x.shape[1]
    half = NSA_DH // 2
    lane = lax.broadcasted_iota(jnp.int32, x.shape, 1)
    other = jnp.where(lane % NSA_DH < half, pltpu.roll(x, width - half, axis=1), pltpu.roll(x, half, axis=1))
    return x * cos + other * sin


def _nsa_prep_kernel(z_ref, cos_ref, sin_ref, q_ref, ks_ref, vs_ref, kw_ref, vw_ref, kc_ref, vc_ref,
                     gc_ref, gs_ref, gw_ref):
    cos, sin = cos_ref[0], sin_ref[0]
    z = z_ref[0]
    q_ref[0] = (_rope_lanes(z[:, 0:NSA_Q_W], cos, sin) * (NSA_DH ** -0.5)).astype(q_ref.dtype)
    kv = lambda i: z[:, NSA_Q_W + i * NSA_KV_W:NSA_Q_W + (i + 1) * NSA_KV_W]
    cos_kv, sin_kv = cos[:, 0:NSA_KV_W], sin[:, 0:NSA_KV_W]
    pieces = ((kc_ref, kv(0)), (vc_ref, kv(1)), (ks_ref, _rope_lanes(kv(2), cos_kv, sin_kv)), (vs_ref, kv(3)),
              (kw_ref, _rope_lanes(kv(4), cos_kv, sin_kv)), (vw_ref, kv(5)))
    n_sel = ks_ref.shape[3] - NSA_DH
    tok = pl.program_id(1) * NSA_PREP_TM + lax.broadcasted_iota(jnp.int32, (NSA_PREP_TM, n_sel), 0)
    block_1hot = jnp.where(tok // SEL_BLOCK == lax.broadcasted_iota(jnp.int32, (NSA_PREP_TM, n_sel), 1), 1.0, 0.0)
    ones_col = jnp.where(lax.broadcasted_iota(jnp.int32, (NSA_PREP_TM, NSA_VAL_W - NSA_DH), 1) == 0, 1.0, 0.0)
    for ref, val in pieces:
        for g in range(NSA_KV_HEADS):
            val_g = val[:, g * NSA_DH:(g + 1) * NSA_DH]
            if ref is ks_ref:
                val_g = jnp.concatenate([val_g, block_1hot], axis=1)
            elif ref is vs_ref or ref is vw_ref:
                val_g = jnp.concatenate([val_g, ones_col], axis=1)
            ref[0, g] = val_g.astype(ref.dtype)
    gate = jax.nn.sigmoid(z[:, NSA_GATE_OFF:NSA_GATE_OFF + LANES])
    src = lax.broadcasted_iota(jnp.int32, (LANES, NSA_Q_W), 0)
    head = lax.broadcasted_iota(jnp.int32, (LANES, NSA_Q_W), 1) // NSA_DH
    for j, ref in enumerate((gc_ref, gs_ref, gw_ref)):
        expand = jnp.where(src == 3 * head + j, 1.0, 0.0).astype(jnp.bfloat16)
        ref[0] = _head_sum(gate, expand)


def _nsa_prep(z, cos, sin):
    bsz, s, zw = z.shape
    tm = NSA_PREP_TM
    wide = pl.BlockSpec((1, tm, NSA_Q_W), lambda b, i: (b, i, 0))
    grp = pl.BlockSpec((1, NSA_KV_HEADS, tm, NSA_DH), lambda b, i: (b, 0, i, 0))
    grp_sd = lambda dt: jax.ShapeDtypeStruct((bsz, NSA_KV_HEADS, s, NSA_DH), dt)
    ks_w = NSA_DH + s // SEL_BLOCK
    ks_spec = pl.BlockSpec((1, NSA_KV_HEADS, tm, ks_w), lambda b, i: (b, 0, i, 0))
    ks_sd = jax.ShapeDtypeStruct((bsz, NSA_KV_HEADS, s, ks_w), jnp.bfloat16)
    val_spec = pl.BlockSpec((1, NSA_KV_HEADS, tm, NSA_VAL_W), lambda b, i: (b, 0, i, 0))
    val_sd = jax.ShapeDtypeStruct((bsz, NSA_KV_HEADS, s, NSA_VAL_W), jnp.bfloat16)
    wide_sd = lambda dt: jax.ShapeDtypeStruct((bsz, s, NSA_Q_W), dt)
    bf16, f32 = jnp.bfloat16, jnp.float32
    return pl.pallas_call(
        _nsa_prep_kernel,
        out_shape=(wide_sd(bf16), ks_sd, val_sd, grp_sd(bf16), val_sd, grp_sd(f32), grp_sd(f32),
                   wide_sd(f32), wide_sd(f32), wide_sd(f32)),
        grid=(bsz, s // tm),
        in_specs=[pl.BlockSpec((1, tm, zw), lambda b, i: (b, i, 0)), wide, wide],
        out_specs=(wide, ks_spec, val_spec, grp, val_spec, grp, grp, wide, wide, wide),
        compiler_params=pltpu.CompilerParams(dimension_semantics=("parallel", "parallel")),
        name="nsa_prep",
    )(z, cos, sin)


def _nsa_compress_kernel(hk_ref, hv_ref, pos_ref, w1_ref, w2_ref, cos_ref, sin_ref, kc_ref, vc_ref):
    half = w1_ref.shape[1] // 2
    n = hk_ref.shape[1]
    for i, (h_ref, o_ref) in enumerate(((hk_ref, kc_ref), (hv_ref, vc_ref))):
        h = h_ref[0]
        first = _bdot(h, w1_ref[i, 0:half, :])
        second = pltpu.roll(_bdot(h, w1_ref[i, half:2 * half, :]), n - 1, axis=0)
        bias = _bdot(jnp.broadcast_to(pos_ref[i], (SUBLANES, 2 * half)), w1_ref[i])[0:1]
        out = _bdot(jax.nn.gelu(first + second + bias, approximate=True), w2_ref[i])
        if i == 0:
            src = lax.broadcasted_iota(jnp.int32, (NSA_DH, NSA_DH), 0)
            dst = lax.broadcasted_iota(jnp.int32, (NSA_DH, NSA_DH), 1)
            swap = jnp.where(src == (dst + NSA_DH // 2) % NSA_DH, 1.0, 0.0).astype(jnp.bfloat16)
            out = out * cos_ref[0] + _head_sum(out, swap) * sin_ref[0]
        o_ref[0] = out.astype(o_ref.dtype)


def _nsa_compress(hk, hv, pos_flat, w1, w2, cos_c, sin_c):
    bg, n, hw = hk.shape
    ng = NSA_KV_HEADS
    blk = pl.BlockSpec((1, n, hw), lambda i: (i, 0, 0))
    tab = pl.BlockSpec((1, n, NSA_DH), lambda i: (i // ng, 0, 0))
    out = pl.BlockSpec((1, n, NSA_DH), lambda i: (i, 0, 0))
    whole = lambda a: pl.BlockSpec(a.shape, lambda i: (0,) * a.ndim)
    sd = jax.ShapeDtypeStruct((bg, n, NSA_DH), jnp.bfloat16)
    return pl.pallas_call(
        _nsa_compress_kernel,
        out_shape=(sd, sd),
        grid=(bg,),
        in_specs=[blk, blk, whole(pos_flat), whole(w1), whole(w2), tab, tab],
        out_specs=(out, out),
        compiler_params=pltpu.CompilerParams(
            dimension_semantics=("parallel",), vmem_limit_bytes=V7X_VMEM_LIMIT_BYTES),
        name="nsa_compress",
    )(hk, hv, pos_flat, w1, w2, cos_c, sin_c)


def _nsa_attn_kernel(q_ref, kc_ref, vc_ref, ks_ref, vs_ref, kw_ref, vw_ref, gc_ref, gs_ref, gw_ref, o_ref):
    f32, bf16 = jnp.float32, jnp.bfloat16
    qi = pl.program_id(2)
    q0 = qi * Q_BLOCK
    hpg, dh, tk = NSA_GROUP, NSA_DH, NSA_KEY_TILE
    rows = hpg * Q_BLOCK
    n_sel = ks_ref.shape[3] - dh
    n_cmp = kc_ref.shape[2]
    q_all = q_ref[0]
    q = jnp.concatenate([q_all[:, h * dh:(h + 1) * dh] for h in range(hpg)], axis=0)
    by_head = lambda o: jnp.concatenate([o[h * Q_BLOCK:(h + 1) * Q_BLOCK] for h in range(hpg)], axis=1)

    tq_c = q0 + lax.broadcasted_iota(jnp.int32, (Q_BLOCK, n_cmp), 0)
    c_end = lax.broadcasted_iota(jnp.int32, (Q_BLOCK, n_cmp), 1) * CMP_STRIDE + (CMP_LEN - 1)
    c_ok = jnp.concatenate([jnp.where(c_end <= tq_c, 1.0, 0.0)] * hpg, axis=0) > 0.5
    s = jnp.where(c_ok, lax.dot_general(q, kc_ref[0, 0], _NT, preferred_element_type=f32), NEG_BIG)
    m = jnp.max(s, axis=-1, keepdims=True)
    e = jnp.where(c_ok, jnp.exp(s - m), 0.0)
    p = e / jnp.maximum(jnp.sum(e, axis=-1, keepdims=True), 1e-30)
    o_cmp = jnp.dot(p.astype(bf16), vc_ref[0, 0], preferred_element_type=f32)

    p_sum = p[0:Q_BLOCK]
    for h in range(1, hpg):
        p_sum = p_sum + p[h * Q_BLOCK:(h + 1) * Q_BLOCK]
    c_start = lax.broadcasted_iota(jnp.int32, (n_sel, n_cmp), 1) * CMP_STRIDE
    j_blk = lax.broadcasted_iota(jnp.int32, (n_sel, n_cmp), 0)
    overlap_t = jnp.where((c_start < (j_blk + 1) * SEL_BLOCK) & (c_start + CMP_LEN > j_blk * SEL_BLOCK),
                          1.0, 0.0).astype(bf16)
    p_hi = p_sum.astype(bf16)
    p_lo = (p_sum - p_hi.astype(f32)).astype(bf16)
    imp = (lax.dot_general(overlap_t, p_hi, _NT, preferred_element_type=f32)
           + lax.dot_general(overlap_t, p_lo, _NT, preferred_element_type=f32))
    jb = lax.broadcasted_iota(jnp.int32, (n_sel, Q_BLOCK), 0)
    cur = (q0 + lax.broadcasted_iota(jnp.int32, (n_sel, Q_BLOCK), 1)) // SEL_BLOCK
    forced = (jb == 0) | (jb == cur) | (jb == cur - 1)
    imp = jnp.where(forced, -NEG_BIG, jnp.where(jb <= cur, imp, NEG_BIG))
    rank = jnp.zeros((n_sel, Q_BLOCK), f32)
    for i in range(n_sel):
        row_i = imp[i:i + 1, :]
        ahead = (row_i > imp) | ((row_i == imp) & (jb > i))
        rank = rank + jnp.where(ahead, 1.0, 0.0)
    sel_bias = jnp.where(rank < float(min(SEL_TOP, n_sel)), 0.0, NEG_BIG).T.astype(bf16)

    q_sel = jnp.concatenate([q, jnp.concatenate([sel_bias] * hpg, axis=0)], axis=1)

    def tile_update(kt, carry, on_diagonal):
        m, acc = carry
        k0 = pl.multiple_of(kt * tk, tk)
        s = lax.dot_general(q_sel, ks_ref[0, 0, pl.ds(k0, tk), :], _NT, preferred_element_type=f32)
        if on_diagonal:
            t_pos = q0 + lax.broadcasted_iota(jnp.int32, (Q_BLOCK, tk), 0)
            key = k0 + lax.broadcasted_iota(jnp.int32, (Q_BLOCK, tk), 1)
            causal = jnp.concatenate([jnp.where(key <= t_pos, 0.0, NEG_BIG)] * hpg, axis=0)
            s = s + causal
        m_new = jnp.maximum(m, jnp.max(s, axis=-1, keepdims=True))
        p = jnp.exp(s - m_new).astype(bf16)
        acc = jnp.exp(m - m_new) * acc + jnp.dot(p, vs_ref[0, 0, pl.ds(k0, tk), :], preferred_element_type=f32)
        return m_new, acc

    init = (jnp.full((rows, 1), NEG_BIG, f32), jnp.zeros((rows, NSA_VAL_W), f32))
    last_tile = (q0 + Q_BLOCK - 1) // tk
    carry = lax.fori_loop(0, last_tile, lambda kt, c: tile_update(kt, c, False), init)
    _, acc = tile_update(last_tile, carry, True)
    o_sel = acc[:, 0:dh] / jnp.maximum(acc[:, dh:dh + 1], 1e-30)

    w0 = pl.multiple_of(jnp.maximum(q0 - WINDOW, 0), Q_BLOCK)
    kwin = kw_ref[0, 0, pl.ds(w0, NSA_WIN_SPAN), :]
    vwin = vw_ref[0, 0, pl.ds(w0, NSA_WIN_SPAN), :]
    tq = q0 + lax.broadcasted_iota(jnp.int32, (Q_BLOCK, NSA_WIN_SPAN), 0)
    kp = w0 + lax.broadcasted_iota(jnp.int32, (Q_BLOCK, NSA_WIN_SPAN), 1)
    dlt = tq - kp
    wbias = jnp.where((dlt >= 0) & (dlt < WINDOW), 0.0, NEG_BIG).astype(f32)
    wbias = jnp.concatenate([wbias] * hpg, axis=0)
    s = lax.dot_general(q, kwin, _NT, preferred_element_type=f32) + wbias
    p = jnp.exp(s - jnp.max(s, axis=-1, keepdims=True)).astype(bf16)
    pv = jnp.dot(p, vwin, preferred_element_type=f32)
    o_win = pv[:, 0:dh] / jnp.maximum(pv[:, dh:dh + 1], 1e-30)

    o_ref[0] = (gc_ref[0] * by_head(o_cmp) + gs_ref[0] * by_head(o_sel) + gw_ref[0] * by_head(o_win)
                ).astype(o_ref.dtype)


def _nsa_attn(q, k_cmp, v_cmp, ks, vs, kw, vw, g_cmp, g_sel, g_win):
    bsz, s, _ = q.shape
    ng, dh = NSA_KV_HEADS, NSA_DH
    gw = NSA_GROUP * dh
    kv_spec = pl.BlockSpec((1, 1, s, dh), lambda b, g, i: (b, g, 0, 0))
    val_spec = pl.BlockSpec((1, 1, s, NSA_VAL_W), lambda b, g, i: (b, g, 0, 0))
    ks_spec = pl.BlockSpec((1, 1, s, ks.shape[3]), lambda b, g, i: (b, g, 0, 0))
    cmp_spec = pl.BlockSpec((1, 1, k_cmp.shape[2], dh), lambda b, g, i: (b, g, 0, 0))
    q_spec = pl.BlockSpec((1, Q_BLOCK, gw), lambda b, g, i: (b, i, g))
    return pl.pallas_call(
        _nsa_attn_kernel,
        out_shape=jax.ShapeDtypeStruct(q.shape, jnp.bfloat16),
        grid=(bsz, ng, s // Q_BLOCK),
        in_specs=[q_spec, cmp_spec, cmp_spec, ks_spec, val_spec, kv_spec, val_spec, q_spec, q_spec, q_spec],
        out_specs=q_spec,
        compiler_params=pltpu.CompilerParams(
            dimension_semantics=("parallel", "parallel", "arbitrary"),
            vmem_limit_bytes=V7X_VMEM_LIMIT_BYTES),
        name="nsa_attn",
    )(q, k_cmp, v_cmp, ks, vs, kw, vw, g_cmp, g_sel, g_win)


def _nsa_mixer(z, nsa_tabs, cmp_pos, cmp_w1, cmp_w2):
    bsz, s, _ = z.shape
    ng, dh = NSA_KV_HEADS, NSA_DH
    cos, sin, cos_c, sin_c = nsa_tabs
    q, ks, vs, kw, vw, kc, vc, g_cmp, g_sel, g_win = _nsa_prep(z, cos, sin)
    n_blk = s // CMP_STRIDE
    flat = lambda t: t.reshape(bsz * ng, n_blk, CMP_STRIDE * dh)
    k_cmp, v_cmp = _nsa_compress(flat(kc), flat(vc), cmp_pos.reshape(2, 1, CMP_LEN * dh), cmp_w1, cmp_w2,
                                 cos_c, sin_c)
    grouped = lambda t: t.reshape(bsz, ng, n_blk, dh)
    return _nsa_attn(q, grouped(k_cmp), grouped(v_cmp), ks, vs, kw, vw, g_cmp, g_sel, g_win)


def _nsa_tables(cos, sin):
    end = CMP_LEN - 1
    at_ends = lambda t: jnp.pad(t[:, end::CMP_STRIDE, :NSA_DH], ((0, 0), (0, 1), (0, 0)))
    return cos, sin, at_ends(cos), at_ends(sin)


IN_GROUPS = (("ret", GROUP_SIZES[0] + GROUP_SIZES[1], GROUP_SIZES[2], 3072),
             ("rw", GROUP_SIZES[0], GROUP_SIZES[1], 2048),
             ("sc", 0, GROUP_SIZES[0], 1536),
             ("nsa", sum(GROUP_SIZES[:3]), GROUP_SIZES[3], 1536),
             ("gate", sum(GROUP_SIZES[:4]), GROUP_SIZES[4], 8192))
IN_COL0 = {name: sum(g[3] for g in IN_GROUPS[:i]) for i, (name, _, _, _) in enumerate(IN_GROUPS)}
IN_WIDTH = {name: padded for name, _, _, padded in IN_GROUPS}


def _aligned_in_proj(w_in):
    w_t = jnp.swapaxes(w_in, 1, 2)
    parts = [jnp.pad(w_t[:, start:start + width], ((0, 0), (0, padded - width), (0, 0)))
             for _, start, width, padded in IN_GROUPS]
    return jnp.concatenate(parts, axis=1).astype(jnp.bfloat16)


def _hybrid_mixer(h, bsz, layer, ret_tabs, nsa_tabs, w_all, sc_conv, rw_mu, rw_w0, rw_w2, rw_a0, rw_a2, rw_g2,
                  rw_k_k, rw_k_a, rw_r_k, rw_lnx_g, rw_lnx_b, nsa_cmp_pos, nsa_cmp_w1, nsa_cmp_w2,
                  wb_stack):
    m = h.shape[0]
    s = m // bsz
    group = lambda name: _mm_cols(h, w_all, layer, IN_COL0[name], IN_WIDTH[name])
    per_batch = lambda z: z.reshape(bsz, s, z.shape[1])
    o_a = _short_conv_mixer(group("sc"), sc_conv, s)
    o_b = _rwkv7_mixer(per_batch(group("rw")), rw_mu, rw_w0, rw_w2, rw_a0, rw_a2, rw_g2, rw_k_k, rw_k_a, rw_r_k,
                       rw_lnx_g, rw_lnx_b)
    o_c = _retention_mixer(per_batch(group("ret")), *ret_tabs)
    o_d = _nsa_mixer(per_batch(group("nsa")), nsa_tabs, nsa_cmp_pos, nsa_cmp_w1, nsa_cmp_w2)
    outs = [o.reshape(m, o.shape[-1]) for o in (o_a, o_b, o_c, o_d)]
    assert tuple(o.shape[1] for o in outs) == BR_SIZES
    return _merge(h, w_all, layer, IN_COL0["gate"], outs, wb_stack)


def kernel(x, mem, positions, ln_mix_pre, ln_mix_post, ln_mem_q, ln_mem_kv, ln_mem_post, ln_ffn_pre, ln_ffn_post, w_in, sc_conv, rw_mu, rw_w0, rw_w2, rw_a0, rw_a2, rw_g2, rw_k_k, rw_k_a, rw_r_k, rw_lnx_g, rw_lnx_b, nsa_cmp_pos, nsa_cmp_w1, nsa_cmp_w2, w_branch, w_out, mem_wq, mem_wkv, mem_wo, ffn_w_up, ffn_conv_w, ffn_conv_b, ffn_w_down):
    bsz, s, d = x.shape
    mem_len = mem.shape[1]
    xf = x.reshape(bsz * s, d)
    h = _prenorm(xf, ln_mix_pre[0])
    ret_cos, ret_sin, nsa_cos, nsa_sin = _rope_tables(positions)
    ret_tabs = (ret_cos, ret_sin)
    nsa_tabs = _nsa_tables(nsa_cos, nsa_sin)
    w_all = _aligned_in_proj(w_in)
    bf16 = lambda w: w.astype(jnp.bfloat16)
    wb_b, w_out_b, wq_b, wo_b, w_down_b = bf16(w_branch), bf16(w_out), bf16(mem_wq), bf16(mem_wo), bf16(ffn_w_down)
    wkv_t = bf16(jnp.swapaxes(mem_wkv, 1, 2))
    for l in range(DEPTH):
        merged = _hybrid_mixer(h, bsz, l, ret_tabs, nsa_tabs, w_all, sc_conv[l], rw_mu[l], rw_w0[l], rw_w2[l], rw_a0[l],
                               rw_a2[l], rw_g2[l], rw_k_k[l], rw_k_a[l], rw_r_k[l], rw_lnx_g[l], rw_lnx_b[l],
                               nsa_cmp_pos[l], nsa_cmp_w1[l], nsa_cmp_w2[l], wb_b)
        xf, h = _mm_norm_res(merged, w_out_b, l, xf, ln_mix_post[l], ln_mem_q[l])
        mem_n = _prenorm(mem.reshape(bsz * mem_len, d), ln_mem_kv[l])
        kv = _mm_cols(mem_n, wkv_t, l, 0, 2 * d).astype(jnp.bfloat16)
        k_mem = kv[:, :d].reshape(bsz, mem_len, d)
        v_mem = kv[:, d:].reshape(bsz, mem_len, d)
        o = _xattn(h, wq_b, l, k_mem, v_mem, s)
        xf, h = _mm_norm_res(o, wo_b, l, xf, ln_mem_post[l], ln_ffn_pre[l])
        act = _ffn_up(h, ffn_w_up, l, ffn_conv_w[l], ffn_conv_b[l], s)
        g_next = ln_mix_pre[l + 1] if l + 1 < DEPTH else None
        xf, h = _mm_norm_res(act, w_down_b, l, xf, ln_ffn_post[l], g_next)
    return xf.reshape(bsz, s, d)
```

```python
import functools
import math

import jax
import jax.numpy as jnp
import numpy as np
from jax import lax
from jax.experimental import pallas as pl
from jax.experimental.pallas import tpu as pltpu

D_MODEL = 2048
DEPTH = 2
MEM_HEADS = 4
MEM_HEAD_DIM = D_MODEL // MEM_HEADS
SC_W = D_MODEL // 4
SC_K = 3
RW_HEAD_DIM = 64
RW_W = D_MODEL // 4
RW_HEADS = RW_W // RW_HEAD_DIM
RW_LORA_W = 96
RW_LORA_A = 96
RW_LORA_G = 256
RW_DECAY_SCALE = math.exp(-0.5)
RW_LN_EPS = 64e-5
RET_HEADS = 4
RET_DK = 128
RET_DV = 2 * RET_DK
RET_CHUNK = 128
RET_LN_EPS = 1e-5
NSA_HEADS = 8
NSA_KV_HEADS = 2
NSA_DH = 64
NSA_GROUP = NSA_HEADS // NSA_KV_HEADS
CMP_LEN = 32
CMP_STRIDE = 16
SEL_BLOCK = 64
SEL_TOP = 16
WINDOW = 512
Q_BLOCK = 256
ROPE_THETA = 10000.0
EPS = 1e-6
N_BRANCH = 4

SC_SIZES = (SC_W, SC_W, SC_W)
RW_SIZES = (RW_W, RW_W, RW_W, RW_LORA_W, RW_LORA_A, RW_LORA_G)
RET_SIZES = (RET_HEADS * RET_DK, RET_HEADS * RET_DK, RET_HEADS * RET_DV, RET_HEADS * RET_DV)
NSA_SIZES = (NSA_HEADS * NSA_DH,) + (NSA_KV_HEADS * NSA_DH,) * 6 + (NSA_HEADS * 3,)
GATE_SIZES = (D_MODEL,) * N_BRANCH
GROUP_SIZES = (sum(SC_SIZES), sum(RW_SIZES), sum(RET_SIZES), sum(NSA_SIZES), sum(GATE_SIZES))
BR_SIZES = (SC_W, RW_W, RET_HEADS * RET_DV, NSA_HEADS * NSA_DH)

V7X_VMEM_LIMIT_BYTES = 48 * 1024 * 1024
LANES = 128
SUBLANES = 8
_NT = (((1,), (1,)), ((), ()))


def _pick_tile(n, candidates):
    for c in candidates:
        if n % c == 0:
            return c
    return n


def _bdot(a, b):
    return jnp.dot(a.astype(jnp.bfloat16), b.astype(jnp.bfloat16), preferred_element_type=jnp.float32)


def _bdot_nt(a, b):
    return lax.dot_general(a.astype(jnp.bfloat16), b.astype(jnp.bfloat16), _NT,
                           preferred_element_type=jnp.float32)


def _head_sum(x, bd):
    hi = x.astype(jnp.bfloat16)
    lo = (x - hi.astype(jnp.float32)).astype(jnp.bfloat16)
    return (jnp.dot(hi, bd, preferred_element_type=jnp.float32)
            + jnp.dot(lo, bd, preferred_element_type=jnp.float32))


def _mm_kernel(a_ref, w_ref, o_ref, acc_ref):
    k = pl.program_id(2)

    @pl.when(k == 0)
    def _():
        acc_ref[...] = jnp.zeros_like(acc_ref)

    acc_ref[...] += lax.dot_general(a_ref[...], w_ref[...], _NT, preferred_element_type=jnp.float32)

    @pl.when(k == pl.num_programs(2) - 1)
    def _():
        o_ref[...] = acc_ref[...]


def _mm_cols(a, wt_stack, layer, col0, n):
    m, k = a.shape
    tn = next(c for c in (n if n <= 1536 else 1024, 1024, 768, 512, 256, 128) if n % c == 0 and col0 % c == 0)
    tm = _pick_tile(m, (1024, 512, 256, 128, 8))
    tk = _pick_tile(k, (2048, 1280, 1408, 1024, 512, 256, 128))
    j0 = col0 // tn
    return pl.pallas_call(
        _mm_kernel,
        out_shape=jax.ShapeDtypeStruct((m, n), jnp.float32),
        grid=(m // tm, n // tn, k // tk),
        in_specs=[pl.BlockSpec((tm, tk), lambda i, j, kk: (i, kk)),
                  pl.BlockSpec((None, tn, tk), lambda i, j, kk: (layer, j0 + j, kk))],
        out_specs=pl.BlockSpec((tm, tn), lambda i, j, kk: (i, j)),
        scratch_shapes=[pltpu.VMEM((tm, tn), jnp.float32)],
        compiler_params=pltpu.CompilerParams(
            dimension_semantics=("parallel", "parallel", "arbitrary"),
            vmem_limit_bytes=V7X_VMEM_LIMIT_BYTES),
        name="mm",
    )(a.astype(jnp.bfloat16), wt_stack)


def _rms(y, g):
    return y * lax.rsqrt(jnp.mean(y * y, axis=-1, keepdims=True) + EPS) * g


def _prenorm_kernel(x_ref, g_ref, h_ref):
    h_ref[...] = _rms(x_ref[...], g_ref[...]).astype(h_ref.dtype)


def _prenorm(x, g):
    m, d = x.shape
    tm = _pick_tile(m, (512, 256, 128, 8))
    return pl.pallas_call(
        _prenorm_kernel,
        out_shape=jax.ShapeDtypeStruct((m, d), jnp.bfloat16),
        grid=(m // tm,),
        in_specs=[pl.BlockSpec((tm, d), lambda i: (i, 0)), pl.BlockSpec((1, d), lambda i: (0, 0))],
        out_specs=pl.BlockSpec((tm, d), lambda i: (i, 0)),
        compiler_params=pltpu.CompilerParams(dimension_semantics=("parallel",)),
        name="prenorm",
    )(x, g.reshape(1, d))


def _mm_norm_res_kernel(emit_h, whole_k, a_ref, w_ref, res_ref, g_ref, *rest):
    if emit_h:
        g2_ref, x_ref, h_ref = rest[:3]
    else:
        x_ref = rest[0]

    def finish(y):
        x_new = res_ref[...] + _rms(y, g_ref[...])
        x_ref[...] = x_new
        if emit_h:
            h_ref[...] = _rms(x_new, g2_ref[...]).astype(h_ref.dtype)

    if whole_k:
        finish(jnp.dot(a_ref[...], w_ref[...], preferred_element_type=jnp.float32))
        return
    acc_ref = rest[-1]
    k = pl.program_id(1)

    @pl.when(k == 0)
    def _():
        acc_ref[...] = jnp.zeros_like(acc_ref)

    acc_ref[...] += jnp.dot(a_ref[...], w_ref[...], preferred_element_type=jnp.float32)

    @pl.when(k == pl.num_programs(1) - 1)
    def _():
        finish(acc_ref[...])


def _mm_norm_res(a, w_stack, layer, res, g, g_next):
    m, k = a.shape
    d = w_stack.shape[2]
    tk = _pick_tile(k, (2048, 1408, 1024, 512))
    tm = 512
    whole_k = tk == k
    emit_h = g_next is not None
    row = pl.BlockSpec((tm, d), lambda i, kk: (i, 0))
    vec = pl.BlockSpec((1, d), lambda i, kk: (0, 0))
    w_spec = (pl.BlockSpec((None, tk, d), lambda i, kk: (layer, 0, 0), pipeline_mode=pl.Buffered(1)) if whole_k
              else pl.BlockSpec((None, tk, d), lambda i, kk: (layer, kk, 0)))
    x_sd = jax.ShapeDtypeStruct((m, d), jnp.float32)
    h_sd = jax.ShapeDtypeStruct((m, d), jnp.bfloat16)
    out = pl.pallas_call(
        functools.partial(_mm_norm_res_kernel, emit_h, whole_k),
        out_shape=(x_sd, h_sd) if emit_h else x_sd,
        grid=(m // tm, k // tk),
        in_specs=[pl.BlockSpec((tm, tk), lambda i, kk: (i, kk)), w_spec, row, vec] + ([vec] if emit_h else []),
        out_specs=(row, row) if emit_h else row,
        scratch_shapes=[] if whole_k else [pltpu.VMEM((tm, d), jnp.float32)],
        compiler_params=pltpu.CompilerParams(
            dimension_semantics=("parallel", "arbitrary"), vmem_limit_bytes=V7X_VMEM_LIMIT_BYTES),
        name="mm_norm_res",
    )(a.astype(jnp.bfloat16), w_stack, res, g.reshape(1, d), *([g_next.reshape(1, d)] if emit_h else []))
    return out if emit_h else (out, None)


FFN_TM = 1024
FFN_TN = 512
HALO = SUBLANES


def _shift_rows(u, prev, n):
    rolled = pltpu.roll(u, n, axis=0)
    rows = lax.broadcasted_iota(jnp.int32, u.shape, 0)
    for r in range(n):
        rolled = jnp.where(rows == r, prev[HALO - n + r:HALO - n + r + 1, :], rolled)
    return rolled


def _ffn_up_kernel(seq_len, h_ref, halo_ref, wa_ref, wb_ref, cwa_ref, cwb_ref, ba_ref, bb_ref, o_ref):
    i = pl.program_id(0)
    h = h_ref[...]
    at_start = (i * FFN_TM) % seq_len == 0
    halo = jnp.where(at_start, jnp.zeros_like(halo_ref[...]), halo_ref[...])

    def conv(w_ref, cw_ref, b_ref):
        w = w_ref[...].astype(jnp.bfloat16)
        u = jnp.dot(h, w, preferred_element_type=jnp.float32)
        up = jnp.dot(halo, w, preferred_element_type=jnp.float32)
        cw = cw_ref[...]
        y = cw[0:1] * _shift_rows(u, up, 2)
        y = y + cw[1:2] * _shift_rows(u, up, 1)
        y = y + cw[2:3] * u
        return y + b_ref[...]

    a = conv(wa_ref, cwa_ref, ba_ref)
    b = conv(wb_ref, cwb_ref, bb_ref)
    o_ref[...] = (jax.nn.gelu(a, approximate=True) * b).astype(o_ref.dtype)


def _ffn_up(h, w_up_stack, layer, w_conv, b_conv, seq_len):
    m, d = h.shape
    f = w_up_stack.shape[2] // 2
    nj = f // FFN_TN
    b_conv = b_conv.reshape(1, 2 * f)
    tiles_per_halo = FFN_TM // HALO
    return pl.pallas_call(
        functools.partial(_ffn_up_kernel, seq_len),
        out_shape=jax.ShapeDtypeStruct((m, f), jnp.bfloat16),
        grid=(m // FFN_TM, nj),
        in_specs=[pl.BlockSpec((FFN_TM, d), lambda i, j: (i, 0)),
                  pl.BlockSpec((HALO, d), lambda i, j: (jnp.maximum(i * tiles_per_halo - 1, 0), 0)),
                  pl.BlockSpec((None, d, FFN_TN), lambda i, j: (layer, 0, j)),
                  pl.BlockSpec((None, d, FFN_TN), lambda i, j: (layer, 0, j + nj)),
                  pl.BlockSpec((3, FFN_TN), lambda i, j: (0, j)),
                  pl.BlockSpec((3, FFN_TN), lambda i, j: (0, j + nj)),
                  pl.BlockSpec((1, FFN_TN), lambda i, j: (0, j)),
                  pl.BlockSpec((1, FFN_TN), lambda i, j: (0, j + nj))],
        out_specs=pl.BlockSpec((FFN_TM, FFN_TN), lambda i, j: (i, j)),
        compiler_params=pltpu.CompilerParams(
            dimension_semantics=("parallel", "arbitrary"), vmem_limit_bytes=V7X_VMEM_LIMIT_BYTES),
        name="ffn_up",
    )(h, h, w_up_stack, w_up_stack, w_conv, w_conv, b_conv, b_conv)


MERGE_TM = 512
MERGE_TN = 512


def _merge_kernel(h_ref, wg0, wg1, wg2, wg3, o0, o1, o2, o3, wb0, wb1, wb2, wb3, out_ref):
    h = h_ref[...]
    acc = None
    for wg, o, wb in ((wg0, o0, wb0), (wg1, o1, wb1), (wg2, o2, wb2), (wg3, o3, wb3)):
        gate = jax.nn.sigmoid(lax.dot_general(h, wg[...], _NT, preferred_element_type=jnp.float32))
        term = gate * jnp.dot(o[...], wb[...], preferred_element_type=jnp.float32)
        acc = term if acc is None else acc + term
    out_ref[...] = acc.astype(out_ref.dtype)


def _merge(h, w_stack, layer, gate_col0, outs, wb_stack):
    m, d = h.shape
    nj = d // MERGE_TN
    j0 = gate_col0 // MERGE_TN
    assert gate_col0 % MERGE_TN == 0
    gate_specs = [pl.BlockSpec((None, MERGE_TN, d), functools.partial(lambda b, i, j: (layer, j0 + b * nj + j, 0), b))
                  for b in range(N_BRANCH)]
    o_specs = [pl.BlockSpec((MERGE_TM, o.shape[1]), lambda i, j: (i, 0)) for o in outs]
    row0 = np.cumsum([0] + [o.shape[1] for o in outs])
    assert all(r % o.shape[1] == 0 for r, o in zip(row0, outs))
    wb_specs = [pl.BlockSpec((None, o.shape[1], MERGE_TN),
                             functools.partial(lambda rb, i, j: (layer, rb, j), int(r) // o.shape[1]))
                for r, o in zip(row0, outs)]
    return pl.pallas_call(
        _merge_kernel,
        out_shape=jax.ShapeDtypeStruct((m, d), jnp.bfloat16),
        grid=(m // MERGE_TM, nj),
        in_specs=[pl.BlockSpec((MERGE_TM, d), lambda i, j: (i, 0))] + gate_specs + o_specs + wb_specs,
        out_specs=pl.BlockSpec((MERGE_TM, MERGE_TN), lambda i, j: (i, j)),
        compiler_params=pltpu.CompilerParams(
            dimension_semantics=("parallel", "arbitrary"), vmem_limit_bytes=V7X_VMEM_LIMIT_BYTES),
        name="merge",
    )(h, *([w_stack] * N_BRANCH), *[o.astype(jnp.bfloat16) for o in outs], *([wb_stack] * N_BRANCH))


XATT_TM = 512


def _xattn_kernel(h_ref, wq_ref, k_ref, v_ref, o_ref):
    q = jnp.dot(h_ref[...], wq_ref[...], preferred_element_type=jnp.float32)
    q = (q * (MEM_HEAD_DIM ** -0.5)).astype(jnp.bfloat16)
    for hh in range(MEM_HEADS):
        sl = slice(hh * MEM_HEAD_DIM, (hh + 1) * MEM_HEAD_DIM)
        s = lax.dot_general(q[:, sl], k_ref[0, :, sl], _NT, preferred_element_type=jnp.float32)
        p = jnp.exp(s - jnp.max(s, axis=-1, keepdims=True))
        p = p / jnp.sum(p, axis=-1, keepdims=True)
        o_ref[:, sl] = jnp.dot(p.astype(jnp.bfloat16), v_ref[0, :, sl],
                               preferred_element_type=jnp.float32).astype(o_ref.dtype)


def _xattn(h, wq_stack, layer, k, v, seq_len):
    m, d = h.shape
    mem_len = k.shape[1]
    per_batch = seq_len // XATT_TM
    kv_spec = pl.BlockSpec((1, mem_len, d), lambda i: (i // per_batch, 0, 0))
    return pl.pallas_call(
        _xattn_kernel,
        out_shape=jax.ShapeDtypeStruct((m, d), jnp.bfloat16),
        grid=(m // XATT_TM,),
        in_specs=[pl.BlockSpec((XATT_TM, d), lambda i: (i, 0)),
                  pl.BlockSpec((None, d, d), lambda i: (layer, 0, 0)), kv_spec, kv_spec],
        out_specs=pl.BlockSpec((XATT_TM, d), lambda i: (i, 0)),
        compiler_params=pltpu.CompilerParams(
            dimension_semantics=("parallel",), vmem_limit_bytes=V7X_VMEM_LIMIT_BYTES),
        name="xattn",
    )(h, wq_stack, k, v)


def _rope_freqs(d):
    return ROPE_THETA ** (-jnp.arange(0, d, 2, dtype=jnp.float32) / d)


def _retnet_freqs(d):
    return 1.0 / (ROPE_THETA ** jnp.linspace(0.0, 1.0, d // 2, dtype=jnp.float32))


WKV_CHUNK = 64


def _rwkv7_kernel(z_ref, mu_ref, w0_ref, a0_ref, kk_ref, ka_ref, rk_ref, lng_ref, lnb_ref,
                  ww2_ref, wa2_ref, wg2_ref, o_ref, s_ref, prev_ref, y_ref):
    c = WKV_CHUNK
    hd = RW_HEAD_DIM
    w = RW_W
    nb = z_ref.shape[0]
    f32, bf16 = jnp.float32, jnp.bfloat16

    @pl.when(pl.program_id(0) == 0)
    def _():
        s_ref[...] = jnp.zeros_like(s_ref)
        prev_ref[...] = jnp.zeros_like(prev_ref)

    lane_h = lax.broadcasted_iota(jnp.int32, (w, w), 0) // hd
    lane_w = lax.broadcasted_iota(jnp.int32, (w, w), 1) // hd
    head_bd = jnp.where(lane_h == lane_w, 1.0, 0.0).astype(bf16)
    row = lax.broadcasted_iota(jnp.int32, (c, c), 0)
    col = lax.broadcasted_iota(jnp.int32, (c, c), 1)
    tri = jnp.where(row >= col, 1.0, 0.0).astype(bf16)
    eye = jnp.where(row == col, 1.0, 0.0).astype(f32)
    r2 = lax.broadcasted_iota(jnp.int32, (2 * c, 2 * c), 0)
    c2 = lax.broadcasted_iota(jnp.int32, (2 * c, 2 * c), 1)
    rr = jnp.where(r2 >= c, r2 - c, r2)
    cc = jnp.where(c2 >= c, c2 - c, c2)
    tri_mask = cc < rr + jnp.where(r2 >= c, 1, 0)

    def prepare(b):
        z = z_ref[b]
        rows = lax.broadcasted_iota(jnp.int32, z.shape, 0)
        z_prev = jnp.where(rows == 0, prev_ref[b], pltpu.roll(z, 1, axis=0))
        prev_ref[b] = z[c - 1:c, :]
        z = z + (z_prev - z) * mu_ref[...]
        r, k, v, lora = z[:, 0:w], z[:, w:2 * w], z[:, 2 * w:3 * w], z[:, 3 * w:4 * w]
        logw = -RW_DECAY_SCALE * jax.nn.sigmoid(w0_ref[...] + _bdot(jnp.tanh(lora), ww2_ref[...]))
        a = jax.nn.sigmoid(a0_ref[...] + _bdot(lora, wa2_ref[...]))
        g = _bdot(jax.nn.sigmoid(lora), wg2_ref[...])
        kk = k * kk_ref[...]
        kk = kk / jnp.maximum(jnp.sqrt(_head_sum(kk * kk, head_bd)), 1e-12)
        k = k * (1.0 + (a - 1.0) * ka_ref[...])
        hi = logw.astype(bf16)
        rem = logw - hi.astype(f32)
        mid = rem.astype(bf16)
        lo = (rem - mid.astype(f32)).astype(bf16)
        cum = (jnp.dot(tri, hi, preferred_element_type=f32) + jnp.dot(tri, mid, preferred_element_type=f32)
               + jnp.dot(tri, lo, preferred_element_type=f32))
        kka = kk * a
        e_neg = jnp.exp(-cum)
        last = cum[c - 1:c, :]
        dec = jnp.exp(last - cum)
        return dict(r=r, k=k, v=v, g=g, vb=v.astype(bf16), g_last=jnp.exp(last),
                    qh=(kk * jnp.exp(cum - logw)).astype(bf16), rh=(r * jnp.exp(cum)).astype(bf16),
                    bh=(kka * e_neg).astype(bf16), kh=(k * e_neg).astype(bf16),
                    bd=(kka * dec).astype(bf16), kd=(k * dec).astype(bf16))

    pre = [prepare(b) for b in range(nb)]

    units = [(b, h) for b in range(nb) for h in range(RW_HEADS)]
    n_units = range(len(units))
    part = lambda name, i: pre[units[i][0]][name][:, units[i][1] * hd:(units[i][1] + 1) * hd]
    a1 = [jnp.concatenate([part("qh", i), part("rh", i)], axis=0) for i in n_units]
    b1 = [jnp.concatenate([part("bh", i), part("kh", i)], axis=0) for i in n_units]
    ss = [jnp.where(tri_mask, lax.dot_general(a1[i], b1[i], _NT, preferred_element_type=f32), 0.0)
          for i in n_units]
    s_old = [s_ref[i] for i in n_units]
    qr = [lax.dot_general(a1[i], s_old[i].astype(bf16), _NT, preferred_element_type=f32) for i in n_units]
    lm = [_bdot(ss[i][:, c:2 * c], part("vb", i)) for i in n_units]
    n = [-ss[i][0:c, 0:c] for i in n_units]
    t = [eye + n[i] for i in n_units]
    for _ in range(5):
        n = [_bdot(n[i], n[i]) for i in n_units]
        t = [t[i] + _bdot(t[i], n[i]) for i in n_units]
    u = [-_bdot(t[i], qr[i][0:c] + lm[i][0:c]) for i in n_units]
    for i in n_units:
        b, h = units[i]
        y_ref[b, :, h * hd:(h + 1) * hd] = qr[i][c:2 * c] + lm[i][c:2 * c] + _bdot(ss[i][c:2 * c, 0:c], u[i])
    for i in n_units:
        zt = jnp.concatenate([u[i], part("v", i)], axis=0).T
        x = jnp.concatenate([part("bd", i), part("kd", i)], axis=0)
        s_ref[i] = s_old[i] * part("g_last", i) + _bdot(zt, x)

    for b in range(nb):
        p = pre[b]
        y = y_ref[b]
        mean = _head_sum(y, head_bd) * (1.0 / hd)
        yc = y - mean
        var = _head_sum(yc * yc, head_bd) * (1.0 / hd)
        yn = yc * lax.rsqrt(var + RW_LN_EPS) * lng_ref[...] + lnb_ref[...]
        bonus = _head_sum(p["r"] * p["k"] * rk_ref[...], head_bd) * p["v"]
        o_ref[b] = ((yn + bonus) * p["g"]).astype(o_ref.dtype)


def _rwkv7_mixer(z, mu, w0, w_w2, a0, w_a2, w_g2, k_k, k_a, r_k, lnx_g, lnx_b):
    bsz, s, zw = z.shape
    w = RW_W
    n_lora = RW_LORA_W + RW_LORA_A + RW_LORA_G
    row = lambda t: t.reshape(1, -1)
    mu = jnp.pad(mu, (0, zw - mu.shape[0])).reshape(1, zw)
    o_w, o_a = RW_LORA_W, RW_LORA_W + RW_LORA_A
    pad_rows = lambda m, lo: jnp.pad(m, ((lo, w - lo - m.shape[0]), (0, 0))).astype(jnp.bfloat16)
    vec = pl.BlockSpec((1, w), lambda c: (0, 0))
    mat = pl.BlockSpec((w, w), lambda c: (0, 0))
    assert zw == 4 * w and n_lora <= w
    return pl.pallas_call(
        _rwkv7_kernel,
        out_shape=jax.ShapeDtypeStruct((bsz, s, w), jnp.bfloat16),
        grid=(s // WKV_CHUNK,),
        in_specs=[pl.BlockSpec((bsz, WKV_CHUNK, zw), lambda c: (0, c, 0)),
                  pl.BlockSpec((1, zw), lambda c: (0, 0))] + [vec] * 7 + [mat] * 3,
        out_specs=pl.BlockSpec((bsz, WKV_CHUNK, w), lambda c: (0, c, 0)),
        scratch_shapes=[pltpu.VMEM((bsz * RW_HEADS, RW_HEAD_DIM, RW_HEAD_DIM), jnp.float32),
                        pltpu.VMEM((bsz, 1, zw), jnp.float32),
                        pltpu.VMEM((bsz, WKV_CHUNK, w), jnp.float32)],
        compiler_params=pltpu.CompilerParams(dimension_semantics=("arbitrary",)),
        name="rwkv7",
    )(z, mu, row(w0), row(a0), row(k_k), row(k_a), row(r_k), row(lnx_g), row(lnx_b),
      pad_rows(w_w2, 0), pad_rows(w_a2, o_w), pad_rows(w_g2, o_a))


RET_LOG_DECAY = tuple(math.log(1.0 - 2.0 ** (-5.0 - h)) for h in range(RET_HEADS))


def _retention_kernel(q_ref, k_ref, v_ref, g_ref, cos_ref, sin_ref, o_ref, s_ref):
    c, dk, dv = RET_CHUNK, RET_DK, RET_DV
    f32 = jnp.float32

    @pl.when(pl.program_id(1) == 0)
    def _():
        s_ref[...] = jnp.zeros_like(s_ref)

    cos = cos_ref[0]
    sin = sin_ref[0]
    diff = (lax.broadcasted_iota(jnp.int32, (c, c), 0) - lax.broadcasted_iota(jnp.int32, (c, c), 1)).astype(f32)
    tok = lax.broadcasted_iota(jnp.int32, (c, 1), 0).astype(f32)
    rope = lambda x: x * cos + pltpu.roll(x, dk // 2, axis=1) * sin
    for h in range(RET_HEADS):
        lg = RET_LOG_DECAY[h]
        q = rope(q_ref[0, :, h * dk:(h + 1) * dk])
        k = rope(k_ref[0, :, h * dk:(h + 1) * dk]) * (dk ** -0.5)
        v = v_ref[0, :, h * dv:(h + 1) * dv]
        decay_in = jnp.where(diff >= 0, jnp.exp(jnp.maximum(diff, 0.0) * lg), 0.0)
        scores = _bdot_nt(q, k) * decay_in
        state = s_ref[h]
        o = _bdot(scores, v) + _bdot(q * jnp.exp((tok + 1.0) * lg), state)
        s_ref[h] = state * math.exp(c * lg) + _bdot((k * jnp.exp((c - 1.0 - tok) * lg)).T, v)
        mu = jnp.mean(o, axis=-1, keepdims=True)
        oc = o - mu
        var = jnp.mean(oc * oc, axis=-1, keepdims=True)
        gate = g_ref[0, :, h * dv:(h + 1) * dv]
        o_ref[0, :, h * dv:(h + 1) * dv] = (oc * lax.rsqrt(var + RET_LN_EPS) * gate * jax.nn.sigmoid(gate)
                                            ).astype(o_ref.dtype)


def _retention_mixer(z, cos, sin):
    bsz, s, _ = z.shape
    qk_w, v_w = RET_HEADS * RET_DK, RET_HEADS * RET_DV
    c = RET_CHUNK
    tab = pl.BlockSpec((1, c, RET_DK), lambda b, i: (b, i, 0))
    return pl.pallas_call(
        _retention_kernel,
        out_shape=jax.ShapeDtypeStruct((bsz, s, v_w), jnp.bfloat16),
        grid=(bsz, s // c),
        in_specs=[pl.BlockSpec((1, c, qk_w), lambda b, i: (b, i, 0)),
                  pl.BlockSpec((1, c, qk_w), lambda b, i: (b, i, 1)),
                  pl.BlockSpec((1, c, v_w), lambda b, i: (b, i, 1)),
                  pl.BlockSpec((1, c, v_w), lambda b, i: (b, i, 2)), tab, tab],
        out_specs=pl.BlockSpec((1, c, v_w), lambda b, i: (b, i, 0)),
        scratch_shapes=[pltpu.VMEM((RET_HEADS, RET_DK, RET_DV), jnp.float32)],
        compiler_params=pltpu.CompilerParams(dimension_semantics=("parallel", "arbitrary")),
        name="retention",
    )(z, z, z, z, cos, sin)


TRIG_TM = 1024


def _trig_kernel(pos_ref, freq_ref, rc_ref, rs_ref, nc_ref, ns_ref):
    ang = pos_ref[...] * freq_ref[...]
    c, sn = jnp.cos(ang), jnp.sin(ang)
    n_ret, n_nsa = RET_DK // 2, NSA_DH // 2
    rc_ref[...] = jnp.concatenate([c[:, :n_ret]] * 2, axis=1)
    rs_ref[...] = jnp.concatenate([-sn[:, :n_ret], sn[:, :n_ret]], axis=1)
    nsa_c, nsa_s = c[:, n_ret:n_ret + n_nsa], sn[:, n_ret:n_ret + n_nsa]
    nc_ref[...] = jnp.concatenate([nsa_c, nsa_c] * NSA_HEADS, axis=1)
    ns_ref[...] = jnp.concatenate([-nsa_s, nsa_s] * NSA_HEADS, axis=1)


def _rope_tables(positions):
    bsz, s = positions.shape
    m = bsz * s
    n_ret, n_nsa = RET_DK // 2, NSA_DH // 2
    assert n_ret + n_nsa <= LANES
    freq = jnp.concatenate([_retnet_freqs(RET_DK), _rope_freqs(NSA_DH),
                            jnp.zeros((LANES - n_ret - n_nsa,), jnp.float32)]).reshape(1, LANES)
    tm = _pick_tile(m, (TRIG_TM, 512, 256, 128, 8))
    spec = lambda w: pl.BlockSpec((tm, w), lambda i: (i, 0))
    sd = lambda w: jax.ShapeDtypeStruct((m, w), jnp.float32)
    nsa_w = NSA_HEADS * NSA_DH
    tabs = pl.pallas_call(
        _trig_kernel,
        out_shape=(sd(RET_DK), sd(RET_DK), sd(nsa_w), sd(nsa_w)),
        grid=(m // tm,),
        in_specs=[spec(1), pl.BlockSpec((1, LANES), lambda i: (0, 0))],
        out_specs=(spec(RET_DK), spec(RET_DK), spec(nsa_w), spec(nsa_w)),
        compiler_params=pltpu.CompilerParams(dimension_semantics=("parallel",)),
        name="rope_tables",
    )(positions.astype(jnp.float32).reshape(m, 1), freq)
    return [t.reshape(bsz, s, t.shape[1]) for t in tabs]


SC_TM = 512


def _short_conv_kernel(seq_len, b_ref, c_ref, x_ref, ch_ref, xh_ref, w_ref, o_ref):
    at_start = (pl.program_id(0) * SC_TM) % seq_len == 0
    u = c_ref[...] * x_ref[...]
    up = jnp.where(at_start, 0.0, ch_ref[...] * xh_ref[...])
    w = w_ref[...]
    y = w[0:1] * _shift_rows(u, up, 2) + w[1:2] * _shift_rows(u, up, 1) + w[2:3] * u
    o_ref[...] = (b_ref[...] * y).astype(o_ref.dtype)


def _short_conv_mixer(z, w_conv, seq_len):
    m = z.shape[0]
    w = SC_W
    per_halo = SC_TM // HALO
    tile = lambda j: pl.BlockSpec((SC_TM, w), lambda i: (i, j))
    halo = lambda j: pl.BlockSpec((HALO, w), lambda i: (jnp.maximum(i * per_halo - 1, 0), j))
    return pl.pallas_call(
        functools.partial(_short_conv_kernel, seq_len),
        out_shape=jax.ShapeDtypeStruct((m, w), jnp.bfloat16),
        grid=(m // SC_TM,),
        in_specs=[tile(0), tile(1), tile(2), halo(1), halo(2), pl.BlockSpec((SC_K, w), lambda i: (0, 0))],
        out_specs=pl.BlockSpec((SC_TM, w), lambda i: (i, 0)),
        compiler_params=pltpu.CompilerParams(dimension_semantics=("parallel",)),
        name="short_conv",
    )(z, z, z, z, z, w_conv)


NSA_KEY_TILE = 512
NSA_WIN_SPAN = WINDOW + Q_BLOCK
NSA_PREP_TM = 512
NEG_BIG = -1e30
NSA_Q_W = NSA_HEADS * NSA_DH
NSA_KV_W = NSA_KV_HEADS * NSA_DH
NSA_GATE_OFF = NSA_Q_W + 6 * NSA_KV_W
NSA_VAL_W = LANES


def _rope_lanes(x, cos, sin):
    width = x.shape[1]
    half = NSA_DH // 2
    lane = lax.broadcasted_iota(jnp.int32, x.shape, 1)
    other = jnp.where(lane % NSA_DH < half, pltpu.roll(x, width - half, axis=1), pltpu.roll(x, half, axis=1))
    return x * cos + other * sin


def _nsa_prep_kernel(z_ref, cos_ref, sin_ref, q_ref, ks_ref, vs_ref, kw_ref, vw_ref, kc_ref, vc_ref,
                     gc_ref, gs_ref, gw_ref):
    cos, sin = cos_ref[0], sin_ref[0]
    z = z_ref[0]
    q_ref[0] = (_rope_lanes(z[:, 0:NSA_Q_W], cos, sin) * (NSA_DH ** -0.5)).astype(q_ref.dtype)
    kv = lambda i: z[:, NSA_Q_W + i * NSA_KV_W:NSA_Q_W + (i + 1) * NSA_KV_W]
    cos_kv, sin_kv = cos[:, 0:NSA_KV_W], sin[:, 0:NSA_KV_W]
    pieces = ((kc_ref, kv(0)), (vc_ref, kv(1)), (ks_ref, _rope_lanes(kv(2), cos_kv, sin_kv)), (vs_ref, kv(3)),
              (kw_ref, _rope_lanes(kv(4), cos_kv, sin_kv)), (vw_ref, kv(5)))
    n_sel = ks_ref.shape[3] - NSA_DH
    tok = pl.program_id(1) * NSA_PREP_TM + lax.broadcasted_iota(jnp.int32, (NSA_PREP_TM, n_sel), 0)
    block_1hot = jnp.where(tok // SEL_BLOCK == lax.broadcasted_iota(jnp.int32, (NSA_PREP_TM, n_sel), 1), 1.0, 0.0)
    ones_col = jnp.where(lax.broadcasted_iota(jnp.int32, (NSA_PREP_TM, NSA_VAL_W - NSA_DH), 1) == 0, 1.0, 0.0)
    for ref, val in pieces:
        for g in range(NSA_KV_HEADS):
            val_g = val[:, g * NSA_DH:(g + 1) * NSA_DH]
            if ref is ks_ref:
                val_g = jnp.concatenate([val_g, block_1hot], axis=1)
            elif ref is vs_ref or ref is vw_ref:
                val_g = jnp.concatenate([val_g, ones_col], axis=1)
            ref[0, g] = val_g.astype(ref.dtype)
    gate = jax.nn.sigmoid(z[:, NSA_GATE_OFF:NSA_GATE_OFF + LANES])
    src = lax.broadcasted_iota(jnp.int32, (LANES, NSA_Q_W), 0)
    head = lax.broadcasted_iota(jnp.int32, (LANES, NSA_Q_W), 1) // NSA_DH
    for j, ref in enumerate((gc_ref, gs_ref, gw_ref)):
        expand = jnp.where(src == 3 * head + j, 1.0, 0.0).astype(jnp.bfloat16)
        ref[0] = _head_sum(gate, expand)


def _nsa_prep(z, cos, sin):
    bsz, s, zw = z.shape
    tm = NSA_PREP_TM
    wide = pl.BlockSpec((1, tm, NSA_Q_W), lambda b, i: (b, i, 0))
    grp = pl.BlockSpec((1, NSA_KV_HEADS, tm, NSA_DH), lambda b, i: (b, 0, i, 0))
    grp_sd = lambda dt: jax.ShapeDtypeStruct((bsz, NSA_KV_HEADS, s, NSA_DH), dt)
    ks_w = NSA_DH + s // SEL_BLOCK
    ks_spec = pl.BlockSpec((1, NSA_KV_HEADS, tm, ks_w), lambda b, i: (b, 0, i, 0))
    ks_sd = jax.ShapeDtypeStruct((bsz, NSA_KV_HEADS, s, ks_w), jnp.bfloat16)
    val_spec = pl.BlockSpec((1, NSA_KV_HEADS, tm, NSA_VAL_W), lambda b, i: (b, 0, i, 0))
    val_sd = jax.ShapeDtypeStruct((bsz, NSA_KV_HEADS, s, NSA_VAL_W), jnp.bfloat16)
    wide_sd = lambda dt: jax.ShapeDtypeStruct((bsz, s, NSA_Q_W), dt)
    bf16, f32 = jnp.bfloat16, jnp.float32
    return pl.pallas_call(
        _nsa_prep_kernel,
        out_shape=(wide_sd(bf16), ks_sd, val_sd, grp_sd(bf16), val_sd, grp_sd(f32), grp_sd(f32),
                   wide_sd(f32), wide_sd(f32), wide_sd(f32)),
        grid=(bsz, s // tm),
        in_specs=[pl.BlockSpec((1, tm, zw), lambda b, i: (b, i, 0)), wide, wide],
        out_specs=(wide, ks_spec, val_spec, grp, val_spec, grp, grp, wide, wide, wide),
        compiler_params=pltpu.CompilerParams(dimension_semantics=("parallel", "parallel")),
        name="nsa_prep",
    )(z, cos, sin)


def _nsa_compress_kernel(hk_ref, hv_ref, pos_ref, w1_ref, w2_ref, cos_ref, sin_ref, kc_ref, vc_ref):
    half = w1_ref.shape[1] // 2
    n = hk_ref.shape[1]
    for i, (h_ref, o_ref) in enumerate(((hk_ref, kc_ref), (hv_ref, vc_ref))):
        h = h_ref[0]
        first = _bdot(h, w1_ref[i, 0:half, :])
        second = pltpu.roll(_bdot(h, w1_ref[i, half:2 * half, :]), n - 1, axis=0)
        bias = _bdot(jnp.broadcast_to(pos_ref[i], (SUBLANES, 2 * half)), w1_ref[i])[0:1]
        out = _bdot(jax.nn.gelu(first + second + bias, approximate=True), w2_ref[i])
        if i == 0:
            src = lax.broadcasted_iota(jnp.int32, (NSA_DH, NSA_DH), 0)
            dst = lax.broadcasted_iota(jnp.int32, (NSA_DH, NSA_DH), 1)
            swap = jnp.where(src == (dst + NSA_DH // 2) % NSA_DH, 1.0, 0.0).astype(jnp.bfloat16)
            out = out * cos_ref[0] + _head_sum(out, swap) * sin_ref[0]
        o_ref[0] = out.astype(o_ref.dtype)


def _nsa_compress(hk, hv, pos_flat, w1, w2, cos_c, sin_c):
    bg, n, hw = hk.shape
    ng = NSA_KV_HEADS
    blk = pl.BlockSpec((1, n, hw), lambda i: (i, 0, 0))
    tab = pl.BlockSpec((1, n, NSA_DH), lambda i: (i // ng, 0, 0))
    out = pl.BlockSpec((1, n, NSA_DH), lambda i: (i, 0, 0))
    whole = lambda a: pl.BlockSpec(a.shape, lambda i: (0,) * a.ndim)
    sd = jax.ShapeDtypeStruct((bg, n, NSA_DH), jnp.bfloat16)
    return pl.pallas_call(
        _nsa_compress_kernel,
        out_shape=(sd, sd),
        grid=(bg,),
        in_specs=[blk, blk, whole(pos_flat), whole(w1), whole(w2), tab, tab],
        out_specs=(out, out),
        compiler_params=pltpu.CompilerParams(
            dimension_semantics=("parallel",), vmem_limit_bytes=V7X_VMEM_LIMIT_BYTES),
        name="nsa_compress",
    )(hk, hv, pos_flat, w1, w2, cos_c, sin_c)


def _nsa_attn_kernel(q_ref, kc_ref, vc_ref, ks_ref, vs_ref, kw_ref, vw_ref, gc_ref, gs_ref, gw_ref, o_ref):
    f32, bf16 = jnp.float32, jnp.bfloat16
    qi = pl.program_id(2)
    q0 = qi * Q_BLOCK
    hpg, dh, tk = NSA_GROUP, NSA_DH, NSA_KEY_TILE
    rows = hpg * Q_BLOCK
    n_sel = ks_ref.shape[3] - dh
    n_cmp = kc_ref.shape[2]
    q_all = q_ref[0]
    q = jnp.concatenate([q_all[:, h * dh:(h + 1) * dh] for h in range(hpg)], axis=0)
    by_head = lambda o: jnp.concatenate([o[h * Q_BLOCK:(h + 1) * Q_BLOCK] for h in range(hpg)], axis=1)

    tq_c = q0 + lax.broadcasted_iota(jnp.int32, (Q_BLOCK, n_cmp), 0)
    c_end = lax.broadcasted_iota(jnp.int32, (Q_BLOCK, n_cmp), 1) * CMP_STRIDE + (CMP_LEN - 1)
    c_ok = jnp.concatenate([jnp.where(c_end <= tq_c, 1.0, 0.0)] * hpg, axis=0) > 0.5
    s = jnp.where(c_ok, lax.dot_general(q, kc_ref[0, 0], _NT, preferred_element_type=f32), NEG_BIG)
    m = jnp.max(s, axis=-1, keepdims=True)
    e = jnp.where(c_ok, jnp.exp(s - m), 0.0)
    p = e / jnp.maximum(jnp.sum(e, axis=-1, keepdims=True), 1e-30)
    o_cmp = jnp.dot(p.astype(bf16), vc_ref[0, 0], preferred_element_type=f32)

    p_sum = p[0:Q_BLOCK]
    for h in range(1, hpg):
        p_sum = p_sum + p[h * Q_BLOCK:(h + 1) * Q_BLOCK]
    c_start = lax.broadcasted_iota(jnp.int32, (n_sel, n_cmp), 1) * CMP_STRIDE
    j_blk = lax.broadcasted_iota(jnp.int32, (n_sel, n_cmp), 0)
    overlap_t = jnp.where((c_start < (j_blk + 1) * SEL_BLOCK) & (c_start + CMP_LEN > j_blk * SEL_BLOCK),
                          1.0, 0.0).astype(bf16)
    p_hi = p_sum.astype(bf16)
    p_lo = (p_sum - p_hi.astype(f32)).astype(bf16)
    imp = (lax.dot_general(overlap_t, p_hi, _NT, preferred_element_type=f32)
           + lax.dot_general(overlap_t, p_lo, _NT, preferred_element_type=f32))
    jb = lax.broadcasted_iota(jnp.int32, (n_sel, Q_BLOCK), 0)
    cur = (q0 + lax.broadcasted_iota(jnp.int32, (n_sel, Q_BLOCK), 1)) // SEL_BLOCK
    forced = (jb == 0) | (jb == cur) | (jb == cur - 1)
    imp = jnp.where(forced, -NEG_BIG, jnp.where(jb <= cur, imp, NEG_BIG))
    rank = jnp.zeros((n_sel, Q_BLOCK), f32)
    for i in range(n_sel):
        row_i = imp[i:i + 1, :]
        ahead = (row_i > imp) | ((row_i == imp) & (jb > i))
        rank = rank + jnp.where(ahead, 1.0, 0.0)
    sel_bias = jnp.where(rank < float(min(SEL_TOP, n_sel)), 0.0, NEG_BIG).T.astype(bf16)

    q_sel = jnp.concatenate([q, jnp.concatenate([sel_bias] * hpg, axis=0)], axis=1)

    def tile_update(kt, carry, on_diagonal):
        m, acc = carry
        k0 = pl.multiple_of(kt * tk, tk)
        s = lax.dot_general(q_sel, ks_ref[0, 0, pl.ds(k0, tk), :], _NT, preferred_element_type=f32)
        if on_diagonal:
            t_pos = q0 + lax.broadcasted_iota(jnp.int32, (Q_BLOCK, tk), 0)
            key = k0 + lax.broadcasted_iota(jnp.int32, (Q_BLOCK, tk), 1)
            causal = jnp.concatenate([jnp.where(key <= t_pos, 0.0, NEG_BIG)] * hpg, axis=0)
            s = s + causal
        m_new = jnp.maximum(m, jnp.max(s, axis=-1, keepdims=True))
        p = jnp.exp(s - m_new).astype(bf16)
        acc = jnp.exp(m - m_new) * acc + jnp.dot(p, vs_ref[0, 0, pl.ds(k0, tk), :], preferred_element_type=f32)
        return m_new, acc

    init = (jnp.full((rows, 1), NEG_BIG, f32), jnp.zeros((rows, NSA_VAL_W), f32))
    last_tile = (q0 + Q_BLOCK - 1) // tk
    carry = lax.fori_loop(0, last_tile, lambda kt, c: tile_update(kt, c, False), init)
    _, acc = tile_update(last_tile, carry, True)
    o_sel = acc[:, 0:dh] / jnp.maximum(acc[:, dh:dh + 1], 1e-30)

    w0 = pl.multiple_of(jnp.maximum(q0 - WINDOW, 0), Q_BLOCK)
    kwin = kw_ref[0, 0, pl.ds(w0, NSA_WIN_SPAN), :]
    vwin = vw_ref[0, 0, pl.ds(w0, NSA_WIN_SPAN), :]
    tq = q0 + lax.broadcasted_iota(jnp.int32, (Q_BLOCK, NSA_WIN_SPAN), 0)
    kp = w0 + lax.broadcasted_iota(jnp.int32, (Q_BLOCK, NSA_WIN_SPAN), 1)
    dlt = tq - kp
    wbias = jnp.where((dlt >= 0) & (dlt < WINDOW), 0.0, NEG_BIG).astype(f32)
    wbias = jnp.concatenate([wbias] * hpg, axis=0)
    s = lax.dot_general(q, kwin, _NT, preferred_element_type=f32) + wbias
    p = jnp.exp(s - jnp.max(s, axis=-1, keepdims=True)).astype(bf16)
    pv = jnp.dot(p, vwin, preferred_element_type=f32)
    o_win = pv[:, 0:dh] / jnp.maximum(pv[:, dh:dh + 1], 1e-30)

    o_ref[0] = (gc_ref[0] * by_head(o_cmp) + gs_ref[0] * by_head(o_sel) + gw_ref[0] * by_head(o_win)
                ).astype(o_ref.dtype)


def _nsa_attn(q, k_cmp, v_cmp, ks, vs, kw, vw, g_cmp, g_sel, g_win):
    bsz, s, _ = q.shape
    ng, dh = NSA_KV_HEADS, NSA_DH
    gw = NSA_GROUP * dh
    kv_spec = pl.BlockSpec((1, 1, s, dh), lambda b, g, i: (b, g, 0, 0))
    val_spec = pl.BlockSpec((1, 1, s, NSA_VAL_W), lambda b, g, i: (b, g, 0, 0))
    ks_spec = pl.BlockSpec((1, 1, s, ks.shape[3]), lambda b, g, i: (b, g, 0, 0))
    cmp_spec = pl.BlockSpec((1, 1, k_cmp.shape[2], dh), lambda b, g, i: (b, g, 0, 0))
    q_spec = pl.BlockSpec((1, Q_BLOCK, gw), lambda b, g, i: (b, i, g))
    return pl.pallas_call(
        _nsa_attn_kernel,
        out_shape=jax.ShapeDtypeStruct(q.shape, jnp.bfloat16),
        grid=(bsz, ng, s // Q_BLOCK),
        in_specs=[q_spec, cmp_spec, cmp_spec, ks_spec, val_spec, kv_spec, val_spec, q_spec, q_spec, q_spec],
        out_specs=q_spec,
        compiler_params=pltpu.CompilerParams(
            dimension_semantics=("parallel", "parallel", "arbitrary"),
            vmem_limit_bytes=V7X_VMEM_LIMIT_BYTES),
        name="nsa_attn",
    )(q, k_cmp, v_cmp, ks, vs, kw, vw, g_cmp, g_sel, g_win)


def _nsa_mixer(z, nsa_tabs, cmp_pos, cmp_w1, cmp_w2):
    bsz, s, _ = z.shape
    ng, dh = NSA_KV_HEADS, NSA_DH
    cos, sin, cos_c, sin_c = nsa_tabs
    q, ks, vs, kw, vw, kc, vc, g_cmp, g_sel, g_win = _nsa_prep(z, cos, sin)
    n_blk = s // CMP_STRIDE
    flat = lambda t: t.reshape(bsz * ng, n_blk, CMP_STRIDE * dh)
    k_cmp, v_cmp = _nsa_compress(flat(kc), flat(vc), cmp_pos.reshape(2, 1, CMP_LEN * dh), cmp_w1, cmp_w2,
                                 cos_c, sin_c)
    grouped = lambda t: t.reshape(bsz, ng, n_blk, dh)
    return _nsa_attn(q, grouped(k_cmp), grouped(v_cmp), ks, vs, kw, vw, g_cmp, g_sel, g_win)


def _nsa_tables(cos, sin):
    end = CMP_LEN - 1
    at_ends = lambda t: jnp.pad(t[:, end::CMP_STRIDE, :NSA_DH], ((0, 0), (0, 1), (0, 0)))
    return cos, sin, at_ends(cos), at_ends(sin)


IN_GROUPS = (("sc", 0, GROUP_SIZES[0], 1536),
             ("rw", GROUP_SIZES[0], GROUP_SIZES[1], 2048),
             ("ret", GROUP_SIZES[0] + GROUP_SIZES[1], GROUP_SIZES[2], 3072),
             ("nsa", sum(GROUP_SIZES[:3]), GROUP_SIZES[3], 1536),
             ("gate", sum(GROUP_SIZES[:4]), GROUP_SIZES[4], 8192))


def _in_proj_groups(w_in):
    w_t = jnp.swapaxes(w_in, 1, 2)
    return {name: jnp.pad(w_t[:, start:start + width], ((0, 0), (0, padded - width), (0, 0))).astype(jnp.bfloat16)
            for name, start, width, padded in IN_GROUPS}


def _hybrid_mixer(h, bsz, layer, ret_tabs, nsa_tabs, w_groups, sc_conv, rw_mu, rw_w0, rw_w2, rw_a0, rw_a2, rw_g2,
                  rw_k_k, rw_k_a, rw_r_k, rw_lnx_g, rw_lnx_b, nsa_cmp_pos, nsa_cmp_w1, nsa_cmp_w2,
                  wb_stack):
    m = h.shape[0]
    s = m // bsz
    group = lambda name: _mm_cols(h, w_groups[name], layer, 0, w_groups[name].shape[1])
    per_batch = lambda z: z.reshape(bsz, s, z.shape[1])
    o_a = _short_conv_mixer(group("sc"), sc_conv, s)
    o_b = _rwkv7_mixer(per_batch(group("rw")), rw_mu, rw_w0, rw_w2, rw_a0, rw_a2, rw_g2, rw_k_k, rw_k_a, rw_r_k,
                       rw_lnx_g, rw_lnx_b)
    o_c = _retention_mixer(per_batch(group("ret")), *ret_tabs)
    o_d = _nsa_mixer(per_batch(group("nsa")), nsa_tabs, nsa_cmp_pos, nsa_cmp_w1, nsa_cmp_w2)
    outs = [o.reshape(m, o.shape[-1]) for o in (o_a, o_b, o_c, o_d)]
    assert tuple(o.shape[1] for o in outs) == BR_SIZES
    return _merge(h, w_groups["gate"], layer, 0, outs, wb_stack)


def kernel(x, mem, positions, ln_mix_pre, ln_mix_post, ln_mem_q, ln_mem_kv, ln_mem_post, ln_ffn_pre, ln_ffn_post, w_in, sc_conv, rw_mu, rw_w0, rw_w2, rw_a0, rw_a2, rw_g2, rw_k_k, rw_k_a, rw_r_k, rw_lnx_g, rw_lnx_b, nsa_cmp_pos, nsa_cmp_w1, nsa_cmp_w2, w_branch, w_out, mem_wq, mem_wkv, mem_wo, ffn_w_up, ffn_conv_w, ffn_conv_b, ffn_w_down):
    bsz, s, d = x.shape
    mem_len = mem.shape[1]
    xf = x.reshape(bsz * s, d)
    h = _prenorm(xf, ln_mix_pre[0])
    ret_cos, ret_sin, nsa_cos, nsa_sin = _rope_tables(positions)
    ret_tabs = (ret_cos, ret_sin)
    nsa_tabs = _nsa_tables(nsa_cos, nsa_sin)
    w_groups = _in_proj_groups(w_in)
    bf16 = lambda w: w.astype(jnp.bfloat16)
    wb_b, w_out_b, wq_b, wo_b, w_down_b = bf16(w_branch), bf16(w_out), bf16(mem_wq), bf16(mem_wo), bf16(ffn_w_down)
    wkv_t = bf16(jnp.swapaxes(mem_wkv, 1, 2))
    for l in range(DEPTH):
        merged = _hybrid_mixer(h, bsz, l, ret_tabs, nsa_tabs, w_groups, sc_conv[l], rw_mu[l], rw_w0[l], rw_w2[l], rw_a0[l],
                               rw_a2[l], rw_g2[l], rw_k_k[l], rw_k_a[l], rw_r_k[l], rw_lnx_g[l], rw_lnx_b[l],
                               nsa_cmp_pos[l], nsa_cmp_w1[l], nsa_cmp_w2[l], wb_b)
        xf, h = _mm_norm_res(merged, w_out_b, l, xf, ln_mix_post[l], ln_mem_q[l])
        mem_n = _prenorm(mem.reshape(bsz * mem_len, d), ln_mem_kv[l])
        kv = _mm_cols(mem_n, wkv_t, l, 0, 2 * d).astype(jnp.bfloat16)
        k_mem = kv[:, :d].reshape(bsz, mem_len, d)
        v_mem = kv[:, d:].reshape(bsz, mem_len, d)
        o = _xattn(h, wq_b, l, k_mem, v_mem, s)
        xf, h = _mm_norm_res(o, wo_b, l, xf, ln_mem_post[l], ln_ffn_pre[l])
        act = _ffn_up(h, ffn_w_up, l, ffn_conv_w[l], ffn_conv_b[l], s)
        g_next = ln_mix_pre[l + 1] if l + 1 < DEPTH else None
        xf, h = _mm_norm_res(act, w_down_b, l, xf, ln_ffn_post[l], g_next)
    return xf.reshape(bsz, s, d)
```

```python
import functools
import math

import jax
import jax.numpy as jnp
import numpy as np
from jax import lax
from jax.experimental import pallas as pl
from jax.experimental.pallas import tpu as pltpu

D_MODEL = 2048
DEPTH = 2
MEM_HEADS = 4
MEM_HEAD_DIM = D_MODEL // MEM_HEADS
SC_W = D_MODEL // 4
SC_K = 3
RW_HEAD_DIM = 64
RW_W = D_MODEL // 4
RW_HEADS = RW_W // RW_HEAD_DIM
RW_LORA_W = 96
RW_LORA_A = 96
RW_LORA_G = 256
RW_DECAY_SCALE = math.exp(-0.5)
RW_LN_EPS = 64e-5
RET_HEADS = 4
RET_DK = 128
RET_DV = 2 * RET_DK
RET_CHUNK = 256
RET_LN_EPS = 1e-5
NSA_HEADS = 8
NSA_KV_HEADS = 2
NSA_DH = 64
NSA_GROUP = NSA_HEADS // NSA_KV_HEADS
CMP_LEN = 32
CMP_STRIDE = 16
SEL_BLOCK = 64
SEL_TOP = 16
WINDOW = 512
Q_BLOCK = 256
ROPE_THETA = 10000.0
EPS = 1e-6
N_BRANCH = 4

SC_SIZES = (SC_W, SC_W, SC_W)
RW_SIZES = (RW_W, RW_W, RW_W, RW_LORA_W, RW_LORA_A, RW_LORA_G)
RET_SIZES = (RET_HEADS * RET_DK, RET_HEADS * RET_DK, RET_HEADS * RET_DV, RET_HEADS * RET_DV)
NSA_SIZES = (NSA_HEADS * NSA_DH,) + (NSA_KV_HEADS * NSA_DH,) * 6 + (NSA_HEADS * 3,)
GATE_SIZES = (D_MODEL,) * N_BRANCH
GROUP_SIZES = (sum(SC_SIZES), sum(RW_SIZES), sum(RET_SIZES), sum(NSA_SIZES), sum(GATE_SIZES))
BR_SIZES = (SC_W, RW_W, RET_HEADS * RET_DV, NSA_HEADS * NSA_DH)

V7X_VMEM_LIMIT_BYTES = 48 * 1024 * 1024
LANES = 128
SUBLANES = 8
_NT = (((1,), (1,)), ((), ()))


def _pick_tile(n, candidates):
    for c in candidates:
        if n % c == 0:
            return c
    return n


def _bdot(a, b):
    return jnp.dot(a.astype(jnp.bfloat16), b.astype(jnp.bfloat16), preferred_element_type=jnp.float32)


def _bdot_nt(a, b):
    return lax.dot_general(a.astype(jnp.bfloat16), b.astype(jnp.bfloat16), _NT,
                           preferred_element_type=jnp.float32)


def _head_sum(x, bd):
    hi = x.astype(jnp.bfloat16)
    lo = (x - hi.astype(jnp.float32)).astype(jnp.bfloat16)
    return (jnp.dot(hi, bd, preferred_element_type=jnp.float32)
            + jnp.dot(lo, bd, preferred_element_type=jnp.float32))


def _mm_kernel(a_ref, w_ref, o_ref, acc_ref):
    k = pl.program_id(2)

    @pl.when(k == 0)
    def _():
        acc_ref[...] = jnp.zeros_like(acc_ref)

    acc_ref[...] += lax.dot_general(a_ref[...], w_ref[...], _NT, preferred_element_type=jnp.float32)

    @pl.when(k == pl.num_programs(2) - 1)
    def _():
        o_ref[...] = acc_ref[...]


def _mm_cols(a, wt_stack, layer, col0, n):
    m, k = a.shape
    tn = next(c for c in (n if n <= 1536 else 1024, 1024, 768, 512, 256, 128) if n % c == 0 and col0 % c == 0)
    tm = _pick_tile(m, (1024, 512, 256, 128, 8))
    tk = _pick_tile(k, (2048, 1280, 1408, 1024, 512, 256, 128))
    j0 = col0 // tn
    return pl.pallas_call(
        _mm_kernel,
        out_shape=jax.ShapeDtypeStruct((m, n), jnp.float32),
        grid=(m // tm, n // tn, k // tk),
        in_specs=[pl.BlockSpec((tm, tk), lambda i, j, kk: (i, kk)),
                  pl.BlockSpec((None, tn, tk), lambda i, j, kk: (layer, j0 + j, kk))],
        out_specs=pl.BlockSpec((tm, tn), lambda i, j, kk: (i, j)),
        scratch_shapes=[pltpu.VMEM((tm, tn), jnp.float32)],
        compiler_params=pltpu.CompilerParams(
            dimension_semantics=("parallel", "parallel", "arbitrary"),
            vmem_limit_bytes=V7X_VMEM_LIMIT_BYTES),
        name="mm",
    )(a.astype(jnp.bfloat16), wt_stack)


def _rms(y, g):
    return y * lax.rsqrt(jnp.mean(y * y, axis=-1, keepdims=True) + EPS) * g


def _prenorm_kernel(x_ref, g_ref, h_ref):
    h_ref[...] = _rms(x_ref[...], g_ref[...]).astype(h_ref.dtype)


def _prenorm(x, g):
    m, d = x.shape
    tm = _pick_tile(m, (512, 256, 128, 8))
    return pl.pallas_call(
        _prenorm_kernel,
        out_shape=jax.ShapeDtypeStruct((m, d), jnp.bfloat16),
        grid=(m // tm,),
        in_specs=[pl.BlockSpec((tm, d), lambda i: (i, 0)), pl.BlockSpec((1, d), lambda i: (0, 0))],
        out_specs=pl.BlockSpec((tm, d), lambda i: (i, 0)),
        compiler_params=pltpu.CompilerParams(dimension_semantics=("parallel",)),
        name="prenorm",
    )(x, g.reshape(1, d))


def _mm_norm_res_kernel(emit_h, whole_k, a_ref, w_ref, res_ref, g_ref, *rest):
    if emit_h:
        g2_ref, x_ref, h_ref = rest[:3]
    else:
        x_ref = rest[0]

    def finish(y):
        x_new = res_ref[...] + _rms(y, g_ref[...])
        x_ref[...] = x_new
        if emit_h:
            h_ref[...] = _rms(x_new, g2_ref[...]).astype(h_ref.dtype)

    if whole_k:
        finish(jnp.dot(a_ref[...], w_ref[...], preferred_element_type=jnp.float32))
        return
    acc_ref = rest[-1]
    k = pl.program_id(1)

    @pl.when(k == 0)
    def _():
        acc_ref[...] = jnp.zeros_like(acc_ref)

    acc_ref[...] += jnp.dot(a_ref[...], w_ref[...], preferred_element_type=jnp.float32)

    @pl.when(k == pl.num_programs(1) - 1)
    def _():
        finish(acc_ref[...])


def _mm_norm_res(a, w_stack, layer, res, g, g_next):
    m, k = a.shape
    d = w_stack.shape[2]
    tk = _pick_tile(k, (2048, 1408, 1024, 512))
    tm = 512
    whole_k = tk == k
    emit_h = g_next is not None
    row = pl.BlockSpec((tm, d), lambda i, kk: (i, 0))
    vec = pl.BlockSpec((1, d), lambda i, kk: (0, 0))
    w_spec = (pl.BlockSpec((None, tk, d), lambda i, kk: (layer, 0, 0), pipeline_mode=pl.Buffered(1)) if whole_k
              else pl.BlockSpec((None, tk, d), lambda i, kk: (layer, kk, 0)))
    x_sd = jax.ShapeDtypeStruct((m, d), jnp.float32)
    h_sd = jax.ShapeDtypeStruct((m, d), jnp.bfloat16)
    out = pl.pallas_call(
        functools.partial(_mm_norm_res_kernel, emit_h, whole_k),
        out_shape=(x_sd, h_sd) if emit_h else x_sd,
        grid=(m // tm, k // tk),
        in_specs=[pl.BlockSpec((tm, tk), lambda i, kk: (i, kk)), w_spec, row, vec] + ([vec] if emit_h else []),
        out_specs=(row, row) if emit_h else row,
        scratch_shapes=[] if whole_k else [pltpu.VMEM((tm, d), jnp.float32)],
        compiler_params=pltpu.CompilerParams(
            dimension_semantics=("parallel", "arbitrary"), vmem_limit_bytes=V7X_VMEM_LIMIT_BYTES),
        name="mm_norm_res",
    )(a.astype(jnp.bfloat16), w_stack, res, g.reshape(1, d), *([g_next.reshape(1, d)] if emit_h else []))
    return out if emit_h else (out, None)


FFN_TM = 1024
FFN_TN = 512
HALO = SUBLANES


def _shift_rows(u, prev, n):
    rolled = pltpu.roll(u, n, axis=0)
    rows = lax.broadcasted_iota(jnp.int32, u.shape, 0)
    for r in range(n):
        rolled = jnp.where(rows == r, prev[HALO - n + r:HALO - n + r + 1, :], rolled)
    return rolled


def _ffn_up_kernel(seq_len, h_ref, halo_ref, wa_ref, wb_ref, cwa_ref, cwb_ref, ba_ref, bb_ref, o_ref):
    i = pl.program_id(0)
    h = h_ref[...]
    at_start = (i * FFN_TM) % seq_len == 0
    halo = jnp.where(at_start, jnp.zeros_like(halo_ref[...]), halo_ref[...])

    def conv(w_ref, cw_ref, b_ref):
        w = w_ref[...].astype(jnp.bfloat16)
        u = jnp.dot(h, w, preferred_element_type=jnp.float32)
        up = jnp.dot(halo, w, preferred_element_type=jnp.float32)
        cw = cw_ref[...]
        y = cw[0:1] * _shift_rows(u, up, 2)
        y = y + cw[1:2] * _shift_rows(u, up, 1)
        y = y + cw[2:3] * u
        return y + b_ref[...]

    a = conv(wa_ref, cwa_ref, ba_ref)
    b = conv(wb_ref, cwb_ref, bb_ref)
    o_ref[...] = (jax.nn.gelu(a, approximate=True) * b).astype(o_ref.dtype)


def _ffn_up(h, w_up_stack, layer, w_conv, b_conv, seq_len):
    m, d = h.shape
    f = w_up_stack.shape[2] // 2
    nj = f // FFN_TN
    b_conv = b_conv.reshape(1, 2 * f)
    tiles_per_halo = FFN_TM // HALO
    return pl.pallas_call(
        functools.partial(_ffn_up_kernel, seq_len),
        out_shape=jax.ShapeDtypeStruct((m, f), jnp.bfloat16),
        grid=(m // FFN_TM, nj),
        in_specs=[pl.BlockSpec((FFN_TM, d), lambda i, j: (i, 0)),
                  pl.BlockSpec((HALO, d), lambda i, j: (jnp.maximum(i * tiles_per_halo - 1, 0), 0)),
                  pl.BlockSpec((None, d, FFN_TN), lambda i, j: (layer, 0, j)),
                  pl.BlockSpec((None, d, FFN_TN), lambda i, j: (layer, 0, j + nj)),
                  pl.BlockSpec((3, FFN_TN), lambda i, j: (0, j)),
                  pl.BlockSpec((3, FFN_TN), lambda i, j: (0, j + nj)),
                  pl.BlockSpec((1, FFN_TN), lambda i, j: (0, j)),
                  pl.BlockSpec((1, FFN_TN), lambda i, j: (0, j + nj))],
        out_specs=pl.BlockSpec((FFN_TM, FFN_TN), lambda i, j: (i, j)),
        compiler_params=pltpu.CompilerParams(
            dimension_semantics=("parallel", "arbitrary"), vmem_limit_bytes=V7X_VMEM_LIMIT_BYTES),
        name="ffn_up",
    )(h, h, w_up_stack, w_up_stack, w_conv, w_conv, b_conv, b_conv)


MERGE_TM = 512
MERGE_TN = 512


def _merge_kernel(h_ref, wg0, wg1, wg2, wg3, o0, o1, o2, o3, wb0, wb1, wb2, wb3, out_ref):
    h = h_ref[...]
    acc = None
    for wg, o, wb in ((wg0, o0, wb0), (wg1, o1, wb1), (wg2, o2, wb2), (wg3, o3, wb3)):
        gate = jax.nn.sigmoid(lax.dot_general(h, wg[...], _NT, preferred_element_type=jnp.float32))
        term = gate * jnp.dot(o[...], wb[...], preferred_element_type=jnp.float32)
        acc = term if acc is None else acc + term
    out_ref[...] = acc.astype(out_ref.dtype)


def _merge(h, w_stack, layer, gate_col0, outs, wb_stack):
    m, d = h.shape
    nj = d // MERGE_TN
    j0 = gate_col0 // MERGE_TN
    assert gate_col0 % MERGE_TN == 0
    gate_specs = [pl.BlockSpec((None, MERGE_TN, d), functools.partial(lambda b, i, j: (layer, j0 + b * nj + j, 0), b))
                  for b in range(N_BRANCH)]
    o_specs = [pl.BlockSpec((MERGE_TM, o.shape[1]), lambda i, j: (i, 0)) for o in outs]
    row0 = np.cumsum([0] + [o.shape[1] for o in outs])
    assert all(r % o.shape[1] == 0 for r, o in zip(row0, outs))
    wb_specs = [pl.BlockSpec((None, o.shape[1], MERGE_TN),
                             functools.partial(lambda rb, i, j: (layer, rb, j), int(r) // o.shape[1]))
                for r, o in zip(row0, outs)]
    return pl.pallas_call(
        _merge_kernel,
        out_shape=jax.ShapeDtypeStruct((m, d), jnp.bfloat16),
        grid=(m // MERGE_TM, nj),
        in_specs=[pl.BlockSpec((MERGE_TM, d), lambda i, j: (i, 0))] + gate_specs + o_specs + wb_specs,
        out_specs=pl.BlockSpec((MERGE_TM, MERGE_TN), lambda i, j: (i, j)),
        compiler_params=pltpu.CompilerParams(
            dimension_semantics=("parallel", "arbitrary"), vmem_limit_bytes=V7X_VMEM_LIMIT_BYTES),
        name="merge",
    )(h, *([w_stack] * N_BRANCH), *[o.astype(jnp.bfloat16) for o in outs], *([wb_stack] * N_BRANCH))


XATT_TM = 512


def _xattn_kernel(h_ref, wq_ref, k_ref, v_ref, o_ref):
    q = jnp.dot(h_ref[...], wq_ref[...], preferred_element_type=jnp.float32)
    q = (q * (MEM_HEAD_DIM ** -0.5)).astype(jnp.bfloat16)
    for hh in range(MEM_HEADS):
        sl = slice(hh * MEM_HEAD_DIM, (hh + 1) * MEM_HEAD_DIM)
        s = lax.dot_general(q[:, sl], k_ref[0, :, sl], _NT, preferred_element_type=jnp.float32)
        p = jnp.exp(s - jnp.max(s, axis=-1, keepdims=True))
        p = p / jnp.sum(p, axis=-1, keepdims=True)
        o_ref[:, sl] = jnp.dot(p.astype(jnp.bfloat16), v_ref[0, :, sl],
                               preferred_element_type=jnp.float32).astype(o_ref.dtype)


def _xattn(h, wq_stack, layer, k, v, seq_len):
    m, d = h.shape
    mem_len = k.shape[1]
    per_batch = seq_len // XATT_TM
    kv_spec = pl.BlockSpec((1, mem_len, d), lambda i: (i // per_batch, 0, 0))
    return pl.pallas_call(
        _xattn_kernel,
        out_shape=jax.ShapeDtypeStruct((m, d), jnp.bfloat16),
        grid=(m // XATT_TM,),
        in_specs=[pl.BlockSpec((XATT_TM, d), lambda i: (i, 0)),
                  pl.BlockSpec((None, d, d), lambda i: (layer, 0, 0)), kv_spec, kv_spec],
        out_specs=pl.BlockSpec((XATT_TM, d), lambda i: (i, 0)),
        compiler_params=pltpu.CompilerParams(
            dimension_semantics=("parallel",), vmem_limit_bytes=V7X_VMEM_LIMIT_BYTES),
        name="xattn",
    )(h, wq_stack, k, v)


def _rope_freqs(d):
    return ROPE_THETA ** (-jnp.arange(0, d, 2, dtype=jnp.float32) / d)


def _retnet_freqs(d):
    return 1.0 / (ROPE_THETA ** jnp.linspace(0.0, 1.0, d // 2, dtype=jnp.float32))


WKV_CHUNK = 64


def _rwkv7_kernel(z_ref, mu_ref, w0_ref, a0_ref, kk_ref, ka_ref, rk_ref, lng_ref, lnb_ref,
                  ww2_ref, wa2_ref, wg2_ref, o_ref, s_ref, prev_ref, y_ref):
    c = WKV_CHUNK
    hd = RW_HEAD_DIM
    w = RW_W
    nb = z_ref.shape[0]
    f32, bf16 = jnp.float32, jnp.bfloat16

    @pl.when(pl.program_id(0) == 0)
    def _():
        s_ref[...] = jnp.zeros_like(s_ref)
        prev_ref[...] = jnp.zeros_like(prev_ref)

    lane_h = lax.broadcasted_iota(jnp.int32, (w, w), 0) // hd
    lane_w = lax.broadcasted_iota(jnp.int32, (w, w), 1) // hd
    head_bd = jnp.where(lane_h == lane_w, 1.0, 0.0).astype(bf16)
    row = lax.broadcasted_iota(jnp.int32, (c, c), 0)
    col = lax.broadcasted_iota(jnp.int32, (c, c), 1)
    tri = jnp.where(row >= col, 1.0, 0.0).astype(bf16)
    eye = jnp.where(row == col, 1.0, 0.0).astype(f32)
    r2 = lax.broadcasted_iota(jnp.int32, (2 * c, 2 * c), 0)
    c2 = lax.broadcasted_iota(jnp.int32, (2 * c, 2 * c), 1)
    rr = jnp.where(r2 >= c, r2 - c, r2)
    cc = jnp.where(c2 >= c, c2 - c, c2)
    tri_mask = cc < rr + jnp.where(r2 >= c, 1, 0)

    def prepare(b):
        z = z_ref[b]
        rows = lax.broadcasted_iota(jnp.int32, z.shape, 0)
        z_prev = jnp.where(rows == 0, prev_ref[b], pltpu.roll(z, 1, axis=0))
        prev_ref[b] = z[c - 1:c, :]
        z = z + (z_prev - z) * mu_ref[...]
        r, k, v, lora = z[:, 0:w], z[:, w:2 * w], z[:, 2 * w:3 * w], z[:, 3 * w:4 * w]
        logw = -RW_DECAY_SCALE * jax.nn.sigmoid(w0_ref[...] + _bdot(jnp.tanh(lora), ww2_ref[...]))
        a = jax.nn.sigmoid(a0_ref[...] + _bdot(lora, wa2_ref[...]))
        g = _bdot(jax.nn.sigmoid(lora), wg2_ref[...])
        kk = k * kk_ref[...]
        kk = kk / jnp.maximum(jnp.sqrt(_head_sum(kk * kk, head_bd)), 1e-12)
        k = k * (1.0 + (a - 1.0) * ka_ref[...])
        hi = logw.astype(bf16)
        rem = logw - hi.astype(f32)
        mid = rem.astype(bf16)
        lo = (rem - mid.astype(f32)).astype(bf16)
        cum = (jnp.dot(tri, hi, preferred_element_type=f32) + jnp.dot(tri, mid, preferred_element_type=f32)
               + jnp.dot(tri, lo, preferred_element_type=f32))
        kka = kk * a
        e_neg = jnp.exp(-cum)
        last = cum[c - 1:c, :]
        dec = jnp.exp(last - cum)
        return dict(r=r, k=k, v=v, g=g, vb=v.astype(bf16), g_last=jnp.exp(last),
                    qh=(kk * jnp.exp(cum - logw)).astype(bf16), rh=(r * jnp.exp(cum)).astype(bf16),
                    bh=(kka * e_neg).astype(bf16), kh=(k * e_neg).astype(bf16),
                    bd=(kka * dec).astype(bf16), kd=(k * dec).astype(bf16))

    pre = [prepare(b) for b in range(nb)]

    units = [(b, h) for b in range(nb) for h in range(RW_HEADS)]
    n_units = range(len(units))
    part = lambda name, i: pre[units[i][0]][name][:, units[i][1] * hd:(units[i][1] + 1) * hd]
    a1 = [jnp.concatenate([part("qh", i), part("rh", i)], axis=0) for i in n_units]
    b1 = [jnp.concatenate([part("bh", i), part("kh", i)], axis=0) for i in n_units]
    ss = [jnp.where(tri_mask, lax.dot_general(a1[i], b1[i], _NT, preferred_element_type=f32), 0.0)
          for i in n_units]
    s_old = [s_ref[i] for i in n_units]
    qr = [lax.dot_general(a1[i], s_old[i].astype(bf16), _NT, preferred_element_type=f32) for i in n_units]
    lm = [_bdot(ss[i][:, c:2 * c], part("vb", i)) for i in n_units]
    n = [-ss[i][0:c, 0:c] for i in n_units]
    t = [eye + n[i] for i in n_units]
    for _ in range(5):
        n = [_bdot(n[i], n[i]) for i in n_units]
        t = [t[i] + _bdot(t[i], n[i]) for i in n_units]
    u = [-_bdot(t[i], qr[i][0:c] + lm[i][0:c]) for i in n_units]
    for i in n_units:
        b, h = units[i]
        y_ref[b, :, h * hd:(h + 1) * hd] = qr[i][c:2 * c] + lm[i][c:2 * c] + _bdot(ss[i][c:2 * c, 0:c], u[i])
    for i in n_units:
        zt = jnp.concatenate([u[i], part("v", i)], axis=0).T
        x = jnp.concatenate([part("bd", i), part("kd", i)], axis=0)
        s_ref[i] = s_old[i] * part("g_last", i) + _bdot(zt, x)

    for b in range(nb):
        p = pre[b]
        y = y_ref[b]
        mean = _head_sum(y, head_bd) * (1.0 / hd)
        yc = y - mean
        var = _head_sum(yc * yc, head_bd) * (1.0 / hd)
        yn = yc * lax.rsqrt(var + RW_LN_EPS) * lng_ref[...] + lnb_ref[...]
        bonus = _head_sum(p["r"] * p["k"] * rk_ref[...], head_bd) * p["v"]
        o_ref[b] = ((yn + bonus) * p["g"]).astype(o_ref.dtype)


def _rwkv7_mixer(z, mu, w0, w_w2, a0, w_a2, w_g2, k_k, k_a, r_k, lnx_g, lnx_b):
    bsz, s, zw = z.shape
    w = RW_W
    n_lora = RW_LORA_W + RW_LORA_A + RW_LORA_G
    row = lambda t: t.reshape(1, -1)
    mu = jnp.pad(mu, (0, zw - mu.shape[0])).reshape(1, zw)
    o_w, o_a = RW_LORA_W, RW_LORA_W + RW_LORA_A
    pad_rows = lambda m, lo: jnp.pad(m, ((lo, w - lo - m.shape[0]), (0, 0))).astype(jnp.bfloat16)
    vec = pl.BlockSpec((1, w), lambda c: (0, 0))
    mat = pl.BlockSpec((w, w), lambda c: (0, 0))
    assert zw == 4 * w and n_lora <= w
    return pl.pallas_call(
        _rwkv7_kernel,
        out_shape=jax.ShapeDtypeStruct((bsz, s, w), jnp.bfloat16),
        grid=(s // WKV_CHUNK,),
        in_specs=[pl.BlockSpec((bsz, WKV_CHUNK, zw), lambda c: (0, c, 0)),
                  pl.BlockSpec((1, zw), lambda c: (0, 0))] + [vec] * 7 + [mat] * 3,
        out_specs=pl.BlockSpec((bsz, WKV_CHUNK, w), lambda c: (0, c, 0)),
        scratch_shapes=[pltpu.VMEM((bsz * RW_HEADS, RW_HEAD_DIM, RW_HEAD_DIM), jnp.float32),
                        pltpu.VMEM((bsz, 1, zw), jnp.float32),
                        pltpu.VMEM((bsz, WKV_CHUNK, w), jnp.float32)],
        compiler_params=pltpu.CompilerParams(dimension_semantics=("arbitrary",)),
        name="rwkv7",
    )(z, mu, row(w0), row(a0), row(k_k), row(k_a), row(r_k), row(lnx_g), row(lnx_b),
      pad_rows(w_w2, 0), pad_rows(w_a2, o_w), pad_rows(w_g2, o_a))


RET_LOG_DECAY = tuple(math.log(1.0 - 2.0 ** (-5.0 - h)) for h in range(RET_HEADS))


def _retention_kernel(q_ref, k_ref, v_ref, g_ref, cos_ref, sin_ref, o_ref, s_ref):
    c, dk, dv = RET_CHUNK, RET_DK, RET_DV
    f32 = jnp.float32

    @pl.when(pl.program_id(1) == 0)
    def _():
        s_ref[...] = jnp.zeros_like(s_ref)

    cos = cos_ref[0]
    sin = sin_ref[0]
    diff = (lax.broadcasted_iota(jnp.int32, (c, c), 0) - lax.broadcasted_iota(jnp.int32, (c, c), 1)).astype(f32)
    tok = lax.broadcasted_iota(jnp.int32, (c, 1), 0).astype(f32)
    rope = lambda x: x * cos + pltpu.roll(x, dk // 2, axis=1) * sin
    for h in range(RET_HEADS):
        lg = RET_LOG_DECAY[h]
        q = rope(q_ref[0, :, h * dk:(h + 1) * dk])
        k = rope(k_ref[0, :, h * dk:(h + 1) * dk]) * (dk ** -0.5)
        v = v_ref[0, :, h * dv:(h + 1) * dv]
        decay_in = jnp.where(diff >= 0, jnp.exp(jnp.maximum(diff, 0.0) * lg), 0.0)
        scores = _bdot_nt(q, k) * decay_in
        state = s_ref[h]
        o = _bdot(scores, v) + _bdot(q * jnp.exp((tok + 1.0) * lg), state)
        s_ref[h] = state * math.exp(c * lg) + _bdot((k * jnp.exp((c - 1.0 - tok) * lg)).T, v)
        mu = jnp.mean(o, axis=-1, keepdims=True)
        oc = o - mu
        var = jnp.mean(oc * oc, axis=-1, keepdims=True)
        gate = g_ref[0, :, h * dv:(h + 1) * dv]
        o_ref[0, :, h * dv:(h + 1) * dv] = (oc * lax.rsqrt(var + RET_LN_EPS) * gate * jax.nn.sigmoid(gate)
                                            ).astype(o_ref.dtype)


def _retention_mixer(z, cos, sin):
    bsz, s, _ = z.shape
    qk_w, v_w = RET_HEADS * RET_DK, RET_HEADS * RET_DV
    c = RET_CHUNK
    tab = pl.BlockSpec((1, c, RET_DK), lambda b, i: (b, i, 0))
    return pl.pallas_call(
        _retention_kernel,
        out_shape=jax.ShapeDtypeStruct((bsz, s, v_w), jnp.bfloat16),
        grid=(bsz, s // c),
        in_specs=[pl.BlockSpec((1, c, qk_w), lambda b, i: (b, i, 0)),
                  pl.BlockSpec((1, c, qk_w), lambda b, i: (b, i, 1)),
                  pl.BlockSpec((1, c, v_w), lambda b, i: (b, i, 1)),
                  pl.BlockSpec((1, c, v_w), lambda b, i: (b, i, 2)), tab, tab],
        out_specs=pl.BlockSpec((1, c, v_w), lambda b, i: (b, i, 0)),
        scratch_shapes=[pltpu.VMEM((RET_HEADS, RET_DK, RET_DV), jnp.float32)],
        compiler_params=pltpu.CompilerParams(dimension_semantics=("parallel", "arbitrary")),
        name="retention",
    )(z, z, z, z, cos, sin)


TRIG_TM = 1024


def _trig_kernel(pos_ref, freq_ref, rc_ref, rs_ref, nc_ref, ns_ref):
    ang = pos_ref[...] * freq_ref[...]
    c, sn = jnp.cos(ang), jnp.sin(ang)
    n_ret, n_nsa = RET_DK // 2, NSA_DH // 2
    rc_ref[...] = jnp.concatenate([c[:, :n_ret]] * 2, axis=1)
    rs_ref[...] = jnp.concatenate([-sn[:, :n_ret], sn[:, :n_ret]], axis=1)
    nsa_c, nsa_s = c[:, n_ret:n_ret + n_nsa], sn[:, n_ret:n_ret + n_nsa]
    nc_ref[...] = jnp.concatenate([nsa_c, nsa_c] * NSA_HEADS, axis=1)
    ns_ref[...] = jnp.concatenate([-nsa_s, nsa_s] * NSA_HEADS, axis=1)


def _rope_tables(positions):
    bsz, s = positions.shape
    m = bsz * s
    n_ret, n_nsa = RET_DK // 2, NSA_DH // 2
    assert n_ret + n_nsa <= LANES
    freq = jnp.concatenate([_retnet_freqs(RET_DK), _rope_freqs(NSA_DH),
                            jnp.zeros((LANES - n_ret - n_nsa,), jnp.float32)]).reshape(1, LANES)
    tm = _pick_tile(m, (TRIG_TM, 512, 256, 128, 8))
    spec = lambda w: pl.BlockSpec((tm, w), lambda i: (i, 0))
    sd = lambda w: jax.ShapeDtypeStruct((m, w), jnp.float32)
    nsa_w = NSA_HEADS * NSA_DH
    tabs = pl.pallas_call(
        _trig_kernel,
        out_shape=(sd(RET_DK), sd(RET_DK), sd(nsa_w), sd(nsa_w)),
        grid=(m // tm,),
        in_specs=[spec(1), pl.BlockSpec((1, LANES), lambda i: (0, 0))],
        out_specs=(spec(RET_DK), spec(RET_DK), spec(nsa_w), spec(nsa_w)),
        compiler_params=pltpu.CompilerParams(dimension_semantics=("parallel",)),
        name="rope_tables",
    )(positions.astype(jnp.float32).reshape(m, 1), freq)
    return [t.reshape(bsz, s, t.shape[1]) for t in tabs]


SC_TM = 512


def _short_conv_kernel(seq_len, b_ref, c_ref, x_ref, ch_ref, xh_ref, w_ref, o_ref):
    at_start = (pl.program_id(0) * SC_TM) % seq_len == 0
    u = c_ref[...] * x_ref[...]
    up = jnp.where(at_start, 0.0, ch_ref[...] * xh_ref[...])
    w = w_ref[...]
    y = w[0:1] * _shift_rows(u, up, 2) + w[1:2] * _shift_rows(u, up, 1) + w[2:3] * u
    o_ref[...] = (b_ref[...] * y).astype(o_ref.dtype)


def _short_conv_mixer(z, w_conv, seq_len):
    m = z.shape[0]
    w = SC_W
    per_halo = SC_TM // HALO
    tile = lambda j: pl.BlockSpec((SC_TM, w), lambda i: (i, j))
    halo = lambda j: pl.BlockSpec((HALO, w), lambda i: (jnp.maximum(i * per_halo - 1, 0), j))
    return pl.pallas_call(
        functools.partial(_short_conv_kernel, seq_len),
        out_shape=jax.ShapeDtypeStruct((m, w), jnp.bfloat16),
        grid=(m // SC_TM,),
        in_specs=[tile(0), tile(1), tile(2), halo(1), halo(2), pl.BlockSpec((SC_K, w), lambda i: (0, 0))],
        out_specs=pl.BlockSpec((SC_TM, w), lambda i: (i, 0)),
        compiler_params=pltpu.CompilerParams(dimension_semantics=("parallel",)),
        name="short_conv",
    )(z, z, z, z, z, w_conv)


NSA_KEY_TILE = 512
NSA_WIN_SPAN = WINDOW + Q_BLOCK
NSA_PREP_TM = 512
NEG_BIG = -1e30
NSA_Q_W = NSA_HEADS * NSA_DH
NSA_KV_W = NSA_KV_HEADS * NSA_DH
NSA_GATE_OFF = NSA_Q_W + 6 * NSA_KV_W
NSA_VAL_W = LANES


def _rope_lanes(x, cos, sin):
    width = x.shape[1]
    half = NSA_DH // 2
    lane = lax.broadcasted_iota(jnp.int32, x.shape, 1)
    other = jnp.where(lane % NSA_DH < half, pltpu.roll(x, width - half, axis=1), pltpu.roll(x, half, axis=1))
    return x * cos + other * sin


def _nsa_prep_kernel(z_ref, cos_ref, sin_ref, q_ref, ks_ref, vs_ref, kw_ref, vw_ref, kc_ref, vc_ref,
                     gc_ref, gs_ref, gw_ref):
    cos, sin = cos_ref[0], sin_ref[0]
    z = z_ref[0]
    q_ref[0] = (_rope_lanes(z[:, 0:NSA_Q_W], cos, sin) * (NSA_DH ** -0.5)).astype(q_ref.dtype)
    kv = lambda i: z[:, NSA_Q_W + i * NSA_KV_W:NSA_Q_W + (i + 1) * NSA_KV_W]
    cos_kv, sin_kv = cos[:, 0:NSA_KV_W], sin[:, 0:NSA_KV_W]
    pieces = ((kc_ref, kv(0)), (vc_ref, kv(1)), (ks_ref, _rope_lanes(kv(2), cos_kv, sin_kv)), (vs_ref, kv(3)),
              (kw_ref, _rope_lanes(kv(4), cos_kv, sin_kv)), (vw_ref, kv(5)))
    n_sel = ks_ref.shape[3] - NSA_DH
    tok = pl.program_id(1) * NSA_PREP_TM + lax.broadcasted_iota(jnp.int32, (NSA_PREP_TM, n_sel), 0)
    block_1hot = jnp.where(tok // SEL_BLOCK == lax.broadcasted_iota(jnp.int32, (NSA_PREP_TM, n_sel), 1), 1.0, 0.0)
    ones_col = jnp.where(lax.broadcasted_iota(jnp.int32, (NSA_PREP_TM, NSA_VAL_W - NSA_DH), 1) == 0, 1.0, 0.0)
    for ref, val in pieces:
        for g in range(NSA_KV_HEADS):
            val_g = val[:, g * NSA_DH:(g + 1) * NSA_DH]
            if ref is ks_ref:
                val_g = jnp.concatenate([val_g, block_1hot], axis=1)
            elif ref is vs_ref or ref is vw_ref:
                val_g = jnp.concatenate([val_g, ones_col], axis=1)
            ref[0, g] = val_g.astype(ref.dtype)
    gate = jax.nn.sigmoid(z[:, NSA_GATE_OFF:NSA_GATE_OFF + LANES])
    src = lax.broadcasted_iota(jnp.int32, (LANES, NSA_Q_W), 0)
    head = lax.broadcasted_iota(jnp.int32, (LANES, NSA_Q_W), 1) // NSA_DH
    for j, ref in enumerate((gc_ref, gs_ref, gw_ref)):
        expand = jnp.where(src == 3 * head + j, 1.0, 0.0).astype(jnp.bfloat16)
        ref[0] = _head_sum(gate, expand)


def _nsa_prep(z, cos, sin):
    bsz, s, zw = z.shape
    tm = NSA_PREP_TM
    wide = pl.BlockSpec((1, tm, NSA_Q_W), lambda b, i: (b, i, 0))
    grp = pl.BlockSpec((1, NSA_KV_HEADS, tm, NSA_DH), lambda b, i: (b, 0, i, 0))
    grp_sd = lambda dt: jax.ShapeDtypeStruct((bsz, NSA_KV_HEADS, s, NSA_DH), dt)
    ks_w = NSA_DH + s // SEL_BLOCK
    ks_spec = pl.BlockSpec((1, NSA_KV_HEADS, tm, ks_w), lambda b, i: (b, 0, i, 0))
    ks_sd = jax.ShapeDtypeStruct((bsz, NSA_KV_HEADS, s, ks_w), jnp.bfloat16)
    val_spec = pl.BlockSpec((1, NSA_KV_HEADS, tm, NSA_VAL_W), lambda b, i: (b, 0, i, 0))
    val_sd = jax.ShapeDtypeStruct((bsz, NSA_KV_HEADS, s, NSA_VAL_W), jnp.bfloat16)
    wide_sd = lambda dt: jax.ShapeDtypeStruct((bsz, s, NSA_Q_W), dt)
    bf16, f32 = jnp.bfloat16, jnp.float32
    return pl.pallas_call(
        _nsa_prep_kernel,
        out_shape=(wide_sd(bf16), ks_sd, val_sd, grp_sd(bf16), val_sd, grp_sd(f32), grp_sd(f32),
                   wide_sd(f32), wide_sd(f32), wide_sd(f32)),
        grid=(bsz, s // tm),
        in_specs=[pl.BlockSpec((1, tm, zw), lambda b, i: (b, i, 0)), wide, wide],
        out_specs=(wide, ks_spec, val_spec, grp, val_spec, grp, grp, wide, wide, wide),
        compiler_params=pltpu.CompilerParams(dimension_semantics=("parallel", "parallel")),
        name="nsa_prep",
    )(z, cos, sin)


def _nsa_compress_kernel(hk_ref, hv_ref, pos_ref, w1_ref, w2_ref, cos_ref, sin_ref, kc_ref, vc_ref):
    half = w1_ref.shape[1] // 2
    n = hk_ref.shape[1]
    for i, (h_ref, o_ref) in enumerate(((hk_ref, kc_ref), (hv_ref, vc_ref))):
        h = h_ref[0]
        first = _bdot(h, w1_ref[i, 0:half, :])
        second = pltpu.roll(_bdot(h, w1_ref[i, half:2 * half, :]), n - 1, axis=0)
        bias = _bdot(jnp.broadcast_to(pos_ref[i], (SUBLANES, 2 * half)), w1_ref[i])[0:1]
        out = _bdot(jax.nn.gelu(first + second + bias, approximate=True), w2_ref[i])
        if i == 0:
            src = lax.broadcasted_iota(jnp.int32, (NSA_DH, NSA_DH), 0)
            dst = lax.broadcasted_iota(jnp.int32, (NSA_DH, NSA_DH), 1)
            swap = jnp.where(src == (dst + NSA_DH // 2) % NSA_DH, 1.0, 0.0).astype(jnp.bfloat16)
            out = out * cos_ref[0] + _head_sum(out, swap) * sin_ref[0]
        o_ref[0] = out.astype(o_ref.dtype)


def _nsa_compress(hk, hv, pos_flat, w1, w2, cos_c, sin_c):
    bg, n, hw = hk.shape
    ng = NSA_KV_HEADS
    blk = pl.BlockSpec((1, n, hw), lambda i: (i, 0, 0))
    tab = pl.BlockSpec((1, n, NSA_DH), lambda i: (i // ng, 0, 0))
    out = pl.BlockSpec((1, n, NSA_DH), lambda i: (i, 0, 0))
    whole = lambda a: pl.BlockSpec(a.shape, lambda i: (0,) * a.ndim)
    sd = jax.ShapeDtypeStruct((bg, n, NSA_DH), jnp.bfloat16)
    return pl.pallas_call(
        _nsa_compress_kernel,
        out_shape=(sd, sd),
        grid=(bg,),
        in_specs=[blk, blk, whole(pos_flat), whole(w1), whole(w2), tab, tab],
        out_specs=(out, out),
        compiler_params=pltpu.CompilerParams(
            dimension_semantics=("parallel",), vmem_limit_bytes=V7X_VMEM_LIMIT_BYTES),
        name="nsa_compress",
    )(hk, hv, pos_flat, w1, w2, cos_c, sin_c)


def _nsa_attn_kernel(q_ref, kc_ref, vc_ref, ks_ref, vs_ref, kw_ref, vw_ref, gc_ref, gs_ref, gw_ref, o_ref):
    f32, bf16 = jnp.float32, jnp.bfloat16
    qi = pl.program_id(2)
    q0 = qi * Q_BLOCK
    hpg, dh, tk = NSA_GROUP, NSA_DH, NSA_KEY_TILE
    rows = hpg * Q_BLOCK
    n_sel = ks_ref.shape[3] - dh
    n_cmp = kc_ref.shape[2]
    q_all = q_ref[0]
    q = jnp.concatenate([q_all[:, h * dh:(h + 1) * dh] for h in range(hpg)], axis=0)
    by_head = lambda o: jnp.concatenate([o[h * Q_BLOCK:(h + 1) * Q_BLOCK] for h in range(hpg)], axis=1)

    tq_c = q0 + lax.broadcasted_iota(jnp.int32, (Q_BLOCK, n_cmp), 0)
    c_end = lax.broadcasted_iota(jnp.int32, (Q_BLOCK, n_cmp), 1) * CMP_STRIDE + (CMP_LEN - 1)
    c_ok = jnp.concatenate([jnp.where(c_end <= tq_c, 1.0, 0.0)] * hpg, axis=0) > 0.5
    s = jnp.where(c_ok, lax.dot_general(q, kc_ref[0, 0], _NT, preferred_element_type=f32), NEG_BIG)
    m = jnp.max(s, axis=-1, keepdims=True)
    e = jnp.where(c_ok, jnp.exp(s - m), 0.0)
    p = e / jnp.maximum(jnp.sum(e, axis=-1, keepdims=True), 1e-30)
    o_cmp = jnp.dot(p.astype(bf16), vc_ref[0, 0], preferred_element_type=f32)

    p_sum = p[0:Q_BLOCK]
    for h in range(1, hpg):
        p_sum = p_sum + p[h * Q_BLOCK:(h + 1) * Q_BLOCK]
    c_start = lax.broadcasted_iota(jnp.int32, (n_sel, n_cmp), 1) * CMP_STRIDE
    j_blk = lax.broadcasted_iota(jnp.int32, (n_sel, n_cmp), 0)
    overlap_t = jnp.where((c_start < (j_blk + 1) * SEL_BLOCK) & (c_start + CMP_LEN > j_blk * SEL_BLOCK),
                          1.0, 0.0).astype(bf16)
    p_hi = p_sum.astype(bf16)
    p_lo = (p_sum - p_hi.astype(f32)).astype(bf16)
    imp = (lax.dot_general(overlap_t, p_hi, _NT, preferred_element_type=f32)
           + lax.dot_general(overlap_t, p_lo, _NT, preferred_element_type=f32))
    jb = lax.broadcasted_iota(jnp.int32, (n_sel, Q_BLOCK), 0)
    cur = (q0 + lax.broadcasted_iota(jnp.int32, (n_sel, Q_BLOCK), 1)) // SEL_BLOCK
    forced = (jb == 0) | (jb == cur) | (jb == cur - 1)
    imp = jnp.where(forced, -NEG_BIG, jnp.where(jb <= cur, imp, NEG_BIG))
    rank = jnp.zeros((n_sel, Q_BLOCK), f32)
    for i in range(n_sel):
        row_i = imp[i:i + 1, :]
        ahead = (row_i > imp) | ((row_i == imp) & (jb > i))
        rank = rank + jnp.where(ahead, 1.0, 0.0)
    sel_bias = jnp.where(rank < float(min(SEL_TOP, n_sel)), 0.0, NEG_BIG).T.astype(bf16)

    q_sel = jnp.concatenate([q, jnp.concatenate([sel_bias] * hpg, axis=0)], axis=1)

    def tile_update(kt, carry, on_diagonal):
        m, acc = carry
        k0 = pl.multiple_of(kt * tk, tk)
        s = lax.dot_general(q_sel, ks_ref[0, 0, pl.ds(k0, tk), :], _NT, preferred_element_type=f32)
        if on_diagonal:
            t_pos = q0 + lax.broadcasted_iota(jnp.int32, (Q_BLOCK, tk), 0)
            key = k0 + lax.broadcasted_iota(jnp.int32, (Q_BLOCK, tk), 1)
            causal = jnp.concatenate([jnp.where(key <= t_pos, 0.0, NEG_BIG)] * hpg, axis=0)
            s = s + causal
        m_new = jnp.maximum(m, jnp.max(s, axis=-1, keepdims=True))
        p = jnp.exp(s - m_new).astype(bf16)
        acc = jnp.exp(m - m_new) * acc + jnp.dot(p, vs_ref[0, 0, pl.ds(k0, tk), :], preferred_element_type=f32)
        return m_new, acc

    init = (jnp.full((rows, 1), NEG_BIG, f32), jnp.zeros((rows, NSA_VAL_W), f32))
    last_tile = (q0 + Q_BLOCK - 1) // tk
    carry = lax.fori_loop(0, last_tile, lambda kt, c: tile_update(kt, c, False), init)
    _, acc = tile_update(last_tile, carry, True)
    o_sel = acc[:, 0:dh] / jnp.maximum(acc[:, dh:dh + 1], 1e-30)

    w0 = pl.multiple_of(jnp.maximum(q0 - WINDOW, 0), Q_BLOCK)
    kwin = kw_ref[0, 0, pl.ds(w0, NSA_WIN_SPAN), :]
    vwin = vw_ref[0, 0, pl.ds(w0, NSA_WIN_SPAN), :]
    tq = q0 + lax.broadcasted_iota(jnp.int32, (Q_BLOCK, NSA_WIN_SPAN), 0)
    kp = w0 + lax.broadcasted_iota(jnp.int32, (Q_BLOCK, NSA_WIN_SPAN), 1)
    dlt = tq - kp
    wbias = jnp.where((dlt >= 0) & (dlt < WINDOW), 0.0, NEG_BIG).astype(f32)
    wbias = jnp.concatenate([wbias] * hpg, axis=0)
    s = lax.dot_general(q, kwin, _NT, preferred_element_type=f32) + wbias
    p = jnp.exp(s - jnp.max(s, axis=-1, keepdims=True)).astype(bf16)
    pv = jnp.dot(p, vwin, preferred_element_type=f32)
    o_win = pv[:, 0:dh] / jnp.maximum(pv[:, dh:dh + 1], 1e-30)

    o_ref[0] = (gc_ref[0] * by_head(o_cmp) + gs_ref[0] * by_head(o_sel) + gw_ref[0] * by_head(o_win)
                ).astype(o_ref.dtype)


def _nsa_attn(q, k_cmp, v_cmp, ks, vs, kw, vw, g_cmp, g_sel, g_win):
    bsz, s, _ = q.shape
    ng, dh = NSA_KV_HEADS, NSA_DH
    gw = NSA_GROUP * dh
    kv_spec = pl.BlockSpec((1, 1, s, dh), lambda b, g, i: (b, g, 0, 0))
    val_spec = pl.BlockSpec((1, 1, s, NSA_VAL_W), lambda b, g, i: (b, g, 0, 0))
    ks_spec = pl.BlockSpec((1, 1, s, ks.shape[3]), lambda b, g, i: (b, g, 0, 0))
    cmp_spec = pl.BlockSpec((1, 1, k_cmp.shape[2], dh), lambda b, g, i: (b, g, 0, 0))
    q_spec = pl.BlockSpec((1, Q_BLOCK, gw), lambda b, g, i: (b, i, g))
    return pl.pallas_call(
        _nsa_attn_kernel,
        out_shape=jax.ShapeDtypeStruct(q.shape, jnp.bfloat16),
        grid=(bsz, ng, s // Q_BLOCK),
        in_specs=[q_spec, cmp_spec, cmp_spec, ks_spec, val_spec, kv_spec, val_spec, q_spec, q_spec, q_spec],
        out_specs=q_spec,
        compiler_params=pltpu.CompilerParams(
            dimension_semantics=("parallel", "parallel", "arbitrary"),
            vmem_limit_bytes=V7X_VMEM_LIMIT_BYTES),
        name="nsa_attn",
    )(q, k_cmp, v_cmp, ks, vs, kw, vw, g_cmp, g_sel, g_win)


def _nsa_mixer(z, nsa_tabs, cmp_pos, cmp_w1, cmp_w2):
    bsz, s, _ = z.shape
    ng, dh = NSA_KV_HEADS, NSA_DH
    cos, sin, cos_c, sin_c = nsa_tabs
    q, ks, vs, kw, vw, kc, vc, g_cmp, g_sel, g_win = _nsa_prep(z, cos, sin)
    n_blk = s // CMP_STRIDE
    flat = lambda t: t.reshape(bsz * ng, n_blk, CMP_STRIDE * dh)
    k_cmp, v_cmp = _nsa_compress(flat(kc), flat(vc), cmp_pos.reshape(2, 1, CMP_LEN * dh), cmp_w1, cmp_w2,
                                 cos_c, sin_c)
    grouped = lambda t: t.reshape(bsz, ng, n_blk, dh)
    return _nsa_attn(q, grouped(k_cmp), grouped(v_cmp), ks, vs, kw, vw, g_cmp, g_sel, g_win)


def _nsa_tables(cos, sin):
    end = CMP_LEN - 1
    at_ends = lambda t: jnp.pad(t[:, end::CMP_STRIDE, :NSA_DH], ((0, 0), (0, 1), (0, 0)))
    return cos, sin, at_ends(cos), at_ends(sin)


IN_GROUPS = (("sc", 0, GROUP_SIZES[0], 1536),
             ("rw", GROUP_SIZES[0], GROUP_SIZES[1], 2048),
             ("ret", GROUP_SIZES[0] + GROUP_SIZES[1], GROUP_SIZES[2], 3072),
             ("nsa", sum(GROUP_SIZES[:3]), GROUP_SIZES[3], 1536),
             ("gate", sum(GROUP_SIZES[:4]), GROUP_SIZES[4], 8192))


def _in_proj_groups(w_in):
    w_t = jnp.swapaxes(w_in, 1, 2)
    return {name: jnp.pad(w_t[:, start:start + width], ((0, 0), (0, padded - width), (0, 0))).astype(jnp.bfloat16)
            for name, start, width, padded in IN_GROUPS}


def _hybrid_mixer(h, bsz, layer, ret_tabs, nsa_tabs, w_groups, sc_conv, rw_mu, rw_w0, rw_w2, rw_a0, rw_a2, rw_g2,
                  rw_k_k, rw_k_a, rw_r_k, rw_lnx_g, rw_lnx_b, nsa_cmp_pos, nsa_cmp_w1, nsa_cmp_w2,
                  wb_stack):
    m = h.shape[0]
    s = m // bsz
    group = lambda name: _mm_cols(h, w_groups[name], layer, 0, w_groups[name].shape[1])
    per_batch = lambda z: z.reshape(bsz, s, z.shape[1])
    o_a = _short_conv_mixer(group("sc"), sc_conv, s)
    o_b = _rwkv7_mixer(per_batch(group("rw")), rw_mu, rw_w0, rw_w2, rw_a0, rw_a2, rw_g2, rw_k_k, rw_k_a, rw_r_k,
                       rw_lnx_g, rw_lnx_b)
    o_c = _retention_mixer(per_batch(group("ret")), *ret_tabs)
    o_d = _nsa_mixer(per_batch(group("nsa")), nsa_tabs, nsa_cmp_pos, nsa_cmp_w1, nsa_cmp_w2)
    outs = [o.reshape(m, o.shape[-1]) for o in (o_a, o_b, o_c, o_d)]
    assert tuple(o.shape[1] for o in outs) == BR_SIZES
    return _merge(h, w_groups["gate"], layer, 0, outs, wb_stack)


def kernel(x, mem, positions, ln_mix_pre, ln_mix_post, ln_mem_q, ln_mem_kv, ln_mem_post, ln_ffn_pre, ln_ffn_post, w_in, sc_conv, rw_mu, rw_w0, rw_w2, rw_a0, rw_a2, rw_g2, rw_k_k, rw_k_a, rw_r_k, rw_lnx_g, rw_lnx_b, nsa_cmp_pos, nsa_cmp_w1, nsa_cmp_w2, w_branch, w_out, mem_wq, mem_wkv, mem_wo, ffn_w_up, ffn_conv_w, ffn_conv_b, ffn_w_down):
    bsz, s, d = x.shape
    mem_len = mem.shape[1]
    xf = x.reshape(bsz * s, d)
    h = _prenorm(xf, ln_mix_pre[0])
    ret_cos, ret_sin, nsa_cos, nsa_sin = _rope_tables(positions)
    ret_tabs = (ret_cos, ret_sin)
    nsa_tabs = _nsa_tables(nsa_cos, nsa_sin)
    w_groups = _in_proj_groups(w_in)
    bf16 = lambda w: w.astype(jnp.bfloat16)
    wb_b, w_out_b, wq_b, wo_b, w_down_b = bf16(w_branch), bf16(w_out), bf16(mem_wq), bf16(mem_wo), bf16(ffn_w_down)
    wkv_t = bf16(jnp.swapaxes(mem_wkv, 1, 2))
    for l in range(DEPTH):
        merged = _hybrid_mixer(h, bsz, l, ret_tabs, nsa_tabs, w_groups, sc_conv[l], rw_mu[l], rw_w0[l], rw_w2[l], rw_a0[l],
                               rw_a2[l], rw_g2[l], rw_k_k[l], rw_k_a[l], rw_r_k[l], rw_lnx_g[l], rw_lnx_b[l],
                               nsa_cmp_pos[l], nsa_cmp_w1[l], nsa_cmp_w2[l], wb_b)
        xf, h = _mm_norm_res(merged, w_out_b, l, xf, ln_mix_post[l], ln_mem_q[l])
        mem_n = _prenorm(mem.reshape(bsz * mem_len, d), ln_mem_kv[l])
        kv = _mm_cols(mem_n, wkv_t, l, 0, 2 * d).astype(jnp.bfloat16)
        k_mem = kv[:, :d].reshape(bsz, mem_len, d)
        v_mem = kv[:, d:].reshape(bsz, mem_len, d)
        o = _xattn(h, wq_b, l, k_mem, v_mem, s)
        xf, h = _mm_norm_res(o, wo_b, l, xf, ln_mem_post[l], ln_ffn_pre[l])
        act = _ffn_up(h, ffn_w_up, l, ffn_conv_w[l], ffn_conv_b[l], s)
        g_next = ln_mix_pre[l + 1] if l + 1 < DEPTH else None
        xf, h = _mm_norm_res(act, w_down_b, l, xf, ln_ffn_post[l], g_next)
    return xf.reshape(bsz, s, d)
```
